```python
import jax
import jax.numpy as jnp
from jax import lax
import numpy as np

D_MODEL = 1024
BATCH = 32
SEQ = 256
DEPTH = 2
DEC_BATCH = 4
DEC_SEQ = 4096
PAST_LEN = 256

GRID_W = 64
NA_HEADS = 8
NA_HEAD_DIM = 64
NA_WIDTH = NA_HEADS * NA_HEAD_DIM
NA_WIN_ROWS = 8
NA_WIN_COLS = 16
CTX_Q_BLOCK = 128
RW_HEADS = 8
RW_HEAD_DIM = 64
RW_WIDTH = RW_HEADS * RW_HEAD_DIM
RW_DECAY_RANK = 64
RW_ICLR_RANK = 64
RW_GATE_RANK = 128
CONV_WIDTH = 512
CONV_K = 31
N_BRANCH = 3
N_MOD = 6
N_EXPERTS = 32
TOP_K = 4
EXPERT_FF = 1024
SWIGLU_LIMIT = 7.0
SWIGLU_ALPHA = 1.702
ROUTE_BLOCK = 512
RMS_EPS = 1e-6
LN_EPS = 1e-5
GN_EPS = 64e-5
NEG_INF = -1e30
A_COLS = 3 * NA_WIDTH
B_COLS = 3 * RW_WIDTH + 2 * RW_DECAY_RANK + 2 * RW_ICLR_RANK + RW_GATE_RANK
C_COLS = 2 * CONV_WIDTH
G_COLS = N_BRANCH * D_MODEL
P_IN = A_COLS + B_COLS + C_COLS + G_COLS

kernel_name = 'hybrid_na_rwkv7_conformer_moe_dit_step'

F32 = jnp.float32


def rms_norm(x, g):
    xf = x.astype(F32)
    y = xf * lax.rsqrt(jnp.mean(xf * xf, axis=-1, keepdims=True) + RMS_EPS)
    return (y * g.astype(F32)).astype(x.dtype)


def standardize(x, eps):
    xf = x.astype(F32)
    xc = xf - jnp.mean(xf, axis=-1, keepdims=True)
    return xc * lax.rsqrt(jnp.mean(xc * xc, axis=-1, keepdims=True) + eps)


def context_attention(q, k, v):
    b, L, h, dh = q.shape
    nb = L // CTX_Q_BLOCK
    qb = jnp.moveaxis(q.reshape(b, nb, CTX_Q_BLOCK, h, dh), 1, 0)
    scale = dh ** -0.5

    def one(qblk):
        s = jnp.einsum('bqhd,bkhd->bhqk', qblk, k).astype(F32) * scale
        p = jax.nn.softmax(s, axis=-1)
        return jnp.einsum('bhqk,bkhd->bqhd', p.astype(v.dtype), v)

    o = lax.map(one, qb)
    return jnp.moveaxis(o, 0, 1).reshape(b, L, h * dh)


def neighbourhood_attention(q, k, v, ctx_k, ctx_v, rpb):
    b, t, h, dh = q.shape
    rows = t // GRID_W
    kh = min(NA_WIN_ROWS, rows)
    kw = NA_WIN_COLS
    qg = (q * (dh ** -0.5)).reshape(b, rows, GRID_W, h, dh)
    kg = k.reshape(b, rows, GRID_W, h, dh)
    vg = v.reshape(b, rows, GRID_W, h, dh)
    r_idx = jnp.arange(rows, dtype=jnp.int32)
    row_start = jnp.clip(r_idx - kh // 2, 0, rows - kh)
    col = jnp.arange(GRID_W, dtype=jnp.int32)
    col_start = jnp.clip(col - kw // 2, 0, GRID_W - kw)
    col_ok = (col[None, :] >= col_start[:, None]) & (col[None, :] < col_start[:, None] + kw)
    dc = jnp.clip(col[None, :] - col[:, None] + kw - 1, 0, 2 * kw - 2)
    bias_c = jnp.where(col_ok, rpb[:, :, dc].astype(F32), NEG_INF)

    def row_block(args):
        qr, r0, rs = args
        kb = lax.dynamic_slice_in_dim(kg, rs, kh, axis=1)
        vb = lax.dynamic_slice_in_dim(vg, rs, kh, axis=1)
        dr = rs + jnp.arange(kh, dtype=jnp.int32) - r0 + NA_WIN_ROWS - 1
        bias = jnp.take(bias_c, dr, axis=1).transpose(2, 0, 1, 3)
        s_loc = jnp.einsum('bqhd,bikhd->bqhik', qr, kb).astype(F32) + bias[None]
        s_ctx = jnp.einsum('bqhd,bchd->bqhc', qr, ctx_k).astype(F32)
        n_loc = kh * GRID_W
        s = jnp.concatenate([s_loc.reshape(b, GRID_W, h, n_loc), s_ctx], axis=-1)
        p = jax.nn.softmax(s, axis=-1)
        p_loc = p[..., :n_loc].reshape(b, GRID_W, h, kh, GRID_W).astype(vb.dtype)
        p_ctx = p[..., n_loc:].astype(ctx_v.dtype)
        return (jnp.einsum('bqhik,bikhd->bqhd', p_loc, vb)
                + jnp.einsum('bqhc,bchd->bqhd', p_ctx, ctx_v))

    o = lax.map(row_block, (jnp.moveaxis(qg, 1, 0), r_idx, row_start))
    return jnp.moveaxis(o, 0, 1).reshape(b, t, h * dh)


def token_shift(z, mu):
    prev = jnp.pad(z, ((0, 0), (1, 0), (0, 0)))[:, :-1]
    nxt = jnp.pad(z, ((0, 0), (0, 1), (0, 0)))[:, 1:]
    return z + mu[0] * (prev - z) + mu[1] * (nxt - z)


def wkv_scan(r, w, k, v, kk, a, s0):
    def step(S, inp):
        r_t, w_t, k_t, v_t, kk_t, a_t = inp
        sk = jnp.einsum('bhvk,bhk->bhv', S, kk_t)
        S = (S * w_t[:, :, None, :] - sk[..., None] * (kk_t * a_t)[:, :, None, :]
             + v_t[..., None] * k_t[:, :, None, :])
        return S, jnp.einsum('bhvk,bhk->bhv', S, r_t)

    xs = tuple(jnp.moveaxis(z, 1, 0) for z in (r, w, k, v, kk, a))
    s_fin, ys = lax.scan(step, s0, xs)
    return jnp.moveaxis(ys, 0, 1), s_fin


def rwkv_mix(zb, s0, w0, w2, a0, a2, g2, k_k, k_a, r_k, lnx_g, lnx_b):
    b, t, _ = zb.shape
    zb = zb.astype(F32)
    C = RW_WIDTH
    heads = lambda z: z.reshape(b, t, RW_HEADS, RW_HEAD_DIM)
    r, k, v = zb[..., :C], zb[..., C:2 * C], zb[..., 2 * C:3 * C]
    o = 3 * C
    zw = (zb[..., o:o + RW_DECAY_RANK], zb[..., o + RW_DECAY_RANK:o + 2 * RW_DECAY_RANK])
    o += 2 * RW_DECAY_RANK
    za = (zb[..., o:o + RW_ICLR_RANK], zb[..., o + RW_ICLR_RANK:o + 2 * RW_ICLR_RANK])
    o += 2 * RW_ICLR_RANK
    zg = zb[..., o:o + RW_GATE_RANK]
    kk = heads(k * k_k)
    kk = kk * lax.rsqrt(jnp.maximum(jnp.sum(kk * kk, axis=-1, keepdims=True), 1e-24))
    rh, vh = heads(r), heads(v)
    s0 = s0.astype(F32)
    y = jnp.zeros_like(rh)
    finals = []
    for d in range(2):
        w_log = -jax.nn.softplus(-(w0[d] + jnp.tanh(zw[d]) @ w2[d])) - 0.5
        decay = jnp.exp(-jnp.exp(w_log))
        a = jax.nn.sigmoid(a0[d] + za[d] @ a2[d])
        kd = heads(k * (1.0 + (a - 1.0) * k_a))
        seq = (rh, heads(decay), kd, vh, kk, heads(a))
        if d == 1:
            seq = tuple(jnp.flip(s, axis=1) for s in seq)
        yd, sd = wkv_scan(*seq, s0[:, d])
        if d == 1:
            yd = jnp.flip(yd, axis=1)
        bonus = jnp.sum(rh * kd * r_k, axis=-1, keepdims=True) * vh
        y = y + yd + bonus
        finals.append(sd)
    y = standardize(y, GN_EPS).reshape(b, t, C) * lnx_g + lnx_b
    g = jax.nn.sigmoid(zg) @ g2
    return y * g, jnp.stack(finals, axis=1)


def conv_module(zc, conv_w, conv_b, ln_g, ln_b):
    h = zc[..., :CONV_WIDTH] * jax.nn.sigmoid(zc[..., CONV_WIDTH:])
    h = lax.conv_general_dilated(h, conv_w[:, None, :].astype(h.dtype), window_strides=(1,),
                                 padding='SAME', dimension_numbers=('NWC', 'WIO', 'NWC'),
                                 feature_group_count=CONV_WIDTH) + conv_b
    h = standardize(h, LN_EPS) * ln_g + ln_b
    return jax.nn.silu(h)


def expert_ffn(xb, e, w1, b1, w2, b2):
    h = xb @ w1[e] + b1[e]
    hg = jnp.minimum(h[..., ::2], SWIGLU_LIMIT)
    hl = jnp.clip(h[..., 1::2], -SWIGLU_LIMIT, SWIGLU_LIMIT)
    return (hg * jax.nn.sigmoid(SWIGLU_ALPHA * hg) * (hl + 1.0)) @ w2[e] + b2[e]


def moe(h, router_w, router_b, w1, b1, w2, b2):
    bsz, t, d = h.shape
    x = h.reshape(-1, d)
    n = x.shape[0]
    nk = n * TOP_K
    logits = (x @ router_w + router_b).astype(F32)
    top_v, top_i = lax.top_k(logits, TOP_K)
    gate = jax.nn.softmax(top_v, axis=-1)
    flat_e = top_i.reshape(-1)
    flat_tok = jnp.arange(nk, dtype=jnp.int32) // TOP_K
    flat_w = gate.reshape(-1)
    onehot = (flat_e[:, None] == jnp.arange(N_EXPERTS, dtype=flat_e.dtype)[None, :]).astype(jnp.int32)
    csum = jnp.cumsum(onehot, axis=0)
    rank = jnp.take_along_axis(csum, flat_e[:, None], axis=1)[:, 0] - 1
    counts = csum[-1]
    padded = ((counts + ROUTE_BLOCK - 1) // ROUTE_BLOCK) * ROUTE_BLOCK
    pad_end = jnp.cumsum(padded)
    pad_start = pad_end - padded
    dest = pad_start[flat_e] + rank
    n_blocks = -(-nk // ROUTE_BLOCK) + N_EXPERTS
    n_rows = n_blocks * ROUTE_BLOCK
    row_tok = jnp.full((n_rows,), n, jnp.int32).at[dest].set(flat_tok)
    row_w = jnp.zeros((n_rows,), F32).at[dest].set(flat_w)
    block_e = jnp.minimum(jnp.searchsorted(pad_end, jnp.arange(n_blocks, dtype=jnp.int32) * ROUTE_BLOCK,
                                           side='right'), N_EXPERTS - 1)
    xp = jnp.concatenate([x, jnp.zeros((1, d), x.dtype)], axis=0)
    xb = xp[row_tok].reshape(n_blocks, ROUTE_BLOCK, d)
    yb = lax.map(lambda a: expert_ffn(a[0], a[1], w1, b1, w2, b2), (xb, block_e))
    y = jnp.zeros((n + 1, d), F32).at[row_tok].add(yb.reshape(n_rows, d).astype(F32) * row_w[:, None])
    return y[:n].reshape(bsz, t, d).astype(h.dtype)


def block(x, cond, l, P, ctx):
    b, t, _ = x.shape
    mod = (jax.nn.silu(cond) @ P['mod_w'][l] + P['mod_b'][l]).reshape(cond.shape[0], N_MOD, 1, D_MODEL)
    sh1, sc1, gt1, sh2, sc2, gt2 = (mod[:, i] for i in range(N_MOD))
    g = P['norm_g'][l]
    u = rms_norm(x, g[0]) * (1.0 + sc1) + sh1
    z = u @ P['w_in'][l] + P['b_in'][l]
    hd = lambda a: a.reshape(b, t, NA_HEADS, NA_HEAD_DIM)
    q = hd(z[..., :NA_WIDTH])
    k = hd(z[..., NA_WIDTH:2 * NA_WIDTH])
    v = hd(z[..., 2 * NA_WIDTH:A_COLS])
    zb = token_shift(z[..., A_COLS:A_COLS + B_COLS], P['rw_mu'][l])
    zc = z[..., A_COLS + B_COLS:A_COLS + B_COLS + C_COLS]
    gates = jax.nn.sigmoid(z[..., A_COLS + B_COLS + C_COLS:].astype(F32)).reshape(b, t, N_BRANCH, D_MODEL)
    if ctx is None:
        o_att = context_attention(q, k, v)
        s0 = jnp.zeros((b, 2, RW_HEADS, RW_HEAD_DIM, RW_HEAD_DIM), F32)
    else:
        ctx_k, ctx_v, s0 = ctx
        o_att = neighbourhood_attention(q, k, v, ctx_k, ctx_v, P['rpb'][l])
    o_rw, s_fin = rwkv_mix(zb, s0, P['rw_w0'][l], P['rw_w2'][l], P['rw_a0'][l], P['rw_a2'][l],
                           P['rw_g2'][l], P['rw_k_k'][l], P['rw_k_a'][l], P['rw_r_k'][l],
                           P['rw_lnx_g'][l], P['rw_lnx_b'][l])
    o_cv = conv_module(zc, P['conv_w'][l], P['conv_b'][l], P['conv_ln_g'][l], P['conv_ln_b'][l])
    merged = (gates[..., 0, :] * (o_att @ P['w_o_attn'][l])
              + gates[..., 1, :] * (o_rw.astype(x.dtype) @ P['w_o_rwkv'][l])
              + gates[..., 2, :] * (o_cv.astype(x.dtype) @ P['w_o_conv'][l]))
    mix = merged.astype(x.dtype) @ P['w_out'][l]
    x = x + gt1 * rms_norm(mix, g[1])
    hm = rms_norm(x, g[2]) * (1.0 + sc2) + sh2
    y = moe(hm, P['router_w'][l], P['router_b'][l], P['exp_w1'][l], P['exp_b1'][l],
            P['exp_w2'][l], P['exp_b2'][l])
    x = x + gt2 * rms_norm(y, g[3])
    return x, (k, v, s_fin)


def setup_inputs(seed: int = 0) -> dict:
    key = jax.random.key(seed)
    ks = iter(jax.random.split(key, 64))
    L, D = DEPTH, D_MODEL

    def nrm(shape, s=1.0):
        return jax.random.normal(next(ks), shape, F32) * s

    def uni(shape, lo, hi):
        return jax.random.uniform(next(ks), shape, F32, lo, hi)

    return {
        'x_prompt': nrm((BATCH, SEQ, D)),
        'x_sample': nrm((DEC_BATCH, DEC_SEQ, D)),
        'cache_attn_k': nrm((DEC_BATCH, L, PAST_LEN, NA_HEADS, NA_HEAD_DIM)),
        'cache_attn_v': nrm((DEC_BATCH, L, PAST_LEN, NA_HEADS, NA_HEAD_DIM)),
        'state_rwkv': nrm((DEC_BATCH, L, 2, RW_HEADS, RW_HEAD_DIM, RW_HEAD_DIM), 0.5),
        'c': nrm((DEC_BATCH, D)),
        'c_ctx': nrm((D,)),
        'mod_w': nrm((L, D, N_MOD * D), 0.3 * D ** -0.5),
        'mod_b': nrm((L, N_MOD * D), 0.02),
        'norm_g': 1.0 + nrm((L, 4, D), 0.02),
        'w_in': nrm((L, D, P_IN), D ** -0.5),
        'b_in': nrm((L, P_IN), 0.02),
        'rpb': nrm((L, NA_HEADS, 2 * NA_WIN_ROWS - 1, 2 * NA_WIN_COLS - 1), 0.1),
        'w_o_attn': nrm((L, NA_WIDTH, D), NA_WIDTH ** -0.5),
        'rw_mu': uni((L, 2, B_COLS), 0.0, 0.5),
        'rw_w0': uni((L, 2, RW_WIDTH), -5.0, 0.0),
        'rw_w2': nrm((L, 2, RW_DECAY_RANK, RW_WIDTH), 0.1 * RW_DECAY_RANK ** -0.5),
        'rw_a0': nrm((L, 2, RW_WIDTH), 0.5),
        'rw_a2': nrm((L, 2, RW_ICLR_RANK, RW_WIDTH), 0.1 * RW_ICLR_RANK ** -0.5),
        'rw_g2': nrm((L, RW_GATE_RANK, RW_WIDTH), RW_GATE_RANK ** -0.5),
        'rw_k_k': 1.0 + nrm((L, RW_WIDTH), 0.1),
        'rw_k_a': 1.0 + nrm((L, RW_WIDTH), 0.1),
        'rw_r_k': nrm((L, RW_HEADS, RW_HEAD_DIM), 0.1),
        'rw_lnx_g': 1.0 + nrm((L, RW_WIDTH), 0.02),
        'rw_lnx_b': nrm((L, RW_WIDTH), 0.02),
        'w_o_rwkv': nrm((L, RW_WIDTH, D), RW_WIDTH ** -0.5),
        'conv_w': nrm((L, CONV_K, CONV_WIDTH), CONV_K ** -0.5),
        'conv_b': nrm((L, CONV_WIDTH), 0.02),
        'conv_ln_g': 1.0 + nrm((L, CONV_WIDTH), 0.02),
        'conv_ln_b': nrm((L, CONV_WIDTH), 0.02),
        'w_o_conv': nrm((L, CONV_WIDTH, D), CONV_WIDTH ** -0.5),
        'w_out': nrm((L, D, D), D ** -0.5),
        'router_w': nrm((L, D, N_EXPERTS), D ** -0.5),
        'router_b': nrm((L, N_EXPERTS), 0.01),
        'exp_w1': nrm((L, N_EXPERTS, D, 2 * EXPERT_FF), D ** -0.5),
        'exp_b1': nrm((L, N_EXPERTS, 2 * EXPERT_FF), 0.02),
        'exp_w2': nrm((L, N_EXPERTS, EXPERT_FF, D), EXPERT_FF ** -0.5),
        'exp_b2': nrm((L, N_EXPERTS, D), 0.02),
    }


def reference(x_prompt, x_sample, cache_attn_k, cache_attn_v, state_rwkv, c, c_ctx, mod_w, mod_b, norm_g,
              w_in, b_in, rpb, w_o_attn, rw_mu, rw_w0, rw_w2, rw_a0, rw_a2, rw_g2, rw_k_k, rw_k_a, rw_r_k,
              rw_lnx_g, rw_lnx_b, w_o_rwkv, conv_w, conv_b, conv_ln_g, conv_ln_b, w_o_conv, w_out,
              router_w, router_b, exp_w1, exp_b1, exp_w2, exp_b2):
    P = {
        'mod_w': mod_w, 'mod_b': mod_b, 'norm_g': norm_g, 'w_in': w_in, 'b_in': b_in, 'rpb': rpb,
        'w_o_attn': w_o_attn, 'rw_mu': rw_mu, 'rw_w0': rw_w0, 'rw_w2': rw_w2, 'rw_a0': rw_a0,
        'rw_a2': rw_a2, 'rw_g2': rw_g2, 'rw_k_k': rw_k_k, 'rw_k_a': rw_k_a, 'rw_r_k': rw_r_k,
        'rw_lnx_g': rw_lnx_g, 'rw_lnx_b': rw_lnx_b, 'w_o_rwkv': w_o_rwkv, 'conv_w': conv_w,
        'conv_b': conv_b, 'conv_ln_g': conv_ln_g, 'conv_ln_b': conv_ln_b, 'w_o_conv': w_o_conv,
        'w_out': w_out, 'router_w': router_w, 'router_b': router_b, 'exp_w1': exp_w1,
        'exp_b1': exp_b1, 'exp_w2': exp_w2, 'exp_b2': exp_b2,
    }
    x = x_prompt
    ks, vs, ss = [], [], []
    for l in range(DEPTH):
        x, (kc, vc, sc) = block(x, c_ctx[None, :], l, P, None)
        ks.append(kc)
        vs.append(vc)
        ss.append(sc)
    y_prompt = x
    x = x_sample
    for l in range(DEPTH):
        x, _ = block(x, c, l, P, (cache_attn_k[:, l], cache_attn_v[:, l], state_rwkv[:, l]))
    y_sample = x
    new_attn_k = jnp.stack(ks, axis=1)
    new_attn_v = jnp.stack(vs, axis=1)
    new_state_rwkv = jnp.stack(ss, axis=1)
    return (y_prompt, y_sample, new_attn_k, new_attn_v, new_state_rwkv)
```

```python
import functools

import jax
import jax.numpy as jnp
from jax import lax
from jax.experimental import pallas as pl
from jax.experimental.pallas import tpu as pltpu

F32 = jnp.float32
BF16 = jnp.bfloat16

D_MODEL = 1024
BATCH, SEQ = 32, 256
DEPTH = 2
DEC_BATCH, DEC_SEQ = 4, 4096
PAST_LEN = 256
GRID_W = 64
NA_HEADS, NA_HEAD_DIM = 8, 64
NA_WIDTH = NA_HEADS * NA_HEAD_DIM
NA_WIN_ROWS, NA_WIN_COLS = 8, 16
RW_HEADS, RW_HEAD_DIM = 8, 64
RW_WIDTH = RW_HEADS * RW_HEAD_DIM
RW_DECAY_RANK, RW_ICLR_RANK, RW_GATE_RANK = 64, 64, 128
CONV_WIDTH, CONV_K = 512, 31
N_BRANCH, N_MOD = 3, 6
N_EXPERTS, TOP_K, EXPERT_FF = 32, 4, 1024
SWIGLU_LIMIT, SWIGLU_ALPHA = 7.0, 1.702
ROUTE_BLOCK = 512
RMS_EPS, LN_EPS, GN_EPS = 1e-6, 1e-5, 64e-5
NEG_INF = -1e30
A_COLS = 3 * NA_WIDTH
B_COLS = 3 * RW_WIDTH + 2 * RW_DECAY_RANK + 2 * RW_ICLR_RANK + RW_GATE_RANK
C_COLS = 2 * CONV_WIDTH
G_COLS = N_BRANCH * D_MODEL
P_IN = A_COLS + B_COLS + C_COLS + G_COLS

LANES = 128
TM = 256
CHUNK = 64
CONV_HALO = 16
TC = 128
VMEM_LIMIT = 56 * 1024 * 1024


def _cparams(sem):
    return pltpu.CompilerParams(dimension_semantics=sem, vmem_limit_bytes=VMEM_LIMIT)


def _dot(a, b, dims=((1,), (0,))):
    return lax.dot_general(a, b, (dims, ((), ())), preferred_element_type=F32)


def _mm(a, b, dims=((1,), (0,))):
    return _dot(a.astype(BF16), b.astype(BF16), dims)


def _split(a):
    hi = a.astype(BF16)
    lo = (a - hi.astype(F32)).astype(BF16)
    return hi, lo


def _mm3(a, b, dims=((1,), (0,))):
    ah, al = _split(a)
    bh, bl = _split(b)
    return _dot(ah, bh, dims) + (_dot(ah, bl, dims) + _dot(al, bh, dims))


_NT = ((1,), (1,))


def _rms(x, g):
    return x * lax.rsqrt(jnp.mean(x * x, axis=-1, keepdims=True) + RMS_EPS) * g


def _sigmoid(x):
    return 1.0 / (1.0 + jnp.exp(-x))


class Layout:
    def __init__(self, n_ctx, ctx_len, n_dec, dec_len):
        assert ctx_len == TM and dec_len % TM == 0
        self.n_ctx, self.ctx_len, self.n_dec, self.dec_len = n_ctx, ctx_len, n_dec, dec_len
        self.ctx_tokens = n_ctx * ctx_len
        self.dec_tokens = n_dec * dec_len
        self.n = self.ctx_tokens + self.dec_tokens
        self.ctx_tiles = self.ctx_tokens // TM
        self.dec_tiles_per_seq = dec_len // TM
        self.tiles = self.n // TM

    def mod_row(self, i):
        return jnp.where(i < self.ctx_tiles, 0, 1 + (i - self.ctx_tiles) // self.dec_tiles_per_seq)

    def tile_pos(self, i):
        j = (i - self.ctx_tiles) % self.dec_tiles_per_seq
        is_ctx = i < self.ctx_tiles
        return is_ctx | (j == 0), is_ctx | (j == self.dec_tiles_per_seq - 1)


def _mod_kernel(c_ref, w_ref, b_ref, o_ref):
    c = c_ref[...]
    o_ref[...] = _mm3(c * _sigmoid(c), w_ref[...]) + b_ref[...]


def mod_table(cond, w, b):
    tn = 1024
    nm = w.shape[1]
    return pl.pallas_call(
        _mod_kernel,
        out_shape=jax.ShapeDtypeStruct((cond.shape[0], nm), F32),
        grid=(nm // tn,),
        in_specs=[pl.BlockSpec(cond.shape, lambda j: (0, 0)),
                  pl.BlockSpec((D_MODEL, tn), lambda j: (0, j)),
                  pl.BlockSpec((1, tn), lambda j: (0, j))],
        out_specs=pl.BlockSpec((cond.shape[0], tn), lambda j: (0, j)),
        compiler_params=_cparams(("arbitrary",)),
        name="mod_table",
    )(cond, w, b.reshape(1, nm))


def _inproj_kernel(x_ref, mod_ref, g_ref, w_ref, b_ref,
                   q_ref, kf_ref, vf_ref, kb_ref, vb_ref, zb_ref, zc_ref, gt_ref):
    u = (_rms(x_ref[...], g_ref[...]) * (1.0 + mod_ref[0, 1:2, :]) + mod_ref[0, 0:1, :]).astype(BF16)

    def seg(c0, width):
        return _dot(u, w_ref[:, c0:c0 + width]) + b_ref[:, c0:c0 + width]

    q_ref[...] = seg(0, NA_WIDTH).astype(BF16)
    k = seg(NA_WIDTH, NA_WIDTH)
    kf_ref[...] = k
    kb_ref[...] = k.astype(BF16)
    v = seg(2 * NA_WIDTH, NA_WIDTH)
    vf_ref[...] = v
    vb_ref[...] = v.astype(BF16)
    wb = 384
    for j in range(B_COLS // wb):
        zb_ref[:, j * wb:(j + 1) * wb] = seg(A_COLS + j * wb, wb)
    wc = 512
    for j in range(C_COLS // wc):
        zc_ref[:, j * wc:(j + 1) * wc] = seg(A_COLS + B_COLS + j * wc, wc)
    for j in range(G_COLS // wc):
        gt_ref[:, j * wc:(j + 1) * wc] = _sigmoid(seg(A_COLS + B_COLS + C_COLS + j * wc, wc)).astype(BF16)


def in_proj(lay, x, mod, g0, w_bf, b):
    n = lay.n
    row = lambda w: pl.BlockSpec((TM, w), lambda i: (i, 0))
    out_shape = (jax.ShapeDtypeStruct((n, NA_WIDTH), BF16),
                 jax.ShapeDtypeStruct((n, NA_WIDTH), F32), jax.ShapeDtypeStruct((n, NA_WIDTH), F32),
                 jax.ShapeDtypeStruct((n, NA_WIDTH), BF16), jax.ShapeDtypeStruct((n, NA_WIDTH), BF16),
                 jax.ShapeDtypeStruct((n, B_COLS), F32), jax.ShapeDtypeStruct((n, C_COLS), F32),
                 jax.ShapeDtypeStruct((n, G_COLS), BF16))
    return pl.pallas_call(
        _inproj_kernel,
        out_shape=out_shape,
        grid=(lay.tiles,),
        in_specs=[row(D_MODEL),
                  pl.BlockSpec((1, N_MOD, D_MODEL), lambda i: (lay.mod_row(i), 0, 0)),
                  pl.BlockSpec((1, D_MODEL), lambda i: (0, 0)),
                  pl.BlockSpec((D_MODEL, P_IN), lambda i: (0, 0), pipeline_mode=pl.Buffered(1)),
                  pl.BlockSpec((1, P_IN), lambda i: (0, 0))],
        out_specs=(row(NA_WIDTH), row(NA_WIDTH), row(NA_WIDTH), row(NA_WIDTH), row(NA_WIDTH),
                   row(B_COLS), row(C_COLS), row(G_COLS)),
        compiler_params=_cparams(("arbitrary",)),
        name="in_proj",
    )(x, mod, g0, w_bf, b)


def _ctx_attn_kernel(q_ref, k_ref, v_ref, o_ref):
    scale = NA_HEAD_DIM ** -0.5
    for h in range(NA_HEADS):
        sl = slice(h * NA_HEAD_DIM, (h + 1) * NA_HEAD_DIM)
        s = _dot(q_ref[:, sl], k_ref[:, sl], _NT) * scale
        m = jnp.max(s, axis=-1, keepdims=True)
        p = jnp.exp(s - m)
        den = jnp.sum(p, axis=-1, keepdims=True)
        o_ref[:, sl] = (_dot(p.astype(BF16), v_ref[:, sl]) / den).astype(BF16)


def ctx_attention(lay, q, kb, vb):
    blk = pl.BlockSpec((lay.ctx_len, NA_WIDTH), lambda b: (b, 0))
    return pl.pallas_call(
        _ctx_attn_kernel,
        out_shape=jax.ShapeDtypeStruct((lay.ctx_tokens, NA_WIDTH), BF16),
        grid=(lay.n_ctx,),
        in_specs=[blk, blk, blk],
        out_specs=blk,
        compiler_params=_cparams(("arbitrary",)),
        name="ctx_attention",
    )(q, kb, vb)


def na_bias_table(rpb):
    kw, kh = NA_WIN_COLS, NA_WIN_ROWS
    col = jnp.arange(GRID_W, dtype=jnp.int32)
    col_start = jnp.clip(col - kw // 2, 0, GRID_W - kw)
    col_ok = (col[None, :] >= col_start[:, None]) & (col[None, :] < col_start[:, None] + kw)
    dc = jnp.clip(col[None, :] - col[:, None] + kw - 1, 0, 2 * kw - 2)
    bias_c = jnp.where(col_ok, rpb[:, :, dc].astype(F32), NEG_INF)
    tabs = [jnp.concatenate([bias_c[:, d + i] for i in range(kh)], axis=-1) for d in range(kh)]
    return jnp.stack(tabs, axis=0)


def _na_row_start(r, rows):
    return jnp.clip(r - NA_WIN_ROWS // 2, 0, rows - NA_WIN_ROWS)


def _na_kernel(q_ref, k_ref, v_ref, ck_ref, cv_ref, bias_ref, o_ref, *, rows):
    r = pl.program_id(1)
    rs = _na_row_start(r, rows)
    n_loc = NA_WIN_ROWS * GRID_W
    start = pl.multiple_of(rs * GRID_W, GRID_W)
    k_loc = k_ref[pl.ds(start, n_loc), :]
    v_loc = v_ref[pl.ds(start, n_loc), :]
    scale = NA_HEAD_DIM ** -0.5
    for h in range(NA_HEADS):
        sl = slice(h * NA_HEAD_DIM, (h + 1) * NA_HEAD_DIM)
        qh = q_ref[:, sl] * scale
        s_loc = _dot(qh, k_loc[:, sl], _NT) + bias_ref[0, h]
        s_ctx = _dot(qh, ck_ref[0, :, sl], _NT)
        m = jnp.maximum(jnp.max(s_loc, axis=-1, keepdims=True), jnp.max(s_ctx, axis=-1, keepdims=True))
        p_loc = jnp.exp(s_loc - m)
        p_ctx = jnp.exp(s_ctx - m)
        den = jnp.sum(p_loc, axis=-1, keepdims=True) + jnp.sum(p_ctx, axis=-1, keepdims=True)
        o = _dot(p_loc.astype(BF16), v_loc[:, sl]) + _dot(p_ctx.astype(BF16), cv_ref[0, :, sl])
        o_ref[:, sl] = (o / den).astype(BF16)


def na_attention(lay, q, kb, vb, ck, cv, bias_tab):
    rows = lay.dec_len // GRID_W
    assert rows >= NA_WIN_ROWS
    qoff = lay.ctx_tokens // GRID_W
    koff = lay.ctx_tokens // lay.dec_len
    assert lay.ctx_tokens % lay.dec_len == 0
    kv = pl.BlockSpec((lay.dec_len, NA_WIDTH), lambda b, r: (koff + b, 0))
    cs = pl.BlockSpec((1, ck.shape[1], NA_WIDTH), lambda b, r: (b, 0, 0))
    return pl.pallas_call(
        functools.partial(_na_kernel, rows=rows),
        out_shape=jax.ShapeDtypeStruct((lay.dec_tokens, NA_WIDTH), BF16),
        grid=(lay.n_dec, rows),
        in_specs=[pl.BlockSpec((GRID_W, NA_WIDTH), lambda b, r: (qoff + b * rows + r, 0)),
                  kv, kv, cs, cs,
                  pl.BlockSpec((1, NA_HEADS, GRID_W, NA_WIN_ROWS * GRID_W),
                               lambda b, r: (_na_row_start(r, rows) - r + NA_WIN_ROWS - 1, 0, 0, 0))],
        out_specs=pl.BlockSpec((GRID_W, NA_WIDTH), lambda b, r: (b * rows + r, 0)),
        compiler_params=_cparams(("arbitrary", "arbitrary")),
        name="na_attention",
    )(q, kb, vb, ck, cv, bias_tab)


def _softplus(x):
    return jnp.maximum(x, 0.0) + jnp.log(1.0 + jnp.exp(-jnp.abs(x)))


def _rwkv_direction(d, z, prev_row, next_row, prm, s_ref):
    (mu_ref, w0_ref, w2_ref, a0_ref, a2_ref, kk_ref, ka_ref, rk_ref) = prm
    c = CHUNK
    cw = RW_WIDTH
    row = lax.broadcasted_iota(jnp.int32, (c, 1), 0)
    prev = jnp.where(row == 0, prev_row, pltpu.roll(z, 1, axis=0))
    nxt = jnp.where(row == c - 1, next_row, pltpu.roll(z, c - 1, axis=0))
    zs = z + mu_ref[0:1, :] * (prev - z) + mu_ref[1:2, :] * (nxt - z)
    r, k, v = zs[:, :cw], zs[:, cw:2 * cw], zs[:, 2 * cw:3 * cw]
    o = 3 * cw
    zw = zs[:, o + d * RW_DECAY_RANK:o + (d + 1) * RW_DECAY_RANK]
    o += 2 * RW_DECAY_RANK
    za = zs[:, o + d * RW_ICLR_RANK:o + (d + 1) * RW_ICLR_RANK]

    w_log = -_softplus(-(w0_ref[d:d + 1, :] + _mm3(jnp.tanh(zw), w2_ref[d]))) - 0.5
    logw = -jnp.exp(w_log)
    a = _sigmoid(a0_ref[d:d + 1, :] + _mm3(za, a2_ref[d]))
    kd = k * (1.0 + (a - 1.0) * ka_ref[...])
    kk = k * kk_ref[...]

    ti = lax.broadcasted_iota(jnp.int32, (c, c), 0)
    tj = lax.broadcasted_iota(jnp.int32, (c, c), 1)
    diff = (ti - tj) if d == 0 else (tj - ti)
    incl = diff >= 0
    strict = diff > 0
    tri = incl.astype(BF16)
    lh, ll = _split(logw)
    cl = _dot(tri, lh) + _dot(tri, ll)
    p_in = jnp.exp(cl)
    p_ex = jnp.exp(cl - logw)
    p_inv = jnp.exp(-cl)
    p_end = p_in[c - 1:c, :] if d == 0 else p_in[0:1, :]
    eye = (diff == 0).astype(F32)

    ys = []
    for h in range(RW_HEADS):
        sl = slice(h * RW_HEAD_DIM, (h + 1) * RW_HEAD_DIM)
        kkh = kk[:, sl]
        kkh = kkh * lax.rsqrt(jnp.maximum(jnp.sum(kkh * kkh, axis=-1, keepdims=True), 1e-24))
        rh, vh, kdh, ah = r[:, sl], v[:, sl], kd[:, sl], a[:, sl]
        at = -kkh * p_ex[:, sl]
        bt = ah * kkh * p_inv[:, sl]
        kt = kdh * p_inv[:, sl]
        rt = rh * p_in[:, sl]
        a_ab = jnp.where(strict, _mm3(at, bt, _NT), 0.0)
        a_ak = jnp.where(strict, _mm3(at, kt, _NT), 0.0)
        a_rb = jnp.where(incl, _mm3(rt, bt, _NT), 0.0)
        a_rk = jnp.where(incl, _mm3(rt, kt, _NT), 0.0)
        tinv = eye + a_ab
        pw = a_ab
        for _ in range(CHUNK.bit_length() - 2):
            pw = _mm3(pw, pw)
            tinv = tinv + _mm3(tinv, pw)
        s0 = s_ref[d, h]
        u = _mm3(tinv, _mm3(at, s0, _NT) + _mm3(a_ak, vh))
        y = _mm3(rt, s0, _NT) + _mm3(a_rb, u) + _mm3(a_rk, vh)
        pe = p_end[:, sl]
        s_ref[d, h] = s0 * pe + _mm3(u.T, bt * pe) + _mm3(vh.T, kt * pe)
        bonus = jnp.sum(rh * kdh * rk_ref[:, sl], axis=-1, keepdims=True) * vh
        ys.append(y + bonus)
    return jnp.concatenate(ys, axis=-1), zs


def _rwkv_kernel(*refs, nc, has_s0):
    zf_ref, zfp_ref, zfn_ref, zr_ref, zrp_ref, zrn_ref = refs[:6]
    n_in = 6 + (1 if has_s0 else 0)
    s0_ref = refs[6] if has_s0 else None
    prm = refs[n_in:n_in + 8]
    g2_ref = refs[n_in + 8]
    yf_ref, yb_ref, g_ref, sfin_ref, s_ref = refs[n_in + 9:]
    i = pl.program_id(1)

    @pl.when(i == 0)
    def _():
        if has_s0:
            s_ref[...] = s0_ref[0]
        else:
            s_ref[...] = jnp.zeros(s_ref.shape, F32)

    zero = jnp.zeros((1, B_COLS), F32)
    pf = jnp.where(i == 0, zero, zfp_ref[7:8, :])
    nf = jnp.where(i == nc - 1, zero, zfn_ref[0:1, :])
    yf, zs = _rwkv_direction(0, zf_ref[...], pf, nf, prm, s_ref)
    yf_ref[...] = yf
    g_ref[...] = _mm(_sigmoid(zs[:, B_COLS - RW_GATE_RANK:]), g2_ref[...])
    pr = jnp.where(i == nc - 1, zero, zrp_ref[7:8, :])
    nr = jnp.where(i == 0, zero, zrn_ref[0:1, :])
    yb, _ = _rwkv_direction(1, zr_ref[...], pr, nr, prm, s_ref)
    yb_ref[...] = yb

    @pl.when(i == nc - 1)
    def _():
        sfin_ref[0] = s_ref[...]


def rwkv_scan(zb, tok0, nseq, seqlen, s0, prm, g2):
    n = zb.shape[0]
    nc = seqlen // CHUNK
    sub = CHUNK // 8
    base = tok0 // CHUNK
    nt = nseq * seqlen
    has_s0 = s0 is not None

    def cf(s, i):
        return base + s * nc + i

    def cr(s, i):
        return base + s * nc + (nc - 1 - i)

    def specs(cidx):
        return [pl.BlockSpec((CHUNK, B_COLS), lambda s, i: (cidx(s, i), 0)),
                pl.BlockSpec((8, B_COLS), lambda s, i: (jnp.maximum(cidx(s, i) * sub - 1, 0), 0)),
                pl.BlockSpec((8, B_COLS), lambda s, i: (jnp.minimum((cidx(s, i) + 1) * sub, n // 8 - 1), 0))]

    full = lambda a: pl.BlockSpec(a.shape, lambda s, i: (0,) * a.ndim)
    sshape = (2, RW_HEADS, RW_HEAD_DIM, RW_HEAD_DIM)
    in_specs = specs(cf) + specs(cr)
    args = [zb] * 6
    if has_s0:
        in_specs.append(pl.BlockSpec((1,) + sshape, lambda s, i: (s, 0, 0, 0, 0)))
        args.append(s0)
    in_specs += [full(p) for p in prm] + [full(g2)]
    args += list(prm) + [g2]
    out = lambda cidx: pl.BlockSpec((CHUNK, RW_WIDTH), lambda s, i: (cidx(s, i) - base, 0))
    tok = jax.ShapeDtypeStruct((nt, RW_WIDTH), F32)
    return pl.pallas_call(
        functools.partial(_rwkv_kernel, nc=nc, has_s0=has_s0),
        out_shape=(tok, tok, tok, jax.ShapeDtypeStruct((nseq,) + sshape, F32)),
        grid=(nseq, nc),
        in_specs=in_specs,
        out_specs=(out(cf), out(cr), out(cf), pl.BlockSpec((1,) + sshape, lambda s, i: (s, 0, 0, 0, 0))),
        scratch_shapes=[pltpu.VMEM(sshape, F32)],
        compiler_params=_cparams(("arbitrary", "arbitrary")),
        name="rwkv_scan",
    )(*args)


def _conv_kernel(z_ref, zp_ref, zn_ref, w_ref, b_ref, g_ref, be_ref, o_ref, h_ref, *, lay):
    i = pl.program_id(0)
    first, last = lay.tile_pos(i)

    def glu(z):
        return z[:, :CONV_WIDTH] * _sigmoid(z[:, CONV_WIDTH:])

    hz = jnp.zeros((CONV_HALO, CONV_WIDTH), F32)
    h_ref[0:CONV_HALO, :] = jnp.where(first, hz, glu(zp_ref[...]))
    h_ref[CONV_HALO:CONV_HALO + TM, :] = glu(z_ref[...])
    h_ref[CONV_HALO + TM:, :] = jnp.where(last, hz, glu(zn_ref[...]))
    off = CONV_HALO - CONV_K // 2
    acc = jnp.zeros((TM, CONV_WIDTH), F32) + b_ref[...]
    for j in range(CONV_K):
        acc = acc + w_ref[j:j + 1, :] * h_ref[off + j:off + j + TM, :]
    xc = acc - jnp.mean(acc, axis=-1, keepdims=True)
    hn = xc * lax.rsqrt(jnp.mean(xc * xc, axis=-1, keepdims=True) + LN_EPS) * g_ref[...] + be_ref[...]
    o_ref[...] = (hn * _sigmoid(hn)).astype(BF16)


def conv_module(lay, zc, w, b, g, be):
    n = lay.n
    hb = TM // CONV_HALO
    vec = pl.BlockSpec((1, CONV_WIDTH), lambda i: (0, 0))
    return pl.pallas_call(
        functools.partial(_conv_kernel, lay=lay),
        out_shape=jax.ShapeDtypeStruct((n, CONV_WIDTH), BF16),
        grid=(lay.tiles,),
        in_specs=[pl.BlockSpec((TM, C_COLS), lambda i: (i, 0)),
                  pl.BlockSpec((CONV_HALO, C_COLS), lambda i: (jnp.maximum(i * hb - 1, 0), 0)),
                  pl.BlockSpec((CONV_HALO, C_COLS), lambda i: (jnp.minimum((i + 1) * hb, n // CONV_HALO - 1), 0)),
                  pl.BlockSpec((CONV_K, CONV_WIDTH), lambda i: (0, 0)), vec, vec, vec],
        out_specs=pl.BlockSpec((TM, CONV_WIDTH), lambda i: (i, 0)),
        scratch_shapes=[pltpu.VMEM((TM + 2 * CONV_HALO, CONV_WIDTH), F32)],
        compiler_params=_cparams(("arbitrary",)),
        name="conv_module",
    )(zc, zc, zc, w, b, g, be)


def _merge_kernel(x_ref, oa_ref, yf_ref, yb_ref, g_ref, oc_ref, gt_ref, mod_ref, ng_ref, lg_ref, lb_ref,
                  wa_ref, wr_ref, wc_ref, wo_ref, rw_ref, rb_ref,
                  x1_ref, hm_ref, ti_ref, tg_ref):
    y = yf_ref[...] + yb_ref[...]
    outs = []
    for h in range(RW_HEADS):
        yh = y[:, h * RW_HEAD_DIM:(h + 1) * RW_HEAD_DIM]
        yc = yh - jnp.mean(yh, axis=-1, keepdims=True)
        outs.append(yc * lax.rsqrt(jnp.mean(yc * yc, axis=-1, keepdims=True) + GN_EPS))
    o_rw = (jnp.concatenate(outs, axis=-1) * lg_ref[...] + lb_ref[...]) * g_ref[...]
    merged = (gt_ref[:, :D_MODEL].astype(F32) * _dot(oa_ref[...], wa_ref[...])
              + gt_ref[:, D_MODEL:2 * D_MODEL].astype(F32) * _mm(o_rw, wr_ref[...])
              + gt_ref[:, 2 * D_MODEL:].astype(F32) * _dot(oc_ref[...], wc_ref[...]))
    mix = _mm(merged, wo_ref[...])
    x1 = x_ref[...] + mod_ref[0, 2:3, :] * _rms(mix, ng_ref[1:2, :])
    x1_ref[...] = x1
    hm = _rms(x1, ng_ref[2:3, :]) * (1.0 + mod_ref[0, 4:5, :]) + mod_ref[0, 3:4, :]
    hm_ref[...] = hm
    logits = _mm3(hm, rw_ref[...]) + rb_ref[...]
    lane = lax.broadcasted_iota(jnp.int32, logits.shape, 1)
    idx_out = jnp.zeros(logits.shape, jnp.int32)
    val_out = jnp.full(logits.shape, NEG_INF, F32)
    for j in range(TOP_K):
        m = jnp.max(logits, axis=-1, keepdims=True)
        idx = jnp.min(jnp.where(logits == m, lane, LANES), axis=-1, keepdims=True)
        idx_out = jnp.where(lane == j, idx, idx_out)
        val_out = jnp.where(lane == j, m, val_out)
        logits = jnp.where(lane == idx, -jnp.inf, logits)
    e = jnp.exp(val_out - jnp.max(val_out, axis=-1, keepdims=True))
    ti_ref[...] = idx_out
    tg_ref[...] = e / jnp.sum(e, axis=-1, keepdims=True)


def merge_router(lay, x, o_att, yf, yb, g, o_cv, gates, mod, ng, lg, lb, wa, wr, wc, wo, rw, rb):
    n = lay.n
    row = lambda w: pl.BlockSpec((TM, w), lambda i: (i, 0))
    full = lambda a: pl.BlockSpec(a.shape, lambda i: (0,) * a.ndim)
    return pl.pallas_call(
        _merge_kernel,
        out_shape=(jax.ShapeDtypeStruct((n, D_MODEL), F32), jax.ShapeDtypeStruct((n, D_MODEL), F32),
                   jax.ShapeDtypeStruct((n, LANES), jnp.int32), jax.ShapeDtypeStruct((n, LANES), F32)),
        grid=(lay.tiles,),
        in_specs=[row(D_MODEL), row(NA_WIDTH), row(RW_WIDTH), row(RW_WIDTH), row(RW_WIDTH), row(CONV_WIDTH),
                  row(G_COLS), pl.BlockSpec((1, N_MOD, D_MODEL), lambda i: (lay.mod_row(i), 0, 0)),
                  full(ng), full(lg), full(lb), full(wa), full(wr), full(wc), full(wo), full(rw), full(rb)],
        out_specs=(row(D_MODEL), row(D_MODEL), row(LANES), row(LANES)),
        compiler_params=_cparams(("arbitrary",)),
        name="merge_router",
    )(x, o_att, yf, yb, g, o_cv, gates, mod, ng, lg, lb, wa, wr, wc, wo, rw, rb)


def route_slots(top_i, n):
    nk = n * TOP_K
    flat_e = top_i.reshape(-1)
    onehot = (flat_e[:, None] == jnp.arange(N_EXPERTS, dtype=flat_e.dtype)[None, :]).astype(jnp.int32)
    csum = jnp.cumsum(onehot, axis=0)
    rank = jnp.take_along_axis(csum, flat_e[:, None], axis=1)[:, 0] - 1
    counts = csum[-1]
    padded = ((counts + ROUTE_BLOCK - 1) // ROUTE_BLOCK) * ROUTE_BLOCK
    pad_end = jnp.cumsum(padded)
    dest = (pad_end - padded)[flat_e] + rank
    n_blocks = -(-nk // ROUTE_BLOCK) + N_EXPERTS
    block_e = jnp.minimum(jnp.searchsorted(pad_end, jnp.arange(n_blocks, dtype=jnp.int32) * ROUTE_BLOCK,
                                           side='right'), N_EXPERTS - 1).astype(jnp.int32)
    n_used = (pad_end[-1] // ROUTE_BLOCK).astype(jnp.int32).reshape(1)
    return dest.astype(jnp.int32), block_e, n_used, n_blocks


def _row_copy_wait(src_rows, dst_rows, sem):
    pltpu.make_async_copy(src_rows, dst_rows, sem).wait()


def _dispatch_kernel(dest_ref, hm_ref, xb_in_ref, xb_ref, sem):
    del xb_in_ref

    def issue(t, carry):
        for j in range(TOP_K):
            d = dest_ref[0, 0, t * TOP_K + j]
            pltpu.make_async_copy(hm_ref.at[pl.ds(t, 1), :], xb_ref.at[pl.ds(d, 1), :], sem).start()
        return carry

    lax.fori_loop(0, TM, issue, 0)
    for j in range(TOP_K):
        _row_copy_wait(hm_ref, xb_ref.at[pl.ds(0, TM), :], sem)


def moe_dispatch(lay, hm, dest, n_rows):
    xb0 = jnp.zeros((n_rows, D_MODEL), F32)
    return pl.pallas_call(
        _dispatch_kernel,
        out_shape=jax.ShapeDtypeStruct((n_rows, D_MODEL), F32),
        grid=(lay.tiles,),
        in_specs=[pl.BlockSpec((1, 1, TM * TOP_K), lambda i: (i, 0, 0), memory_space=pltpu.SMEM),
                  pl.BlockSpec((TM, D_MODEL), lambda i: (i, 0)),
                  pl.BlockSpec(memory_space=pl.ANY)],
        out_specs=pl.BlockSpec(memory_space=pl.ANY),
        scratch_shapes=[pltpu.SemaphoreType.DMA],
        input_output_aliases={2: 0},
        compiler_params=_cparams(("arbitrary",)),
        name="moe_dispatch",
    )(dest.reshape(lay.tiles, 1, TM * TOP_K), hm, xb0)


def _expert_kernel(be_ref, nu_ref, x_ref, w1g_ref, w1l_ref, b1g_ref, b1l_ref, w2_ref, b2_ref, y_ref):
    b = pl.program_id(0)

    @pl.when(b < nu_ref[0])
    def _():
        x = x_ref[...].astype(BF16)
        cw = 512
        acc = jnp.zeros((ROUTE_BLOCK, D_MODEL), F32) + b2_ref[0]
        for j in range(EXPERT_FF // cw):
            cs = slice(j * cw, (j + 1) * cw)
            hg = jnp.minimum(_dot(x, w1g_ref[0, :, cs]) + b1g_ref[0, :, cs], SWIGLU_LIMIT)
            hl = jnp.clip(_dot(x, w1l_ref[0, :, cs]) + b1l_ref[0, :, cs], -SWIGLU_LIMIT, SWIGLU_LIMIT)
            act = hg * _sigmoid(SWIGLU_ALPHA * hg) * (hl + 1.0)
            acc = acc + _dot(act.astype(BF16), w2_ref[0, cs, :])
        y_ref[...] = acc

    @pl.when(b >= nu_ref[0])
    def _():
        y_ref[...] = jnp.zeros(y_ref.shape, F32)


def moe_experts(xb, block_e, n_used, n_blocks, w1g, w1l, b1g, b1l, w2, b2):
    def xmap(b, be, nu):
        return (jnp.minimum(b, jnp.maximum(nu[0] - 1, 0)), 0)

    def wmap(b, be, nu):
        return (be[jnp.minimum(b, jnp.maximum(nu[0] - 1, 0))], 0, 0)

    wspec = lambda a: pl.BlockSpec((1,) + a.shape[1:], wmap)
    return pl.pallas_call(
        _expert_kernel,
        out_shape=jax.ShapeDtypeStruct(xb.shape, F32),
        grid_spec=pltpu.PrefetchScalarGridSpec(
            num_scalar_prefetch=2,
            grid=(n_blocks,),
            in_specs=[pl.BlockSpec((ROUTE_BLOCK, D_MODEL), xmap),
                      wspec(w1g), wspec(w1l), wspec(b1g), wspec(b1l), wspec(w2), wspec(b2)],
            out_specs=pl.BlockSpec((ROUTE_BLOCK, D_MODEL), lambda b, be, nu: (b, 0))),
        compiler_params=_cparams(("arbitrary",)),
        name="moe_experts",
    )(block_e, n_used, xb, w1g, w1l, b1g, b1l, w2, b2)


def _combine_kernel(dest_ref, yb_ref, tg_ref, x1_ref, mod_ref, ng_ref, o_ref, buf_ref, sem):
    def issue(t, carry):
        for j in range(TOP_K):
            d = dest_ref[0, 0, t * TOP_K + j]
            pltpu.make_async_copy(yb_ref.at[pl.ds(d, 1), :], buf_ref.at[j, pl.ds(t, 1), :], sem).start()
        return carry

    lax.fori_loop(0, TC, issue, 0)
    for j in range(TOP_K):
        _row_copy_wait(yb_ref.at[pl.ds(0, TC), :], buf_ref.at[j], sem)
    y = jnp.zeros((TC, D_MODEL), F32)
    for j in range(TOP_K):
        y = y + tg_ref[:, j:j + 1] * buf_ref[j]
    o_ref[...] = x1_ref[...] + mod_ref[0, 5:6, :] * _rms(y, ng_ref[3:4, :])


def moe_combine(lay, yb, dest, tg, x1, mod, ng):
    n = lay.n
    per = TM // TC
    row = lambda w: pl.BlockSpec((TC, w), lambda i: (i, 0))
    return pl.pallas_call(
        _combine_kernel,
        out_shape=jax.ShapeDtypeStruct((n, D_MODEL), F32),
        grid=(n // TC,),
        in_specs=[pl.BlockSpec((1, 1, TC * TOP_K), lambda i: (i, 0, 0), memory_space=pltpu.SMEM),
                  pl.BlockSpec(memory_space=pl.ANY),
                  row(LANES), row(D_MODEL),
                  pl.BlockSpec((1, N_MOD, D_MODEL), lambda i: (lay.mod_row(i // per), 0, 0)),
                  pl.BlockSpec(ng.shape, lambda i: (0, 0))],
        out_specs=row(D_MODEL),
        scratch_shapes=[pltpu.VMEM((TOP_K, TC, D_MODEL), F32), pltpu.SemaphoreType.DMA],
        compiler_params=_cparams(("arbitrary",)),
        name="moe_combine",
    )(dest.reshape(n // TC, 1, TC * TOP_K), yb, tg, x1, mod, ng)


def layer(lay, x, l, cond, P, ctx_k, ctx_v, s0_dec):
    mod = mod_table(cond, P['mod_w'][l], P['mod_b'][l]).reshape(cond.shape[0], N_MOD, D_MODEL)
    ng = P['norm_g'][l]
    q, kf, vf, kb, vb, zb, zc, gates = in_proj(lay, x, mod, ng[0:1], P['w_in'][l].astype(BF16),
                                               P['b_in'][l].reshape(1, P_IN))
    oa_ctx = ctx_attention(lay, q, kb, vb)
    oa_dec = na_attention(lay, q, kb, vb, ctx_k.astype(BF16), ctx_v.astype(BF16), na_bias_table(P['rpb'][l]))
    o_att = jnp.concatenate([oa_ctx, oa_dec], axis=0)
    vec = lambda a: a.reshape(1, -1)
    prm = (P['rw_mu'][l], P['rw_w0'][l], P['rw_w2'][l], P['rw_a0'][l], P['rw_a2'][l],
           vec(P['rw_k_k'][l]), vec(P['rw_k_a'][l]), vec(P['rw_r_k'][l]))
    yf_c, yb_c, g_c, s_ctx = rwkv_scan(zb, 0, lay.n_ctx, lay.ctx_len, None, prm, P['rw_g2'][l])
    yf_d, yb_d, g_d, _ = rwkv_scan(zb, lay.ctx_tokens, lay.n_dec, lay.dec_len, s0_dec, prm, P['rw_g2'][l])
    yf = jnp.concatenate([yf_c, yf_d], axis=0)
    yb = jnp.concatenate([yb_c, yb_d], axis=0)
    g = jnp.concatenate([g_c, g_d], axis=0)
    o_cv = conv_module(lay, zc, P['conv_w'][l], vec(P['conv_b'][l]), vec(P['conv_ln_g'][l]),
                       vec(P['conv_ln_b'][l]))
    rw = jnp.zeros((D_MODEL, LANES), F32).at[:, :N_EXPERTS].set(P['router_w'][l])
    rb = jnp.full((1, LANES), NEG_INF, F32).at[0, :N_EXPERTS].set(P['router_b'][l])
    x1, hm, top_i, top_g = merge_router(
        lay, x, o_att, yf, yb, g, o_cv, gates, mod, ng, vec(P['rw_lnx_g'][l]), vec(P['rw_lnx_b'][l]),
        P['w_o_attn'][l].astype(BF16), P['w_o_rwkv'][l].astype(BF16), P['w_o_conv'][l].astype(BF16),
        P['w_out'][l].astype(BF16), rw, rb)
    dest, block_e, n_used, n_blocks = route_slots(top_i[:, :TOP_K], lay.n)
    xb = moe_dispatch(lay, hm, dest, n_blocks * ROUTE_BLOCK)
    w1 = P['exp_w1'][l]
    b1 = P['exp_b1'][l]
    ybk = moe_experts(xb, block_e, n_used, n_blocks,
                      w1[..., 0::2].astype(BF16), w1[..., 1::2].astype(BF16),
                      b1[:, None, 0::2], b1[:, None, 1::2],
                      P['exp_w2'][l].astype(BF16), P['exp_b2'][l][:, None, :])
    x2 = moe_combine(lay, ybk, dest, top_g, x1, mod, ng)
    return x2, kf, vf, s_ctx


def kernel(x_prompt, x_sample, cache_attn_k, cache_attn_v, state_rwkv, c, c_ctx, mod_w, mod_b, norm_g, w_in, b_in, rpb, w_o_attn, rw_mu, rw_w0, rw_w2, rw_a0, rw_a2, rw_g2, rw_k_k, rw_k_a, rw_r_k, rw_lnx_g, rw_lnx_b, w_o_rwkv, conv_w, conv_b, conv_ln_g, conv_ln_b, w_o_conv, w_out, router_w, router_b, exp_w1, exp_b1, exp_w2, exp_b2):
    P = {
        'mod_w': mod_w, 'mod_b': mod_b, 'norm_g': norm_g, 'w_in': w_in, 'b_in': b_in, 'rpb': rpb,
        'w_o_attn': w_o_attn, 'rw_mu': rw_mu, 'rw_w0': rw_w0, 'rw_w2': rw_w2, 'rw_a0': rw_a0,
        'rw_a2': rw_a2, 'rw_g2': rw_g2, 'rw_k_k': rw_k_k, 'rw_k_a': rw_k_a, 'rw_r_k': rw_r_k,
        'rw_lnx_g': rw_lnx_g, 'rw_lnx_b': rw_lnx_b, 'w_o_rwkv': w_o_rwkv, 'conv_w': conv_w,
        'conv_b': conv_b, 'conv_ln_g': conv_ln_g, 'conv_ln_b': conv_ln_b, 'w_o_conv': w_o_conv,
        'w_out': w_out, 'router_w': router_w, 'router_b': router_b, 'exp_w1': exp_w1,
        'exp_b1': exp_b1, 'exp_w2': exp_w2, 'exp_b2': exp_b2,
    }
    nb, sl, _ = x_prompt.shape
    db, ds, _ = x_sample.shape
    lay = Layout(nb, sl, db, ds)
    x = jnp.concatenate([x_prompt.reshape(-1, D_MODEL), x_sample.reshape(-1, D_MODEL)], axis=0)
    cond = jnp.zeros((8, D_MODEL), F32).at[0].set(c_ctx).at[1:1 + db].set(c)
    ks, vs, ss = [], [], []
    for l in range(DEPTH):
        ck = cache_attn_k[:, l].reshape(db, -1, NA_WIDTH)
        cv = cache_attn_v[:, l].reshape(db, -1, NA_WIDTH)
        x, kf, vf, s_ctx = layer(lay, x, l, cond, P, ck, cv, state_rwkv[:, l])
        ks.append(kf[:lay.ctx_tokens].reshape(nb, sl, NA_HEADS, NA_HEAD_DIM))
        vs.append(vf[:lay.ctx_tokens].reshape(nb, sl, NA_HEADS, NA_HEAD_DIM))
        ss.append(s_ctx)
    y_prompt = x[:lay.ctx_tokens].reshape(nb, sl, D_MODEL)
    y_sample = x[lay.ctx_tokens:].reshape(db, ds, D_MODEL)
    return (y_prompt, y_sample, jnp.stack(ks, axis=1), jnp.stack(vs, axis=1), jnp.stack(ss, axis=1))
```

```python
import functools

import jax
import jax.numpy as jnp
from jax import lax
from jax.experimental import pallas as pl
from jax.experimental.pallas import tpu as pltpu

F32 = jnp.float32
BF16 = jnp.bfloat16

D_MODEL = 1024
BATCH, SEQ = 32, 256
DEPTH = 2
DEC_BATCH, DEC_SEQ = 4, 4096
PAST_LEN = 256
GRID_W = 64
NA_HEADS, NA_HEAD_DIM = 8, 64
NA_WIDTH = NA_HEADS * NA_HEAD_DIM
NA_WIN_ROWS, NA_WIN_COLS = 8, 16
RW_HEADS, RW_HEAD_DIM = 8, 64
RW_WIDTH = RW_HEADS * RW_HEAD_DIM
RW_DECAY_RANK, RW_ICLR_RANK, RW_GATE_RANK = 64, 64, 128
CONV_WIDTH, CONV_K = 512, 31
N_BRANCH, N_MOD = 3, 6
N_EXPERTS, TOP_K, EXPERT_FF = 32, 4, 1024
SWIGLU_LIMIT, SWIGLU_ALPHA = 7.0, 1.702
ROUTE_BLOCK = 512
RMS_EPS, LN_EPS, GN_EPS = 1e-6, 1e-5, 64e-5
NEG_INF = -1e30
A_COLS = 3 * NA_WIDTH
B_COLS = 3 * RW_WIDTH + 2 * RW_DECAY_RANK + 2 * RW_ICLR_RANK + RW_GATE_RANK
C_COLS = 2 * CONV_WIDTH
G_COLS = N_BRANCH * D_MODEL
P_IN = A_COLS + B_COLS + C_COLS + G_COLS

LANES = 128
TM = 256
CHUNK = 64
CONV_HALO = 16
TC = 128
VMEM_LIMIT = 56 * 1024 * 1024


def _cparams(sem):
    return pltpu.CompilerParams(dimension_semantics=sem, vmem_limit_bytes=VMEM_LIMIT)


def _dot(a, b, dims=((1,), (0,))):
    return lax.dot_general(a, b, (dims, ((), ())), preferred_element_type=F32)


def _mm(a, b, dims=((1,), (0,))):
    return _dot(a.astype(BF16), b.astype(BF16), dims)


def _split(a):
    hi = a.astype(BF16)
    lo = (a - hi.astype(F32)).astype(BF16)
    return hi, lo


def _mm3(a, b, dims=((1,), (0,))):
    ah, al = _split(a)
    bh, bl = _split(b)
    return _dot(ah, bh, dims) + (_dot(ah, bl, dims) + _dot(al, bh, dims))


_NT = ((1,), (1,))


def _rms(x, g):
    return x * lax.rsqrt(jnp.mean(x * x, axis=-1, keepdims=True) + RMS_EPS) * g


def _sigmoid(x):
    return 1.0 / (1.0 + jnp.exp(-x))


class Layout:
    def __init__(self, n_ctx, ctx_len, n_dec, dec_len):
        assert ctx_len == TM and dec_len % TM == 0
        self.n_ctx, self.ctx_len, self.n_dec, self.dec_len = n_ctx, ctx_len, n_dec, dec_len
        self.ctx_tokens = n_ctx * ctx_len
        self.dec_tokens = n_dec * dec_len
        self.n = self.ctx_tokens + self.dec_tokens
        self.ctx_tiles = self.ctx_tokens // TM
        self.dec_tiles_per_seq = dec_len // TM
        self.tiles = self.n // TM

    def mod_row(self, i):
        return jnp.where(i < self.ctx_tiles, 0, 1 + (i - self.ctx_tiles) // self.dec_tiles_per_seq)

    def tile_pos(self, i):
        j = (i - self.ctx_tiles) % self.dec_tiles_per_seq
        is_ctx = i < self.ctx_tiles
        return is_ctx | (j == 0), is_ctx | (j == self.dec_tiles_per_seq - 1)


def _mod_kernel(c_ref, w_ref, b_ref, o_ref):
    c = c_ref[...]
    o_ref[...] = _mm3(c * _sigmoid(c), w_ref[...]) + b_ref[...]


def mod_table(cond, w, b):
    tn = 1024
    nm = w.shape[1]
    return pl.pallas_call(
        _mod_kernel,
        out_shape=jax.ShapeDtypeStruct((cond.shape[0], nm), F32),
        grid=(nm // tn,),
        in_specs=[pl.BlockSpec(cond.shape, lambda j: (0, 0)),
                  pl.BlockSpec((D_MODEL, tn), lambda j: (0, j)),
                  pl.BlockSpec((1, tn), lambda j: (0, j))],
        out_specs=pl.BlockSpec((cond.shape[0], tn), lambda j: (0, j)),
        compiler_params=_cparams(("arbitrary",)),
        name="mod_table",
    )(cond, w, b.reshape(1, nm))


def _inproj_kernel(x_ref, mod_ref, g_ref, w_ref, b_ref,
                   q_ref, kf_ref, vf_ref, kb_ref, vb_ref, zb_ref, zc_ref, gt_ref):
    u = (_rms(x_ref[...], g_ref[...]) * (1.0 + mod_ref[0, 1:2, :]) + mod_ref[0, 0:1, :]).astype(BF16)

    def seg(c0, width):
        return _dot(u, w_ref[:, c0:c0 + width]) + b_ref[:, c0:c0 + width]

    q_ref[...] = seg(0, NA_WIDTH).astype(BF16)
    k = seg(NA_WIDTH, NA_WIDTH)
    kf_ref[...] = k
    kb_ref[...] = k.astype(BF16)
    v = seg(2 * NA_WIDTH, NA_WIDTH)
    vf_ref[...] = v
    vb_ref[...] = v.astype(BF16)
    wb = 384
    for j in range(B_COLS // wb):
        zb_ref[:, j * wb:(j + 1) * wb] = seg(A_COLS + j * wb, wb)
    wc = 512
    for j in range(C_COLS // wc):
        zc_ref[:, j * wc:(j + 1) * wc] = seg(A_COLS + B_COLS + j * wc, wc)
    for j in range(G_COLS // wc):
        gt_ref[:, j * wc:(j + 1) * wc] = _sigmoid(seg(A_COLS + B_COLS + C_COLS + j * wc, wc)).astype(BF16)


def in_proj(lay, x, mod, g0, w_bf, b):
    n = lay.n
    row = lambda w: pl.BlockSpec((TM, w), lambda i: (i, 0))
    out_shape = (jax.ShapeDtypeStruct((n, NA_WIDTH), BF16),
                 jax.ShapeDtypeStruct((n, NA_WIDTH), F32), jax.ShapeDtypeStruct((n, NA_WIDTH), F32),
                 jax.ShapeDtypeStruct((n, NA_WIDTH), BF16), jax.ShapeDtypeStruct((n, NA_WIDTH), BF16),
                 jax.ShapeDtypeStruct((n, B_COLS), F32), jax.ShapeDtypeStruct((n, C_COLS), F32),
                 jax.ShapeDtypeStruct((n, G_COLS), BF16))
    return pl.pallas_call(
        _inproj_kernel,
        out_shape=out_shape,
        grid=(lay.tiles,),
        in_specs=[row(D_MODEL),
                  pl.BlockSpec((1, N_MOD, D_MODEL), lambda i: (lay.mod_row(i), 0, 0)),
                  pl.BlockSpec((1, D_MODEL), lambda i: (0, 0)),
                  pl.BlockSpec((D_MODEL, P_IN), lambda i: (0, 0), pipeline_mode=pl.Buffered(1)),
                  pl.BlockSpec((1, P_IN), lambda i: (0, 0))],
        out_specs=(row(NA_WIDTH), row(NA_WIDTH), row(NA_WIDTH), row(NA_WIDTH), row(NA_WIDTH),
                   row(B_COLS), row(C_COLS), row(G_COLS)),
        compiler_params=_cparams(("arbitrary",)),
        name="in_proj",
    )(x, mod, g0, w_bf, b)


def _ctx_attn_kernel(q_ref, k_ref, v_ref, o_ref):
    scale = NA_HEAD_DIM ** -0.5
    for h in range(NA_HEADS):
        sl = slice(h * NA_HEAD_DIM, (h + 1) * NA_HEAD_DIM)
        s = _dot(q_ref[:, sl], k_ref[:, sl], _NT) * scale
        m = jnp.max(s, axis=-1, keepdims=True)
        p = jnp.exp(s - m)
        den = jnp.sum(p, axis=-1, keepdims=True)
        o_ref[:, sl] = (_dot(p.astype(BF16), v_ref[:, sl]) / den).astype(BF16)


def ctx_attention(lay, q, kb, vb):
    blk = pl.BlockSpec((lay.ctx_len, NA_WIDTH), lambda b: (b, 0))
    return pl.pallas_call(
        _ctx_attn_kernel,
        out_shape=jax.ShapeDtypeStruct((lay.ctx_tokens, NA_WIDTH), BF16),
        grid=(lay.n_ctx,),
        in_specs=[blk, blk, blk],
        out_specs=blk,
        compiler_params=_cparams(("arbitrary",)),
        name="ctx_attention",
    )(q, kb, vb)


def na_bias_table(rpb):
    kw, kh = NA_WIN_COLS, NA_WIN_ROWS
    col = jnp.arange(GRID_W, dtype=jnp.int32)
    col_start = jnp.clip(col - kw // 2, 0, GRID_W - kw)
    col_ok = (col[None, :] >= col_start[:, None]) & (col[None, :] < col_start[:, None] + kw)
    dc = jnp.clip(col[None, :] - col[:, None] + kw - 1, 0, 2 * kw - 2)
    bias_c = jnp.where(col_ok, rpb[:, :, dc].astype(F32), NEG_INF)
    tabs = [jnp.concatenate([bias_c[:, d + i] for i in range(kh)], axis=-1) for d in range(kh)]
    return jnp.stack(tabs, axis=0)


def _na_row_start(r, rows):
    return jnp.clip(r - NA_WIN_ROWS // 2, 0, rows - NA_WIN_ROWS)


def _na_kernel(q_ref, k_ref, v_ref, ck_ref, cv_ref, bias_ref, o_ref, *, rows):
    r = pl.program_id(1)
    rs = _na_row_start(r, rows)
    n_loc = NA_WIN_ROWS * GRID_W
    start = pl.multiple_of(rs * GRID_W, GRID_W)
    k_loc = k_ref[pl.ds(start, n_loc), :]
    v_loc = v_ref[pl.ds(start, n_loc), :]
    scale = NA_HEAD_DIM ** -0.5
    for h in range(NA_HEADS):
        sl = slice(h * NA_HEAD_DIM, (h + 1) * NA_HEAD_DIM)
        qh = q_ref[:, sl] * scale
        s_loc = _dot(qh, k_loc[:, sl], _NT) + bias_ref[0, h]
        s_ctx = _dot(qh, ck_ref[0, :, sl], _NT)
        m = jnp.maximum(jnp.max(s_loc, axis=-1, keepdims=True), jnp.max(s_ctx, axis=-1, keepdims=True))
        p_loc = jnp.exp(s_loc - m)
        p_ctx = jnp.exp(s_ctx - m)
        den = jnp.sum(p_loc, axis=-1, keepdims=True) + jnp.sum(p_ctx, axis=-1, keepdims=True)
        o = _dot(p_loc.astype(BF16), v_loc[:, sl]) + _dot(p_ctx.astype(BF16), cv_ref[0, :, sl])
        o_ref[:, sl] = (o / den).astype(BF16)


def na_attention(lay, q, kb, vb, ck, cv, bias_tab):
    rows = lay.dec_len // GRID_W
    assert rows >= NA_WIN_ROWS
    qoff = lay.ctx_tokens // GRID_W
    koff = lay.ctx_tokens // lay.dec_len
    assert lay.ctx_tokens % lay.dec_len == 0
    kv = pl.BlockSpec((lay.dec_len, NA_WIDTH), lambda b, r: (koff + b, 0))
    cs = pl.BlockSpec((1, ck.shape[1], NA_WIDTH), lambda b, r: (b, 0, 0))
    return pl.pallas_call(
        functools.partial(_na_kernel, rows=rows),
        out_shape=jax.ShapeDtypeStruct((lay.dec_tokens, NA_WIDTH), BF16),
        grid=(lay.n_dec, rows),
        in_specs=[pl.BlockSpec((GRID_W, NA_WIDTH), lambda b, r: (qoff + b * rows + r, 0)),
                  kv, kv, cs, cs,
                  pl.BlockSpec((1, NA_HEADS, GRID_W, NA_WIN_ROWS * GRID_W),
                               lambda b, r: (_na_row_start(r, rows) - r + NA_WIN_ROWS - 1, 0, 0, 0))],
        out_specs=pl.BlockSpec((GRID_W, NA_WIDTH), lambda b, r: (b * rows + r, 0)),
        compiler_params=_cparams(("arbitrary", "arbitrary")),
        name="na_attention",
    )(q, kb, vb, ck, cv, bias_tab)


def _softplus(x):
    return jnp.maximum(x, 0.0) + jnp.log(1.0 + jnp.exp(-jnp.abs(x)))


def _rwkv_prepare(d, z, prev_row, next_row, prm):
    (mu_ref, w0_ref, w2_ref, a0_ref, a2_ref, kk_ref, ka_ref, rk_ref) = prm
    c = CHUNK
    cw = RW_WIDTH
    row = lax.broadcasted_iota(jnp.int32, (c, 1), 0)
    prev = jnp.where(row == 0, prev_row, pltpu.roll(z, 1, axis=0))
    nxt = jnp.where(row == c - 1, next_row, pltpu.roll(z, c - 1, axis=0))
    zs = z + mu_ref[0:1, :] * (prev - z) + mu_ref[1:2, :] * (nxt - z)
    r, k, v = zs[:, :cw], zs[:, cw:2 * cw], zs[:, 2 * cw:3 * cw]
    o = 3 * cw
    zw = zs[:, o + d * RW_DECAY_RANK:o + (d + 1) * RW_DECAY_RANK]
    o += 2 * RW_DECAY_RANK
    za = zs[:, o + d * RW_ICLR_RANK:o + (d + 1) * RW_ICLR_RANK]

    w_log = -_softplus(-(w0_ref[d:d + 1, :] + _mm3(jnp.tanh(zw), w2_ref[d]))) - 0.5
    logw = -jnp.exp(w_log)
    a = _sigmoid(a0_ref[d:d + 1, :] + _mm3(za, a2_ref[d]))
    kd = k * (1.0 + (a - 1.0) * ka_ref[...])
    kk = k * kk_ref[...]

    ti = lax.broadcasted_iota(jnp.int32, (c, c), 0)
    tj = lax.broadcasted_iota(jnp.int32, (c, c), 1)
    diff = (ti - tj) if d == 0 else (tj - ti)
    lh, ll = _split(logw)
    tri = (diff >= 0).astype(BF16)
    cl = _dot(tri, lh) + _dot(tri, ll)
    p_in = jnp.exp(cl)
    p_ex = jnp.exp(cl - logw)
    p_inv = jnp.exp(-cl)
    p_end = p_in[c - 1:c, :] if d == 0 else p_in[0:1, :]
    rkd = r * kd * rk_ref[...]

    units = []
    for h in range(RW_HEADS):
        sl = slice(h * RW_HEAD_DIM, (h + 1) * RW_HEAD_DIM)
        kkh = kk[:, sl]
        kkh = kkh * lax.rsqrt(jnp.maximum(jnp.sum(kkh * kkh, axis=-1, keepdims=True), 1e-24))
        ar = jnp.concatenate([-kkh * p_ex[:, sl], r[:, sl] * p_in[:, sl]], axis=0)
        bk = jnp.concatenate([a[:, sl] * kkh * p_inv[:, sl], kd[:, sl] * p_inv[:, sl]], axis=0)
        bonus = jnp.sum(rkd[:, sl], axis=-1, keepdims=True) * v[:, sl]
        units.append(dict(d=d, h=h, ar=ar, bk=bk, v=v[:, sl], bkp=bk * p_end[:, sl], pe=p_end[:, sl],
                          bonus=bonus, diff=diff))
    return units, zs


def _rwkv_solve(units, s_ref):
    c = CHUNK
    for u in units:
        g = _mm(u['ar'], u['bk'], _NT)
        strict = u['diff'] > 0
        incl = u['diff'] >= 0
        u['a_ab'] = jnp.where(strict, g[:c, :c], 0.0)
        u['a_rb'] = jnp.where(incl, g[c:, :c], 0.0)
        u['akrk'] = jnp.concatenate([jnp.where(strict, g[:c, c:], 0.0), jnp.where(incl, g[c:, c:], 0.0)], axis=0)
        u['s0'] = s_ref[u['d'], u['h']]
    for u in units:
        u['xs'] = _mm(u['ar'], u['s0'], _NT) + _mm(u['akrk'], u['v'])
    for u in units:
        u['pw'] = _mm(u['a_ab'], u['a_ab'])
        u['tinv'] = (u['diff'] == 0).astype(F32) + u['a_ab']
    n_sq = CHUNK.bit_length() - 2
    for i in range(n_sq):
        for u in units:
            if i < n_sq - 1:
                both = _mm(u['pw'], jnp.concatenate([u['pw'], u['tinv']], axis=1))
                u['pw'], u['tinv'] = both[:, :c], u['tinv'] + both[:, c:]
            else:
                u['tinv'] = u['tinv'] + _mm(u['pw'], u['tinv'])
    for u in units:
        u['u'] = _mm(u['tinv'], u['xs'][:c])
    ys = []
    for u in units:
        ys.append(u['xs'][c:] + _mm(u['a_rb'], u['u']) + u['bonus'])
        uv = jnp.concatenate([u['u'], u['v']], axis=0)
        s_ref[u['d'], u['h']] = u['s0'] * u['pe'] + _mm(uv.T, u['bkp'])
    return ys


def _rwkv_kernel(*refs, nc, has_s0):
    zf_ref, zfp_ref, zfn_ref, zr_ref, zrp_ref, zrn_ref = refs[:6]
    n_in = 6 + (1 if has_s0 else 0)
    s0_ref = refs[6] if has_s0 else None
    prm = refs[n_in:n_in + 8]
    g2_ref = refs[n_in + 8]
    yf_ref, yb_ref, g_ref, sfin_ref, s_ref = refs[n_in + 9:]
    i = pl.program_id(1)

    @pl.when(i == 0)
    def _():
        if has_s0:
            s_ref[...] = s0_ref[0]
        else:
            s_ref[...] = jnp.zeros(s_ref.shape, F32)

    zero = jnp.zeros((1, B_COLS), F32)
    pf = jnp.where(i == 0, zero, zfp_ref[7:8, :])
    nf = jnp.where(i == nc - 1, zero, zfn_ref[0:1, :])
    uf, zs = _rwkv_prepare(0, zf_ref[...], pf, nf, prm)
    g_ref[...] = _mm(_sigmoid(zs[:, B_COLS - RW_GATE_RANK:]), g2_ref[...])
    pr = jnp.where(i == nc - 1, zero, zrp_ref[7:8, :])
    nr = jnp.where(i == 0, zero, zrn_ref[0:1, :])
    ub, _ = _rwkv_prepare(1, zr_ref[...], pr, nr, prm)
    ys = _rwkv_solve(uf + ub, s_ref)
    yf_ref[...] = jnp.concatenate(ys[:RW_HEADS], axis=-1)
    yb_ref[...] = jnp.concatenate(ys[RW_HEADS:], axis=-1)

    @pl.when(i == nc - 1)
    def _():
        sfin_ref[0] = s_ref[...]


def rwkv_scan(zb, tok0, nseq, seqlen, s0, prm, g2):
    n = zb.shape[0]
    nc = seqlen // CHUNK
    sub = CHUNK // 8
    base = tok0 // CHUNK
    nt = nseq * seqlen
    has_s0 = s0 is not None

    def cf(s, i):
        return base + s * nc + i

    def cr(s, i):
        return base + s * nc + (nc - 1 - i)

    def specs(cidx):
        return [pl.BlockSpec((CHUNK, B_COLS), lambda s, i: (cidx(s, i), 0)),
                pl.BlockSpec((8, B_COLS), lambda s, i: (jnp.maximum(cidx(s, i) * sub - 1, 0), 0)),
                pl.BlockSpec((8, B_COLS), lambda s, i: (jnp.minimum((cidx(s, i) + 1) * sub, n // 8 - 1), 0))]

    full = lambda a: pl.BlockSpec(a.shape, lambda s, i: (0,) * a.ndim)
    sshape = (2, RW_HEADS, RW_HEAD_DIM, RW_HEAD_DIM)
    in_specs = specs(cf) + specs(cr)
    args = [zb] * 6
    if has_s0:
        in_specs.append(pl.BlockSpec((1,) + sshape, lambda s, i: (s, 0, 0, 0, 0)))
        args.append(s0)
    in_specs += [full(p) for p in prm] + [full(g2)]
    args += list(prm) + [g2]
    out = lambda cidx: pl.BlockSpec((CHUNK, RW_WIDTH), lambda s, i: (cidx(s, i) - base, 0))
    tok = jax.ShapeDtypeStruct((nt, RW_WIDTH), F32)
    return pl.pallas_call(
        functools.partial(_rwkv_kernel, nc=nc, has_s0=has_s0),
        out_shape=(tok, tok, tok, jax.ShapeDtypeStruct((nseq,) + sshape, F32)),
        grid=(nseq, nc),
        in_specs=in_specs,
        out_specs=(out(cf), out(cr), out(cf), pl.BlockSpec((1,) + sshape, lambda s, i: (s, 0, 0, 0, 0))),
        scratch_shapes=[pltpu.VMEM(sshape, F32)],
        compiler_params=_cparams(("arbitrary", "arbitrary")),
        name="rwkv_scan",
    )(*args)


def _conv_kernel(z_ref, zp_ref, zn_ref, w_ref, b_ref, g_ref, be_ref, o_ref, h_ref, *, lay):
    i = pl.program_id(0)
    first, last = lay.tile_pos(i)

    def glu(z):
        return z[:, :CONV_WIDTH] * _sigmoid(z[:, CONV_WIDTH:])

    hz = jnp.zeros((CONV_HALO, CONV_WIDTH), F32)
    h_ref[0:CONV_HALO, :] = jnp.where(first, hz, glu(zp_ref[...]))
    h_ref[CONV_HALO:CONV_HALO + TM, :] = glu(z_ref[...])
    h_ref[CONV_HALO + TM:, :] = jnp.where(last, hz, glu(zn_ref[...]))
    off = CONV_HALO - CONV_K // 2
    acc = jnp.zeros((TM, CONV_WIDTH), F32) + b_ref[...]
    for j in range(CONV_K):
        acc = acc + w_ref[j:j + 1, :] * h_ref[off + j:off + j + TM, :]
    xc = acc - jnp.mean(acc, axis=-1, keepdims=True)
    hn = xc * lax.rsqrt(jnp.mean(xc * xc, axis=-1, keepdims=True) + LN_EPS) * g_ref[...] + be_ref[...]
    o_ref[...] = (hn * _sigmoid(hn)).astype(BF16)


def conv_module(lay, zc, w, b, g, be):
    n = lay.n
    hb = TM // CONV_HALO
    vec = pl.BlockSpec((1, CONV_WIDTH), lambda i: (0, 0))
    return pl.pallas_call(
        functools.partial(_conv_kernel, lay=lay),
        out_shape=jax.ShapeDtypeStruct((n, CONV_WIDTH), BF16),
        grid=(lay.tiles,),
        in_specs=[pl.BlockSpec((TM, C_COLS), lambda i: (i, 0)),
                  pl.BlockSpec((CONV_HALO, C_COLS), lambda i: (jnp.maximum(i * hb - 1, 0), 0)),
                  pl.BlockSpec((CONV_HALO, C_COLS), lambda i: (jnp.minimum((i + 1) * hb, n // CONV_HALO - 1), 0)),
                  pl.BlockSpec((CONV_K, CONV_WIDTH), lambda i: (0, 0)), vec, vec, vec],
        out_specs=pl.BlockSpec((TM, CONV_WIDTH), lambda i: (i, 0)),
        scratch_shapes=[pltpu.VMEM((TM + 2 * CONV_HALO, CONV_WIDTH), F32)],
        compiler_params=_cparams(("arbitrary",)),
        name="conv_module",
    )(zc, zc, zc, w, b, g, be)


def _merge_kernel(x_ref, oa_ref, yf_ref, yb_ref, g_ref, oc_ref, gt_ref, mod_ref, ng_ref, lg_ref, lb_ref,
                  wa_ref, wr_ref, wc_ref, wo_ref, rw_ref, rb_ref,
                  x1_ref, hm_ref, ti_ref, tg_ref):
    y = yf_ref[...] + yb_ref[...]
    outs = []
    for h in range(RW_HEADS):
        yh = y[:, h * RW_HEAD_DIM:(h + 1) * RW_HEAD_DIM]
        yc = yh - jnp.mean(yh, axis=-1, keepdims=True)
        outs.append(yc * lax.rsqrt(jnp.mean(yc * yc, axis=-1, keepdims=True) + GN_EPS))
    o_rw = (jnp.concatenate(outs, axis=-1) * lg_ref[...] + lb_ref[...]) * g_ref[...]
    merged = (gt_ref[:, :D_MODEL].astype(F32) * _dot(oa_ref[...], wa_ref[...])
              + gt_ref[:, D_MODEL:2 * D_MODEL].astype(F32) * _mm(o_rw, wr_ref[...])
              + gt_ref[:, 2 * D_MODEL:].astype(F32) * _dot(oc_ref[...], wc_ref[...]))
    mix = _mm(merged, wo_ref[...])
    x1 = x_ref[...] + mod_ref[0, 2:3, :] * _rms(mix, ng_ref[1:2, :])
    x1_ref[...] = x1
    hm = _rms(x1, ng_ref[2:3, :]) * (1.0 + mod_ref[0, 4:5, :]) + mod_ref[0, 3:4, :]
    hm_ref[...] = hm
    logits = _mm3(hm, rw_ref[...]) + rb_ref[...]
    lane = lax.broadcasted_iota(jnp.int32, logits.shape, 1)
    idx_out = jnp.zeros(logits.shape, jnp.int32)
    val_out = jnp.full(logits.shape, NEG_INF, F32)
    for j in range(TOP_K):
        m = jnp.max(logits, axis=-1, keepdims=True)
        idx = jnp.min(jnp.where(logits == m, lane, LANES), axis=-1, keepdims=True)
        idx_out = jnp.where(lane == j, idx, idx_out)
        val_out = jnp.where(lane == j, m, val_out)
        logits = jnp.where(lane == idx, -jnp.inf, logits)
    e = jnp.exp(val_out - jnp.max(val_out, axis=-1, keepdims=True))
    ti_ref[...] = idx_out
    tg_ref[...] = e / jnp.sum(e, axis=-1, keepdims=True)


def merge_router(lay, x, o_att, yf, yb, g, o_cv, gates, mod, ng, lg, lb, wa, wr, wc, wo, rw, rb):
    n = lay.n
    row = lambda w: pl.BlockSpec((TM, w), lambda i: (i, 0))
    full = lambda a: pl.BlockSpec(a.shape, lambda i: (0,) * a.ndim)
    return pl.pallas_call(
        _merge_kernel,
        out_shape=(jax.ShapeDtypeStruct((n, D_MODEL), F32), jax.ShapeDtypeStruct((n, D_MODEL), F32),
                   jax.ShapeDtypeStruct((n, LANES), jnp.int32), jax.ShapeDtypeStruct((n, LANES), F32)),
        grid=(lay.tiles,),
        in_specs=[row(D_MODEL), row(NA_WIDTH), row(RW_WIDTH), row(RW_WIDTH), row(RW_WIDTH), row(CONV_WIDTH),
                  row(G_COLS), pl.BlockSpec((1, N_MOD, D_MODEL), lambda i: (lay.mod_row(i), 0, 0)),
                  full(ng), full(lg), full(lb), full(wa), full(wr), full(wc), full(wo), full(rw), full(rb)],
        out_specs=(row(D_MODEL), row(D_MODEL), row(LANES), row(LANES)),
        compiler_params=_cparams(("arbitrary",)),
        name="merge_router",
    )(x, o_att, yf, yb, g, o_cv, gates, mod, ng, lg, lb, wa, wr, wc, wo, rw, rb)


def route_slots(top_i, n):
    nk = n * TOP_K
    flat_e = top_i.reshape(-1)
    onehot = (flat_e[:, None] == jnp.arange(N_EXPERTS, dtype=flat_e.dtype)[None, :]).astype(jnp.int32)
    csum = jnp.cumsum(onehot, axis=0)
    rank = jnp.take_along_axis(csum, flat_e[:, None], axis=1)[:, 0] - 1
    counts = csum[-1]
    padded = ((counts + ROUTE_BLOCK - 1) // ROUTE_BLOCK) * ROUTE_BLOCK
    pad_end = jnp.cumsum(padded)
    dest = (pad_end - padded)[flat_e] + rank
    n_blocks = -(-nk // ROUTE_BLOCK) + N_EXPERTS
    block_e = jnp.minimum(jnp.searchsorted(pad_end, jnp.arange(n_blocks, dtype=jnp.int32) * ROUTE_BLOCK,
                                           side='right'), N_EXPERTS - 1).astype(jnp.int32)
    n_used = (pad_end[-1] // ROUTE_BLOCK).astype(jnp.int32).reshape(1)
    return dest.astype(jnp.int32), block_e, n_used, n_blocks


def _row_copy_wait(src_rows, dst_rows, sem):
    pltpu.make_async_copy(src_rows, dst_rows, sem).wait()


def _dispatch_kernel(dest_ref, hm_ref, xb_in_ref, xb_ref, sem):
    del xb_in_ref

    def issue(t, carry):
        for j in range(TOP_K):
            d = dest_ref[0, 0, t * TOP_K + j]
            pltpu.make_async_copy(hm_ref.at[pl.ds(t, 1), :], xb_ref.at[pl.ds(d, 1), :], sem).start()
        return carry

    lax.fori_loop(0, TM, issue, 0)
    for j in range(TOP_K):
        _row_copy_wait(hm_ref, xb_ref.at[pl.ds(0, TM), :], sem)


def moe_dispatch(lay, hm, dest, n_rows):
    xb0 = jnp.zeros((n_rows, D_MODEL), F32)
    return pl.pallas_call(
        _dispatch_kernel,
        out_shape=jax.ShapeDtypeStruct((n_rows, D_MODEL), F32),
        grid=(lay.tiles,),
        in_specs=[pl.BlockSpec((1, 1, TM * TOP_K), lambda i: (i, 0, 0), memory_space=pltpu.SMEM),
                  pl.BlockSpec((TM, D_MODEL), lambda i: (i, 0)),
                  pl.BlockSpec(memory_space=pl.ANY)],
        out_specs=pl.BlockSpec(memory_space=pl.ANY),
        scratch_shapes=[pltpu.SemaphoreType.DMA],
        input_output_aliases={2: 0},
        compiler_params=_cparams(("arbitrary",)),
        name="moe_dispatch",
    )(dest.reshape(lay.tiles, 1, TM * TOP_K), hm, xb0)


GL_GROUP = 2 * LANES


def _regroup_perm():
    r = jnp.arange(GL_GROUP, dtype=jnp.int32)[:, None]
    c = jnp.arange(GL_GROUP, dtype=jnp.int32)[None, :]
    src = jnp.where(c < LANES, 2 * c, 2 * (c - LANES) + 1)
    return (r == src).astype(BF16)


def _w1_regroup_kernel(w_ref, p_ref, o_ref):
    for j in range(2 * EXPERT_FF // GL_GROUP):
        cs = slice(j * GL_GROUP, (j + 1) * GL_GROUP)
        o_ref[0, :, cs] = _dot(w_ref[0, :, cs].astype(BF16), p_ref[...]).astype(BF16)


def w1_regroup(w1):
    blk = pl.BlockSpec((1,) + w1.shape[1:], lambda e: (e, 0, 0))
    return pl.pallas_call(
        _w1_regroup_kernel,
        out_shape=jax.ShapeDtypeStruct(w1.shape, BF16),
        grid=(w1.shape[0],),
        in_specs=[blk, pl.BlockSpec((GL_GROUP, GL_GROUP), lambda e: (0, 0))],
        out_specs=blk,
        compiler_params=_cparams(("arbitrary",)),
        name="w1_regroup",
    )(w1, _regroup_perm())


def _expert_kernel(be_ref, nu_ref, x_ref, w1_ref, b1_ref, w2_ref, b2_ref, y_ref):
    b = pl.program_id(0)

    @pl.when(b < nu_ref[0])
    def _():
        x = x_ref[...].astype(BF16)
        acc = jnp.zeros((ROUTE_BLOCK, D_MODEL), F32) + b2_ref[0]
        for j in range(EXPERT_FF // GL_GROUP):
            acts = []
            for g in range(2):
                c0 = (2 * j + g) * GL_GROUP
                h = _dot(x, w1_ref[0, :, c0:c0 + GL_GROUP]) + b1_ref[0, :, c0:c0 + GL_GROUP]
                hg = jnp.minimum(h[:, :LANES], SWIGLU_LIMIT)
                hl = jnp.clip(h[:, LANES:], -SWIGLU_LIMIT, SWIGLU_LIMIT)
                acts.append(hg * _sigmoid(SWIGLU_ALPHA * hg) * (hl + 1.0))
            act = jnp.concatenate(acts, axis=-1).astype(BF16)
            acc = acc + _dot(act, w2_ref[0, j * GL_GROUP:(j + 1) * GL_GROUP, :])
        y_ref[...] = acc

    @pl.when(b >= nu_ref[0])
    def _():
        y_ref[...] = jnp.zeros(y_ref.shape, F32)


def moe_experts(xb, block_e, n_used, n_blocks, w1, b1, w2, b2):
    def xmap(b, be, nu):
        return (jnp.minimum(b, jnp.maximum(nu[0] - 1, 0)), 0)

    def wmap(b, be, nu):
        return (be[jnp.minimum(b, jnp.maximum(nu[0] - 1, 0))], 0, 0)

    wspec = lambda a: pl.BlockSpec((1,) + a.shape[1:], wmap)
    return pl.pallas_call(
        _expert_kernel,
        out_shape=jax.ShapeDtypeStruct(xb.shape, F32),
        grid_spec=pltpu.PrefetchScalarGridSpec(
            num_scalar_prefetch=2,
            grid=(n_blocks,),
            in_specs=[pl.BlockSpec((ROUTE_BLOCK, D_MODEL), xmap),
                      wspec(w1), wspec(b1), wspec(w2), wspec(b2)],
            out_specs=pl.BlockSpec((ROUTE_BLOCK, D_MODEL), lambda b, be, nu: (b, 0))),
        compiler_params=_cparams(("arbitrary",)),
        name="moe_experts",
    )(block_e, n_used, xb, w1, b1, w2, b2)


def _combine_kernel(dest_ref, yb_ref, tg_ref, x1_ref, mod_ref, ng_ref, o_ref, buf_ref, sem):
    def issue(t, carry):
        for j in range(TOP_K):
            d = dest_ref[0, 0, t * TOP_K + j]
            pltpu.make_async_copy(yb_ref.at[pl.ds(d, 1), :], buf_ref.at[j, pl.ds(t, 1), :], sem).start()
        return carry

    lax.fori_loop(0, TC, issue, 0)
    for j in range(TOP_K):
        _row_copy_wait(yb_ref.at[pl.ds(0, TC), :], buf_ref.at[j], sem)
    y = jnp.zeros((TC, D_MODEL), F32)
    for j in range(TOP_K):
        y = y + tg_ref[:, j:j + 1] * buf_ref[j]
    o_ref[...] = x1_ref[...] + mod_ref[0, 5:6, :] * _rms(y, ng_ref[3:4, :])


def moe_combine(lay, yb, dest, tg, x1, mod, ng):
    n = lay.n
    per = TM // TC
    row = lambda w: pl.BlockSpec((TC, w), lambda i: (i, 0))
    return pl.pallas_call(
        _combine_kernel,
        out_shape=jax.ShapeDtypeStruct((n, D_MODEL), F32),
        grid=(n // TC,),
        in_specs=[pl.BlockSpec((1, 1, TC * TOP_K), lambda i: (i, 0, 0), memory_space=pltpu.SMEM),
                  pl.BlockSpec(memory_space=pl.ANY),
                  row(LANES), row(D_MODEL),
                  pl.BlockSpec((1, N_MOD, D_MODEL), lambda i: (lay.mod_row(i // per), 0, 0)),
                  pl.BlockSpec(ng.shape, lambda i: (0, 0))],
        out_specs=row(D_MODEL),
        scratch_shapes=[pltpu.VMEM((TOP_K, TC, D_MODEL), F32), pltpu.SemaphoreType.DMA],
        compiler_params=_cparams(("arbitrary",)),
        name="moe_combine",
    )(dest.reshape(n // TC, 1, TC * TOP_K), yb, tg, x1, mod, ng)


def layer(lay, x, l, cond, P, ctx_k, ctx_v, s0_dec):
    mod = mod_table(cond, P['mod_w'][l], P['mod_b'][l]).reshape(cond.shape[0], N_MOD, D_MODEL)
    ng = P['norm_g'][l]
    q, kf, vf, kb, vb, zb, zc, gates = in_proj(lay, x, mod, ng[0:1], P['w_in'][l].astype(BF16),
                                               P['b_in'][l].reshape(1, P_IN))
    oa_ctx = ctx_attention(lay, q, kb, vb)
    oa_dec = na_attention(lay, q, kb, vb, ctx_k.astype(BF16), ctx_v.astype(BF16), na_bias_table(P['rpb'][l]))
    o_att = jnp.concatenate([oa_ctx, oa_dec], axis=0)
    vec = lambda a: a.reshape(1, -1)
    prm = (P['rw_mu'][l], P['rw_w0'][l], P['rw_w2'][l], P['rw_a0'][l], P['rw_a2'][l],
           vec(P['rw_k_k'][l]), vec(P['rw_k_a'][l]), vec(P['rw_r_k'][l]))
    yf_c, yb_c, g_c, s_ctx = rwkv_scan(zb, 0, lay.n_ctx, lay.ctx_len, None, prm, P['rw_g2'][l])
    yf_d, yb_d, g_d, _ = rwkv_scan(zb, lay.ctx_tokens, lay.n_dec, lay.dec_len, s0_dec, prm, P['rw_g2'][l])
    yf = jnp.concatenate([yf_c, yf_d], axis=0)
    yb = jnp.concatenate([yb_c, yb_d], axis=0)
    g = jnp.concatenate([g_c, g_d], axis=0)
    o_cv = conv_module(lay, zc, P['conv_w'][l], vec(P['conv_b'][l]), vec(P['conv_ln_g'][l]),
                       vec(P['conv_ln_b'][l]))
    rw = jnp.zeros((D_MODEL, LANES), F32).at[:, :N_EXPERTS].set(P['router_w'][l])
    rb = jnp.full((1, LANES), NEG_INF, F32).at[0, :N_EXPERTS].set(P['router_b'][l])
    x1, hm, top_i, top_g = merge_router(
        lay, x, o_att, yf, yb, g, o_cv, gates, mod, ng, vec(P['rw_lnx_g'][l]), vec(P['rw_lnx_b'][l]),
        P['w_o_attn'][l].astype(BF16), P['w_o_rwkv'][l].astype(BF16), P['w_o_conv'][l].astype(BF16),
        P['w_out'][l].astype(BF16), rw, rb)
    dest, block_e, n_used, n_blocks = route_slots(top_i[:, :TOP_K], lay.n)
    xb = moe_dispatch(lay, hm, dest, n_blocks * ROUTE_BLOCK)
    b1 = P['exp_b1'][l].reshape(N_EXPERTS, -1, LANES, 2).transpose(0, 1, 3, 2).reshape(N_EXPERTS, 1, -1)
    ybk = moe_experts(xb, block_e, n_used, n_blocks, w1_regroup(P['exp_w1'][l]), b1,
                      P['exp_w2'][l].astype(BF16), P['exp_b2'][l][:, None, :])
    x2 = moe_combine(lay, ybk, dest, top_g, x1, mod, ng)
    return x2, kf, vf, s_ctx


def kernel(x_prompt, x_sample, cache_attn_k, cache_attn_v, state_rwkv, c, c_ctx, mod_w, mod_b, norm_g, w_in, b_in, rpb, w_o_attn, rw_mu, rw_w0, rw_w2, rw_a0, rw_a2, rw_g2, rw_k_k, rw_k_a, rw_r_k, rw_lnx_g, rw_lnx_b, w_o_rwkv, conv_w, conv_b, conv_ln_g, conv_ln_b, w_o_conv, w_out, router_w, router_b, exp_w1, exp_b1, exp_w2, exp_b2):
    P = {
        'mod_w': mod_w, 'mod_b': mod_b, 'norm_g': norm_g, 'w_in': w_in, 'b_in': b_in, 'rpb': rpb,
        'w_o_attn': w_o_attn, 'rw_mu': rw_mu, 'rw_w0': rw_w0, 'rw_w2': rw_w2, 'rw_a0': rw_a0,
        'rw_a2': rw_a2, 'rw_g2': rw_g2, 'rw_k_k': rw_k_k, 'rw_k_a': rw_k_a, 'rw_r_k': rw_r_k,
        'rw_lnx_g': rw_lnx_g, 'rw_lnx_b': rw_lnx_b, 'w_o_rwkv': w_o_rwkv, 'conv_w': conv_w,
        'conv_b': conv_b, 'conv_ln_g': conv_ln_g, 'conv_ln_b': conv_ln_b, 'w_o_conv': w_o_conv,
        'w_out': w_out, 'router_w': router_w, 'router_b': router_b, 'exp_w1': exp_w1,
        'exp_b1': exp_b1, 'exp_w2': exp_w2, 'exp_b2': exp_b2,
    }
    nb, sl, _ = x_prompt.shape
    db, ds, _ = x_sample.shape
    lay = Layout(nb, sl, db, ds)
    x = jnp.concatenate([x_prompt.reshape(-1, D_MODEL), x_sample.reshape(-1, D_MODEL)], axis=0)
    cond = jnp.zeros((8, D_MODEL), F32).at[0].set(c_ctx).at[1:1 + db].set(c)
    ks, vs, ss = [], [], []
    for l in range(DEPTH):
        ck = cache_attn_k[:, l].reshape(db, -1, NA_WIDTH)
        cv = cache_attn_v[:, l].reshape(db, -1, NA_WIDTH)
        x, kf, vf, s_ctx = layer(lay, x, l, cond, P, ck, cv, state_rwkv[:, l])
        ks.append(kf[:lay.ctx_tokens].reshape(nb, sl, NA_HEADS, NA_HEAD_DIM))
        vs.append(vf[:lay.ctx_tokens].reshape(nb, sl, NA_HEADS, NA_HEAD_DIM))
        ss.append(s_ctx)
    y_prompt = x[:lay.ctx_tokens].reshape(nb, sl, D_MODEL)
    y_sample = x[lay.ctx_tokens:].reshape(db, ds, D_MODEL)
    return (y_prompt, y_sample, jnp.stack(ks, axis=1), jnp.stack(vs, axis=1), jnp.stack(ss, axis=1))
```

```python
import functools

import jax
import jax.numpy as jnp
from jax import lax
from jax.experimental import pallas as pl
from jax.experimental.pallas import tpu as pltpu

F32 = jnp.float32
BF16 = jnp.bfloat16

D_MODEL = 1024
BATCH, SEQ = 32, 256
DEPTH = 2
DEC_BATCH, DEC_SEQ = 4, 4096
PAST_LEN = 256
GRID_W = 64
NA_HEADS, NA_HEAD_DIM = 8, 64
NA_WIDTH = NA_HEADS * NA_HEAD_DIM
NA_WIN_ROWS, NA_WIN_COLS = 8, 16
RW_HEADS, RW_HEAD_DIM = 8, 64
RW_WIDTH = RW_HEADS * RW_HEAD_DIM
RW_DECAY_RANK, RW_ICLR_RANK, RW_GATE_RANK = 64, 64, 128
CONV_WIDTH, CONV_K = 512, 31
N_BRANCH, N_MOD = 3, 6
N_EXPERTS, TOP_K, EXPERT_FF = 32, 4, 1024
SWIGLU_LIMIT, SWIGLU_ALPHA = 7.0, 1.702
ROUTE_BLOCK = 512
RMS_EPS, LN_EPS, GN_EPS = 1e-6, 1e-5, 64e-5
NEG_INF = -1e30
A_COLS = 3 * NA_WIDTH
B_COLS = 3 * RW_WIDTH + 2 * RW_DECAY_RANK + 2 * RW_ICLR_RANK + RW_GATE_RANK
C_COLS = 2 * CONV_WIDTH
G_COLS = N_BRANCH * D_MODEL
P_IN = A_COLS + B_COLS + C_COLS + G_COLS

LANES = 128
TM = 256
CHUNK = 64
CONV_HALO = 16
TC = 128
VMEM_LIMIT = 56 * 1024 * 1024


def _cparams(sem):
    return pltpu.CompilerParams(dimension_semantics=sem, vmem_limit_bytes=VMEM_LIMIT)


def _dot(a, b, dims=((1,), (0,))):
    return lax.dot_general(a, b, (dims, ((), ())), preferred_element_type=F32)


def _mm(a, b, dims=((1,), (0,))):
    return _dot(a.astype(BF16), b.astype(BF16), dims)


def _split(a):
    hi = a.astype(BF16)
    lo = (a - hi.astype(F32)).astype(BF16)
    return hi, lo


def _mm3(a, b, dims=((1,), (0,))):
    ah, al = _split(a)
    bh, bl = _split(b)
    return _dot(ah, bh, dims) + (_dot(ah, bl, dims) + _dot(al, bh, dims))


_NT = ((1,), (1,))


def _rms(x, g):
    return x * lax.rsqrt(jnp.mean(x * x, axis=-1, keepdims=True) + RMS_EPS) * g


def _sigmoid(x):
    return 1.0 / (1.0 + jnp.exp(-x))


class Layout:
    def __init__(self, n_ctx, ctx_len, n_dec, dec_len):
        assert ctx_len == TM and dec_len % TM == 0
        self.n_ctx, self.ctx_len, self.n_dec, self.dec_len = n_ctx, ctx_len, n_dec, dec_len
        self.ctx_tokens = n_ctx * ctx_len
        self.dec_tokens = n_dec * dec_len
        self.n = self.ctx_tokens + self.dec_tokens
        self.ctx_tiles = self.ctx_tokens // TM
        self.dec_tiles_per_seq = dec_len // TM
        self.tiles = self.n // TM

    def mod_row(self, i):
        return jnp.where(i < self.ctx_tiles, 0, 1 + (i - self.ctx_tiles) // self.dec_tiles_per_seq)

    def tile_pos(self, i):
        j = (i - self.ctx_tiles) % self.dec_tiles_per_seq
        is_ctx = i < self.ctx_tiles
        return is_ctx | (j == 0), is_ctx | (j == self.dec_tiles_per_seq - 1)


def _mod_kernel(c_ref, w_ref, b_ref, o_ref):
    c = c_ref[...]
    o_ref[...] = _mm3(c * _sigmoid(c), w_ref[...]) + b_ref[...]


def mod_table(cond, w, b):
    tn = 1024
    nm = w.shape[1]
    return pl.pallas_call(
        _mod_kernel,
        out_shape=jax.ShapeDtypeStruct((cond.shape[0], nm), F32),
        grid=(nm // tn,),
        in_specs=[pl.BlockSpec(cond.shape, lambda j: (0, 0)),
                  pl.BlockSpec((D_MODEL, tn), lambda j: (0, j)),
                  pl.BlockSpec((1, tn), lambda j: (0, j))],
        out_specs=pl.BlockSpec((cond.shape[0], tn), lambda j: (0, j)),
        compiler_params=_cparams(("arbitrary",)),
        name="mod_table",
    )(cond, w, b.reshape(1, nm))


def _inproj_kernel(x_ref, mod_ref, g_ref, w_ref, b_ref,
                   q_ref, kf_ref, vf_ref, kb_ref, vb_ref, zb_ref, zc_ref, gt_ref):
    u = (_rms(x_ref[...], g_ref[...]) * (1.0 + mod_ref[0, 1:2, :]) + mod_ref[0, 0:1, :]).astype(BF16)

    def seg(c0, width):
        return _dot(u, w_ref[:, c0:c0 + width]) + b_ref[:, c0:c0 + width]

    q_ref[...] = seg(0, NA_WIDTH).astype(BF16)
    k = seg(NA_WIDTH, NA_WIDTH)
    kf_ref[...] = k
    kb_ref[...] = k.astype(BF16)
    v = seg(2 * NA_WIDTH, NA_WIDTH)
    vf_ref[...] = v
    vb_ref[...] = v.astype(BF16)
    wb = 384
    for j in range(B_COLS // wb):
        zb_ref[:, j * wb:(j + 1) * wb] = seg(A_COLS + j * wb, wb)
    wc = 512
    for j in range(C_COLS // wc):
        zc_ref[:, j * wc:(j + 1) * wc] = seg(A_COLS + B_COLS + j * wc, wc)
    for j in range(G_COLS // wc):
        gt_ref[:, j * wc:(j + 1) * wc] = _sigmoid(seg(A_COLS + B_COLS + C_COLS + j * wc, wc)).astype(BF16)


def in_proj(lay, x, mod, g0, w_bf, b):
    n = lay.n
    row = lambda w: pl.BlockSpec((TM, w), lambda i: (i, 0))
    out_shape = (jax.ShapeDtypeStruct((n, NA_WIDTH), BF16),
                 jax.ShapeDtypeStruct((n, NA_WIDTH), F32), jax.ShapeDtypeStruct((n, NA_WIDTH), F32),
                 jax.ShapeDtypeStruct((n, NA_WIDTH), BF16), jax.ShapeDtypeStruct((n, NA_WIDTH), BF16),
                 jax.ShapeDtypeStruct((n, B_COLS), F32), jax.ShapeDtypeStruct((n, C_COLS), F32),
                 jax.ShapeDtypeStruct((n, G_COLS), BF16))
    return pl.pallas_call(
        _inproj_kernel,
        out_shape=out_shape,
        grid=(lay.tiles,),
        in_specs=[row(D_MODEL),
                  pl.BlockSpec((1, N_MOD, D_MODEL), lambda i: (lay.mod_row(i), 0, 0)),
                  pl.BlockSpec((1, D_MODEL), lambda i: (0, 0)),
                  pl.BlockSpec((D_MODEL, P_IN), lambda i: (0, 0), pipeline_mode=pl.Buffered(1)),
                  pl.BlockSpec((1, P_IN), lambda i: (0, 0))],
        out_specs=(row(NA_WIDTH), row(NA_WIDTH), row(NA_WIDTH), row(NA_WIDTH), row(NA_WIDTH),
                   row(B_COLS), row(C_COLS), row(G_COLS)),
        compiler_params=_cparams(("arbitrary",)),
        name="in_proj",
    )(x, mod, g0, w_bf, b)


def _ctx_attn_kernel(q_ref, k_ref, v_ref, o_ref):
    scale = NA_HEAD_DIM ** -0.5
    for h in range(NA_HEADS):
        sl = slice(h * NA_HEAD_DIM, (h + 1) * NA_HEAD_DIM)
        s = _dot(q_ref[:, sl], k_ref[:, sl], _NT) * scale
        m = jnp.max(s, axis=-1, keepdims=True)
        p = jnp.exp(s - m)
        den = jnp.sum(p, axis=-1, keepdims=True)
        o_ref[:, sl] = (_dot(p.astype(BF16), v_ref[:, sl]) / den).astype(BF16)


def ctx_attention(lay, q, kb, vb):
    blk = pl.BlockSpec((lay.ctx_len, NA_WIDTH), lambda b: (b, 0))
    return pl.pallas_call(
        _ctx_attn_kernel,
        out_shape=jax.ShapeDtypeStruct((lay.ctx_tokens, NA_WIDTH), BF16),
        grid=(lay.n_ctx,),
        in_specs=[blk, blk, blk],
        out_specs=blk,
        compiler_params=_cparams(("arbitrary",)),
        name="ctx_attention",
    )(q, kb, vb)


def na_bias_table(rpb):
    kw, kh = NA_WIN_COLS, NA_WIN_ROWS
    col = jnp.arange(GRID_W, dtype=jnp.int32)
    col_start = jnp.clip(col - kw // 2, 0, GRID_W - kw)
    col_ok = (col[None, :] >= col_start[:, None]) & (col[None, :] < col_start[:, None] + kw)
    dc = jnp.clip(col[None, :] - col[:, None] + kw - 1, 0, 2 * kw - 2)
    bias_c = jnp.where(col_ok, rpb[:, :, dc].astype(F32), NEG_INF)
    tabs = [jnp.concatenate([bias_c[:, d + i] for i in range(kh)], axis=-1) for d in range(kh)]
    return jnp.stack(tabs, axis=0)


def _na_row_start(r, rows):
    return jnp.clip(r - NA_WIN_ROWS // 2, 0, rows - NA_WIN_ROWS)


def _na_kernel(q_ref, k_ref, v_ref, ck_ref, cv_ref, bias_ref, o_ref, *, rows):
    r = pl.program_id(1)
    rs = _na_row_start(r, rows)
    n_loc = NA_WIN_ROWS * GRID_W
    start = pl.multiple_of(rs * GRID_W, GRID_W)
    k_loc = k_ref[pl.ds(start, n_loc), :]
    v_loc = v_ref[pl.ds(start, n_loc), :]
    scale = NA_HEAD_DIM ** -0.5
    for h in range(NA_HEADS):
        sl = slice(h * NA_HEAD_DIM, (h + 1) * NA_HEAD_DIM)
        qh = q_ref[:, sl] * scale
        s_loc = _dot(qh, k_loc[:, sl], _NT) + bias_ref[0, h]
        s_ctx = _dot(qh, ck_ref[0, :, sl], _NT)
        m = jnp.maximum(jnp.max(s_loc, axis=-1, keepdims=True), jnp.max(s_ctx, axis=-1, keepdims=True))
        p_loc = jnp.exp(s_loc - m)
        p_ctx = jnp.exp(s_ctx - m)
        den = jnp.sum(p_loc, axis=-1, keepdims=True) + jnp.sum(p_ctx, axis=-1, keepdims=True)
        o = _dot(p_loc.astype(BF16), v_loc[:, sl]) + _dot(p_ctx.astype(BF16), cv_ref[0, :, sl])
        o_ref[:, sl] = (o / den).astype(BF16)


def na_attention(lay, q, kb, vb, ck, cv, bias_tab):
    rows = lay.dec_len // GRID_W
    assert rows >= NA_WIN_ROWS
    qoff = lay.ctx_tokens // GRID_W
    koff = lay.ctx_tokens // lay.dec_len
    assert lay.ctx_tokens % lay.dec_len == 0
    kv = pl.BlockSpec((lay.dec_len, NA_WIDTH), lambda b, r: (koff + b, 0))
    cs = pl.BlockSpec((1, ck.shape[1], NA_WIDTH), lambda b, r: (b, 0, 0))
    return pl.pallas_call(
        functools.partial(_na_kernel, rows=rows),
        out_shape=jax.ShapeDtypeStruct((lay.dec_tokens, NA_WIDTH), BF16),
        grid=(lay.n_dec, rows),
        in_specs=[pl.BlockSpec((GRID_W, NA_WIDTH), lambda b, r: (qoff + b * rows + r, 0)),
                  kv, kv, cs, cs,
                  pl.BlockSpec((1, NA_HEADS, GRID_W, NA_WIN_ROWS * GRID_W),
                               lambda b, r: (_na_row_start(r, rows) - r + NA_WIN_ROWS - 1, 0, 0, 0))],
        out_specs=pl.BlockSpec((GRID_W, NA_WIDTH), lambda b, r: (b * rows + r, 0)),
        compiler_params=_cparams(("arbitrary", "arbitrary")),
        name="na_attention",
    )(q, kb, vb, ck, cv, bias_tab)


def _softplus(x):
    return jnp.maximum(x, 0.0) + jnp.log(1.0 + jnp.exp(-jnp.abs(x)))


def _rwkv_prepare(d, z, prev_row, next_row, prm):
    (mu_ref, w0_ref, w2_ref, a0_ref, a2_ref, kk_ref, ka_ref, rk_ref) = prm
    c = CHUNK
    cw = RW_WIDTH
    row = lax.broadcasted_iota(jnp.int32, (c, 1), 0)
    prev = jnp.where(row == 0, prev_row, pltpu.roll(z, 1, axis=0))
    nxt = jnp.where(row == c - 1, next_row, pltpu.roll(z, c - 1, axis=0))
    zs = z + mu_ref[0:1, :] * (prev - z) + mu_ref[1:2, :] * (nxt - z)
    r, k, v = zs[:, :cw], zs[:, cw:2 * cw], zs[:, 2 * cw:3 * cw]
    o = 3 * cw
    zw = zs[:, o + d * RW_DECAY_RANK:o + (d + 1) * RW_DECAY_RANK]
    o += 2 * RW_DECAY_RANK
    za = zs[:, o + d * RW_ICLR_RANK:o + (d + 1) * RW_ICLR_RANK]

    w_log = -_softplus(-(w0_ref[d:d + 1, :] + _mm3(jnp.tanh(zw), w2_ref[d]))) - 0.5
    logw = -jnp.exp(w_log)
    a = _sigmoid(a0_ref[d:d + 1, :] + _mm3(za, a2_ref[d]))
    kd = k * (1.0 + (a - 1.0) * ka_ref[...])
    kk = k * kk_ref[...]

    ti = lax.broadcasted_iota(jnp.int32, (c, c), 0)
    tj = lax.broadcasted_iota(jnp.int32, (c, c), 1)
    tri = ((ti >= tj) if d == 0 else (tj >= ti)).astype(BF16)
    lh, ll = _split(logw)
    cl = _dot(tri, lh) + _dot(tri, ll)
    p_in = jnp.exp(cl)
    p_ex = jnp.exp(cl - logw)
    p_inv = jnp.exp(-cl)
    p_end = p_in[c - 1:c, :] if d == 0 else p_in[0:1, :]

    pi = lax.broadcasted_iota(jnp.int32, (LANES, LANES), 0)
    pj = lax.broadcasted_iota(jnp.int32, (LANES, LANES), 1)
    same_head = (pi // RW_HEAD_DIM) == (pj // RW_HEAD_DIM)
    ones_bd = same_head.astype(BF16)

    def head_sum(x):
        xh, xl = _split(x)
        return _dot(xh, ones_bd) + _dot(xl, ones_bd)

    t2 = lax.broadcasted_iota(jnp.int32, (2 * c, LANES), 0) % c
    j2 = lax.broadcasted_iota(jnp.int32, (2 * c, LANES), 1) % c
    diff = (t2 - j2) if d == 0 else (j2 - t2)
    upper = lax.broadcasted_iota(jnp.int32, (2 * c, LANES), 0) < c
    mask = diff >= jnp.where(upper, 1, 0)
    units = []
    for p in range(RW_HEADS // 2):
        sl = slice(p * LANES, (p + 1) * LANES)
        kkp = kk[:, sl]
        kkp = kkp * lax.rsqrt(jnp.maximum(head_sum(kkp * kkp), 1e-24))
        pinv = p_inv[:, sl]
        bt = a[:, sl] * kkp * pinv
        kt = kd[:, sl] * pinv
        ar = jnp.concatenate([-kkp * p_ex[:, sl], r[:, sl] * p_in[:, sl]], axis=0).astype(BF16)
        pe = p_end[:, sl]
        bonus = head_sum(r[:, sl] * kd[:, sl] * rk_ref[:, sl]) * v[:, sl]
        units.append(dict(d=d, p=p, ar=ar, bt=bt, kt=kt, v=v[:, sl], pe=pe,
                          bkp=jnp.concatenate([bt * pe, kt * pe], axis=0), bonus=bonus,
                          mask=mask, eye=(diff[:c] == 0).astype(F32), bd=same_head))
    return units, zs


def _bd(x):
    lo = lax.broadcasted_iota(jnp.int32, x.shape, 1) < RW_HEAD_DIM
    xb = x.astype(BF16)
    z = jnp.zeros_like(xb)
    return jnp.concatenate([jnp.where(lo, xb, z), jnp.where(lo, z, xb)], axis=0)


def _rwkv_solve(units, s_ref):
    c = CHUNK
    for u in units:
        gb = _dot(u['ar'], _bd(u['bt']), _NT)
        gk = _dot(u['ar'], _bd(u['kt']), _NT)
        gb = jnp.where(u['mask'], gb, 0.0)
        u['a_ab'], u['a_rb'] = gb[:c], gb[c:]
        u['s0'] = s_ref[u['d'], u['p']]
        u['xs'] = _dot(u['ar'], u['s0'].astype(BF16), _NT) + _mm(jnp.where(u['mask'], gk, 0.0), _bd(u['v']))
    for u in units:
        u['pw'] = _mm(u['a_ab'], _bd(u['a_ab']))
        u['tinv'] = u['eye'] + u['a_ab']
    n_sq = CHUNK.bit_length() - 2
    for i in range(n_sq):
        for u in units:
            if i < n_sq - 1:
                both = _mm(jnp.concatenate([u['pw'], u['tinv']], axis=0), _bd(u['pw']))
                u['pw'], u['tinv'] = both[:c], u['tinv'] + both[c:]
            else:
                u['tinv'] = u['tinv'] + _mm(u['tinv'], _bd(u['pw']))
    for u in units:
        u['u'] = _mm(u['tinv'], _bd(u['xs'][:c]))
    ys = []
    for u in units:
        ys.append(u['xs'][c:] + _mm(u['a_rb'], _bd(u['u'])) + u['bonus'])
        uv = jnp.concatenate([u['u'], u['v']], axis=0)
        s_ref[u['d'], u['p']] = u['s0'] * u['pe'] + jnp.where(u['bd'], _mm(uv.T, u['bkp']), 0.0)
    return ys


def _rwkv_kernel(*refs, nc, has_s0):
    zf_ref, zfp_ref, zfn_ref, zr_ref, zrp_ref, zrn_ref = refs[:6]
    n_in = 6 + (1 if has_s0 else 0)
    s0_ref = refs[6] if has_s0 else None
    prm = refs[n_in:n_in + 8]
    g2_ref = refs[n_in + 8]
    yf_ref, yb_ref, g_ref, sfin_ref, s_ref = refs[n_in + 9:]
    i = pl.program_id(1)

    hd = RW_HEAD_DIM
    npair = RW_HEADS // 2

    @pl.when(i == 0)
    def _():
        s_ref[...] = jnp.zeros(s_ref.shape, F32)
        if has_s0:
            for d in range(2):
                for p in range(npair):
                    s_ref[d, p, :hd, :hd] = s0_ref[0, d, 2 * p]
                    s_ref[d, p, hd:, hd:] = s0_ref[0, d, 2 * p + 1]

    zero = jnp.zeros((1, B_COLS), F32)
    pf = jnp.where(i == 0, zero, zfp_ref[7:8, :])
    nf = jnp.where(i == nc - 1, zero, zfn_ref[0:1, :])
    uf, zs = _rwkv_prepare(0, zf_ref[...], pf, nf, prm)
    g_ref[...] = _mm(_sigmoid(zs[:, B_COLS - RW_GATE_RANK:]), g2_ref[...])
    pr = jnp.where(i == nc - 1, zero, zrp_ref[7:8, :])
    nr = jnp.where(i == 0, zero, zrn_ref[0:1, :])
    ub, _ = _rwkv_prepare(1, zr_ref[...], pr, nr, prm)
    ys = _rwkv_solve(uf + ub, s_ref)
    yf_ref[...] = jnp.concatenate(ys[:npair], axis=-1)
    yb_ref[...] = jnp.concatenate(ys[npair:], axis=-1)

    @pl.when(i == nc - 1)
    def _():
        for d in range(2):
            for p in range(npair):
                sfin_ref[0, d, 2 * p] = s_ref[d, p, :hd, :hd]
                sfin_ref[0, d, 2 * p + 1] = s_ref[d, p, hd:, hd:]


def rwkv_scan(zb, tok0, nseq, seqlen, s0, prm, g2):
    n = zb.shape[0]
    nc = seqlen // CHUNK
    sub = CHUNK // 8
    base = tok0 // CHUNK
    nt = nseq * seqlen
    has_s0 = s0 is not None

    def cf(s, i):
        return base + s * nc + i

    def cr(s, i):
        return base + s * nc + (nc - 1 - i)

    def specs(cidx):
        return [pl.BlockSpec((CHUNK, B_COLS), lambda s, i: (cidx(s, i), 0)),
                pl.BlockSpec((8, B_COLS), lambda s, i: (jnp.maximum(cidx(s, i) * sub - 1, 0), 0)),
                pl.BlockSpec((8, B_COLS), lambda s, i: (jnp.minimum((cidx(s, i) + 1) * sub, n // 8 - 1), 0))]

    full = lambda a: pl.BlockSpec(a.shape, lambda s, i: (0,) * a.ndim)
    sshape = (2, RW_HEADS, RW_HEAD_DIM, RW_HEAD_DIM)
    in_specs = specs(cf) + specs(cr)
    args = [zb] * 6
    if has_s0:
        in_specs.append(pl.BlockSpec((1,) + sshape, lambda s, i: (s, 0, 0, 0, 0)))
        args.append(s0)
    in_specs += [full(p) for p in prm] + [full(g2)]
    args += list(prm) + [g2]
    out = lambda cidx: pl.BlockSpec((CHUNK, RW_WIDTH), lambda s, i: (cidx(s, i) - base, 0))
    tok = jax.ShapeDtypeStruct((nt, RW_WIDTH), F32)
    return pl.pallas_call(
        functools.partial(_rwkv_kernel, nc=nc, has_s0=has_s0),
        out_shape=(tok, tok, tok, jax.ShapeDtypeStruct((nseq,) + sshape, F32)),
        grid=(nseq, nc),
        in_specs=in_specs,
        out_specs=(out(cf), out(cr), out(cf), pl.BlockSpec((1,) + sshape, lambda s, i: (s, 0, 0, 0, 0))),
        scratch_shapes=[pltpu.VMEM((2, RW_HEADS // 2, LANES, LANES), F32)],
        compiler_params=_cparams(("arbitrary", "arbitrary")),
        name="rwkv_scan",
    )(*args)


def _conv_kernel(z_ref, zp_ref, zn_ref, w_ref, b_ref, g_ref, be_ref, o_ref, h_ref, *, lay):
    i = pl.program_id(0)
    first, last = lay.tile_pos(i)

    def glu(z):
        return z[:, :CONV_WIDTH] * _sigmoid(z[:, CONV_WIDTH:])

    hz = jnp.zeros((CONV_HALO, CONV_WIDTH), F32)
    h_ref[0:CONV_HALO, :] = jnp.where(first, hz, glu(zp_ref[...]))
    h_ref[CONV_HALO:CONV_HALO + TM, :] = glu(z_ref[...])
    h_ref[CONV_HALO + TM:, :] = jnp.where(last, hz, glu(zn_ref[...]))
    off = CONV_HALO - CONV_K // 2
    acc = jnp.zeros((TM, CONV_WIDTH), F32) + b_ref[...]
    for j in range(CONV_K):
        acc = acc + w_ref[j:j + 1, :] * h_ref[off + j:off + j + TM, :]
    xc = acc - jnp.mean(acc, axis=-1, keepdims=True)
    hn = xc * lax.rsqrt(jnp.mean(xc * xc, axis=-1, keepdims=True) + LN_EPS) * g_ref[...] + be_ref[...]
    o_ref[...] = (hn * _sigmoid(hn)).astype(BF16)


def conv_module(lay, zc, w, b, g, be):
    n = lay.n
    hb = TM // CONV_HALO
    vec = pl.BlockSpec((1, CONV_WIDTH), lambda i: (0, 0))
    return pl.pallas_call(
        functools.partial(_conv_kernel, lay=lay),
        out_shape=jax.ShapeDtypeStruct((n, CONV_WIDTH), BF16),
        grid=(lay.tiles,),
        in_specs=[pl.BlockSpec((TM, C_COLS), lambda i: (i, 0)),
                  pl.BlockSpec((CONV_HALO, C_COLS), lambda i: (jnp.maximum(i * hb - 1, 0), 0)),
                  pl.BlockSpec((CONV_HALO, C_COLS), lambda i: (jnp.minimum((i + 1) * hb, n // CONV_HALO - 1), 0)),
                  pl.BlockSpec((CONV_K, CONV_WIDTH), lambda i: (0, 0)), vec, vec, vec],
        out_specs=pl.BlockSpec((TM, CONV_WIDTH), lambda i: (i, 0)),
        scratch_shapes=[pltpu.VMEM((TM + 2 * CONV_HALO, CONV_WIDTH), F32)],
        compiler_params=_cparams(("arbitrary",)),
        name="conv_module",
    )(zc, zc, zc, w, b, g, be)


def _merge_kernel(x_ref, oa_ref, yf_ref, yb_ref, g_ref, oc_ref, gt_ref, mod_ref, ng_ref, lg_ref, lb_ref,
                  wa_ref, wr_ref, wc_ref, wo_ref, rw_ref, rb_ref,
                  x1_ref, hm_ref, ti_ref, tg_ref):
    y = yf_ref[...] + yb_ref[...]
    outs = []
    for h in range(RW_HEADS):
        yh = y[:, h * RW_HEAD_DIM:(h + 1) * RW_HEAD_DIM]
        yc = yh - jnp.mean(yh, axis=-1, keepdims=True)
        outs.append(yc * lax.rsqrt(jnp.mean(yc * yc, axis=-1, keepdims=True) + GN_EPS))
    o_rw = (jnp.concatenate(outs, axis=-1) * lg_ref[...] + lb_ref[...]) * g_ref[...]
    merged = (gt_ref[:, :D_MODEL].astype(F32) * _dot(oa_ref[...], wa_ref[...])
              + gt_ref[:, D_MODEL:2 * D_MODEL].astype(F32) * _mm(o_rw, wr_ref[...])
              + gt_ref[:, 2 * D_MODEL:].astype(F32) * _dot(oc_ref[...], wc_ref[...]))
    mix = _mm(merged, wo_ref[...])
    x1 = x_ref[...] + mod_ref[0, 2:3, :] * _rms(mix, ng_ref[1:2, :])
    x1_ref[...] = x1
    hm = _rms(x1, ng_ref[2:3, :]) * (1.0 + mod_ref[0, 4:5, :]) + mod_ref[0, 3:4, :]
    hm_ref[...] = hm
    logits = _mm3(hm, rw_ref[...]) + rb_ref[...]
    lane = lax.broadcasted_iota(jnp.int32, logits.shape, 1)
    idx_out = jnp.zeros(logits.shape, jnp.int32)
    val_out = jnp.full(logits.shape, NEG_INF, F32)
    for j in range(TOP_K):
        m = jnp.max(logits, axis=-1, keepdims=True)
        idx = jnp.min(jnp.where(logits == m, lane, LANES), axis=-1, keepdims=True)
        idx_out = jnp.where(lane == j, idx, idx_out)
        val_out = jnp.where(lane == j, m, val_out)
        logits = jnp.where(lane == idx, -jnp.inf, logits)
    e = jnp.exp(val_out - jnp.max(val_out, axis=-1, keepdims=True))
    ti_ref[...] = idx_out
    tg_ref[...] = e / jnp.sum(e, axis=-1, keepdims=True)


def merge_router(lay, x, o_att, yf, yb, g, o_cv, gates, mod, ng, lg, lb, wa, wr, wc, wo, rw, rb):
    n = lay.n
    row = lambda w: pl.BlockSpec((TM, w), lambda i: (i, 0))
    full = lambda a: pl.BlockSpec(a.shape, lambda i: (0,) * a.ndim)
    return pl.pallas_call(
        _merge_kernel,
        out_shape=(jax.ShapeDtypeStruct((n, D_MODEL), F32), jax.ShapeDtypeStruct((n, D_MODEL), F32),
                   jax.ShapeDtypeStruct((n, LANES), jnp.int32), jax.ShapeDtypeStruct((n, LANES), F32)),
        grid=(lay.tiles,),
        in_specs=[row(D_MODEL), row(NA_WIDTH), row(RW_WIDTH), row(RW_WIDTH), row(RW_WIDTH), row(CONV_WIDTH),
                  row(G_COLS), pl.BlockSpec((1, N_MOD, D_MODEL), lambda i: (lay.mod_row(i), 0, 0)),
                  full(ng), full(lg), full(lb), full(wa), full(wr), full(wc), full(wo), full(rw), full(rb)],
        out_specs=(row(D_MODEL), row(D_MODEL), row(LANES), row(LANES)),
        compiler_params=_cparams(("arbitrary",)),
        name="merge_router",
    )(x, o_att, yf, yb, g, o_cv, gates, mod, ng, lg, lb, wa, wr, wc, wo, rw, rb)


def route_slots(top_i, n):
    nk = n * TOP_K
    flat_e = top_i.reshape(-1)
    onehot = (flat_e[:, None] == jnp.arange(N_EXPERTS, dtype=flat_e.dtype)[None, :]).astype(jnp.int32)
    csum = jnp.cumsum(onehot, axis=0)
    rank = jnp.take_along_axis(csum, flat_e[:, None], axis=1)[:, 0] - 1
    counts = csum[-1]
    padded = ((counts + ROUTE_BLOCK - 1) // ROUTE_BLOCK) * ROUTE_BLOCK
    pad_end = jnp.cumsum(padded)
    dest = (pad_end - padded)[flat_e] + rank
    n_blocks = -(-nk // ROUTE_BLOCK) + N_EXPERTS
    block_e = jnp.minimum(jnp.searchsorted(pad_end, jnp.arange(n_blocks, dtype=jnp.int32) * ROUTE_BLOCK,
                                           side='right'), N_EXPERTS - 1).astype(jnp.int32)
    n_used = (pad_end[-1] // ROUTE_BLOCK).astype(jnp.int32).reshape(1)
    return dest.astype(jnp.int32), block_e, n_used, n_blocks


def _row_copy_wait(src_rows, dst_rows, sem):
    pltpu.make_async_copy(src_rows, dst_rows, sem).wait()


def _dispatch_kernel(dest_ref, hm_ref, xb_in_ref, xb_ref, sem):
    del xb_in_ref

    def issue(t, carry):
        for j in range(TOP_K):
            d = dest_ref[0, 0, t * TOP_K + j]
            pltpu.make_async_copy(hm_ref.at[pl.ds(t, 1), :], xb_ref.at[pl.ds(d, 1), :], sem).start()
        return carry

    lax.fori_loop(0, TM, issue, 0)
    for j in range(TOP_K):
        _row_copy_wait(hm_ref, xb_ref.at[pl.ds(0, TM), :], sem)


def moe_dispatch(lay, hm, dest, n_rows):
    xb0 = jnp.zeros((n_rows, D_MODEL), F32)
    return pl.pallas_call(
        _dispatch_kernel,
        out_shape=jax.ShapeDtypeStruct((n_rows, D_MODEL), F32),
        grid=(lay.tiles,),
        in_specs=[pl.BlockSpec((1, 1, TM * TOP_K), lambda i: (i, 0, 0), memory_space=pltpu.SMEM),
                  pl.BlockSpec((TM, D_MODEL), lambda i: (i, 0)),
                  pl.BlockSpec(memory_space=pl.ANY)],
        out_specs=pl.BlockSpec(memory_space=pl.ANY),
        scratch_shapes=[pltpu.SemaphoreType.DMA],
        input_output_aliases={2: 0},
        compiler_params=_cparams(("arbitrary",)),
        name="moe_dispatch",
    )(dest.reshape(lay.tiles, 1, TM * TOP_K), hm, xb0)


GL_GROUP = 2 * LANES


def _regroup_perm():
    r = jnp.arange(GL_GROUP, dtype=jnp.int32)[:, None]
    c = jnp.arange(GL_GROUP, dtype=jnp.int32)[None, :]
    src = jnp.where(c < LANES, 2 * c, 2 * (c - LANES) + 1)
    return (r == src).astype(BF16)


def _w1_regroup_kernel(w_ref, p_ref, o_ref):
    for j in range(2 * EXPERT_FF // GL_GROUP):
        cs = slice(j * GL_GROUP, (j + 1) * GL_GROUP)
        o_ref[0, :, cs] = _dot(w_ref[0, :, cs].astype(BF16), p_ref[...]).astype(BF16)


def w1_regroup(w1):
    blk = pl.BlockSpec((1,) + w1.shape[1:], lambda e: (e, 0, 0))
    return pl.pallas_call(
        _w1_regroup_kernel,
        out_shape=jax.ShapeDtypeStruct(w1.shape, BF16),
        grid=(w1.shape[0],),
        in_specs=[blk, pl.BlockSpec((GL_GROUP, GL_GROUP), lambda e: (0, 0))],
        out_specs=blk,
        compiler_params=_cparams(("arbitrary",)),
        name="w1_regroup",
    )(w1, _regroup_perm())


def _expert_kernel(be_ref, nu_ref, x_ref, w1_ref, b1_ref, w2_ref, b2_ref, y_ref):
    b = pl.program_id(0)

    @pl.when(b < nu_ref[0])
    def _():
        x = x_ref[...].astype(BF16)
        acc = jnp.zeros((ROUTE_BLOCK, D_MODEL), F32) + b2_ref[0]
        for j in range(EXPERT_FF // GL_GROUP):
            acts = []
            for g in range(2):
                c0 = (2 * j + g) * GL_GROUP
                h = _dot(x, w1_ref[0, :, c0:c0 + GL_GROUP]) + b1_ref[0, :, c0:c0 + GL_GROUP]
                hg = jnp.minimum(h[:, :LANES], SWIGLU_LIMIT)
                hl = jnp.clip(h[:, LANES:], -SWIGLU_LIMIT, SWIGLU_LIMIT)
                acts.append(hg * _sigmoid(SWIGLU_ALPHA * hg) * (hl + 1.0))
            act = jnp.concatenate(acts, axis=-1).astype(BF16)
            acc = acc + _dot(act, w2_ref[0, j * GL_GROUP:(j + 1) * GL_GROUP, :])
        y_ref[...] = acc

    @pl.when(b >= nu_ref[0])
    def _():
        y_ref[...] = jnp.zeros(y_ref.shape, F32)


def moe_experts(xb, block_e, n_used, n_blocks, w1, b1, w2, b2):
    def xmap(b, be, nu):
        return (jnp.minimum(b, jnp.maximum(nu[0] - 1, 0)), 0)

    def wmap(b, be, nu):
        return (be[jnp.minimum(b, jnp.maximum(nu[0] - 1, 0))], 0, 0)

    wspec = lambda a: pl.BlockSpec((1,) + a.shape[1:], wmap)
    return pl.pallas_call(
        _expert_kernel,
        out_shape=jax.ShapeDtypeStruct(xb.shape, F32),
        grid_spec=pltpu.PrefetchScalarGridSpec(
            num_scalar_prefetch=2,
            grid=(n_blocks,),
            in_specs=[pl.BlockSpec((ROUTE_BLOCK, D_MODEL), xmap),
                      wspec(w1), wspec(b1), wspec(w2), wspec(b2)],
            out_specs=pl.BlockSpec((ROUTE_BLOCK, D_MODEL), lambda b, be, nu: (b, 0))),
        compiler_params=_cparams(("arbitrary",)),
        name="moe_experts",
    )(block_e, n_used, xb, w1, b1, w2, b2)


def _combine_kernel(dest_ref, yb_ref, tg_ref, x1_ref, mod_ref, ng_ref, o_ref, buf_ref, sem):
    def issue(t, carry):
        for j in range(TOP_K):
            d = dest_ref[0, 0, t * TOP_K + j]
            pltpu.make_async_copy(yb_ref.at[pl.ds(d, 1), :], buf_ref.at[j, pl.ds(t, 1), :], sem).start()
        return carry

    lax.fori_loop(0, TC, issue, 0)
    for j in range(TOP_K):
        _row_copy_wait(yb_ref.at[pl.ds(0, TC), :], buf_ref.at[j], sem)
    y = jnp.zeros((TC, D_MODEL), F32)
    for j in range(TOP_K):
        y = y + tg_ref[:, j:j + 1] * buf_ref[j]
    o_ref[...] = x1_ref[...] + mod_ref[0, 5:6, :] * _rms(y, ng_ref[3:4, :])


def moe_combine(lay, yb, dest, tg, x1, mod, ng):
    n = lay.n
    per = TM // TC
    row = lambda w: pl.BlockSpec((TC, w), lambda i: (i, 0))
    return pl.pallas_call(
        _combine_kernel,
        out_shape=jax.ShapeDtypeStruct((n, D_MODEL), F32),
        grid=(n // TC,),
        in_specs=[pl.BlockSpec((1, 1, TC * TOP_K), lambda i: (i, 0, 0), memory_space=pltpu.SMEM),
                  pl.BlockSpec(memory_space=pl.ANY),
                  row(LANES), row(D_MODEL),
                  pl.BlockSpec((1, N_MOD, D_MODEL), lambda i: (lay.mod_row(i // per), 0, 0)),
                  pl.BlockSpec(ng.shape, lambda i: (0, 0))],
        out_specs=row(D_MODEL),
        scratch_shapes=[pltpu.VMEM((TOP_K, TC, D_MODEL), F32), pltpu.SemaphoreType.DMA],
        compiler_params=_cparams(("arbitrary",)),
        name="moe_combine",
    )(dest.reshape(n // TC, 1, TC * TOP_K), yb, tg, x1, mod, ng)


def layer(lay, x, l, cond, P, ctx_k, ctx_v, s0_dec):
    mod = mod_table(cond, P['mod_w'][l], P['mod_b'][l]).reshape(cond.shape[0], N_MOD, D_MODEL)
    ng = P['norm_g'][l]
    q, kf, vf, kb, vb, zb, zc, gates = in_proj(lay, x, mod, ng[0:1], P['w_in'][l].astype(BF16),
                                               P['b_in'][l].reshape(1, P_IN))
    oa_ctx = ctx_attention(lay, q, kb, vb)
    oa_dec = na_attention(lay, q, kb, vb, ctx_k.astype(BF16), ctx_v.astype(BF16), na_bias_table(P['rpb'][l]))
    o_att = jnp.concatenate([oa_ctx, oa_dec], axis=0)
    vec = lambda a: a.reshape(1, -1)
    prm = (P['rw_mu'][l], P['rw_w0'][l], P['rw_w2'][l], P['rw_a0'][l], P['rw_a2'][l],
           vec(P['rw_k_k'][l]), vec(P['rw_k_a'][l]), vec(P['rw_r_k'][l]))
    yf_c, yb_c, g_c, s_ctx = rwkv_scan(zb, 0, lay.n_ctx, lay.ctx_len, None, prm, P['rw_g2'][l])
    yf_d, yb_d, g_d, _ = rwkv_scan(zb, lay.ctx_tokens, lay.n_dec, lay.dec_len, s0_dec, prm, P['rw_g2'][l])
    yf = jnp.concatenate([yf_c, yf_d], axis=0)
    yb = jnp.concatenate([yb_c, yb_d], axis=0)
    g = jnp.concatenate([g_c, g_d], axis=0)
    o_cv = conv_module(lay, zc, P['conv_w'][l], vec(P['conv_b'][l]), vec(P['conv_ln_g'][l]),
                       vec(P['conv_ln_b'][l]))
    rw = jnp.zeros((D_MODEL, LANES), F32).at[:, :N_EXPERTS].set(P['router_w'][l])
    rb = jnp.full((1, LANES), NEG_INF, F32).at[0, :N_EXPERTS].set(P['router_b'][l])
    x1, hm, top_i, top_g = merge_router(
        lay, x, o_att, yf, yb, g, o_cv, gates, mod, ng, vec(P['rw_lnx_g'][l]), vec(P['rw_lnx_b'][l]),
        P['w_o_attn'][l].astype(BF16), P['w_o_rwkv'][l].astype(BF16), P['w_o_conv'][l].astype(BF16),
        P['w_out'][l].astype(BF16), rw, rb)
    dest, block_e, n_used, n_blocks = route_slots(top_i[:, :TOP_K], lay.n)
    xb = moe_dispatch(lay, hm, dest, n_blocks * ROUTE_BLOCK)
    b1 = P['exp_b1'][l].reshape(N_EXPERTS, -1, LANES, 2).transpose(0, 1, 3, 2).reshape(N_EXPERTS, 1, -1)
    ybk = moe_experts(xb, block_e, n_used, n_blocks, w1_regroup(P['exp_w1'][l]), b1,
                      P['exp_w2'][l].astype(BF16), P['exp_b2'][l][:, None, :])
    x2 = moe_combine(lay, ybk, dest, top_g, x1, mod, ng)
    return x2, kf, vf, s_ctx


def kernel(x_prompt, x_sample, cache_attn_k, cache_attn_v, state_rwkv, c, c_ctx, mod_w, mod_b, norm_g, w_in, b_in, rpb, w_o_attn, rw_mu, rw_w0, rw_w2, rw_a0, rw_a2, rw_g2, rw_k_k, rw_k_a, rw_r_k, rw_lnx_g, rw_lnx_b, w_o_rwkv, conv_w, conv_b, conv_ln_g, conv_ln_b, w_o_conv, w_out, router_w, router_b, exp_w1, exp_b1, exp_w2, exp_b2):
    P = {
        'mod_w': mod_w, 'mod_b': mod_b, 'norm_g': norm_g, 'w_in': w_in, 'b_in': b_in, 'rpb': rpb,
        'w_o_attn': w_o_attn, 'rw_mu': rw_mu, 'rw_w0': rw_w0, 'rw_w2': rw_w2, 'rw_a0': rw_a0,
        'rw_a2': rw_a2, 'rw_g2': rw_g2, 'rw_k_k': rw_k_k, 'rw_k_a': rw_k_a, 'rw_r_k': rw_r_k,
        'rw_lnx_g': rw_lnx_g, 'rw_lnx_b': rw_lnx_b, 'w_o_rwkv': w_o_rwkv, 'conv_w': conv_w,
        'conv_b': conv_b, 'conv_ln_g': conv_ln_g, 'conv_ln_b': conv_ln_b, 'w_o_conv': w_o_conv,
        'w_out': w_out, 'router_w': router_w, 'router_b': router_b, 'exp_w1': exp_w1,
        'exp_b1': exp_b1, 'exp_w2': exp_w2, 'exp_b2': exp_b2,
    }
    nb, sl, _ = x_prompt.shape
    db, ds, _ = x_sample.shape
    lay = Layout(nb, sl, db, ds)
    x = jnp.concatenate([x_prompt.reshape(-1, D_MODEL), x_sample.reshape(-1, D_MODEL)], axis=0)
    cond = jnp.zeros((8, D_MODEL), F32).at[0].set(c_ctx).at[1:1 + db].set(c)
    ks, vs, ss = [], [], []
    for l in range(DEPTH):
        ck = cache_attn_k[:, l].reshape(db, -1, NA_WIDTH)
        cv = cache_attn_v[:, l].reshape(db, -1, NA_WIDTH)
        x, kf, vf, s_ctx = layer(lay, x, l, cond, P, ck, cv, state_rwkv[:, l])
        ks.append(kf[:lay.ctx_tokens].reshape(nb, sl, NA_HEADS, NA_HEAD_DIM))
        vs.append(vf[:lay.ctx_tokens].reshape(nb, sl, NA_HEADS, NA_HEAD_DIM))
        ss.append(s_ctx)
    y_prompt = x[:lay.ctx_tokens].reshape(nb, sl, D_MODEL)
    y_sample = x[lay.ctx_tokens:].reshape(db, ds, D_MODEL)
    return (y_prompt, y_sample, jnp.stack(ks, axis=1), jnp.stack(vs, axis=1), jnp.stack(ss, axis=1))
```

```python
import functools

import jax
import jax.numpy as jnp
from jax import lax
from jax.experimental import pallas as pl
from jax.experimental.pallas import tpu as pltpu

F32 = jnp.float32
BF16 = jnp.bfloat16

D_MODEL = 1024
BATCH, SEQ = 32, 256
DEPTH = 2
DEC_BATCH, DEC_SEQ = 4, 4096
PAST_LEN = 256
GRID_W = 64
NA_HEADS, NA_HEAD_DIM = 8, 64
NA_WIDTH = NA_HEADS * NA_HEAD_DIM
NA_WIN_ROWS, NA_WIN_COLS = 8, 16
RW_HEADS, RW_HEAD_DIM = 8, 64
RW_WIDTH = RW_HEADS * RW_HEAD_DIM
RW_DECAY_RANK, RW_ICLR_RANK, RW_GATE_RANK = 64, 64, 128
CONV_WIDTH, CONV_K = 512, 31
N_BRANCH, N_MOD = 3, 6
N_EXPERTS, TOP_K, EXPERT_FF = 32, 4, 1024
SWIGLU_LIMIT, SWIGLU_ALPHA = 7.0, 1.702
ROUTE_BLOCK = 512
RMS_EPS, LN_EPS, GN_EPS = 1e-6, 1e-5, 64e-5
NEG_INF = -1e30
A_COLS = 3 * NA_WIDTH
B_COLS = 3 * RW_WIDTH + 2 * RW_DECAY_RANK + 2 * RW_ICLR_RANK + RW_GATE_RANK
C_COLS = 2 * CONV_WIDTH
G_COLS = N_BRANCH * D_MODEL
P_IN = A_COLS + B_COLS + C_COLS + G_COLS

LANES = 128
TM = 256
CHUNK = 64
CONV_HALO = 16
TC = 128
VMEM_LIMIT = 56 * 1024 * 1024


def _cparams(sem):
    return pltpu.CompilerParams(dimension_semantics=sem, vmem_limit_bytes=VMEM_LIMIT)


def _dot(a, b, dims=((1,), (0,))):
    return lax.dot_general(a, b, (dims, ((), ())), preferred_element_type=F32)


def _mm(a, b, dims=((1,), (0,))):
    return _dot(a.astype(BF16), b.astype(BF16), dims)


def _split(a):
    hi = a.astype(BF16)
    lo = (a - hi.astype(F32)).astype(BF16)
    return hi, lo


def _mm3(a, b, dims=((1,), (0,))):
    ah, al = _split(a)
    bh, bl = _split(b)
    return _dot(ah, bh, dims) + (_dot(ah, bl, dims) + _dot(al, bh, dims))


_NT = ((1,), (1,))


def _rms(x, g):
    return x * lax.rsqrt(jnp.mean(x * x, axis=-1, keepdims=True) + RMS_EPS) * g


def _sigmoid(x):
    return 1.0 / (1.0 + jnp.exp(-x))


class Layout:
    def __init__(self, n_ctx, ctx_len, n_dec, dec_len):
        assert ctx_len == TM and dec_len % TM == 0
        self.n_ctx, self.ctx_len, self.n_dec, self.dec_len = n_ctx, ctx_len, n_dec, dec_len
        self.ctx_tokens = n_ctx * ctx_len
        self.dec_tokens = n_dec * dec_len
        self.n = self.ctx_tokens + self.dec_tokens
        self.ctx_tiles = self.ctx_tokens // TM
        self.dec_tiles_per_seq = dec_len // TM
        self.tiles = self.n // TM

    def mod_row(self, i):
        return jnp.where(i < self.ctx_tiles, 0, 1 + (i - self.ctx_tiles) // self.dec_tiles_per_seq)

    def tile_pos(self, i):
        j = (i - self.ctx_tiles) % self.dec_tiles_per_seq
        is_ctx = i < self.ctx_tiles
        return is_ctx | (j == 0), is_ctx | (j == self.dec_tiles_per_seq - 1)


def _mod_kernel(c_ref, w_ref, b_ref, o_ref):
    c = c_ref[...]
    o_ref[...] = _mm3(c * _sigmoid(c), w_ref[...]) + b_ref[...]


def mod_table(cond, w, b):
    tn = 1024
    nm = w.shape[1]
    return pl.pallas_call(
        _mod_kernel,
        out_shape=jax.ShapeDtypeStruct((cond.shape[0], nm), F32),
        grid=(nm // tn,),
        in_specs=[pl.BlockSpec(cond.shape, lambda j: (0, 0)),
                  pl.BlockSpec((D_MODEL, tn), lambda j: (0, j)),
                  pl.BlockSpec((1, tn), lambda j: (0, j))],
        out_specs=pl.BlockSpec((cond.shape[0], tn), lambda j: (0, j)),
        compiler_params=_cparams(("arbitrary",)),
        name="mod_table",
    )(cond, w, b.reshape(1, nm))


def _inproj_kernel(x_ref, mod_ref, g_ref, w_ref, b_ref,
                   q_ref, kf_ref, vf_ref, kb_ref, vb_ref, zb_ref, zc_ref, gt_ref):
    u = (_rms(x_ref[...], g_ref[...]) * (1.0 + mod_ref[0, 1:2, :]) + mod_ref[0, 0:1, :]).astype(BF16)

    def seg(c0, width):
        return _dot(u, w_ref[:, c0:c0 + width]) + b_ref[:, c0:c0 + width]

    q_ref[...] = seg(0, NA_WIDTH).astype(BF16)
    k = seg(NA_WIDTH, NA_WIDTH)
    kf_ref[...] = k
    kb_ref[...] = k.astype(BF16)
    v = seg(2 * NA_WIDTH, NA_WIDTH)
    vf_ref[...] = v
    vb_ref[...] = v.astype(BF16)
    wb = 384
    for j in range(B_COLS // wb):
        zb_ref[:, j * wb:(j + 1) * wb] = seg(A_COLS + j * wb, wb)
    wc = 512
    for j in range(C_COLS // wc):
        zc_ref[:, j * wc:(j + 1) * wc] = seg(A_COLS + B_COLS + j * wc, wc)
    for j in range(G_COLS // wc):
        gt_ref[:, j * wc:(j + 1) * wc] = _sigmoid(seg(A_COLS + B_COLS + C_COLS + j * wc, wc)).astype(BF16)


def in_proj(lay, x, mod, g0, w_bf, b):
    n = lay.n
    row = lambda w: pl.BlockSpec((TM, w), lambda i: (i, 0))
    out_shape = (jax.ShapeDtypeStruct((n, NA_WIDTH), BF16),
                 jax.ShapeDtypeStruct((n, NA_WIDTH), F32), jax.ShapeDtypeStruct((n, NA_WIDTH), F32),
                 jax.ShapeDtypeStruct((n, NA_WIDTH), BF16), jax.ShapeDtypeStruct((n, NA_WIDTH), BF16),
                 jax.ShapeDtypeStruct((n, B_COLS), F32), jax.ShapeDtypeStruct((n, C_COLS), F32),
                 jax.ShapeDtypeStruct((n, G_COLS), BF16))
    return pl.pallas_call(
        _inproj_kernel,
        out_shape=out_shape,
        grid=(lay.tiles,),
        in_specs=[row(D_MODEL),
                  pl.BlockSpec((1, N_MOD, D_MODEL), lambda i: (lay.mod_row(i), 0, 0)),
                  pl.BlockSpec((1, D_MODEL), lambda i: (0, 0)),
                  pl.BlockSpec((D_MODEL, P_IN), lambda i: (0, 0), pipeline_mode=pl.Buffered(1)),
                  pl.BlockSpec((1, P_IN), lambda i: (0, 0))],
        out_specs=(row(NA_WIDTH), row(NA_WIDTH), row(NA_WIDTH), row(NA_WIDTH), row(NA_WIDTH),
                   row(B_COLS), row(C_COLS), row(G_COLS)),
        compiler_params=_cparams(("arbitrary",)),
        name="in_proj",
    )(x, mod, g0, w_bf, b)


def _ctx_attn_kernel(q_ref, k_ref, v_ref, o_ref):
    scale = NA_HEAD_DIM ** -0.5
    lo = lax.broadcasted_iota(jnp.int32, (q_ref.shape[0], LANES), 1) < NA_HEAD_DIM
    heads = []
    for h in range(NA_HEADS):
        ps = slice((h // 2) * LANES, (h // 2 + 1) * LANES)
        qp = q_ref[:, ps] * scale
        qh = jnp.where(lo if h % 2 == 0 else jnp.logical_not(lo), qp, jnp.zeros_like(qp))
        heads.append(dict(ps=ps, s=_dot(qh, k_ref[:, ps], _NT)))
    for u in heads:
        u['p'] = jnp.exp(u['s'] - jnp.max(u['s'], axis=-1, keepdims=True))
    for u in heads:
        u['o'] = _dot(u['p'].astype(BF16), v_ref[:, u['ps']]) / jnp.sum(u['p'], axis=-1, keepdims=True)
    for p in range(NA_HEADS // 2):
        o_ref[:, p * LANES:(p + 1) * LANES] = jnp.where(lo, heads[2 * p]['o'], heads[2 * p + 1]['o']).astype(BF16)


def ctx_attention(lay, q, kb, vb):
    blk = pl.BlockSpec((lay.ctx_len, NA_WIDTH), lambda b: (b, 0))
    return pl.pallas_call(
        _ctx_attn_kernel,
        out_shape=jax.ShapeDtypeStruct((lay.ctx_tokens, NA_WIDTH), BF16),
        grid=(lay.n_ctx,),
        in_specs=[blk, blk, blk],
        out_specs=blk,
        compiler_params=_cparams(("arbitrary",)),
        name="ctx_attention",
    )(q, kb, vb)


def na_bias_table(rpb):
    kw, kh = NA_WIN_COLS, NA_WIN_ROWS
    col = jnp.arange(GRID_W, dtype=jnp.int32)
    col_start = jnp.clip(col - kw // 2, 0, GRID_W - kw)
    col_ok = (col[None, :] >= col_start[:, None]) & (col[None, :] < col_start[:, None] + kw)
    dc = jnp.clip(col[None, :] - col[:, None] + kw - 1, 0, 2 * kw - 2)
    bias_c = jnp.where(col_ok, rpb[:, :, dc].astype(F32), NEG_INF)
    tabs = [jnp.concatenate([bias_c[:, d + i] for i in range(kh)], axis=-1) for d in range(kh)]
    return jnp.stack(tabs, axis=0)


def _na_row_start(r, rows):
    return jnp.clip(r - NA_WIN_ROWS // 2, 0, rows - NA_WIN_ROWS)


def _na_kernel(q_ref, k_ref, v_ref, ck_ref, cv_ref, bias_ref, o_ref, *, rows):
    r = pl.program_id(1)
    rs = _na_row_start(r, rows)
    n_loc = NA_WIN_ROWS * GRID_W
    start = pl.multiple_of(rs * GRID_W, GRID_W)
    scale = NA_HEAD_DIM ** -0.5
    lo = lax.broadcasted_iota(jnp.int32, (GRID_W, LANES), 1) < NA_HEAD_DIM
    heads = []
    for h in range(NA_HEADS):
        ps = slice((h // 2) * LANES, (h // 2 + 1) * LANES)
        qp = q_ref[:, ps] * scale
        qh = jnp.where(lo if h % 2 == 0 else jnp.logical_not(lo), qp, jnp.zeros_like(qp))
        s_loc = _dot(qh, k_ref[pl.ds(start, n_loc), ps], _NT) + bias_ref[0, h]
        s_ctx = _dot(qh, ck_ref[0, :, ps], _NT)
        heads.append(dict(ps=ps, s_loc=s_loc, s_ctx=s_ctx))
    for u in heads:
        u['m'] = jnp.maximum(jnp.max(u['s_loc'], axis=-1, keepdims=True),
                             jnp.max(u['s_ctx'], axis=-1, keepdims=True))
    for u in heads:
        u['p_loc'] = jnp.exp(u['s_loc'] - u['m'])
        u['p_ctx'] = jnp.exp(u['s_ctx'] - u['m'])
        u['den'] = jnp.sum(u['p_loc'], axis=-1, keepdims=True) + jnp.sum(u['p_ctx'], axis=-1, keepdims=True)
    for u in heads:
        o = (_dot(u['p_loc'].astype(BF16), v_ref[pl.ds(start, n_loc), u['ps']])
             + _dot(u['p_ctx'].astype(BF16), cv_ref[0, :, u['ps']]))
        u['o'] = o / u['den']
    for p in range(NA_HEADS // 2):
        o_ref[:, p * LANES:(p + 1) * LANES] = jnp.where(lo, heads[2 * p]['o'], heads[2 * p + 1]['o']).astype(BF16)


def na_attention(lay, q, kb, vb, ck, cv, bias_tab):
    rows = lay.dec_len // GRID_W
    assert rows >= NA_WIN_ROWS
    qoff = lay.ctx_tokens // GRID_W
    koff = lay.ctx_tokens // lay.dec_len
    assert lay.ctx_tokens % lay.dec_len == 0
    kv = pl.BlockSpec((lay.dec_len, NA_WIDTH), lambda b, r: (koff + b, 0))
    cs = pl.BlockSpec((1, ck.shape[1], NA_WIDTH), lambda b, r: (b, 0, 0))
    return pl.pallas_call(
        functools.partial(_na_kernel, rows=rows),
        out_shape=jax.ShapeDtypeStruct((lay.dec_tokens, NA_WIDTH), BF16),
        grid=(lay.n_dec, rows),
        in_specs=[pl.BlockSpec((GRID_W, NA_WIDTH), lambda b, r: (qoff + b * rows + r, 0)),
                  kv, kv, cs, cs,
                  pl.BlockSpec((1, NA_HEADS, GRID_W, NA_WIN_ROWS * GRID_W),
                               lambda b, r: (_na_row_start(r, rows) - r + NA_WIN_ROWS - 1, 0, 0, 0))],
        out_specs=pl.BlockSpec((GRID_W, NA_WIDTH), lambda b, r: (b * rows + r, 0)),
        compiler_params=_cparams(("arbitrary", "arbitrary")),
        name="na_attention",
    )(q, kb, vb, ck, cv, bias_tab)


def _softplus(x):
    return jnp.maximum(x, 0.0) + jnp.log(1.0 + jnp.exp(-jnp.abs(x)))


def _rwkv_prepare(d, z, prev_row, next_row, prm):
    (mu_ref, w0_ref, w2_ref, a0_ref, a2_ref, kk_ref, ka_ref, rk_ref) = prm
    c = CHUNK
    cw = RW_WIDTH
    row = lax.broadcasted_iota(jnp.int32, (c, 1), 0)
    prev = jnp.where(row == 0, prev_row, pltpu.roll(z, 1, axis=0))
    nxt = jnp.where(row == c - 1, next_row, pltpu.roll(z, c - 1, axis=0))
    zs = z + mu_ref[0:1, :] * (prev - z) + mu_ref[1:2, :] * (nxt - z)
    r, k, v = zs[:, :cw], zs[:, cw:2 * cw], zs[:, 2 * cw:3 * cw]
    o = 3 * cw
    zw = zs[:, o + d * RW_DECAY_RANK:o + (d + 1) * RW_DECAY_RANK]
    o += 2 * RW_DECAY_RANK
    za = zs[:, o + d * RW_ICLR_RANK:o + (d + 1) * RW_ICLR_RANK]

    w_log = -_softplus(-(w0_ref[d:d + 1, :] + _mm3(jnp.tanh(zw), w2_ref[d]))) - 0.5
    logw = -jnp.exp(w_log)
    a = _sigmoid(a0_ref[d:d + 1, :] + _mm3(za, a2_ref[d]))
    kd = k * (1.0 + (a - 1.0) * ka_ref[...])
    kk = k * kk_ref[...]

    ti = lax.broadcasted_iota(jnp.int32, (c, c), 0)
    tj = lax.broadcasted_iota(jnp.int32, (c, c), 1)
    tri = ((ti >= tj) if d == 0 else (tj >= ti)).astype(BF16)
    lh, ll = _split(logw)
    cl = _dot(tri, lh) + _dot(tri, ll)
    p_in = jnp.exp(cl)
    p_ex = jnp.exp(cl - logw)
    p_inv = jnp.exp(-cl)
    p_end = p_in[c - 1:c, :] if d == 0 else p_in[0:1, :]

    pi = lax.broadcasted_iota(jnp.int32, (LANES, LANES), 0)
    pj = lax.broadcasted_iota(jnp.int32, (LANES, LANES), 1)
    same_head = (pi // RW_HEAD_DIM) == (pj // RW_HEAD_DIM)
    ones_bd = same_head.astype(BF16)

    def head_sum(x):
        xh, xl = _split(x)
        return _dot(xh, ones_bd) + _dot(xl, ones_bd)

    t2 = lax.broadcasted_iota(jnp.int32, (2 * c, LANES), 0) % c
    j2 = lax.broadcasted_iota(jnp.int32, (2 * c, LANES), 1) % c
    diff = (t2 - j2) if d == 0 else (j2 - t2)
    upper = lax.broadcasted_iota(jnp.int32, (2 * c, LANES), 0) < c
    mask = diff >= jnp.where(upper, 1, 0)
    units = []
    for p in range(RW_HEADS // 2):
        sl = slice(p * LANES, (p + 1) * LANES)
        kkp = kk[:, sl]
        kkp = kkp * lax.rsqrt(jnp.maximum(head_sum(kkp * kkp), 1e-24))
        pinv = p_inv[:, sl]
        bt = a[:, sl] * kkp * pinv
        kt = kd[:, sl] * pinv
        ar = jnp.concatenate([-kkp * p_ex[:, sl], r[:, sl] * p_in[:, sl]], axis=0).astype(BF16)
        pe = p_end[:, sl]
        bonus = head_sum(r[:, sl] * kd[:, sl] * rk_ref[:, sl]) * v[:, sl]
        units.append(dict(d=d, p=p, ar=ar, bt=bt, kt=kt, v=v[:, sl], pe=pe,
                          bkp=jnp.concatenate([bt * pe, kt * pe], axis=0), bonus=bonus,
                          mask=mask, eye=(diff[:c] == 0).astype(F32), bd=same_head))
    return units, zs


def _bd(x):
    lo = lax.broadcasted_iota(jnp.int32, x.shape, 1) < RW_HEAD_DIM
    xb = x.astype(BF16)
    z = jnp.zeros_like(xb)
    return jnp.concatenate([jnp.where(lo, xb, z), jnp.where(lo, z, xb)], axis=0)


def _rwkv_solve(units, s_ref):
    c = CHUNK
    for u in units:
        gb = _dot(u['ar'], _bd(u['bt']), _NT)
        gk = _dot(u['ar'], _bd(u['kt']), _NT)
        gb = jnp.where(u['mask'], gb, 0.0)
        u['a_ab'], u['a_rb'] = gb[:c], gb[c:]
        u['s0'] = s_ref[u['d'], u['p']]
        u['xs'] = _dot(u['ar'], u['s0'].astype(BF16), _NT) + _mm(jnp.where(u['mask'], gk, 0.0), _bd(u['v']))
    for u in units:
        u['pw'] = _mm(u['a_ab'], _bd(u['a_ab']))
        u['tinv'] = u['eye'] + u['a_ab']
    n_sq = CHUNK.bit_length() - 2
    for i in range(n_sq):
        for u in units:
            if i < n_sq - 1:
                both = _mm(jnp.concatenate([u['pw'], u['tinv']], axis=0), _bd(u['pw']))
                u['pw'], u['tinv'] = both[:c], u['tinv'] + both[c:]
            else:
                u['tinv'] = u['tinv'] + _mm(u['tinv'], _bd(u['pw']))
    for u in units:
        u['u'] = _mm(u['tinv'], _bd(u['xs'][:c]))
    ys = []
    for u in units:
        ys.append(u['xs'][c:] + _mm(u['a_rb'], _bd(u['u'])) + u['bonus'])
        uv = jnp.concatenate([u['u'], u['v']], axis=0)
        s_ref[u['d'], u['p']] = u['s0'] * u['pe'] + jnp.where(u['bd'], _mm(uv.T, u['bkp']), 0.0)
    return ys


def _rwkv_kernel(*refs, nc, has_s0):
    zf_ref, zfp_ref, zfn_ref, zr_ref, zrp_ref, zrn_ref = refs[:6]
    n_in = 6 + (1 if has_s0 else 0)
    s0_ref = refs[6] if has_s0 else None
    prm = refs[n_in:n_in + 8]
    g2_ref = refs[n_in + 8]
    yf_ref, yb_ref, g_ref, sfin_ref, s_ref = refs[n_in + 9:]
    i = pl.program_id(1)

    hd = RW_HEAD_DIM
    npair = RW_HEADS // 2

    @pl.when(i == 0)
    def _():
        s_ref[...] = jnp.zeros(s_ref.shape, F32)
        if has_s0:
            for d in range(2):
                for p in range(npair):
                    s_ref[d, p, :hd, :hd] = s0_ref[0, d, 2 * p]
                    s_ref[d, p, hd:, hd:] = s0_ref[0, d, 2 * p + 1]

    zero = jnp.zeros((1, B_COLS), F32)
    pf = jnp.where(i == 0, zero, zfp_ref[7:8, :])
    nf = jnp.where(i == nc - 1, zero, zfn_ref[0:1, :])
    uf, zs = _rwkv_prepare(0, zf_ref[...], pf, nf, prm)
    g_ref[...] = _mm(_sigmoid(zs[:, B_COLS - RW_GATE_RANK:]), g2_ref[...])
    pr = jnp.where(i == nc - 1, zero, zrp_ref[7:8, :])
    nr = jnp.where(i == 0, zero, zrn_ref[0:1, :])
    ub, _ = _rwkv_prepare(1, zr_ref[...], pr, nr, prm)
    ys = _rwkv_solve(uf + ub, s_ref)
    yf_ref[...] = jnp.concatenate(ys[:npair], axis=-1)
    yb_ref[...] = jnp.concatenate(ys[npair:], axis=-1)

    @pl.when(i == nc - 1)
    def _():
        for d in range(2):
            for p in range(npair):
                sfin_ref[0, d, 2 * p] = s_ref[d, p, :hd, :hd]
                sfin_ref[0, d, 2 * p + 1] = s_ref[d, p, hd:, hd:]


def rwkv_scan(zb, tok0, nseq, seqlen, s0, prm, g2):
    n = zb.shape[0]
    nc = seqlen // CHUNK
    sub = CHUNK // 8
    base = tok0 // CHUNK
    nt = nseq * seqlen
    has_s0 = s0 is not None

    def cf(s, i):
        return base + s * nc + i

    def cr(s, i):
        return base + s * nc + (nc - 1 - i)

    def specs(cidx):
        return [pl.BlockSpec((CHUNK, B_COLS), lambda s, i: (cidx(s, i), 0)),
                pl.BlockSpec((8, B_COLS), lambda s, i: (jnp.maximum(cidx(s, i) * sub - 1, 0), 0)),
                pl.BlockSpec((8, B_COLS), lambda s, i: (jnp.minimum((cidx(s, i) + 1) * sub, n // 8 - 1), 0))]

    full = lambda a: pl.BlockSpec(a.shape, lambda s, i: (0,) * a.ndim)
    sshape = (2, RW_HEADS, RW_HEAD_DIM, RW_HEAD_DIM)
    in_specs = specs(cf) + specs(cr)
    args = [zb] * 6
    if has_s0:
        in_specs.append(pl.BlockSpec((1,) + sshape, lambda s, i: (s, 0, 0, 0, 0)))
        args.append(s0)
    in_specs += [full(p) for p in prm] + [full(g2)]
    args += list(prm) + [g2]
    out = lambda cidx: pl.BlockSpec((CHUNK, RW_WIDTH), lambda s, i: (cidx(s, i) - base, 0))
    tok = jax.ShapeDtypeStruct((nt, RW_WIDTH), F32)
    return pl.pallas_call(
        functools.partial(_rwkv_kernel, nc=nc, has_s0=has_s0),
        out_shape=(tok, tok, tok, jax.ShapeDtypeStruct((nseq,) + sshape, F32)),
        grid=(nseq, nc),
        in_specs=in_specs,
        out_specs=(out(cf), out(cr), out(cf), pl.BlockSpec((1,) + sshape, lambda s, i: (s, 0, 0, 0, 0))),
        scratch_shapes=[pltpu.VMEM((2, RW_HEADS // 2, LANES, LANES), F32)],
        compiler_params=_cparams(("arbitrary", "arbitrary")),
        name="rwkv_scan",
    )(*args)


def _conv_kernel(z_ref, zp_ref, zn_ref, w_ref, b_ref, g_ref, be_ref, o_ref, h_ref, *, lay):
    i = pl.program_id(0)
    first, last = lay.tile_pos(i)

    def glu(z):
        return z[:, :CONV_WIDTH] * _sigmoid(z[:, CONV_WIDTH:])

    hz = jnp.zeros((CONV_HALO, CONV_WIDTH), F32)
    h_ref[0:CONV_HALO, :] = jnp.where(first, hz, glu(zp_ref[...]))
    h_ref[CONV_HALO:CONV_HALO + TM, :] = glu(z_ref[...])
    h_ref[CONV_HALO + TM:, :] = jnp.where(last, hz, glu(zn_ref[...]))
    off = CONV_HALO - CONV_K // 2
    acc = jnp.zeros((TM, CONV_WIDTH), F32) + b_ref[...]
    for j in range(CONV_K):
        acc = acc + w_ref[j:j + 1, :] * h_ref[off + j:off + j + TM, :]
    xc = acc - jnp.mean(acc, axis=-1, keepdims=True)
    hn = xc * lax.rsqrt(jnp.mean(xc * xc, axis=-1, keepdims=True) + LN_EPS) * g_ref[...] + be_ref[...]
    o_ref[...] = (hn * _sigmoid(hn)).astype(BF16)


def conv_module(lay, zc, w, b, g, be):
    n = lay.n
    hb = TM // CONV_HALO
    vec = pl.BlockSpec((1, CONV_WIDTH), lambda i: (0, 0))
    return pl.pallas_call(
        functools.partial(_conv_kernel, lay=lay),
        out_shape=jax.ShapeDtypeStruct((n, CONV_WIDTH), BF16),
        grid=(lay.tiles,),
        in_specs=[pl.BlockSpec((TM, C_COLS), lambda i: (i, 0)),
                  pl.BlockSpec((CONV_HALO, C_COLS), lambda i: (jnp.maximum(i * hb - 1, 0), 0)),
                  pl.BlockSpec((CONV_HALO, C_COLS), lambda i: (jnp.minimum((i + 1) * hb, n // CONV_HALO - 1), 0)),
                  pl.BlockSpec((CONV_K, CONV_WIDTH), lambda i: (0, 0)), vec, vec, vec],
        out_specs=pl.BlockSpec((TM, CONV_WIDTH), lambda i: (i, 0)),
        scratch_shapes=[pltpu.VMEM((TM + 2 * CONV_HALO, CONV_WIDTH), F32)],
        compiler_params=_cparams(("arbitrary",)),
        name="conv_module",
    )(zc, zc, zc, w, b, g, be)


def _merge_kernel(x_ref, oa_c_ref, yf_c_ref, yb_c_ref, g_c_ref, oa_d_ref, yf_d_ref, yb_d_ref, g_d_ref,
                  oc_ref, gt_ref, mod_ref, ng_ref, lg_ref, lb_ref,
                  wa_ref, wr_ref, wc_ref, wo_ref, rw_ref, rb_ref,
                  x1_ref, hm_ref, ti_ref, tg_ref, rk_ref, cnt_ref, run_ref, *, ctx_tiles):
    is_ctx = pl.program_id(0) < ctx_tiles
    pick = lambda c_ref, d_ref: jnp.where(is_ctx, c_ref[...], d_ref[...])
    oa = pick(oa_c_ref, oa_d_ref)
    g_rw = pick(g_c_ref, g_d_ref)
    y = pick(yf_c_ref, yf_d_ref) + pick(yb_c_ref, yb_d_ref)
    pi = lax.broadcasted_iota(jnp.int32, (LANES, LANES), 0)
    pj = lax.broadcasted_iota(jnp.int32, (LANES, LANES), 1)
    ones_bd = ((pi // RW_HEAD_DIM) == (pj // RW_HEAD_DIM)).astype(BF16)

    def head_mean(v):
        vh, vl = _split(v)
        return (_dot(vh, ones_bd) + _dot(vl, ones_bd)) * (1.0 / RW_HEAD_DIM)

    outs = []
    for p in range(RW_HEADS // 2):
        yp = y[:, p * LANES:(p + 1) * LANES]
        yc = yp - head_mean(yp)
        outs.append(yc * lax.rsqrt(head_mean(yc * yc) + GN_EPS))
    o_rw = (jnp.concatenate(outs, axis=-1) * lg_ref[...] + lb_ref[...]) * g_rw
    merged = (gt_ref[:, :D_MODEL].astype(F32) * _dot(oa, wa_ref[...])
              + gt_ref[:, D_MODEL:2 * D_MODEL].astype(F32) * _mm(o_rw, wr_ref[...])
              + gt_ref[:, 2 * D_MODEL:].astype(F32) * _dot(oc_ref[...], wc_ref[...]))
    mix = _mm(merged, wo_ref[...])
    x1 = x_ref[...] + mod_ref[0, 2:3, :] * _rms(mix, ng_ref[1:2, :])
    x1_ref[...] = x1
    hm = _rms(x1, ng_ref[2:3, :]) * (1.0 + mod_ref[0, 4:5, :]) + mod_ref[0, 3:4, :]
    hm_ref[...] = hm
    logits = _mm3(hm, rw_ref[...]) + rb_ref[...]
    lane = lax.broadcasted_iota(jnp.int32, logits.shape, 1)
    idx_out = jnp.zeros(logits.shape, jnp.int32)
    val_out = jnp.full(logits.shape, NEG_INF, F32)
    picks = []
    for j in range(TOP_K):
        m = jnp.max(logits, axis=-1, keepdims=True)
        idx = jnp.min(jnp.where(logits == m, lane, LANES), axis=-1, keepdims=True)
        idx_out = jnp.where(lane == j, idx, idx_out)
        val_out = jnp.where(lane == j, m, val_out)
        picks.append(lane == idx)
        logits = jnp.where(picks[-1], -jnp.inf, logits)
    e = jnp.exp(val_out - jnp.max(val_out, axis=-1, keepdims=True))
    ti_ref[...] = idx_out
    tg_ref[...] = e / jnp.sum(e, axis=-1, keepdims=True)

    @pl.when(pl.program_id(0) == 0)
    def _():
        run_ref[...] = jnp.zeros(run_ref.shape, F32)

    chosen = jnp.zeros(logits.shape, F32)
    for pk in picks:
        chosen = chosen + pk.astype(F32)
    ti_ = lax.broadcasted_iota(jnp.int32, (TM, TM), 0)
    tj_ = lax.broadcasted_iota(jnp.int32, (TM, TM), 1)
    before = run_ref[...] + _dot((ti_ > tj_).astype(BF16), chosen.astype(BF16))
    rank = jnp.zeros(logits.shape, F32)
    for j, pk in enumerate(picks):
        rank = jnp.where(lane == j, jnp.sum(jnp.where(pk, before, 0.0), axis=-1, keepdims=True), rank)
    rk_ref[...] = rank.astype(jnp.int32)
    run_ref[...] = run_ref[...] + jnp.sum(chosen, axis=0, keepdims=True)
    cnt_ref[...] = run_ref[...]


def merge_router(lay, x, ctx_br, dec_br, o_cv, gates, mod, ng, lg, lb, wa, wr, wc, wo, rw, rb):
    n = lay.n
    row = lambda w: pl.BlockSpec((TM, w), lambda i: (i, 0))
    crow = lambda w: pl.BlockSpec((TM, w), lambda i: (jnp.minimum(i, lay.ctx_tiles - 1), 0))
    drow = lambda w: pl.BlockSpec((TM, w), lambda i: (jnp.maximum(i - lay.ctx_tiles, 0), 0))
    full = lambda a: pl.BlockSpec(a.shape, lambda i: (0,) * a.ndim)
    br_w = (NA_WIDTH, RW_WIDTH, RW_WIDTH, RW_WIDTH)
    return pl.pallas_call(
        functools.partial(_merge_kernel, ctx_tiles=lay.ctx_tiles),
        out_shape=(jax.ShapeDtypeStruct((n, D_MODEL), F32), jax.ShapeDtypeStruct((n, D_MODEL), F32),
                   jax.ShapeDtypeStruct((n, LANES), jnp.int32), jax.ShapeDtypeStruct((n, LANES), F32),
                   jax.ShapeDtypeStruct((n, LANES), jnp.int32), jax.ShapeDtypeStruct((1, LANES), F32)),
        grid=(lay.tiles,),
        in_specs=[row(D_MODEL)] + [crow(w) for w in br_w] + [drow(w) for w in br_w] + [
                  row(CONV_WIDTH),
                  row(G_COLS), pl.BlockSpec((1, N_MOD, D_MODEL), lambda i: (lay.mod_row(i), 0, 0)),
                  full(ng), full(lg), full(lb), full(wa), full(wr), full(wc), full(wo), full(rw), full(rb)],
        out_specs=(row(D_MODEL), row(D_MODEL), row(LANES), row(LANES), row(LANES),
                   pl.BlockSpec((1, LANES), lambda i: (0, 0))),
        scratch_shapes=[pltpu.VMEM((1, LANES), F32)],
        compiler_params=_cparams(("arbitrary",)),
        name="merge_router",
    )(x, *ctx_br, *dec_br, o_cv, gates, mod, ng, lg, lb, wa, wr, wc, wo, rw, rb)


def route_slots(top_i, rank, counts, n):
    nk = n * TOP_K
    counts = counts[0, :N_EXPERTS].astype(jnp.int32)
    padded = ((counts + ROUTE_BLOCK - 1) // ROUTE_BLOCK) * ROUTE_BLOCK
    pad_end = jnp.cumsum(padded)
    pad_start = pad_end - padded
    onehot = top_i[:, :, None] == jnp.arange(N_EXPERTS, dtype=jnp.int32)[None, None, :]
    dest = (jnp.sum(jnp.where(onehot, pad_start[None, None, :], 0), axis=-1) + rank).reshape(-1)
    n_blocks = -(-nk // ROUTE_BLOCK) + N_EXPERTS
    block_e = jnp.minimum(jnp.searchsorted(pad_end, jnp.arange(n_blocks, dtype=jnp.int32) * ROUTE_BLOCK,
                                           side='right'), N_EXPERTS - 1).astype(jnp.int32)
    n_used = (pad_end[-1] // ROUTE_BLOCK).astype(jnp.int32).reshape(1)
    return dest.astype(jnp.int32), block_e, n_used, n_blocks


def _row_copy_wait(src_rows, dst_rows, sem):
    pltpu.make_async_copy(src_rows, dst_rows, sem).wait()


def _dispatch_kernel(dest_ref, hm_ref, xb_in_ref, xb_ref, sem):
    del xb_in_ref

    def issue(t, carry):
        for j in range(TOP_K):
            d = dest_ref[0, 0, t * TOP_K + j]
            pltpu.make_async_copy(hm_ref.at[pl.ds(t, 1), :], xb_ref.at[pl.ds(d, 1), :], sem).start()
        return carry

    lax.fori_loop(0, TM, issue, 0)
    for j in range(TOP_K):
        _row_copy_wait(hm_ref, xb_ref.at[pl.ds(0, TM), :], sem)


def moe_dispatch(lay, hm, dest, xb0):
    n_rows = xb0.shape[0]
    return pl.pallas_call(
        _dispatch_kernel,
        out_shape=jax.ShapeDtypeStruct((n_rows, D_MODEL), F32),
        grid=(lay.tiles,),
        in_specs=[pl.BlockSpec((1, 1, TM * TOP_K), lambda i: (i, 0, 0), memory_space=pltpu.SMEM),
                  pl.BlockSpec((TM, D_MODEL), lambda i: (i, 0)),
                  pl.BlockSpec(memory_space=pl.ANY)],
        out_specs=pl.BlockSpec(memory_space=pl.ANY),
        scratch_shapes=[pltpu.SemaphoreType.DMA],
        input_output_aliases={2: 0},
        compiler_params=_cparams(("arbitrary",)),
        name="moe_dispatch",
    )(dest.reshape(lay.tiles, 1, TM * TOP_K), hm, xb0)


GL_GROUP = 2 * LANES


def _regroup_perm():
    r = jnp.arange(GL_GROUP, dtype=jnp.int32)[:, None]
    c = jnp.arange(GL_GROUP, dtype=jnp.int32)[None, :]
    src = jnp.where(c < LANES, 2 * c, 2 * (c - LANES) + 1)
    return (r == src).astype(BF16)


def _w1_regroup_kernel(w_ref, p_ref, o_ref):
    for j in range(2 * EXPERT_FF // GL_GROUP):
        cs = slice(j * GL_GROUP, (j + 1) * GL_GROUP)
        o_ref[0, :, cs] = _dot(w_ref[0, :, cs].astype(BF16), p_ref[...]).astype(BF16)


def w1_regroup(w1, l):
    return pl.pallas_call(
        _w1_regroup_kernel,
        out_shape=jax.ShapeDtypeStruct(w1.shape[1:], BF16),
        grid=(w1.shape[1],),
        in_specs=[pl.BlockSpec((None, 1) + w1.shape[2:], lambda e: (l, e, 0, 0)),
                  pl.BlockSpec((GL_GROUP, GL_GROUP), lambda e: (0, 0))],
        out_specs=pl.BlockSpec((1,) + w1.shape[2:], lambda e: (e, 0, 0)),
        compiler_params=_cparams(("arbitrary",)),
        name="w1_regroup",
    )(w1, _regroup_perm())


def _expert_kernel(be_ref, nu_ref, x_ref, w1_ref, b1_ref, w2_ref, b2_ref, y_ref):
    b = pl.program_id(0)

    @pl.when(b < nu_ref[0])
    def _():
        x = x_ref[...].astype(BF16)
        acc = jnp.zeros((ROUTE_BLOCK, D_MODEL), F32) + b2_ref[0]
        for j in range(EXPERT_FF // GL_GROUP):
            acts = []
            for g in range(2):
                c0 = (2 * j + g) * GL_GROUP
                h = _dot(x, w1_ref[0, :, c0:c0 + GL_GROUP]) + b1_ref[0, :, c0:c0 + GL_GROUP]
                hg = jnp.minimum(h[:, :LANES], SWIGLU_LIMIT)
                hl = jnp.clip(h[:, LANES:], -SWIGLU_LIMIT, SWIGLU_LIMIT)
                acts.append(hg * _sigmoid(SWIGLU_ALPHA * hg) * (hl + 1.0))
            act = jnp.concatenate(acts, axis=-1).astype(BF16)
            acc = acc + _dot(act, w2_ref[0, j * GL_GROUP:(j + 1) * GL_GROUP, :])
        y_ref[...] = acc

    @pl.when(b >= nu_ref[0])
    def _():
        y_ref[...] = jnp.zeros(y_ref.shape, F32)


def moe_experts(xb, block_e, n_used, n_blocks, w1, b1, w2, b2):
    def xmap(b, be, nu):
        return (jnp.minimum(b, jnp.maximum(nu[0] - 1, 0)), 0)

    def wmap(b, be, nu):
        return (be[jnp.minimum(b, jnp.maximum(nu[0] - 1, 0))], 0, 0)

    wspec = lambda a: pl.BlockSpec((1,) + a.shape[1:], wmap)
    return pl.pallas_call(
        _expert_kernel,
        out_shape=jax.ShapeDtypeStruct(xb.shape, F32),
        grid_spec=pltpu.PrefetchScalarGridSpec(
            num_scalar_prefetch=2,
            grid=(n_blocks,),
            in_specs=[pl.BlockSpec((ROUTE_BLOCK, D_MODEL), xmap),
                      wspec(w1), wspec(b1), wspec(w2), wspec(b2)],
            out_specs=pl.BlockSpec((ROUTE_BLOCK, D_MODEL), lambda b, be, nu: (b, 0))),
        compiler_params=_cparams(("arbitrary",)),
        name="moe_experts",
    )(block_e, n_used, xb, w1, b1, w2, b2)


def _combine_kernel(dest_ref, yb_ref, tg_ref, x1_ref, mod_ref, ng_ref, o_ref, buf_ref, sem):
    def issue(t, carry):
        for j in range(TOP_K):
            d = dest_ref[0, 0, t * TOP_K + j]
            pltpu.make_async_copy(yb_ref.at[pl.ds(d, 1), :], buf_ref.at[j, pl.ds(t, 1), :], sem).start()
        return carry

    lax.fori_loop(0, TC, issue, 0)
    for j in range(TOP_K):
        _row_copy_wait(yb_ref.at[pl.ds(0, TC), :], buf_ref.at[j], sem)
    y = jnp.zeros((TC, D_MODEL), F32)
    for j in range(TOP_K):
        y = y + tg_ref[:, j:j + 1] * buf_ref[j]
    o_ref[...] = x1_ref[...] + mod_ref[0, 5:6, :] * _rms(y, ng_ref[3:4, :])


def moe_combine(lay, yb, dest, tg, x1, mod, ng):
    n = lay.n
    per = TM // TC
    row = lambda w: pl.BlockSpec((TC, w), lambda i: (i, 0))
    return pl.pallas_call(
        _combine_kernel,
        out_shape=jax.ShapeDtypeStruct((n, D_MODEL), F32),
        grid=(n // TC,),
        in_specs=[pl.BlockSpec((1, 1, TC * TOP_K), lambda i: (i, 0, 0), memory_space=pltpu.SMEM),
                  pl.BlockSpec(memory_space=pl.ANY),
                  row(LANES), row(D_MODEL),
                  pl.BlockSpec((1, N_MOD, D_MODEL), lambda i: (lay.mod_row(i // per), 0, 0)),
                  pl.BlockSpec(ng.shape, lambda i: (0, 0))],
        out_specs=row(D_MODEL),
        scratch_shapes=[pltpu.VMEM((TOP_K, TC, D_MODEL), F32), pltpu.SemaphoreType.DMA],
        compiler_params=_cparams(("arbitrary",)),
        name="moe_combine",
    )(dest.reshape(n // TC, 1, TC * TOP_K), yb, tg, x1, mod, ng)


def layer(lay, x, l, cond, P, ctx_k, ctx_v, s0_dec, xb_buf):
    mod = mod_table(cond, P['mod_w'][l], P['mod_b'][l]).reshape(cond.shape[0], N_MOD, D_MODEL)
    ng = P['norm_g'][l]
    q, kf, vf, kb, vb, zb, zc, gates = in_proj(lay, x, mod, ng[0:1], P['w_in'][l].astype(BF16),
                                               P['b_in'][l].reshape(1, P_IN))
    oa_c = ctx_attention(lay, q, kb, vb)
    oa_d = na_attention(lay, q, kb, vb, ctx_k.astype(BF16), ctx_v.astype(BF16), na_bias_table(P['rpb'][l]))
    vec = lambda a: a.reshape(1, -1)
    prm = (P['rw_mu'][l], P['rw_w0'][l], P['rw_w2'][l], P['rw_a0'][l], P['rw_a2'][l],
           vec(P['rw_k_k'][l]), vec(P['rw_k_a'][l]), vec(P['rw_r_k'][l]))
    yf_c, yb_c, g_c, s_ctx = rwkv_scan(zb, 0, lay.n_ctx, lay.ctx_len, None, prm, P['rw_g2'][l])
    yf_d, yb_d, g_d, _ = rwkv_scan(zb, lay.ctx_tokens, lay.n_dec, lay.dec_len, s0_dec, prm, P['rw_g2'][l])
    o_cv = conv_module(lay, zc, P['conv_w'][l], vec(P['conv_b'][l]), vec(P['conv_ln_g'][l]),
                       vec(P['conv_ln_b'][l]))
    rw = jnp.zeros((D_MODEL, LANES), F32).at[:, :N_EXPERTS].set(P['router_w'][l])
    rb = jnp.full((1, LANES), NEG_INF, F32).at[0, :N_EXPERTS].set(P['router_b'][l])
    x1, hm, top_i, top_g, rank, counts = merge_router(
        lay, x, (oa_c, yf_c, yb_c, g_c), (oa_d, yf_d, yb_d, g_d), o_cv, gates, mod, ng,
        vec(P['rw_lnx_g'][l]), vec(P['rw_lnx_b'][l]),
        P['w_o_attn'][l].astype(BF16), P['w_o_rwkv'][l].astype(BF16), P['w_o_conv'][l].astype(BF16),
        P['w_out'][l].astype(BF16), rw, rb)
    dest, block_e, n_used, n_blocks = route_slots(top_i[:, :TOP_K], rank[:, :TOP_K], counts, lay.n)
    if xb_buf is None:
        xb_buf = jnp.zeros((n_blocks * ROUTE_BLOCK, D_MODEL), F32)
    xb = moe_dispatch(lay, hm, dest, xb_buf)
    b1 = P['exp_b1'][l].reshape(N_EXPERTS, -1, LANES, 2).transpose(0, 1, 3, 2).reshape(N_EXPERTS, 1, -1)
    ybk = moe_experts(xb, block_e, n_used, n_blocks, w1_regroup(P['exp_w1'], l), b1,
                      P['exp_w2'][l].astype(BF16), P['exp_b2'][l][:, None, :])
    x2 = moe_combine(lay, ybk, dest, top_g, x1, mod, ng)
    return x2, kf, vf, s_ctx, xb


def kernel(x_prompt, x_sample, cache_attn_k, cache_attn_v, state_rwkv, c, c_ctx, mod_w, mod_b, norm_g, w_in, b_in, rpb, w_o_attn, rw_mu, rw_w0, rw_w2, rw_a0, rw_a2, rw_g2, rw_k_k, rw_k_a, rw_r_k, rw_lnx_g, rw_lnx_b, w_o_rwkv, conv_w, conv_b, conv_ln_g, conv_ln_b, w_o_conv, w_out, router_w, router_b, exp_w1, exp_b1, exp_w2, exp_b2):
    P = {
        'mod_w': mod_w, 'mod_b': mod_b, 'norm_g': norm_g, 'w_in': w_in, 'b_in': b_in, 'rpb': rpb,
        'w_o_attn': w_o_attn, 'rw_mu': rw_mu, 'rw_w0': rw_w0, 'rw_w2': rw_w2, 'rw_a0': rw_a0,
        'rw_a2': rw_a2, 'rw_g2': rw_g2, 'rw_k_k': rw_k_k, 'rw_k_a': rw_k_a, 'rw_r_k': rw_r_k,
        'rw_lnx_g': rw_lnx_g, 'rw_lnx_b': rw_lnx_b, 'w_o_rwkv': w_o_rwkv, 'conv_w': conv_w,
        'conv_b': conv_b, 'conv_ln_g': conv_ln_g, 'conv_ln_b': conv_ln_b, 'w_o_conv': w_o_conv,
        'w_out': w_out, 'router_w': router_w, 'router_b': router_b, 'exp_w1': exp_w1,
        'exp_b1': exp_b1, 'exp_w2': exp_w2, 'exp_b2': exp_b2,
    }
    nb, sl, _ = x_prompt.shape
    db, ds, _ = x_sample.shape
    lay = Layout(nb, sl, db, ds)
    x = jnp.concatenate([x_prompt.reshape(-1, D_MODEL), x_sample.reshape(-1, D_MODEL)], axis=0)
    cond = jnp.zeros((8, D_MODEL), F32).at[0].set(c_ctx).at[1:1 + db].set(c)
    ks, vs, ss = [], [], []
    xb_buf = None
    for l in range(DEPTH):
        ck = cache_attn_k[:, l].reshape(db, -1, NA_WIDTH)
        cv = cache_attn_v[:, l].reshape(db, -1, NA_WIDTH)
        x, kf, vf, s_ctx, xb_buf = layer(lay, x, l, cond, P, ck, cv, state_rwkv[:, l], xb_buf)
        ks.append(kf[:lay.ctx_tokens].reshape(nb, sl, NA_HEADS, NA_HEAD_DIM))
        vs.append(vf[:lay.ctx_tokens].reshape(nb, sl, NA_HEADS, NA_HEAD_DIM))
        ss.append(s_ctx)
    y_prompt = x[:lay.ctx_tokens].reshape(nb, sl, D_MODEL)
    y_sample = x[lay.ctx_tokens:].reshape(db, ds, D_MODEL)
    return (y_prompt, y_sample, jnp.stack(ks, axis=1), jnp.stack(vs, axis=1), jnp.stack(ss, axis=1))
```

```python
import functools

import jax
import jax.numpy as jnp
from jax import lax
from jax.experimental import pallas as pl
from jax.experimental.pallas import tpu as pltpu

F32 = jnp.float32
BF16 = jnp.bfloat16

D_MODEL = 1024
BATCH, SEQ = 32, 256
DEPTH = 2
DEC_BATCH, DEC_SEQ = 4, 4096
PAST_LEN = 256
GRID_W = 64
NA_HEADS, NA_HEAD_DIM = 8, 64
NA_WIDTH = NA_HEADS * NA_HEAD_DIM
NA_WIN_ROWS, NA_WIN_COLS = 8, 16
RW_HEADS, RW_HEAD_DIM = 8, 64
RW_WIDTH = RW_HEADS * RW_HEAD_DIM
RW_DECAY_RANK, RW_ICLR_RANK, RW_GATE_RANK = 64, 64, 128
CONV_WIDTH, CONV_K = 512, 31
N_BRANCH, N_MOD = 3, 6
N_EXPERTS, TOP_K, EXPERT_FF = 32, 4, 1024
SWIGLU_LIMIT, SWIGLU_ALPHA = 7.0, 1.702
ROUTE_BLOCK = 512
RMS_EPS, LN_EPS, GN_EPS = 1e-6, 1e-5, 64e-5
NEG_INF = -1e30
A_COLS = 3 * NA_WIDTH
B_COLS = 3 * RW_WIDTH + 2 * RW_DECAY_RANK + 2 * RW_ICLR_RANK + RW_GATE_RANK
C_COLS = 2 * CONV_WIDTH
G_COLS = N_BRANCH * D_MODEL
P_IN = A_COLS + B_COLS + C_COLS + G_COLS

LANES = 128
TM = 256
CHUNK = 64
CONV_HALO = 16
TC = 256
VMEM_LIMIT = 56 * 1024 * 1024


def _cparams(sem):
    return pltpu.CompilerParams(dimension_semantics=sem, vmem_limit_bytes=VMEM_LIMIT)


def _dot(a, b, dims=((1,), (0,))):
    return lax.dot_general(a, b, (dims, ((), ())), preferred_element_type=F32)


def _mm(a, b, dims=((1,), (0,))):
    return _dot(a.astype(BF16), b.astype(BF16), dims)


def _split(a):
    hi = a.astype(BF16)
    lo = (a - hi.astype(F32)).astype(BF16)
    return hi, lo


def _mm3(a, b, dims=((1,), (0,))):
    ah, al = _split(a)
    bh, bl = _split(b)
    return _dot(ah, bh, dims) + (_dot(ah, bl, dims) + _dot(al, bh, dims))


_NT = ((1,), (1,))


def _rms(x, g):
    return x * lax.rsqrt(jnp.mean(x * x, axis=-1, keepdims=True) + RMS_EPS) * g


def _sigmoid(x):
    return 1.0 / (1.0 + jnp.exp(-x))


class Layout:
    def __init__(self, n_ctx, ctx_len, n_dec, dec_len):
        assert ctx_len == TM and dec_len % TM == 0
        self.n_ctx, self.ctx_len, self.n_dec, self.dec_len = n_ctx, ctx_len, n_dec, dec_len
        self.ctx_tokens = n_ctx * ctx_len
        self.dec_tokens = n_dec * dec_len
        self.n = self.ctx_tokens + self.dec_tokens
        self.ctx_tiles = self.ctx_tokens // TM
        self.dec_tiles_per_seq = dec_len // TM
        self.tiles = self.n // TM

    def mod_row(self, i):
        return jnp.where(i < self.ctx_tiles, 0, 1 + (i - self.ctx_tiles) // self.dec_tiles_per_seq)

    def tile_pos(self, i):
        j = (i - self.ctx_tiles) % self.dec_tiles_per_seq
        is_ctx = i < self.ctx_tiles
        return is_ctx | (j == 0), is_ctx | (j == self.dec_tiles_per_seq - 1)


def _mod_kernel(c_ref, w_ref, b_ref, o_ref):
    c = c_ref[...]
    o_ref[...] = _mm3(c * _sigmoid(c), w_ref[...]) + b_ref[...]


def mod_table(cond, w, b):
    tn = 1024
    nm = w.shape[1]
    return pl.pallas_call(
        _mod_kernel,
        out_shape=jax.ShapeDtypeStruct((cond.shape[0], nm), F32),
        grid=(nm // tn,),
        in_specs=[pl.BlockSpec(cond.shape, lambda j: (0, 0)),
                  pl.BlockSpec((D_MODEL, tn), lambda j: (0, j)),
                  pl.BlockSpec((1, tn), lambda j: (0, j))],
        out_specs=pl.BlockSpec((cond.shape[0], tn), lambda j: (0, j)),
        compiler_params=_cparams(("arbitrary",)),
        name="mod_table",
    )(cond, w, b.reshape(1, nm))


def _inproj_kernel(x_ref, mod_ref, g_ref, w_ref, b_ref,
                   q_ref, kf_ref, vf_ref, kb_ref, vb_ref, zb_ref, zc_ref, gt_ref):
    u = (_rms(x_ref[...], g_ref[...]) * (1.0 + mod_ref[0, 1:2, :]) + mod_ref[0, 0:1, :]).astype(BF16)

    def seg(c0, width):
        return _dot(u, w_ref[:, c0:c0 + width]) + b_ref[:, c0:c0 + width]

    q_ref[...] = seg(0, NA_WIDTH).astype(BF16)
    k = seg(NA_WIDTH, NA_WIDTH)
    kf_ref[...] = k
    kb_ref[...] = k.astype(BF16)
    v = seg(2 * NA_WIDTH, NA_WIDTH)
    vf_ref[...] = v
    vb_ref[...] = v.astype(BF16)
    wb = 384
    for j in range(B_COLS // wb):
        zb_ref[:, j * wb:(j + 1) * wb] = seg(A_COLS + j * wb, wb)
    wc = 512
    for j in range(C_COLS // wc):
        zc_ref[:, j * wc:(j + 1) * wc] = seg(A_COLS + B_COLS + j * wc, wc)
    for j in range(G_COLS // wc):
        gt_ref[:, j * wc:(j + 1) * wc] = _sigmoid(seg(A_COLS + B_COLS + C_COLS + j * wc, wc)).astype(BF16)


def in_proj(lay, x, mod, g0, w_bf, b):
    n = lay.n
    row = lambda w: pl.BlockSpec((TM, w), lambda i: (i, 0))
    out_shape = (jax.ShapeDtypeStruct((n, NA_WIDTH), BF16),
                 jax.ShapeDtypeStruct((n, NA_WIDTH), F32), jax.ShapeDtypeStruct((n, NA_WIDTH), F32),
                 jax.ShapeDtypeStruct((n, NA_WIDTH), BF16), jax.ShapeDtypeStruct((n, NA_WIDTH), BF16),
                 jax.ShapeDtypeStruct((n, B_COLS), F32), jax.ShapeDtypeStruct((n, C_COLS), F32),
                 jax.ShapeDtypeStruct((n, G_COLS), BF16))
    return pl.pallas_call(
        _inproj_kernel,
        out_shape=out_shape,
        grid=(lay.tiles,),
        in_specs=[row(D_MODEL),
                  pl.BlockSpec((1, N_MOD, D_MODEL), lambda i: (lay.mod_row(i), 0, 0)),
                  pl.BlockSpec((1, D_MODEL), lambda i: (0, 0)),
                  pl.BlockSpec((D_MODEL, P_IN), lambda i: (0, 0), pipeline_mode=pl.Buffered(1)),
                  pl.BlockSpec((1, P_IN), lambda i: (0, 0))],
        out_specs=(row(NA_WIDTH), row(NA_WIDTH), row(NA_WIDTH), row(NA_WIDTH), row(NA_WIDTH),
                   row(B_COLS), row(C_COLS), row(G_COLS)),
        compiler_params=_cparams(("arbitrary",)),
        name="in_proj",
    )(x, mod, g0, w_bf, b)


def _ctx_attn_kernel(q_ref, k_ref, v_ref, o_ref):
    scale = NA_HEAD_DIM ** -0.5
    lo = lax.broadcasted_iota(jnp.int32, (q_ref.shape[0], LANES), 1) < NA_HEAD_DIM
    heads = []
    for h in range(NA_HEADS):
        ps = slice((h // 2) * LANES, (h // 2 + 1) * LANES)
        qp = q_ref[:, ps] * scale
        qh = jnp.where(lo if h % 2 == 0 else jnp.logical_not(lo), qp, jnp.zeros_like(qp))
        heads.append(dict(ps=ps, s=_dot(qh, k_ref[:, ps], _NT)))
    for u in heads:
        u['p'] = jnp.exp(u['s'] - jnp.max(u['s'], axis=-1, keepdims=True))
    for u in heads:
        u['o'] = _dot(u['p'].astype(BF16), v_ref[:, u['ps']]) / jnp.sum(u['p'], axis=-1, keepdims=True)
    for p in range(NA_HEADS // 2):
        o_ref[:, p * LANES:(p + 1) * LANES] = jnp.where(lo, heads[2 * p]['o'], heads[2 * p + 1]['o']).astype(BF16)


def ctx_attention(lay, q, kb, vb):
    blk = pl.BlockSpec((lay.ctx_len, NA_WIDTH), lambda b: (b, 0))
    return pl.pallas_call(
        _ctx_attn_kernel,
        out_shape=jax.ShapeDtypeStruct((lay.ctx_tokens, NA_WIDTH), BF16),
        grid=(lay.n_ctx,),
        in_specs=[blk, blk, blk],
        out_specs=blk,
        compiler_params=_cparams(("arbitrary",)),
        name="ctx_attention",
    )(q, kb, vb)


def na_bias_table(rpb):
    kw, kh = NA_WIN_COLS, NA_WIN_ROWS
    col = jnp.arange(GRID_W, dtype=jnp.int32)
    col_start = jnp.clip(col - kw // 2, 0, GRID_W - kw)
    col_ok = (col[None, :] >= col_start[:, None]) & (col[None, :] < col_start[:, None] + kw)
    dc = jnp.clip(col[None, :] - col[:, None] + kw - 1, 0, 2 * kw - 2)
    bias_c = jnp.where(col_ok, rpb[:, :, dc].astype(F32), NEG_INF)
    tabs = [jnp.concatenate([bias_c[:, d + i] for i in range(kh)], axis=-1) for d in range(kh)]
    return jnp.stack(tabs, axis=0)


def _na_row_start(r, rows):
    return jnp.clip(r - NA_WIN_ROWS // 2, 0, rows - NA_WIN_ROWS)


def _na_kernel(q_ref, k_ref, v_ref, ck_ref, cv_ref, bias_ref, o_ref, *, rows):
    r = pl.program_id(1)
    rs = _na_row_start(r, rows)
    n_loc = NA_WIN_ROWS * GRID_W
    start = pl.multiple_of(rs * GRID_W, GRID_W)
    scale = NA_HEAD_DIM ** -0.5
    lo = lax.broadcasted_iota(jnp.int32, (GRID_W, LANES), 1) < NA_HEAD_DIM
    heads = []
    for h in range(NA_HEADS):
        ps = slice((h // 2) * LANES, (h // 2 + 1) * LANES)
        qp = q_ref[:, ps] * scale
        qh = jnp.where(lo if h % 2 == 0 else jnp.logical_not(lo), qp, jnp.zeros_like(qp))
        s_loc = _dot(qh, k_ref[pl.ds(start, n_loc), ps], _NT) + bias_ref[0, h]
        s_ctx = _dot(qh, ck_ref[0, :, ps], _NT)
        heads.append(dict(ps=ps, s_loc=s_loc, s_ctx=s_ctx))
    for u in heads:
        u['m'] = jnp.maximum(jnp.max(u['s_loc'], axis=-1, keepdims=True),
                             jnp.max(u['s_ctx'], axis=-1, keepdims=True))
    for u in heads:
        u['p_loc'] = jnp.exp(u['s_loc'] - u['m'])
        u['p_ctx'] = jnp.exp(u['s_ctx'] - u['m'])
        u['den'] = jnp.sum(u['p_loc'], axis=-1, keepdims=True) + jnp.sum(u['p_ctx'], axis=-1, keepdims=True)
    for u in heads:
        o = (_dot(u['p_loc'].astype(BF16), v_ref[pl.ds(start, n_loc), u['ps']])
             + _dot(u['p_ctx'].astype(BF16), cv_ref[0, :, u['ps']]))
        u['o'] = o / u['den']
    for p in range(NA_HEADS // 2):
        o_ref[:, p * LANES:(p + 1) * LANES] = jnp.where(lo, heads[2 * p]['o'], heads[2 * p + 1]['o']).astype(BF16)


def na_attention(lay, q, kb, vb, ck, cv, bias_tab):
    rows = lay.dec_len // GRID_W
    assert rows >= NA_WIN_ROWS
    qoff = lay.ctx_tokens // GRID_W
    koff = lay.ctx_tokens // lay.dec_len
    assert lay.ctx_tokens % lay.dec_len == 0
    kv = pl.BlockSpec((lay.dec_len, NA_WIDTH), lambda b, r: (koff + b, 0))
    cs = pl.BlockSpec((1, ck.shape[1], NA_WIDTH), lambda b, r: (b, 0, 0))
    return pl.pallas_call(
        functools.partial(_na_kernel, rows=rows),
        out_shape=jax.ShapeDtypeStruct((lay.dec_tokens, NA_WIDTH), BF16),
        grid=(lay.n_dec, rows),
        in_specs=[pl.BlockSpec((GRID_W, NA_WIDTH), lambda b, r: (qoff + b * rows + r, 0)),
                  kv, kv, cs, cs,
                  pl.BlockSpec((1, NA_HEADS, GRID_W, NA_WIN_ROWS * GRID_W),
                               lambda b, r: (_na_row_start(r, rows) - r + NA_WIN_ROWS - 1, 0, 0, 0))],
        out_specs=pl.BlockSpec((GRID_W, NA_WIDTH), lambda b, r: (b * rows + r, 0)),
        compiler_params=_cparams(("arbitrary", "arbitrary")),
        name="na_attention",
    )(q, kb, vb, ck, cv, bias_tab)


def _softplus(x):
    return jnp.maximum(x, 0.0) + jnp.log(1.0 + jnp.exp(-jnp.abs(x)))


def _rwkv_prepare(d, z, prev_row, next_row, prm):
    (mu_ref, w0_ref, w2_ref, a0_ref, a2_ref, kk_ref, ka_ref, rk_ref) = prm
    c = CHUNK
    cw = RW_WIDTH
    row = lax.broadcasted_iota(jnp.int32, (c, 1), 0)
    prev = jnp.where(row == 0, prev_row, pltpu.roll(z, 1, axis=0))
    nxt = jnp.where(row == c - 1, next_row, pltpu.roll(z, c - 1, axis=0))
    zs = z + mu_ref[0:1, :] * (prev - z) + mu_ref[1:2, :] * (nxt - z)
    r, k, v = zs[:, :cw], zs[:, cw:2 * cw], zs[:, 2 * cw:3 * cw]
    o = 3 * cw
    zw = zs[:, o + d * RW_DECAY_RANK:o + (d + 1) * RW_DECAY_RANK]
    o += 2 * RW_DECAY_RANK
    za = zs[:, o + d * RW_ICLR_RANK:o + (d + 1) * RW_ICLR_RANK]

    w_log = -_softplus(-(w0_ref[d:d + 1, :] + _mm3(jnp.tanh(zw), w2_ref[d]))) - 0.5
    logw = -jnp.exp(w_log)
    a = _sigmoid(a0_ref[d:d + 1, :] + _mm3(za, a2_ref[d]))
    kd = k * (1.0 + (a - 1.0) * ka_ref[...])
    kk = k * kk_ref[...]

    ti = lax.broadcasted_iota(jnp.int32, (c, c), 0)
    tj = lax.broadcasted_iota(jnp.int32, (c, c), 1)
    tri = ((ti >= tj) if d == 0 else (tj >= ti)).astype(BF16)
    lh, ll = _split(logw)
    cl = _dot(tri, lh) + _dot(tri, ll)
    p_in = jnp.exp(cl)
    p_ex = jnp.exp(cl - logw)
    p_inv = jnp.exp(-cl)
    p_end = p_in[c - 1:c, :] if d == 0 else p_in[0:1, :]

    pi = lax.broadcasted_iota(jnp.int32, (LANES, LANES), 0)
    pj = lax.broadcasted_iota(jnp.int32, (LANES, LANES), 1)
    same_head = (pi // RW_HEAD_DIM) == (pj // RW_HEAD_DIM)
    ones_bd = same_head.astype(BF16)

    def head_sum(x):
        xh, xl = _split(x)
        return _dot(xh, ones_bd) + _dot(xl, ones_bd)

    t2 = lax.broadcasted_iota(jnp.int32, (2 * c, LANES), 0) % c
    j2 = lax.broadcasted_iota(jnp.int32, (2 * c, LANES), 1) % c
    diff = (t2 - j2) if d == 0 else (j2 - t2)
    upper = lax.broadcasted_iota(jnp.int32, (2 * c, LANES), 0) < c
    mask = diff >= jnp.where(upper, 1, 0)
    units = []
    for p in range(RW_HEADS // 2):
        sl = slice(p * LANES, (p + 1) * LANES)
        kkp = kk[:, sl]
        kkp = kkp * lax.rsqrt(jnp.maximum(head_sum(kkp * kkp), 1e-24))
        pinv = p_inv[:, sl]
        bt = a[:, sl] * kkp * pinv
        kt = kd[:, sl] * pinv
        ar = jnp.concatenate([-kkp * p_ex[:, sl], r[:, sl] * p_in[:, sl]], axis=0).astype(BF16)
        pe = p_end[:, sl]
        bonus = head_sum(r[:, sl] * kd[:, sl] * rk_ref[:, sl]) * v[:, sl]
        units.append(dict(d=d, p=p, ar=ar, bt=bt, kt=kt, v=v[:, sl], pe=pe,
                          bkp=jnp.concatenate([bt * pe, kt * pe], axis=0), bonus=bonus,
                          mask=mask, eye=(diff[:c] == 0).astype(F32), bd=same_head))
    return units, zs


def _bd(x):
    lo = lax.broadcasted_iota(jnp.int32, x.shape, 1) < RW_HEAD_DIM
    xb = x.astype(BF16)
    z = jnp.zeros_like(xb)
    return jnp.concatenate([jnp.where(lo, xb, z), jnp.where(lo, z, xb)], axis=0)


def _rwkv_solve(units, s_ref):
    c = CHUNK
    for u in units:
        gb = _dot(u['ar'], _bd(u['bt']), _NT)
        gk = _dot(u['ar'], _bd(u['kt']), _NT)
        gb = jnp.where(u['mask'], gb, 0.0)
        u['a_ab'], u['a_rb'] = gb[:c], gb[c:]
        u['s0'] = s_ref[u['d'], u['p']]
        u['xs'] = _dot(u['ar'], u['s0'].astype(BF16), _NT) + _mm(jnp.where(u['mask'], gk, 0.0), _bd(u['v']))
    for u in units:
        u['pw'] = _mm(u['a_ab'], _bd(u['a_ab']))
        u['tinv'] = u['eye'] + u['a_ab']
    n_sq = CHUNK.bit_length() - 2
    for i in range(n_sq):
        for u in units:
            if i < n_sq - 1:
                both = _mm(jnp.concatenate([u['pw'], u['tinv']], axis=0), _bd(u['pw']))
                u['pw'], u['tinv'] = both[:c], u['tinv'] + both[c:]
            else:
                u['tinv'] = u['tinv'] + _mm(u['tinv'], _bd(u['pw']))
    for u in units:
        u['u'] = _mm(u['tinv'], _bd(u['xs'][:c]))
    ys = []
    for u in units:
        ys.append(u['xs'][c:] + _mm(u['a_rb'], _bd(u['u'])) + u['bonus'])
        uv = jnp.concatenate([u['u'], u['v']], axis=0)
        s_ref[u['d'], u['p']] = u['s0'] * u['pe'] + jnp.where(u['bd'], _mm(uv.T, u['bkp']), 0.0)
    return ys


RW_SEQS = 2


def _rwkv_kernel(*refs, nc, has_s0):
    zf_ref, zfp_ref, zfn_ref, zr_ref, zrp_ref, zrn_ref = refs[:6]
    n_in = 6 + (1 if has_s0 else 0)
    s0_ref = refs[6] if has_s0 else None
    prm = refs[n_in:n_in + 8]
    g2_ref = refs[n_in + 8]
    yf_ref, yb_ref, g_ref, sfin_ref, s_ref = refs[n_in + 9:]
    i = pl.program_id(1)

    hd = RW_HEAD_DIM
    npair = RW_HEADS // 2

    @pl.when(i == 0)
    def _():
        s_ref[...] = jnp.zeros(s_ref.shape, F32)
        if has_s0:
            for k in range(RW_SEQS):
                for d in range(2):
                    for p in range(npair):
                        s_ref[2 * k + d, p, :hd, :hd] = s0_ref[k, d, 2 * p]
                        s_ref[2 * k + d, p, hd:, hd:] = s0_ref[k, d, 2 * p + 1]

    zero = jnp.zeros((1, B_COLS), F32)
    units = []
    for k in range(RW_SEQS):
        pf = jnp.where(i == 0, zero, zfp_ref[k, 7:8, :])
        nf = jnp.where(i == nc - 1, zero, zfn_ref[k, 0:1, :])
        uf, zs = _rwkv_prepare(0, zf_ref[k], pf, nf, prm)
        g_ref[k] = _mm(_sigmoid(zs[:, B_COLS - RW_GATE_RANK:]), g2_ref[...])
        pr = jnp.where(i == nc - 1, zero, zrp_ref[k, 7:8, :])
        nr = jnp.where(i == 0, zero, zrn_ref[k, 0:1, :])
        ub, _ = _rwkv_prepare(1, zr_ref[k], pr, nr, prm)
        for u in uf + ub:
            u['d'] = 2 * k + u['d']
        units += uf + ub
    ys = _rwkv_solve(units, s_ref)
    for k in range(RW_SEQS):
        yk = ys[2 * npair * k:2 * npair * (k + 1)]
        yf_ref[k] = jnp.concatenate(yk[:npair], axis=-1)
        yb_ref[k] = jnp.concatenate(yk[npair:], axis=-1)

    @pl.when(i == nc - 1)
    def _():
        for k in range(RW_SEQS):
            for d in range(2):
                for p in range(npair):
                    sfin_ref[k, d, 2 * p] = s_ref[2 * k + d, p, :hd, :hd]
                    sfin_ref[k, d, 2 * p + 1] = s_ref[2 * k + d, p, hd:, hd:]


def rwkv_scan(zb, seq0, nseq, s0, prm, g2):
    seqlen = zb.shape[1]
    ns = RW_SEQS
    assert seq0 % ns == 0 and nseq % ns == 0 and seqlen % CHUNK == 0
    nc = seqlen // CHUNK
    sub = CHUNK // 8
    b0 = seq0 // ns
    has_s0 = s0 is not None

    def specs(cidx):
        return [pl.BlockSpec((ns, CHUNK, B_COLS), lambda s, i: (b0 + s, cidx(i), 0)),
                pl.BlockSpec((ns, 8, B_COLS), lambda s, i: (b0 + s, jnp.maximum(cidx(i) * sub - 1, 0), 0)),
                pl.BlockSpec((ns, 8, B_COLS),
                             lambda s, i: (b0 + s, jnp.minimum((cidx(i) + 1) * sub, seqlen // 8 - 1), 0))]

    cf = lambda i: i
    cr = lambda i: nc - 1 - i
    full = lambda a: pl.BlockSpec(a.shape, lambda s, i: (0,) * a.ndim)
    sshape = (2, RW_HEADS, RW_HEAD_DIM, RW_HEAD_DIM)
    sspec = pl.BlockSpec((ns,) + sshape, lambda s, i: (s, 0, 0, 0, 0))
    in_specs = specs(cf) + specs(cr)
    args = [zb] * 6
    if has_s0:
        in_specs.append(sspec)
        args.append(s0)
    in_specs += [full(p) for p in prm] + [full(g2)]
    args += list(prm) + [g2]
    out = lambda cidx: pl.BlockSpec((ns, CHUNK, RW_WIDTH), lambda s, i: (s, cidx(i), 0))
    tok = jax.ShapeDtypeStruct((nseq, seqlen, RW_WIDTH), F32)
    yf, yb, g, sfin = pl.pallas_call(
        functools.partial(_rwkv_kernel, nc=nc, has_s0=has_s0),
        out_shape=(tok, tok, tok, jax.ShapeDtypeStruct((nseq,) + sshape, F32)),
        grid=(nseq // ns, nc),
        in_specs=in_specs,
        out_specs=(out(cf), out(cr), out(cf), sspec),
        scratch_shapes=[pltpu.VMEM((2 * ns, RW_HEADS // 2, LANES, LANES), F32)],
        compiler_params=_cparams(("arbitrary", "arbitrary")),
        name="rwkv_scan",
    )(*args)
    flat = lambda a: a.reshape(nseq * seqlen, RW_WIDTH)
    return flat(yf), flat(yb), flat(g), sfin


def _conv_kernel(z_ref, zp_ref, zn_ref, w_ref, b_ref, g_ref, be_ref, o_ref, h_ref, *, lay):
    i = pl.program_id(0)
    first, last = lay.tile_pos(i)

    def glu(z):
        return z[:, :CONV_WIDTH] * _sigmoid(z[:, CONV_WIDTH:])

    hz = jnp.zeros((CONV_HALO, CONV_WIDTH), F32)
    h_ref[0:CONV_HALO, :] = jnp.where(first, hz, glu(zp_ref[...]))
    h_ref[CONV_HALO:CONV_HALO + TM, :] = glu(z_ref[...])
    h_ref[CONV_HALO + TM:, :] = jnp.where(last, hz, glu(zn_ref[...]))
    off = CONV_HALO - CONV_K // 2
    acc = jnp.zeros((TM, CONV_WIDTH), F32) + b_ref[...]
    for j in range(CONV_K):
        acc = acc + w_ref[j:j + 1, :] * h_ref[off + j:off + j + TM, :]
    xc = acc - jnp.mean(acc, axis=-1, keepdims=True)
    hn = xc * lax.rsqrt(jnp.mean(xc * xc, axis=-1, keepdims=True) + LN_EPS) * g_ref[...] + be_ref[...]
    o_ref[...] = (hn * _sigmoid(hn)).astype(BF16)


def conv_module(lay, zc, w, b, g, be):
    n = lay.n
    hb = TM // CONV_HALO
    vec = pl.BlockSpec((1, CONV_WIDTH), lambda i: (0, 0))
    return pl.pallas_call(
        functools.partial(_conv_kernel, lay=lay),
        out_shape=jax.ShapeDtypeStruct((n, CONV_WIDTH), BF16),
        grid=(lay.tiles,),
        in_specs=[pl.BlockSpec((TM, C_COLS), lambda i: (i, 0)),
                  pl.BlockSpec((CONV_HALO, C_COLS), lambda i: (jnp.maximum(i * hb - 1, 0), 0)),
                  pl.BlockSpec((CONV_HALO, C_COLS), lambda i: (jnp.minimum((i + 1) * hb, n // CONV_HALO - 1), 0)),
                  pl.BlockSpec((CONV_K, CONV_WIDTH), lambda i: (0, 0)), vec, vec, vec],
        out_specs=pl.BlockSpec((TM, CONV_WIDTH), lambda i: (i, 0)),
        scratch_shapes=[pltpu.VMEM((TM + 2 * CONV_HALO, CONV_WIDTH), F32)],
        compiler_params=_cparams(("arbitrary",)),
        name="conv_module",
    )(zc, zc, zc, w, b, g, be)


def _merge_kernel(x_ref, oa_c_ref, yf_c_ref, yb_c_ref, g_c_ref, oa_d_ref, yf_d_ref, yb_d_ref, g_d_ref,
                  oc_ref, gt_ref, mod_ref, ng_ref, lg_ref, lb_ref,
                  wa_ref, wr_ref, wc_ref, wo_ref, rw_ref, rb_ref,
                  x1_ref, hm_ref, ti_ref, tg_ref, rk_ref, cnt_ref, run_ref, *, ctx_tiles):
    is_ctx = pl.program_id(0) < ctx_tiles
    pick = lambda c_ref, d_ref: jnp.where(is_ctx, c_ref[...], d_ref[...])
    oa = pick(oa_c_ref, oa_d_ref)
    g_rw = pick(g_c_ref, g_d_ref)
    y = pick(yf_c_ref, yf_d_ref) + pick(yb_c_ref, yb_d_ref)
    pi = lax.broadcasted_iota(jnp.int32, (LANES, LANES), 0)
    pj = lax.broadcasted_iota(jnp.int32, (LANES, LANES), 1)
    ones_bd = ((pi // RW_HEAD_DIM) == (pj // RW_HEAD_DIM)).astype(BF16)

    def head_mean(v):
        vh, vl = _split(v)
        return (_dot(vh, ones_bd) + _dot(vl, ones_bd)) * (1.0 / RW_HEAD_DIM)

    outs = []
    for p in range(RW_HEADS // 2):
        yp = y[:, p * LANES:(p + 1) * LANES]
        yc = yp - head_mean(yp)
        outs.append(yc * lax.rsqrt(head_mean(yc * yc) + GN_EPS))
    o_rw = (jnp.concatenate(outs, axis=-1) * lg_ref[...] + lb_ref[...]) * g_rw
    merged = (gt_ref[:, :D_MODEL].astype(F32) * _dot(oa, wa_ref[...])
              + gt_ref[:, D_MODEL:2 * D_MODEL].astype(F32) * _mm(o_rw, wr_ref[...])
              + gt_ref[:, 2 * D_MODEL:].astype(F32) * _dot(oc_ref[...], wc_ref[...]))
    mix = _mm(merged, wo_ref[...])
    x1 = x_ref[...] + mod_ref[0, 2:3, :] * _rms(mix, ng_ref[1:2, :])
    x1_ref[...] = x1
    hm = _rms(x1, ng_ref[2:3, :]) * (1.0 + mod_ref[0, 4:5, :]) + mod_ref[0, 3:4, :]
    hm_ref[...] = hm
    logits = _mm3(hm, rw_ref[...]) + rb_ref[...]
    lane = lax.broadcasted_iota(jnp.int32, logits.shape, 1)
    idx_out = jnp.zeros(logits.shape, jnp.int32)
    val_out = jnp.full(logits.shape, NEG_INF, F32)
    picks = []
    for j in range(TOP_K):
        m = jnp.max(logits, axis=-1, keepdims=True)
        idx = jnp.min(jnp.where(logits == m, lane, LANES), axis=-1, keepdims=True)
        idx_out = jnp.where(lane == j, idx, idx_out)
        val_out = jnp.where(lane == j, m, val_out)
        picks.append(lane == idx)
        logits = jnp.where(picks[-1], -jnp.inf, logits)
    e = jnp.exp(val_out - jnp.max(val_out, axis=-1, keepdims=True))
    ti_ref[...] = idx_out
    tg_ref[...] = e / jnp.sum(e, axis=-1, keepdims=True)

    @pl.when(pl.program_id(0) == 0)
    def _():
        run_ref[...] = jnp.zeros(run_ref.shape, F32)

    chosen = jnp.zeros(logits.shape, F32)
    for pk in picks:
        chosen = chosen + pk.astype(F32)
    ti_ = lax.broadcasted_iota(jnp.int32, (TM, TM), 0)
    tj_ = lax.broadcasted_iota(jnp.int32, (TM, TM), 1)
    before = run_ref[...] + _dot((ti_ > tj_).astype(BF16), chosen.astype(BF16))
    rank = jnp.zeros(logits.shape, F32)
    for j, pk in enumerate(picks):
        rank = jnp.where(lane == j, jnp.sum(jnp.where(pk, before, 0.0), axis=-1, keepdims=True), rank)
    rk_ref[...] = rank.astype(jnp.int32)
    run_ref[...] = run_ref[...] + jnp.sum(chosen, axis=0, keepdims=True)
    cnt_ref[...] = run_ref[...]


def merge_router(lay, x, ctx_br, dec_br, o_cv, gates, mod, ng, lg, lb, wa, wr, wc, wo, rw, rb):
    n = lay.n
    row = lambda w: pl.BlockSpec((TM, w), lambda i: (i, 0))
    crow = lambda w: pl.BlockSpec((TM, w), lambda i: (jnp.minimum(i, lay.ctx_tiles - 1), 0))
    drow = lambda w: pl.BlockSpec((TM, w), lambda i: (jnp.maximum(i - lay.ctx_tiles, 0), 0))
    full = lambda a: pl.BlockSpec(a.shape, lambda i: (0,) * a.ndim)
    br_w = (NA_WIDTH, RW_WIDTH, RW_WIDTH, RW_WIDTH)
    return pl.pallas_call(
        functools.partial(_merge_kernel, ctx_tiles=lay.ctx_tiles),
        out_shape=(jax.ShapeDtypeStruct((n, D_MODEL), F32), jax.ShapeDtypeStruct((n, D_MODEL), F32),
                   jax.ShapeDtypeStruct((n, LANES), jnp.int32), jax.ShapeDtypeStruct((n, LANES), F32),
                   jax.ShapeDtypeStruct((n, LANES), jnp.int32), jax.ShapeDtypeStruct((1, LANES), F32)),
        grid=(lay.tiles,),
        in_specs=[row(D_MODEL)] + [crow(w) for w in br_w] + [drow(w) for w in br_w] + [
                  row(CONV_WIDTH),
                  row(G_COLS), pl.BlockSpec((1, N_MOD, D_MODEL), lambda i: (lay.mod_row(i), 0, 0)),
                  full(ng), full(lg), full(lb), full(wa), full(wr), full(wc), full(wo), full(rw), full(rb)],
        out_specs=(row(D_MODEL), row(D_MODEL), row(LANES), row(LANES), row(LANES),
                   pl.BlockSpec((1, LANES), lambda i: (0, 0))),
        scratch_shapes=[pltpu.VMEM((1, LANES), F32)],
        compiler_params=_cparams(("arbitrary",)),
        name="merge_router",
    )(x, *ctx_br, *dec_br, o_cv, gates, mod, ng, lg, lb, wa, wr, wc, wo, rw, rb)


def route_slots(top_i, rank, counts, n):
    nk = n * TOP_K
    counts = counts[0, :N_EXPERTS].astype(jnp.int32)
    padded = ((counts + ROUTE_BLOCK - 1) // ROUTE_BLOCK) * ROUTE_BLOCK
    pad_end = jnp.cumsum(padded)
    pad_start = pad_end - padded
    onehot = top_i[:, :, None] == jnp.arange(N_EXPERTS, dtype=jnp.int32)[None, None, :]
    dest = (jnp.sum(jnp.where(onehot, pad_start[None, None, :], 0), axis=-1) + rank).reshape(-1)
    n_blocks = -(-nk // ROUTE_BLOCK) + N_EXPERTS
    starts = jnp.arange(n_blocks, dtype=jnp.int32) * ROUTE_BLOCK
    block_e = jnp.minimum(jnp.sum((pad_end[None, :] <= starts[:, None]).astype(jnp.int32), axis=1),
                          N_EXPERTS - 1)
    n_used = (pad_end[-1] // ROUTE_BLOCK).astype(jnp.int32).reshape(1)
    return dest.astype(jnp.int32), block_e, n_used, n_blocks


def _row_copy_wait(src_rows, dst_rows, sem):
    pltpu.make_async_copy(src_rows, dst_rows, sem).wait()


def _dispatch_kernel(dest_ref, hm_ref, xb_in_ref, xb_ref, sem):
    del xb_in_ref

    def issue(t, carry):
        for j in range(TOP_K):
            d = dest_ref[0, 0, t * TOP_K + j]
            pltpu.make_async_copy(hm_ref.at[pl.ds(t, 1), :], xb_ref.at[pl.ds(d, 1), :], sem).start()
        return carry

    lax.fori_loop(0, TM, issue, 0)
    for j in range(TOP_K):
        _row_copy_wait(hm_ref, xb_ref.at[pl.ds(0, TM), :], sem)


def moe_dispatch(lay, hm, dest, xb0):
    n_rows = xb0.shape[0]
    return pl.pallas_call(
        _dispatch_kernel,
        out_shape=jax.ShapeDtypeStruct((n_rows, D_MODEL), F32),
        grid=(lay.tiles,),
        in_specs=[pl.BlockSpec((1, 1, TM * TOP_K), lambda i: (i, 0, 0), memory_space=pltpu.SMEM),
                  pl.BlockSpec((TM, D_MODEL), lambda i: (i, 0)),
                  pl.BlockSpec(memory_space=pl.ANY)],
        out_specs=pl.BlockSpec(memory_space=pl.ANY),
        scratch_shapes=[pltpu.SemaphoreType.DMA],
        input_output_aliases={2: 0},
        compiler_params=_cparams(("arbitrary",)),
        name="moe_dispatch",
    )(dest.reshape(lay.tiles, 1, TM * TOP_K), hm, xb0)


GL_GROUP = 2 * LANES


def _regroup_perm():
    r = jnp.arange(GL_GROUP, dtype=jnp.int32)[:, None]
    c = jnp.arange(GL_GROUP, dtype=jnp.int32)[None, :]
    src = jnp.where(c < LANES, 2 * c, 2 * (c - LANES) + 1)
    return (r == src).astype(BF16)


def _w1_regroup_kernel(w_ref, p_ref, o_ref):
    for j in range(2 * EXPERT_FF // GL_GROUP):
        cs = slice(j * GL_GROUP, (j + 1) * GL_GROUP)
        o_ref[0, :, cs] = _dot(w_ref[0, :, cs].astype(BF16), p_ref[...]).astype(BF16)


def w1_regroup(w1, l):
    return pl.pallas_call(
        _w1_regroup_kernel,
        out_shape=jax.ShapeDtypeStruct(w1.shape[1:], BF16),
        grid=(w1.shape[1],),
        in_specs=[pl.BlockSpec((None, 1) + w1.shape[2:], lambda e: (l, e, 0, 0)),
                  pl.BlockSpec((GL_GROUP, GL_GROUP), lambda e: (0, 0))],
        out_specs=pl.BlockSpec((1,) + w1.shape[2:], lambda e: (e, 0, 0)),
        compiler_params=_cparams(("arbitrary",)),
        name="w1_regroup",
    )(w1, _regroup_perm())


def _expert_kernel(be_ref, nu_ref, x_ref, w1_ref, b1_ref, w2_ref, b2_ref, y_ref):
    b = pl.program_id(0)

    @pl.when(b < nu_ref[0])
    def _():
        x = x_ref[...].astype(BF16)
        acc = jnp.zeros((ROUTE_BLOCK, D_MODEL), F32) + b2_ref[0]
        for j in range(EXPERT_FF // GL_GROUP):
            acts = []
            for g in range(2):
                c0 = (2 * j + g) * GL_GROUP
                h = _dot(x, w1_ref[0, :, c0:c0 + GL_GROUP]) + b1_ref[0, :, c0:c0 + GL_GROUP]
                hg = jnp.minimum(h[:, :LANES], SWIGLU_LIMIT)
                hl = jnp.clip(h[:, LANES:], -SWIGLU_LIMIT, SWIGLU_LIMIT)
                acts.append(hg * _sigmoid(SWIGLU_ALPHA * hg) * (hl + 1.0))
            act = jnp.concatenate(acts, axis=-1).astype(BF16)
            acc = acc + _dot(act, w2_ref[0, j * GL_GROUP:(j + 1) * GL_GROUP, :])
        y_ref[...] = acc

    @pl.when(b >= nu_ref[0])
    def _():
        y_ref[...] = jnp.zeros(y_ref.shape, F32)


def moe_experts(xb, block_e, n_used, n_blocks, w1, b1, w2, b2):
    def xmap(b, be, nu):
        return (jnp.minimum(b, jnp.maximum(nu[0] - 1, 0)), 0)

    def wmap(b, be, nu):
        return (be[jnp.minimum(b, jnp.maximum(nu[0] - 1, 0))], 0, 0)

    wspec = lambda a: pl.BlockSpec((1,) + a.shape[1:], wmap)
    return pl.pallas_call(
        _expert_kernel,
        out_shape=jax.ShapeDtypeStruct(xb.shape, F32),
        grid_spec=pltpu.PrefetchScalarGridSpec(
            num_scalar_prefetch=2,
            grid=(n_blocks,),
            in_specs=[pl.BlockSpec((ROUTE_BLOCK, D_MODEL), xmap),
                      wspec(w1), wspec(b1), wspec(w2), wspec(b2)],
            out_specs=pl.BlockSpec((ROUTE_BLOCK, D_MODEL), lambda b, be, nu: (b, 0))),
        compiler_params=_cparams(("arbitrary",)),
        name="moe_experts",
    )(block_e, n_used, xb, w1, b1, w2, b2)


def _combine_kernel(dest_ref, yb_ref, tg_ref, x1_ref, mod_ref, ng_ref, o_ref, buf_ref, sem):
    def issue(t, carry):
        for j in range(TOP_K):
            d = dest_ref[0, 0, t * TOP_K + j]
            pltpu.make_async_copy(yb_ref.at[pl.ds(d, 1), :], buf_ref.at[j, pl.ds(t, 1), :], sem).start()
        return carry

    lax.fori_loop(0, TC, issue, 0)
    for j in range(TOP_K):
        _row_copy_wait(yb_ref.at[pl.ds(0, TC), :], buf_ref.at[j], sem)
    y = jnp.zeros((TC, D_MODEL), F32)
    for j in range(TOP_K):
        y = y + tg_ref[:, j:j + 1] * buf_ref[j]
    o_ref[...] = x1_ref[...] + mod_ref[0, 5:6, :] * _rms(y, ng_ref[3:4, :])


def moe_combine(lay, yb, dest, tg, x1, mod, ng):
    n = lay.n
    per = TM // TC
    row = lambda w: pl.BlockSpec((TC, w), lambda i: (i, 0))
    return pl.pallas_call(
        _combine_kernel,
        out_shape=jax.ShapeDtypeStruct((n, D_MODEL), F32),
        grid=(n // TC,),
        in_specs=[pl.BlockSpec((1, 1, TC * TOP_K), lambda i: (i, 0, 0), memory_space=pltpu.SMEM),
                  pl.BlockSpec(memory_space=pl.ANY),
                  row(LANES), row(D_MODEL),
                  pl.BlockSpec((1, N_MOD, D_MODEL), lambda i: (lay.mod_row(i // per), 0, 0)),
                  pl.BlockSpec(ng.shape, lambda i: (0, 0))],
        out_specs=row(D_MODEL),
        scratch_shapes=[pltpu.VMEM((TOP_K, TC, D_MODEL), F32), pltpu.SemaphoreType.DMA],
        compiler_params=_cparams(("arbitrary",)),
        name="moe_combine",
    )(dest.reshape(n // TC, 1, TC * TOP_K), yb, tg, x1, mod, ng)


def layer(lay, x, l, cond, P, ctx_k, ctx_v, s0_dec, xb_buf):
    mod = mod_table(cond, P['mod_w'][l], P['mod_b'][l]).reshape(cond.shape[0], N_MOD, D_MODEL)
    ng = P['norm_g'][l]
    q, kf, vf, kb, vb, zb, zc, gates = in_proj(lay, x, mod, ng[0:1], P['w_in'][l].astype(BF16),
                                               P['b_in'][l].reshape(1, P_IN))
    oa_c = ctx_attention(lay, q, kb, vb)
    oa_d = na_attention(lay, q, kb, vb, ctx_k.astype(BF16), ctx_v.astype(BF16), na_bias_table(P['rpb'][l]))
    vec = lambda a: a.reshape(1, -1)
    prm = (P['rw_mu'][l], P['rw_w0'][l], P['rw_w2'][l], P['rw_a0'][l], P['rw_a2'][l],
           vec(P['rw_k_k'][l]), vec(P['rw_k_a'][l]), vec(P['rw_r_k'][l]))
    yf_c, yb_c, g_c, s_ctx = rwkv_scan(zb.reshape(-1, lay.ctx_len, B_COLS), 0, lay.n_ctx, None, prm,
                                       P['rw_g2'][l])
    yf_d, yb_d, g_d, _ = rwkv_scan(zb.reshape(-1, lay.dec_len, B_COLS), lay.ctx_tokens // lay.dec_len,
                                   lay.n_dec, s0_dec, prm, P['rw_g2'][l])
    o_cv = conv_module(lay, zc, P['conv_w'][l], vec(P['conv_b'][l]), vec(P['conv_ln_g'][l]),
                       vec(P['conv_ln_b'][l]))
    rw = jnp.zeros((D_MODEL, LANES), F32).at[:, :N_EXPERTS].set(P['router_w'][l])
    rb = jnp.full((1, LANES), NEG_INF, F32).at[0, :N_EXPERTS].set(P['router_b'][l])
    x1, hm, top_i, top_g, rank, counts = merge_router(
        lay, x, (oa_c, yf_c, yb_c, g_c), (oa_d, yf_d, yb_d, g_d), o_cv, gates, mod, ng,
        vec(P['rw_lnx_g'][l]), vec(P['rw_lnx_b'][l]),
        P['w_o_attn'][l].astype(BF16), P['w_o_rwkv'][l].astype(BF16), P['w_o_conv'][l].astype(BF16),
        P['w_out'][l].astype(BF16), rw, rb)
    dest, block_e, n_used, n_blocks = route_slots(top_i[:, :TOP_K], rank[:, :TOP_K], counts, lay.n)
    if xb_buf is None:
        xb_buf = jnp.zeros((n_blocks * ROUTE_BLOCK, D_MODEL), F32)
    xb = moe_dispatch(lay, hm, dest, xb_buf)
    b1 = P['exp_b1'][l].reshape(N_EXPERTS, -1, LANES, 2).transpose(0, 1, 3, 2).reshape(N_EXPERTS, 1, -1)
    ybk = moe_experts(xb, block_e, n_used, n_blocks, w1_regroup(P['exp_w1'], l), b1,
                      P['exp_w2'][l].astype(BF16), P['exp_b2'][l][:, None, :])
    x2 = moe_combine(lay, ybk, dest, top_g, x1, mod, ng)
    return x2, kf, vf, s_ctx, xb


def kernel(x_prompt, x_sample, cache_attn_k, cache_attn_v, state_rwkv, c, c_ctx, mod_w, mod_b, norm_g, w_in, b_in, rpb, w_o_attn, rw_mu, rw_w0, rw_w2, rw_a0, rw_a2, rw_g2, rw_k_k, rw_k_a, rw_r_k, rw_lnx_g, rw_lnx_b, w_o_rwkv, conv_w, conv_b, conv_ln_g, conv_ln_b, w_o_conv, w_out, router_w, router_b, exp_w1, exp_b1, exp_w2, exp_b2):
    P = {
        'mod_w': mod_w, 'mod_b': mod_b, 'norm_g': norm_g, 'w_in': w_in, 'b_in': b_in, 'rpb': rpb,
        'w_o_attn': w_o_attn, 'rw_mu': rw_mu, 'rw_w0': rw_w0, 'rw_w2': rw_w2, 'rw_a0': rw_a0,
        'rw_a2': rw_a2, 'rw_g2': rw_g2, 'rw_k_k': rw_k_k, 'rw_k_a': rw_k_a, 'rw_r_k': rw_r_k,
        'rw_lnx_g': rw_lnx_g, 'rw_lnx_b': rw_lnx_b, 'w_o_rwkv': w_o_rwkv, 'conv_w': conv_w,
        'conv_b': conv_b, 'conv_ln_g': conv_ln_g, 'conv_ln_b': conv_ln_b, 'w_o_conv': w_o_conv,
        'w_out': w_out, 'router_w': router_w, 'router_b': router_b, 'exp_w1': exp_w1,
        'exp_b1': exp_b1, 'exp_w2': exp_w2, 'exp_b2': exp_b2,
    }
    nb, sl, _ = x_prompt.shape
    db, ds, _ = x_sample.shape
    lay = Layout(nb, sl, db, ds)
    x = jnp.concatenate([x_prompt.reshape(-1, D_MODEL), x_sample.reshape(-1, D_MODEL)], axis=0)
    cond = jnp.zeros((8, D_MODEL), F32).at[0].set(c_ctx).at[1:1 + db].set(c)
    ks, vs, ss = [], [], []
    xb_buf = None
    for l in range(DEPTH):
        ck = cache_attn_k[:, l].reshape(db, -1, NA_WIDTH)
        cv = cache_attn_v[:, l].reshape(db, -1, NA_WIDTH)
        x, kf, vf, s_ctx, xb_buf = layer(lay, x, l, cond, P, ck, cv, state_rwkv[:, l], xb_buf)
        ks.append(kf[:lay.ctx_tokens].reshape(nb, sl, NA_HEADS, NA_HEAD_DIM))
        vs.append(vf[:lay.ctx_tokens].reshape(nb, sl, NA_HEADS, NA_HEAD_DIM))
        ss.append(s_ctx)
    y_prompt = x[:lay.ctx_tokens].reshape(nb, sl, D_MODEL)
    y_sample = x[lay.ctx_tokens:].reshape(db, ds, D_MODEL)
    return (y_prompt, y_sample, jnp.stack(ks, axis=1), jnp.stack(vs, axis=1), jnp.stack(ss, axis=1))
```

```python
import functools

import jax
import jax.numpy as jnp
from jax import lax
from jax.experimental import pallas as pl
from jax.experimental.pallas import tpu as pltpu

F32 = jnp.float32
BF16 = jnp.bfloat16

D_MODEL = 1024
BATCH, SEQ = 32, 256
DEPTH = 2
DEC_BATCH, DEC_SEQ = 4, 4096
PAST_LEN = 256
GRID_W = 64
NA_HEADS, NA_HEAD_DIM = 8, 64
NA_WIDTH = NA_HEADS * NA_HEAD_DIM
NA_WIN_ROWS, NA_WIN_COLS = 8, 16
RW_HEADS, RW_HEAD_DIM = 8, 64
RW_WIDTH = RW_HEADS * RW_HEAD_DIM
RW_DECAY_RANK, RW_ICLR_RANK, RW_GATE_RANK = 64, 64, 128
CONV_WIDTH, CONV_K = 512, 31
N_BRANCH, N_MOD = 3, 6
N_EXPERTS, TOP_K, EXPERT_FF = 32, 4, 1024
SWIGLU_LIMIT, SWIGLU_ALPHA = 7.0, 1.702
ROUTE_BLOCK = 512
RMS_EPS, LN_EPS, GN_EPS = 1e-6, 1e-5, 64e-5
NEG_INF = -1e30
A_COLS = 3 * NA_WIDTH
B_COLS = 3 * RW_WIDTH + 2 * RW_DECAY_RANK + 2 * RW_ICLR_RANK + RW_GATE_RANK
C_COLS = 2 * CONV_WIDTH
G_COLS = N_BRANCH * D_MODEL
P_IN = A_COLS + B_COLS + C_COLS + G_COLS

LANES = 128
TM = 256
CHUNK = 64
CONV_HALO = 16
VMEM_LIMIT = 56 * 1024 * 1024


def _cparams(sem):
    return pltpu.CompilerParams(dimension_semantics=sem, vmem_limit_bytes=VMEM_LIMIT)


def _dot(a, b, dims=((1,), (0,))):
    return lax.dot_general(a, b, (dims, ((), ())), preferred_element_type=F32)


def _mm(a, b, dims=((1,), (0,))):
    return _dot(a.astype(BF16), b.astype(BF16), dims)


def _split(a):
    hi = a.astype(BF16)
    lo = (a - hi.astype(F32)).astype(BF16)
    return hi, lo


def _mm3(a, b, dims=((1,), (0,))):
    ah, al = _split(a)
    bh, bl = _split(b)
    return _dot(ah, bh, dims) + (_dot(ah, bl, dims) + _dot(al, bh, dims))


_NT = ((1,), (1,))


def _rms(x, g):
    return x * lax.rsqrt(jnp.mean(x * x, axis=-1, keepdims=True) + RMS_EPS) * g


def _sigmoid(x):
    return 1.0 / (1.0 + jnp.exp(-x))


class Layout:
    def __init__(self, n_ctx, ctx_len, n_dec, dec_len):
        assert ctx_len == TM and dec_len % TM == 0
        self.n_ctx, self.ctx_len, self.n_dec, self.dec_len = n_ctx, ctx_len, n_dec, dec_len
        self.ctx_tokens = n_ctx * ctx_len
        self.dec_tokens = n_dec * dec_len
        self.n = self.ctx_tokens + self.dec_tokens
        self.ctx_tiles = self.ctx_tokens // TM
        self.dec_tiles_per_seq = dec_len // TM
        self.tiles = self.n // TM

    def mod_row(self, i):
        return jnp.where(i < self.ctx_tiles, 0, 1 + (i - self.ctx_tiles) // self.dec_tiles_per_seq)

    def tile_pos(self, i):
        j = (i - self.ctx_tiles) % self.dec_tiles_per_seq
        is_ctx = i < self.ctx_tiles
        return is_ctx | (j == 0), is_ctx | (j == self.dec_tiles_per_seq - 1)


def _mod_kernel(c_ref, w_ref, b_ref, o_ref):
    c = c_ref[...]
    o_ref[...] = _mm3(c * _sigmoid(c), w_ref[...]) + b_ref[...]


def mod_table(cond, w, b):
    tn = 1024
    nm = w.shape[1]
    return pl.pallas_call(
        _mod_kernel,
        out_shape=jax.ShapeDtypeStruct((cond.shape[0], nm), F32),
        grid=(nm // tn,),
        in_specs=[pl.BlockSpec(cond.shape, lambda j: (0, 0)),
                  pl.BlockSpec((D_MODEL, tn), lambda j: (0, j)),
                  pl.BlockSpec((1, tn), lambda j: (0, j))],
        out_specs=pl.BlockSpec((cond.shape[0], tn), lambda j: (0, j)),
        compiler_params=_cparams(("arbitrary",)),
        name="mod_table",
    )(cond, w, b.reshape(1, nm))


def _inproj_kernel(x_ref, mod_ref, g_ref, w_ref, b_ref,
                   q_ref, kf_ref, vf_ref, kb_ref, vb_ref, zb_ref, zc_ref, gt_ref):
    u = (_rms(x_ref[...], g_ref[...]) * (1.0 + mod_ref[0, 1:2, :]) + mod_ref[0, 0:1, :]).astype(BF16)

    def seg(c0, width):
        return _dot(u, w_ref[:, c0:c0 + width]) + b_ref[:, c0:c0 + width]

    q_ref[...] = seg(0, NA_WIDTH).astype(BF16)
    k = seg(NA_WIDTH, NA_WIDTH)
    kf_ref[...] = k
    kb_ref[...] = k.astype(BF16)
    v = seg(2 * NA_WIDTH, NA_WIDTH)
    vf_ref[...] = v
    vb_ref[...] = v.astype(BF16)
    wb = 384
    for j in range(B_COLS // wb):
        zb_ref[:, j * wb:(j + 1) * wb] = seg(A_COLS + j * wb, wb)
    wc = 512
    for j in range(C_COLS // wc):
        zc_ref[:, j * wc:(j + 1) * wc] = seg(A_COLS + B_COLS + j * wc, wc)
    for j in range(G_COLS // wc):
        gt_ref[:, j * wc:(j + 1) * wc] = _sigmoid(seg(A_COLS + B_COLS + C_COLS + j * wc, wc)).astype(BF16)


def in_proj(lay, x, mod, g0, w_bf, b):
    n = lay.n
    row = lambda w: pl.BlockSpec((TM, w), lambda i: (i, 0))
    out_shape = (jax.ShapeDtypeStruct((n, NA_WIDTH), BF16),
                 jax.ShapeDtypeStruct((n, NA_WIDTH), F32), jax.ShapeDtypeStruct((n, NA_WIDTH), F32),
                 jax.ShapeDtypeStruct((n, NA_WIDTH), BF16), jax.ShapeDtypeStruct((n, NA_WIDTH), BF16),
                 jax.ShapeDtypeStruct((n, B_COLS), F32), jax.ShapeDtypeStruct((n, C_COLS), F32),
                 jax.ShapeDtypeStruct((n, G_COLS), BF16))
    return pl.pallas_call(
        _inproj_kernel,
        out_shape=out_shape,
        grid=(lay.tiles,),
        in_specs=[row(D_MODEL),
                  pl.BlockSpec((1, N_MOD, D_MODEL), lambda i: (lay.mod_row(i), 0, 0)),
                  pl.BlockSpec((1, D_MODEL), lambda i: (0, 0)),
                  pl.BlockSpec((D_MODEL, P_IN), lambda i: (0, 0), pipeline_mode=pl.Buffered(1)),
                  pl.BlockSpec((1, P_IN), lambda i: (0, 0))],
        out_specs=(row(NA_WIDTH), row(NA_WIDTH), row(NA_WIDTH), row(NA_WIDTH), row(NA_WIDTH),
                   row(B_COLS), row(C_COLS), row(G_COLS)),
        compiler_params=_cparams(("arbitrary",)),
        name="in_proj",
    )(x, mod, g0, w_bf, b)


def _ctx_attn_kernel(q_ref, k_ref, v_ref, o_ref):
    scale = NA_HEAD_DIM ** -0.5
    lo = lax.broadcasted_iota(jnp.int32, (q_ref.shape[0], LANES), 1) < NA_HEAD_DIM
    heads = []
    for h in range(NA_HEADS):
        ps = slice((h // 2) * LANES, (h // 2 + 1) * LANES)
        qp = q_ref[:, ps] * scale
        qh = jnp.where(lo if h % 2 == 0 else jnp.logical_not(lo), qp, jnp.zeros_like(qp))
        heads.append(dict(ps=ps, s=_dot(qh, k_ref[:, ps], _NT)))
    for u in heads:
        u['p'] = jnp.exp(u['s'] - jnp.max(u['s'], axis=-1, keepdims=True))
    for u in heads:
        u['o'] = _dot(u['p'].astype(BF16), v_ref[:, u['ps']]) / jnp.sum(u['p'], axis=-1, keepdims=True)
    for p in range(NA_HEADS // 2):
        o_ref[:, p * LANES:(p + 1) * LANES] = jnp.where(lo, heads[2 * p]['o'], heads[2 * p + 1]['o']).astype(BF16)


def ctx_attention(lay, q, kb, vb):
    blk = pl.BlockSpec((lay.ctx_len, NA_WIDTH), lambda b: (b, 0))
    return pl.pallas_call(
        _ctx_attn_kernel,
        out_shape=jax.ShapeDtypeStruct((lay.ctx_tokens, NA_WIDTH), BF16),
        grid=(lay.n_ctx,),
        in_specs=[blk, blk, blk],
        out_specs=blk,
        compiler_params=_cparams(("arbitrary",)),
        name="ctx_attention",
    )(q, kb, vb)


def na_bias_table(rpb):
    kw, kh = NA_WIN_COLS, NA_WIN_ROWS
    col = jnp.arange(GRID_W, dtype=jnp.int32)
    col_start = jnp.clip(col - kw // 2, 0, GRID_W - kw)
    col_ok = (col[None, :] >= col_start[:, None]) & (col[None, :] < col_start[:, None] + kw)
    dc = jnp.clip(col[None, :] - col[:, None] + kw - 1, 0, 2 * kw - 2)
    bias_c = jnp.where(col_ok, rpb[:, :, dc].astype(F32), NEG_INF)
    tabs = [jnp.concatenate([bias_c[:, d + i] for i in range(kh)], axis=-1) for d in range(kh)]
    return jnp.stack(tabs, axis=0)


def _na_row_start(r, rows):
    return jnp.clip(r - NA_WIN_ROWS // 2, 0, rows - NA_WIN_ROWS)


def _na_kernel(q_ref, k_ref, v_ref, ck_ref, cv_ref, bias_ref, o_ref, *, rows):
    r = pl.program_id(1)
    rs = _na_row_start(r, rows)
    n_loc = NA_WIN_ROWS * GRID_W
    start = pl.multiple_of(rs * GRID_W, GRID_W)
    scale = NA_HEAD_DIM ** -0.5
    lo = lax.broadcasted_iota(jnp.int32, (GRID_W, LANES), 1) < NA_HEAD_DIM
    heads = []
    for h in range(NA_HEADS):
        ps = slice((h // 2) * LANES, (h // 2 + 1) * LANES)
        qp = q_ref[:, ps] * scale
        qh = jnp.where(lo if h % 2 == 0 else jnp.logical_not(lo), qp, jnp.zeros_like(qp))
        s_loc = _dot(qh, k_ref[pl.ds(start, n_loc), ps], _NT) + bias_ref[0, h]
        s_ctx = _dot(qh, ck_ref[0, :, ps], _NT)
        heads.append(dict(ps=ps, s_loc=s_loc, s_ctx=s_ctx))
    for u in heads:
        u['m'] = jnp.maximum(jnp.max(u['s_loc'], axis=-1, keepdims=True),
                             jnp.max(u['s_ctx'], axis=-1, keepdims=True))
    for u in heads:
        u['p_loc'] = jnp.exp(u['s_loc'] - u['m'])
        u['p_ctx'] = jnp.exp(u['s_ctx'] - u['m'])
        u['den'] = jnp.sum(u['p_loc'], axis=-1, keepdims=True) + jnp.sum(u['p_ctx'], axis=-1, keepdims=True)
    for u in heads:
        o = (_dot(u['p_loc'].astype(BF16), v_ref[pl.ds(start, n_loc), u['ps']])
             + _dot(u['p_ctx'].astype(BF16), cv_ref[0, :, u['ps']]))
        u['o'] = o / u['den']
    for p in range(NA_HEADS // 2):
        o_ref[:, p * LANES:(p + 1) * LANES] = jnp.where(lo, heads[2 * p]['o'], heads[2 * p + 1]['o']).astype(BF16)


def na_attention(lay, q, kb, vb, ck, cv, bias_tab):
    rows = lay.dec_len // GRID_W
    assert rows >= NA_WIN_ROWS
    qoff = lay.ctx_tokens // GRID_W
    koff = lay.ctx_tokens // lay.dec_len
    assert lay.ctx_tokens % lay.dec_len == 0
    kv = pl.BlockSpec((lay.dec_len, NA_WIDTH), lambda b, r: (koff + b, 0))
    cs = pl.BlockSpec((1, ck.shape[1], NA_WIDTH), lambda b, r: (b, 0, 0))
    return pl.pallas_call(
        functools.partial(_na_kernel, rows=rows),
        out_shape=jax.ShapeDtypeStruct((lay.dec_tokens, NA_WIDTH), BF16),
        grid=(lay.n_dec, rows),
        in_specs=[pl.BlockSpec((GRID_W, NA_WIDTH), lambda b, r: (qoff + b * rows + r, 0)),
                  kv, kv, cs, cs,
                  pl.BlockSpec((1, NA_HEADS, GRID_W, NA_WIN_ROWS * GRID_W),
                               lambda b, r: (_na_row_start(r, rows) - r + NA_WIN_ROWS - 1, 0, 0, 0))],
        out_specs=pl.BlockSpec((GRID_W, NA_WIDTH), lambda b, r: (b * rows + r, 0)),
        compiler_params=_cparams(("arbitrary", "arbitrary")),
        name="na_attention",
    )(q, kb, vb, ck, cv, bias_tab)


def _softplus(x):
    return jnp.maximum(x, 0.0) + jnp.log(1.0 + jnp.exp(-jnp.abs(x)))


def _rwkv_prepare(d, z, prev_row, next_row, prm):
    (mu_ref, w0_ref, w2_ref, a0_ref, a2_ref, kk_ref, ka_ref, rk_ref) = prm
    c = CHUNK
    cw = RW_WIDTH
    row = lax.broadcasted_iota(jnp.int32, (c, 1), 0)
    prev = jnp.where(row == 0, prev_row, pltpu.roll(z, 1, axis=0))
    nxt = jnp.where(row == c - 1, next_row, pltpu.roll(z, c - 1, axis=0))
    zs = z + mu_ref[0:1, :] * (prev - z) + mu_ref[1:2, :] * (nxt - z)
    r, k, v = zs[:, :cw], zs[:, cw:2 * cw], zs[:, 2 * cw:3 * cw]
    o = 3 * cw
    zw = zs[:, o + d * RW_DECAY_RANK:o + (d + 1) * RW_DECAY_RANK]
    o += 2 * RW_DECAY_RANK
    za = zs[:, o + d * RW_ICLR_RANK:o + (d + 1) * RW_ICLR_RANK]

    w_log = -_softplus(-(w0_ref[d:d + 1, :] + _mm3(jnp.tanh(zw), w2_ref[d]))) - 0.5
    logw = -jnp.exp(w_log)
    a = _sigmoid(a0_ref[d:d + 1, :] + _mm3(za, a2_ref[d]))
    kd = k * (1.0 + (a - 1.0) * ka_ref[...])
    kk = k * kk_ref[...]

    ti = lax.broadcasted_iota(jnp.int32, (c, c), 0)
    tj = lax.broadcasted_iota(jnp.int32, (c, c), 1)
    tri = ((ti >= tj) if d == 0 else (tj >= ti)).astype(BF16)
    lh, ll = _split(logw)
    cl = _dot(tri, lh) + _dot(tri, ll)
    p_in = jnp.exp(cl)
    p_ex = jnp.exp(cl - logw)
    p_inv = jnp.exp(-cl)
    p_end = p_in[c - 1:c, :] if d == 0 else p_in[0:1, :]

    pi = lax.broadcasted_iota(jnp.int32, (LANES, LANES), 0)
    pj = lax.broadcasted_iota(jnp.int32, (LANES, LANES), 1)
    same_head = (pi // RW_HEAD_DIM) == (pj // RW_HEAD_DIM)
    ones_bd = same_head.astype(BF16)

    def head_sum(x):
        xh, xl = _split(x)
        return _dot(xh, ones_bd) + _dot(xl, ones_bd)

    t2 = lax.broadcasted_iota(jnp.int32, (2 * c, LANES), 0) % c
    j2 = lax.broadcasted_iota(jnp.int32, (2 * c, LANES), 1) % c
    diff = (t2 - j2) if d == 0 else (j2 - t2)
    upper = lax.broadcasted_iota(jnp.int32, (2 * c, LANES), 0) < c
    mask = diff >= jnp.where(upper, 1, 0)
    units = []
    for p in range(RW_HEADS // 2):
        sl = slice(p * LANES, (p + 1) * LANES)
        kkp = kk[:, sl]
        kkp = kkp * lax.rsqrt(jnp.maximum(head_sum(kkp * kkp), 1e-24))
        pinv = p_inv[:, sl]
        bt = a[:, sl] * kkp * pinv
        kt = kd[:, sl] * pinv
        ar = jnp.concatenate([-kkp * p_ex[:, sl], r[:, sl] * p_in[:, sl]], axis=0).astype(BF16)
        pe = p_end[:, sl]
        bonus = head_sum(r[:, sl] * kd[:, sl] * rk_ref[:, sl]) * v[:, sl]
        units.append(dict(d=d, p=p, ar=ar, bt=bt, kt=kt, v=v[:, sl], pe=pe,
                          bkp=jnp.concatenate([bt * pe, kt * pe], axis=0), bonus=bonus,
                          mask=mask, eye=(diff[:c] == 0).astype(F32), bd=same_head))
    return units, zs


def _bd(x):
    lo = lax.broadcasted_iota(jnp.int32, x.shape, 1) < RW_HEAD_DIM
    xb = x.astype(BF16)
    z = jnp.zeros_like(xb)
    return jnp.concatenate([jnp.where(lo, xb, z), jnp.where(lo, z, xb)], axis=0)


def _rwkv_solve(units, s_ref):
    c = CHUNK
    for u in units:
        gb = _dot(u['ar'], _bd(u['bt']), _NT)
        gk = _dot(u['ar'], _bd(u['kt']), _NT)
        gb = jnp.where(u['mask'], gb, 0.0)
        u['a_ab'], u['a_rb'] = gb[:c], gb[c:]
        u['s0'] = s_ref[u['d'], u['p']]
        u['xs'] = _dot(u['ar'], u['s0'].astype(BF16), _NT) + _mm(jnp.where(u['mask'], gk, 0.0), _bd(u['v']))
    for u in units:
        u['pw'] = _mm(u['a_ab'], _bd(u['a_ab']))
        u['tinv'] = u['eye'] + u['a_ab']
    n_sq = CHUNK.bit_length() - 2
    for i in range(n_sq):
        for u in units:
            if i < n_sq - 1:
                both = _mm(jnp.concatenate([u['pw'], u['tinv']], axis=0), _bd(u['pw']))
                u['pw'], u['tinv'] = both[:c], u['tinv'] + both[c:]
            else:
                u['tinv'] = u['tinv'] + _mm(u['tinv'], _bd(u['pw']))
    for u in units:
        u['u'] = _mm(u['tinv'], _bd(u['xs'][:c]))
    ys = []
    for u in units:
        ys.append(u['xs'][c:] + _mm(u['a_rb'], _bd(u['u'])) + u['bonus'])
        uv = jnp.concatenate([u['u'], u['v']], axis=0)
        s_ref[u['d'], u['p']] = u['s0'] * u['pe'] + jnp.where(u['bd'], _mm(uv.T, u['bkp']), 0.0)
    return ys


RW_SEQS = 2


def _rwkv_kernel(*refs, nc, has_s0):
    zf_ref, zfp_ref, zfn_ref, zr_ref, zrp_ref, zrn_ref = refs[:6]
    n_in = 6 + (1 if has_s0 else 0)
    s0_ref = refs[6] if has_s0 else None
    prm = refs[n_in:n_in + 8]
    g2_ref = refs[n_in + 8]
    yf_ref, yb_ref, g_ref, sfin_ref, s_ref = refs[n_in + 9:]
    i = pl.program_id(1)

    hd = RW_HEAD_DIM
    npair = RW_HEADS // 2

    @pl.when(i == 0)
    def _():
        s_ref[...] = jnp.zeros(s_ref.shape, F32)
        if has_s0:
            for k in range(RW_SEQS):
                for d in range(2):
                    for p in range(npair):
                        s_ref[2 * k + d, p, :hd, :hd] = s0_ref[k, d, 2 * p]
                        s_ref[2 * k + d, p, hd:, hd:] = s0_ref[k, d, 2 * p + 1]

    zero = jnp.zeros((1, B_COLS), F32)
    units = []
    for k in range(RW_SEQS):
        pf = jnp.where(i == 0, zero, zfp_ref[k, 7:8, :])
        nf = jnp.where(i == nc - 1, zero, zfn_ref[k, 0:1, :])
        uf, zs = _rwkv_prepare(0, zf_ref[k], pf, nf, prm)
        g_ref[k] = _mm(_sigmoid(zs[:, B_COLS - RW_GATE_RANK:]), g2_ref[...])
        pr = jnp.where(i == nc - 1, zero, zrp_ref[k, 7:8, :])
        nr = jnp.where(i == 0, zero, zrn_ref[k, 0:1, :])
        ub, _ = _rwkv_prepare(1, zr_ref[k], pr, nr, prm)
        for u in uf + ub:
            u['d'] = 2 * k + u['d']
        units += uf + ub
    ys = _rwkv_solve(units, s_ref)
    for k in range(RW_SEQS):
        yk = ys[2 * npair * k:2 * npair * (k + 1)]
        yf_ref[k] = jnp.concatenate(yk[:npair], axis=-1)
        yb_ref[k] = jnp.concatenate(yk[npair:], axis=-1)

    @pl.when(i == nc - 1)
    def _():
        for k in range(RW_SEQS):
            for d in range(2):
                for p in range(npair):
                    sfin_ref[k, d, 2 * p] = s_ref[2 * k + d, p, :hd, :hd]
                    sfin_ref[k, d, 2 * p + 1] = s_ref[2 * k + d, p, hd:, hd:]


def rwkv_scan(zb, seq0, nseq, s0, prm, g2):
    seqlen = zb.shape[1]
    ns = RW_SEQS
    assert seq0 % ns == 0 and nseq % ns == 0 and seqlen % CHUNK == 0
    nc = seqlen // CHUNK
    sub = CHUNK // 8
    b0 = seq0 // ns
    has_s0 = s0 is not None

    def specs(cidx):
        return [pl.BlockSpec((ns, CHUNK, B_COLS), lambda s, i: (b0 + s, cidx(i), 0)),
                pl.BlockSpec((ns, 8, B_COLS), lambda s, i: (b0 + s, jnp.maximum(cidx(i) * sub - 1, 0), 0)),
                pl.BlockSpec((ns, 8, B_COLS),
                             lambda s, i: (b0 + s, jnp.minimum((cidx(i) + 1) * sub, seqlen // 8 - 1), 0))]

    cf = lambda i: i
    cr = lambda i: nc - 1 - i
    full = lambda a: pl.BlockSpec(a.shape, lambda s, i: (0,) * a.ndim)
    sshape = (2, RW_HEADS, RW_HEAD_DIM, RW_HEAD_DIM)
    sspec = pl.BlockSpec((ns,) + sshape, lambda s, i: (s, 0, 0, 0, 0))
    in_specs = specs(cf) + specs(cr)
    args = [zb] * 6
    if has_s0:
        in_specs.append(sspec)
        args.append(s0)
    in_specs += [full(p) for p in prm] + [full(g2)]
    args += list(prm) + [g2]
    out = lambda cidx: pl.BlockSpec((ns, CHUNK, RW_WIDTH), lambda s, i: (s, cidx(i), 0))
    tok = jax.ShapeDtypeStruct((nseq, seqlen, RW_WIDTH), F32)
    yf, yb, g, sfin = pl.pallas_call(
        functools.partial(_rwkv_kernel, nc=nc, has_s0=has_s0),
        out_shape=(tok, tok, tok, jax.ShapeDtypeStruct((nseq,) + sshape, F32)),
        grid=(nseq // ns, nc),
        in_specs=in_specs,
        out_specs=(out(cf), out(cr), out(cf), sspec),
        scratch_shapes=[pltpu.VMEM((2 * ns, RW_HEADS // 2, LANES, LANES), F32)],
        compiler_params=_cparams(("arbitrary", "arbitrary")),
        name="rwkv_scan",
    )(*args)
    flat = lambda a: a.reshape(nseq * seqlen, RW_WIDTH)
    return flat(yf), flat(yb), flat(g), sfin


def _conv_kernel(z_ref, zp_ref, zn_ref, w_ref, b_ref, g_ref, be_ref, o_ref, h_ref, *, lay):
    i = pl.program_id(0)
    first, last = lay.tile_pos(i)

    def glu(z):
        return z[:, :CONV_WIDTH] * _sigmoid(z[:, CONV_WIDTH:])

    hz = jnp.zeros((CONV_HALO, CONV_WIDTH), F32)
    h_ref[0:CONV_HALO, :] = jnp.where(first, hz, glu(zp_ref[...]))
    h_ref[CONV_HALO:CONV_HALO + TM, :] = glu(z_ref[...])
    h_ref[CONV_HALO + TM:, :] = jnp.where(last, hz, glu(zn_ref[...]))
    off = CONV_HALO - CONV_K // 2
    acc = jnp.zeros((TM, CONV_WIDTH), F32) + b_ref[...]
    for j in range(CONV_K):
        acc = acc + w_ref[j:j + 1, :] * h_ref[off + j:off + j + TM, :]
    xc = acc - jnp.mean(acc, axis=-1, keepdims=True)
    hn = xc * lax.rsqrt(jnp.mean(xc * xc, axis=-1, keepdims=True) + LN_EPS) * g_ref[...] + be_ref[...]
    o_ref[...] = (hn * _sigmoid(hn)).astype(BF16)


def conv_module(lay, zc, w, b, g, be):
    n = lay.n
    hb = TM // CONV_HALO
    vec = pl.BlockSpec((1, CONV_WIDTH), lambda i: (0, 0))
    return pl.pallas_call(
        functools.partial(_conv_kernel, lay=lay),
        out_shape=jax.ShapeDtypeStruct((n, CONV_WIDTH), BF16),
        grid=(lay.tiles,),
        in_specs=[pl.BlockSpec((TM, C_COLS), lambda i: (i, 0)),
                  pl.BlockSpec((CONV_HALO, C_COLS), lambda i: (jnp.maximum(i * hb - 1, 0), 0)),
                  pl.BlockSpec((CONV_HALO, C_COLS), lambda i: (jnp.minimum((i + 1) * hb, n // CONV_HALO - 1), 0)),
                  pl.BlockSpec((CONV_K, CONV_WIDTH), lambda i: (0, 0)), vec, vec, vec],
        out_specs=pl.BlockSpec((TM, CONV_WIDTH), lambda i: (i, 0)),
        scratch_shapes=[pltpu.VMEM((TM + 2 * CONV_HALO, CONV_WIDTH), F32)],
        compiler_params=_cparams(("arbitrary",)),
        name="conv_module",
    )(zc, zc, zc, w, b, g, be)


def _merge_kernel(x_ref, oa_c_ref, yf_c_ref, yb_c_ref, g_c_ref, oa_d_ref, yf_d_ref, yb_d_ref, g_d_ref,
                  oc_ref, gt_ref, mod_ref, ng_ref, lg_ref, lb_ref,
                  wa_ref, wr_ref, wc_ref, wo_ref, rw_ref, rb_ref,
                  x1_ref, hm_ref, ti_ref, tg_ref, rk_ref, cnt_ref, *, ctx_tiles):
    is_ctx = pl.program_id(0) < ctx_tiles
    pick = lambda c_ref, d_ref: jnp.where(is_ctx, c_ref[...], d_ref[...])
    oa = pick(oa_c_ref, oa_d_ref)
    g_rw = pick(g_c_ref, g_d_ref)
    y = pick(yf_c_ref, yf_d_ref) + pick(yb_c_ref, yb_d_ref)
    pi = lax.broadcasted_iota(jnp.int32, (LANES, LANES), 0)
    pj = lax.broadcasted_iota(jnp.int32, (LANES, LANES), 1)
    ones_bd = ((pi // RW_HEAD_DIM) == (pj // RW_HEAD_DIM)).astype(BF16)

    def head_mean(v):
        vh, vl = _split(v)
        return (_dot(vh, ones_bd) + _dot(vl, ones_bd)) * (1.0 / RW_HEAD_DIM)

    outs = []
    for p in range(RW_HEADS // 2):
        yp = y[:, p * LANES:(p + 1) * LANES]
        yc = yp - head_mean(yp)
        outs.append(yc * lax.rsqrt(head_mean(yc * yc) + GN_EPS))
    o_rw = (jnp.concatenate(outs, axis=-1) * lg_ref[...] + lb_ref[...]) * g_rw
    merged = (gt_ref[:, :D_MODEL].astype(F32) * _dot(oa, wa_ref[...])
              + gt_ref[:, D_MODEL:2 * D_MODEL].astype(F32) * _mm(o_rw, wr_ref[...])
              + gt_ref[:, 2 * D_MODEL:].astype(F32) * _dot(oc_ref[...], wc_ref[...]))
    mix = _mm(merged, wo_ref[...])
    x1 = x_ref[...] + mod_ref[0, 2:3, :] * _rms(mix, ng_ref[1:2, :])
    x1_ref[...] = x1
    hm = _rms(x1, ng_ref[2:3, :]) * (1.0 + mod_ref[0, 4:5, :]) + mod_ref[0, 3:4, :]
    hm_ref[...] = hm
    logits = _mm3(hm, rw_ref[...]) + rb_ref[...]
    lane = lax.broadcasted_iota(jnp.int32, logits.shape, 1)
    idx_out = jnp.zeros(logits.shape, jnp.int32)
    val_out = jnp.full(logits.shape, NEG_INF, F32)
    picks = []
    for j in range(TOP_K):
        m = jnp.max(logits, axis=-1, keepdims=True)
        idx = jnp.min(jnp.where(logits == m, lane, LANES), axis=-1, keepdims=True)
        idx_out = jnp.where(lane == j, idx, idx_out)
        val_out = jnp.where(lane == j, m, val_out)
        picks.append(lane == idx)
        logits = jnp.where(picks[-1], -jnp.inf, logits)
    e = jnp.exp(val_out - jnp.max(val_out, axis=-1, keepdims=True))
    ti_ref[...] = idx_out
    tg_ref[...] = e / jnp.sum(e, axis=-1, keepdims=True)

    chosen = jnp.zeros(logits.shape, F32)
    for pk in picks:
        chosen = chosen + pk.astype(F32)
    ti_ = lax.broadcasted_iota(jnp.int32, (TM, TM), 0)
    tj_ = lax.broadcasted_iota(jnp.int32, (TM, TM), 1)
    before = _dot((ti_ > tj_).astype(BF16), chosen.astype(BF16))
    rank = jnp.zeros(logits.shape, F32)
    for j, pk in enumerate(picks):
        rank = jnp.where(lane == j, jnp.sum(jnp.where(pk, before, 0.0), axis=-1, keepdims=True), rank)
    rk_ref[...] = rank.astype(jnp.int32)
    cnt_ref[0] = jnp.sum(chosen, axis=0, keepdims=True)


def merge_router(lay, x, ctx_br, dec_br, o_cv, gates, mod, ng, lg, lb, wa, wr, wc, wo, rw, rb):
    n = lay.n
    row = lambda w: pl.BlockSpec((TM, w), lambda i: (i, 0))
    crow = lambda w: pl.BlockSpec((TM, w), lambda i: (jnp.minimum(i, lay.ctx_tiles - 1), 0))
    drow = lambda w: pl.BlockSpec((TM, w), lambda i: (jnp.maximum(i - lay.ctx_tiles, 0), 0))
    full = lambda a: pl.BlockSpec(a.shape, lambda i: (0,) * a.ndim)
    br_w = (NA_WIDTH, RW_WIDTH, RW_WIDTH, RW_WIDTH)
    return pl.pallas_call(
        functools.partial(_merge_kernel, ctx_tiles=lay.ctx_tiles),
        out_shape=(jax.ShapeDtypeStruct((n, D_MODEL), F32), jax.ShapeDtypeStruct((n, D_MODEL), F32),
                   jax.ShapeDtypeStruct((n, LANES), jnp.int32), jax.ShapeDtypeStruct((n, LANES), F32),
                   jax.ShapeDtypeStruct((n, LANES), jnp.int32), jax.ShapeDtypeStruct((lay.tiles, 1, LANES), F32)),
        grid=(lay.tiles,),
        in_specs=[row(D_MODEL)] + [crow(w) for w in br_w] + [drow(w) for w in br_w] + [
                  row(CONV_WIDTH),
                  row(G_COLS), pl.BlockSpec((1, N_MOD, D_MODEL), lambda i: (lay.mod_row(i), 0, 0)),
                  full(ng), full(lg), full(lb), full(wa), full(wr), full(wc), full(wo), full(rw), full(rb)],
        out_specs=(row(D_MODEL), row(D_MODEL), row(LANES), row(LANES), row(LANES),
                   pl.BlockSpec((1, 1, LANES), lambda i: (i, 0, 0))),
        compiler_params=_cparams(("arbitrary",)),
        name="merge_router",
    )(x, *ctx_br, *dec_br, o_cv, gates, mod, ng, lg, lb, wa, wr, wc, wo, rw, rb)


RUN_ALIGN = 8
RUN_BITS = (TM // RUN_ALIGN).bit_length()
SORT_ROWS = -(-(TM * TOP_K + N_EXPERTS * (RUN_ALIGN - 1)) // LANES) * LANES


def route_tables(top_i, wrank, cnt, n):
    tiles = n // TM
    cnt = cnt[:, 0, :N_EXPERTS].astype(jnp.int32)
    run = ((cnt + RUN_ALIGN - 1) // RUN_ALIGN) * RUN_ALIGN
    off = jnp.cumsum(run, axis=1) - run
    before = jnp.cumsum(run, axis=0) - run
    region = jnp.sum(run, axis=0)
    padded = ((region + ROUTE_BLOCK - 1) // ROUTE_BLOCK) * ROUTE_BLOCK
    pad_end = jnp.cumsum(padded)
    first = (pad_end - padded)[None, :] + before
    table = jnp.concatenate([off, first, run, jnp.zeros((tiles, LANES - 3 * N_EXPERTS), jnp.int32)], axis=1)
    onehot = top_i.reshape(tiles, TM, TOP_K, 1) == jnp.arange(N_EXPERTS, dtype=jnp.int32)
    pos = jnp.sum(jnp.where(onehot, off[:, None, None, :], 0), axis=-1) + wrank.reshape(tiles, TM, TOP_K)
    n_blocks = -(-tiles * (TM * TOP_K + N_EXPERTS * (RUN_ALIGN - 1)) // ROUTE_BLOCK) + N_EXPERTS
    starts = jnp.arange(n_blocks, dtype=jnp.int32) * ROUTE_BLOCK
    block_e = jnp.minimum(jnp.sum((pad_end[None, :] <= starts[:, None]).astype(jnp.int32), axis=1),
                          N_EXPERTS - 1)
    n_used = (pad_end[-1] // ROUTE_BLOCK).astype(jnp.int32).reshape(1)
    return table.reshape(tiles, 1, LANES), pos, block_e, n_used, n_blocks


def _for_each_piece(tab_ref, fn):
    def body(e, carry):
        off = tab_ref[0, 0, e]
        first = tab_ref[0, 0, N_EXPERTS + e]
        run = tab_ref[0, 0, 2 * N_EXPERTS + e]
        for k in range(RUN_BITS):
            size = RUN_ALIGN << k

            @pl.when((run & size) != 0)
            def _():
                done = run & (-2 * size)
                fn(pl.multiple_of(off + done, RUN_ALIGN), pl.multiple_of(first + done, RUN_ALIGN), size)
        return carry

    lax.fori_loop(0, N_EXPERTS, body, 0)


def _dispatch_kernel(tab_ref, tabp_ref, pos_ref, hm_ref, xb_in_ref, xb_ref, srt_ref, sems):
    del xb_in_ref
    i = pl.program_id(0)
    slot = i % 2

    def copy(slot_, off, row, size):
        return pltpu.make_async_copy(srt_ref.at[slot_, pl.ds(off, size), :], xb_ref.at[pl.ds(row, size), :],
                                     sems.at[slot_])

    p_iota = lax.broadcasted_iota(jnp.int32, (SORT_ROWS, TM), 0)
    hit = p_iota == pos_ref[0, 0:1, :]
    for j in range(1, TOP_K):
        hit = hit | (p_iota == pos_ref[0, j:j + 1, :])
    srt_ref[slot] = _dot(jnp.where(hit, 1.0, 0.0).astype(BF16), hm_ref[...].astype(BF16))
    _for_each_piece(tab_ref, lambda off, row, size: copy(slot, off, row, size).start())

    @pl.when(i > 0)
    def _():
        _for_each_piece(tabp_ref, lambda off, row, size: copy(1 - slot, off, row, size).wait())

    @pl.when(i == pl.num_programs(0) - 1)
    def _():
        _for_each_piece(tab_ref, lambda off, row, size: copy(slot, off, row, size).wait())


def moe_dispatch(lay, hm, table, pos, xb0):
    n_rows = xb0.shape[0]
    post = jnp.pad(pos.transpose(0, 2, 1), ((0, 0), (0, 8 - TOP_K), (0, 0)))
    tab = lambda f: pl.BlockSpec((1, 1, LANES), lambda i: (f(i), 0, 0), memory_space=pltpu.SMEM)
    return pl.pallas_call(
        _dispatch_kernel,
        out_shape=jax.ShapeDtypeStruct((n_rows, D_MODEL), F32),
        grid=(lay.tiles,),
        in_specs=[tab(lambda i: i), tab(lambda i: jnp.maximum(i - 1, 0)),
                  pl.BlockSpec((1, 8, TM), lambda i: (i, 0, 0)),
                  pl.BlockSpec((TM, D_MODEL), lambda i: (i, 0)),
                  pl.BlockSpec(memory_space=pl.ANY)],
        out_specs=pl.BlockSpec(memory_space=pl.ANY),
        scratch_shapes=[pltpu.VMEM((2, SORT_ROWS, D_MODEL), F32), pltpu.SemaphoreType.DMA((2,))],
        input_output_aliases={4: 0},
        compiler_params=_cparams(("arbitrary",)),
        name="moe_dispatch",
    )(table, table, post, hm, xb0)


GL_GROUP = 2 * LANES


def _regroup_perm():
    r = jnp.arange(GL_GROUP, dtype=jnp.int32)[:, None]
    c = jnp.arange(GL_GROUP, dtype=jnp.int32)[None, :]
    src = jnp.where(c < LANES, 2 * c, 2 * (c - LANES) + 1)
    return (r == src).astype(BF16)


def _w1_regroup_kernel(w_ref, p_ref, o_ref):
    for j in range(2 * EXPERT_FF // GL_GROUP):
        cs = slice(j * GL_GROUP, (j + 1) * GL_GROUP)
        o_ref[0, :, cs] = _dot(w_ref[0, :, cs].astype(BF16), p_ref[...]).astype(BF16)


def w1_regroup(w1, l):
    return pl.pallas_call(
        _w1_regroup_kernel,
        out_shape=jax.ShapeDtypeStruct(w1.shape[1:], BF16),
        grid=(w1.shape[1],),
        in_specs=[pl.BlockSpec((None, 1) + w1.shape[2:], lambda e: (l, e, 0, 0)),
                  pl.BlockSpec((GL_GROUP, GL_GROUP), lambda e: (0, 0))],
        out_specs=pl.BlockSpec((1,) + w1.shape[2:], lambda e: (e, 0, 0)),
        compiler_params=_cparams(("arbitrary",)),
        name="w1_regroup",
    )(w1, _regroup_perm())


def _expert_kernel(be_ref, nu_ref, x_ref, w1_ref, b1_ref, w2_ref, b2_ref, y_ref):
    b = pl.program_id(0)

    @pl.when(b < nu_ref[0])
    def _():
        x = x_ref[...].astype(BF16)
        acc = jnp.zeros((ROUTE_BLOCK, D_MODEL), F32) + b2_ref[0]
        for j in range(EXPERT_FF // GL_GROUP):
            acts = []
            for g in range(2):
                c0 = (2 * j + g) * GL_GROUP
                h = _dot(x, w1_ref[0, :, c0:c0 + GL_GROUP]) + b1_ref[0, :, c0:c0 + GL_GROUP]
                hg = jnp.minimum(h[:, :LANES], SWIGLU_LIMIT)
                hl = jnp.clip(h[:, LANES:], -SWIGLU_LIMIT, SWIGLU_LIMIT)
                acts.append(hg * _sigmoid(SWIGLU_ALPHA * hg) * (hl + 1.0))
            act = jnp.concatenate(acts, axis=-1).astype(BF16)
            acc = acc + _dot(act, w2_ref[0, j * GL_GROUP:(j + 1) * GL_GROUP, :])
        y_ref[...] = acc

    @pl.when(b >= nu_ref[0])
    def _():
        y_ref[...] = jnp.zeros(y_ref.shape, F32)


def moe_experts(xb, block_e, n_used, n_blocks, w1, b1, w2, b2):
    def xmap(b, be, nu):
        return (jnp.minimum(b, jnp.maximum(nu[0] - 1, 0)), 0)

    def wmap(b, be, nu):
        return (be[jnp.minimum(b, jnp.maximum(nu[0] - 1, 0))], 0, 0)

    wspec = lambda a: pl.BlockSpec((1,) + a.shape[1:], wmap)
    return pl.pallas_call(
        _expert_kernel,
        out_shape=jax.ShapeDtypeStruct(xb.shape, F32),
        grid_spec=pltpu.PrefetchScalarGridSpec(
            num_scalar_prefetch=2,
            grid=(n_blocks,),
            in_specs=[pl.BlockSpec((ROUTE_BLOCK, D_MODEL), xmap),
                      wspec(w1), wspec(b1), wspec(w2), wspec(b2)],
            out_specs=pl.BlockSpec((ROUTE_BLOCK, D_MODEL), lambda b, be, nu: (b, 0))),
        compiler_params=_cparams(("arbitrary",)),
        name="moe_experts",
    )(block_e, n_used, xb, w1, b1, w2, b2)


def _combine_kernel(tab_ref, tabn_ref, yb_ref, pos_ref, tg_ref, x1_ref, mod_ref, ng_ref, o_ref, stg_ref, sems):
    i = pl.program_id(0)
    last = pl.num_programs(0) - 1
    slot = i % 2

    def copy(slot_, off, row, size):
        return pltpu.make_async_copy(yb_ref.at[pl.ds(row, size), :], stg_ref.at[slot_, pl.ds(off, size), :],
                                     sems.at[slot_])

    @pl.when(i == 0)
    def _():
        stg_ref[...] = jnp.zeros(stg_ref.shape, F32)
        _for_each_piece(tab_ref, lambda off, row, size: copy(slot, off, row, size).start())

    @pl.when(i < last)
    def _():
        _for_each_piece(tabn_ref, lambda off, row, size: copy(1 - slot, off, row, size).start())

    _for_each_piece(tab_ref, lambda off, row, size: copy(slot, off, row, size).wait())
    p_iota = lax.broadcasted_iota(jnp.int32, (TM, SORT_ROWS), 1)
    w = jnp.zeros((TM, SORT_ROWS), F32)
    for j in range(TOP_K):
        w = w + jnp.where(p_iota == pos_ref[:, j:j + 1], tg_ref[:, j:j + 1], 0.0)
    y = _mm3(w, stg_ref[slot])
    o_ref[...] = x1_ref[...] + mod_ref[0, 5:6, :] * _rms(y, ng_ref[3:4, :])


def moe_combine(lay, yb, table, pos, tg, x1, mod, ng):
    n = lay.n
    posl = jnp.pad(pos.reshape(n, TOP_K), ((0, 0), (0, LANES - TOP_K)))
    row = lambda w: pl.BlockSpec((TM, w), lambda i: (i, 0))
    tab = lambda f: pl.BlockSpec((1, 1, LANES), lambda i: (f(i), 0, 0), memory_space=pltpu.SMEM)
    return pl.pallas_call(
        _combine_kernel,
        out_shape=jax.ShapeDtypeStruct((n, D_MODEL), F32),
        grid=(lay.tiles,),
        in_specs=[tab(lambda i: i), tab(lambda i: jnp.minimum(i + 1, lay.tiles - 1)),
                  pl.BlockSpec(memory_space=pl.ANY),
                  row(LANES), row(LANES), row(D_MODEL),
                  pl.BlockSpec((1, N_MOD, D_MODEL), lambda i: (lay.mod_row(i), 0, 0)),
                  pl.BlockSpec(ng.shape, lambda i: (0, 0))],
        out_specs=row(D_MODEL),
        scratch_shapes=[pltpu.VMEM((2, SORT_ROWS, D_MODEL), F32), pltpu.SemaphoreType.DMA((2,))],
        compiler_params=_cparams(("arbitrary",)),
        name="moe_combine",
    )(table, table, yb, posl, tg, x1, mod, ng)


def layer(lay, x, l, cond, P, ctx_k, ctx_v, s0_dec, xb_buf):
    mod = mod_table(cond, P['mod_w'][l], P['mod_b'][l]).reshape(cond.shape[0], N_MOD, D_MODEL)
    ng = P['norm_g'][l]
    q, kf, vf, kb, vb, zb, zc, gates = in_proj(lay, x, mod, ng[0:1], P['w_in'][l].astype(BF16),
                                               P['b_in'][l].reshape(1, P_IN))
    oa_c = ctx_attention(lay, q, kb, vb)
    oa_d = na_attention(lay, q, kb, vb, ctx_k.astype(BF16), ctx_v.astype(BF16), na_bias_table(P['rpb'][l]))
    vec = lambda a: a.reshape(1, -1)
    prm = (P['rw_mu'][l], P['rw_w0'][l], P['rw_w2'][l], P['rw_a0'][l], P['rw_a2'][l],
           vec(P['rw_k_k'][l]), vec(P['rw_k_a'][l]), vec(P['rw_r_k'][l]))
    yf_c, yb_c, g_c, s_ctx = rwkv_scan(zb.reshape(-1, lay.ctx_len, B_COLS), 0, lay.n_ctx, None, prm,
                                       P['rw_g2'][l])
    yf_d, yb_d, g_d, _ = rwkv_scan(zb.reshape(-1, lay.dec_len, B_COLS), lay.ctx_tokens // lay.dec_len,
                                   lay.n_dec, s0_dec, prm, P['rw_g2'][l])
    o_cv = conv_module(lay, zc, P['conv_w'][l], vec(P['conv_b'][l]), vec(P['conv_ln_g'][l]),
                       vec(P['conv_ln_b'][l]))
    rw = jnp.zeros((D_MODEL, LANES), F32).at[:, :N_EXPERTS].set(P['router_w'][l])
    rb = jnp.full((1, LANES), NEG_INF, F32).at[0, :N_EXPERTS].set(P['router_b'][l])
    x1, hm, top_i, top_g, rank, counts = merge_router(
        lay, x, (oa_c, yf_c, yb_c, g_c), (oa_d, yf_d, yb_d, g_d), o_cv, gates, mod, ng,
        vec(P['rw_lnx_g'][l]), vec(P['rw_lnx_b'][l]),
        P['w_o_attn'][l].astype(BF16), P['w_o_rwkv'][l].astype(BF16), P['w_o_conv'][l].astype(BF16),
        P['w_out'][l].astype(BF16), rw, rb)
    table, pos, block_e, n_used, n_blocks = route_tables(top_i[:, :TOP_K], rank[:, :TOP_K], counts, lay.n)
    if xb_buf is None:
        xb_buf = jnp.zeros((n_blocks * ROUTE_BLOCK, D_MODEL), F32)
    xb = moe_dispatch(lay, hm, table, pos, xb_buf)
    b1 = P['exp_b1'][l].reshape(N_EXPERTS, -1, LANES, 2).transpose(0, 1, 3, 2).reshape(N_EXPERTS, 1, -1)
    ybk = moe_experts(xb, block_e, n_used, n_blocks, w1_regroup(P['exp_w1'], l), b1,
                      P['exp_w2'][l].astype(BF16), P['exp_b2'][l][:, None, :])
    x2 = moe_combine(lay, ybk, table, pos, top_g, x1, mod, ng)
    return x2, kf, vf, s_ctx, xb


def kernel(x_prompt, x_sample, cache_attn_k, cache_attn_v, state_rwkv, c, c_ctx, mod_w, mod_b, norm_g, w_in, b_in, rpb, w_o_attn, rw_mu, rw_w0, rw_w2, rw_a0, rw_a2, rw_g2, rw_k_k, rw_k_a, rw_r_k, rw_lnx_g, rw_lnx_b, w_o_rwkv, conv_w, conv_b, conv_ln_g, conv_ln_b, w_o_conv, w_out, router_w, router_b, exp_w1, exp_b1, exp_w2, exp_b2):
    P = {
        'mod_w': mod_w, 'mod_b': mod_b, 'norm_g': norm_g, 'w_in': w_in, 'b_in': b_in, 'rpb': rpb,
        'w_o_attn': w_o_attn, 'rw_mu': rw_mu, 'rw_w0': rw_w0, 'rw_w2': rw_w2, 'rw_a0': rw_a0,
        'rw_a2': rw_a2, 'rw_g2': rw_g2, 'rw_k_k': rw_k_k, 'rw_k_a': rw_k_a, 'rw_r_k': rw_r_k,
        'rw_lnx_g': rw_lnx_g, 'rw_lnx_b': rw_lnx_b, 'w_o_rwkv': w_o_rwkv, 'conv_w': conv_w,
        'conv_b': conv_b, 'conv_ln_g': conv_ln_g, 'conv_ln_b': conv_ln_b, 'w_o_conv': w_o_conv,
        'w_out': w_out, 'router_w': router_w, 'router_b': router_b, 'exp_w1': exp_w1,
        'exp_b1': exp_b1, 'exp_w2': exp_w2, 'exp_b2': exp_b2,
    }
    nb, sl, _ = x_prompt.shape
    db, ds, _ = x_sample.shape
    lay = Layout(nb, sl, db, ds)
    x = jnp.concatenate([x_prompt.reshape(-1, D_MODEL), x_sample.reshape(-1, D_MODEL)], axis=0)
    cond = jnp.zeros((8, D_MODEL), F32).at[0].set(c_ctx).at[1:1 + db].set(c)
    ks, vs, ss = [], [], []
    xb_buf = None
    for l in range(DEPTH):
        ck = cache_attn_k[:, l].reshape(db, -1, NA_WIDTH)
        cv = cache_attn_v[:, l].reshape(db, -1, NA_WIDTH)
        x, kf, vf, s_ctx, xb_buf = layer(lay, x, l, cond, P, ck, cv, state_rwkv[:, l], xb_buf)
        ks.append(kf[:lay.ctx_tokens].reshape(nb, sl, NA_HEADS, NA_HEAD_DIM))
        vs.append(vf[:lay.ctx_tokens].reshape(nb, sl, NA_HEADS, NA_HEAD_DIM))
        ss.append(s_ctx)
    y_prompt = x[:lay.ctx_tokens].reshape(nb, sl, D_MODEL)
    y_sample = x[lay.ctx_tokens:].reshape(db, ds, D_MODEL)
    return (y_prompt, y_sample, jnp.stack(ks, axis=1), jnp.stack(vs, axis=1), jnp.stack(ss, axis=1))
```

```python
import functools

import jax
import jax.numpy as jnp
from jax import lax
from jax.experimental import pallas as pl
from jax.experimental.pallas import tpu as pltpu

F32 = jnp.float32
BF16 = jnp.bfloat16

D_MODEL = 1024
BATCH, SEQ = 32, 256
DEPTH = 2
DEC_BATCH, DEC_SEQ = 4, 4096
PAST_LEN = 256
GRID_W = 64
NA_HEADS, NA_HEAD_DIM = 8, 64
NA_WIDTH = NA_HEADS * NA_HEAD_DIM
NA_WIN_ROWS, NA_WIN_COLS = 8, 16
RW_HEADS, RW_HEAD_DIM = 8, 64
RW_WIDTH = RW_HEADS * RW_HEAD_DIM
RW_DECAY_RANK, RW_ICLR_RANK, RW_GATE_RANK = 64, 64, 128
CONV_WIDTH, CONV_K = 512, 31
N_BRANCH, N_MOD = 3, 6
N_EXPERTS, TOP_K, EXPERT_FF = 32, 4, 1024
SWIGLU_LIMIT, SWIGLU_ALPHA = 7.0, 1.702
ROUTE_BLOCK = 512
RMS_EPS, LN_EPS, GN_EPS = 1e-6, 1e-5, 64e-5
NEG_INF = -1e30
A_COLS = 3 * NA_WIDTH
B_COLS = 3 * RW_WIDTH + 2 * RW_DECAY_RANK + 2 * RW_ICLR_RANK + RW_GATE_RANK
C_COLS = 2 * CONV_WIDTH
G_COLS = N_BRANCH * D_MODEL
P_IN = A_COLS + B_COLS + C_COLS + G_COLS

LANES = 128
TM = 256
CHUNK = 64
CONV_HALO = 16
VMEM_LIMIT = 56 * 1024 * 1024


def _cparams(sem):
    return pltpu.CompilerParams(dimension_semantics=sem, vmem_limit_bytes=VMEM_LIMIT)


def _dot(a, b, dims=((1,), (0,))):
    return lax.dot_general(a, b, (dims, ((), ())), preferred_element_type=F32)


def _mm(a, b, dims=((1,), (0,))):
    return _dot(a.astype(BF16), b.astype(BF16), dims)


def _split(a):
    hi = a.astype(BF16)
    lo = (a - hi.astype(F32)).astype(BF16)
    return hi, lo


def _mm3(a, b, dims=((1,), (0,))):
    ah, al = _split(a)
    bh, bl = _split(b)
    return _dot(ah, bh, dims) + (_dot(ah, bl, dims) + _dot(al, bh, dims))


_NT = ((1,), (1,))


def _rms(x, g):
    return x * lax.rsqrt(jnp.mean(x * x, axis=-1, keepdims=True) + RMS_EPS) * g


def _sigmoid(x):
    return 1.0 / (1.0 + jnp.exp(-x))


class Layout:
    def __init__(self, n_ctx, ctx_len, n_dec, dec_len):
        assert ctx_len == TM and dec_len % TM == 0
        self.n_ctx, self.ctx_len, self.n_dec, self.dec_len = n_ctx, ctx_len, n_dec, dec_len
        self.ctx_tokens = n_ctx * ctx_len
        self.dec_tokens = n_dec * dec_len
        self.n = self.ctx_tokens + self.dec_tokens
        self.ctx_tiles = self.ctx_tokens // TM
        self.dec_tiles_per_seq = dec_len // TM
        self.tiles = self.n // TM

    def mod_row(self, i):
        return jnp.where(i < self.ctx_tiles, 0, 1 + (i - self.ctx_tiles) // self.dec_tiles_per_seq)

    def tile_pos(self, i):
        j = (i - self.ctx_tiles) % self.dec_tiles_per_seq
        is_ctx = i < self.ctx_tiles
        return is_ctx | (j == 0), is_ctx | (j == self.dec_tiles_per_seq - 1)


def _mod_kernel(c_ref, w_ref, b_ref, o_ref):
    c = c_ref[...]
    o_ref[...] = _mm3(c * _sigmoid(c), w_ref[...]) + b_ref[...]


def mod_table(cond, w, b):
    tn = 1024
    nm = w.shape[1]
    return pl.pallas_call(
        _mod_kernel,
        out_shape=jax.ShapeDtypeStruct((cond.shape[0], nm), F32),
        grid=(nm // tn,),
        in_specs=[pl.BlockSpec(cond.shape, lambda j: (0, 0)),
                  pl.BlockSpec((D_MODEL, tn), lambda j: (0, j)),
                  pl.BlockSpec((1, tn), lambda j: (0, j))],
        out_specs=pl.BlockSpec((cond.shape[0], tn), lambda j: (0, j)),
        compiler_params=_cparams(("arbitrary",)),
        name="mod_table",
    )(cond, w, b.reshape(1, nm))


def _inproj_kernel(x_ref, mod_ref, g_ref, w_ref, b_ref,
                   q_ref, kf_ref, vf_ref, kb_ref, vb_ref, zb_ref, zc_ref, gt_ref):
    u = (_rms(x_ref[...], g_ref[...]) * (1.0 + mod_ref[0, 1:2, :]) + mod_ref[0, 0:1, :]).astype(BF16)

    def seg(c0, width):
        return _dot(u, w_ref[:, c0:c0 + width]) + b_ref[:, c0:c0 + width]

    q_ref[...] = seg(0, NA_WIDTH).astype(BF16)
    k = seg(NA_WIDTH, NA_WIDTH)
    kf_ref[...] = k
    kb_ref[...] = k.astype(BF16)
    v = seg(2 * NA_WIDTH, NA_WIDTH)
    vf_ref[...] = v
    vb_ref[...] = v.astype(BF16)
    wb = 384
    for j in range(B_COLS // wb):
        zb_ref[:, j * wb:(j + 1) * wb] = seg(A_COLS + j * wb, wb)
    wc = 512
    for j in range(C_COLS // wc):
        zc_ref[:, j * wc:(j + 1) * wc] = seg(A_COLS + B_COLS + j * wc, wc)
    for j in range(G_COLS // wc):
        gt_ref[:, j * wc:(j + 1) * wc] = _sigmoid(seg(A_COLS + B_COLS + C_COLS + j * wc, wc)).astype(BF16)


def in_proj(lay, x, mod, g0, w_bf, b):
    n = lay.n
    row = lambda w: pl.BlockSpec((TM, w), lambda i: (i, 0))
    out_shape = (jax.ShapeDtypeStruct((n, NA_WIDTH), BF16),
                 jax.ShapeDtypeStruct((n, NA_WIDTH), F32), jax.ShapeDtypeStruct((n, NA_WIDTH), F32),
                 jax.ShapeDtypeStruct((n, NA_WIDTH), BF16), jax.ShapeDtypeStruct((n, NA_WIDTH), BF16),
                 jax.ShapeDtypeStruct((n, B_COLS), F32), jax.ShapeDtypeStruct((n, C_COLS), F32),
                 jax.ShapeDtypeStruct((n, G_COLS), BF16))
    return pl.pallas_call(
        _inproj_kernel,
        out_shape=out_shape,
        grid=(lay.tiles,),
        in_specs=[row(D_MODEL),
                  pl.BlockSpec((1, N_MOD, D_MODEL), lambda i: (lay.mod_row(i), 0, 0)),
                  pl.BlockSpec((1, D_MODEL), lambda i: (0, 0)),
                  pl.BlockSpec((D_MODEL, P_IN), lambda i: (0, 0), pipeline_mode=pl.Buffered(1)),
                  pl.BlockSpec((1, P_IN), lambda i: (0, 0))],
        out_specs=(row(NA_WIDTH), row(NA_WIDTH), row(NA_WIDTH), row(NA_WIDTH), row(NA_WIDTH),
                   row(B_COLS), row(C_COLS), row(G_COLS)),
        compiler_params=_cparams(("arbitrary",)),
        name="in_proj",
    )(x, mod, g0, w_bf, b)


def _ctx_attn_kernel(q_ref, k_ref, v_ref, o_ref):
    scale = NA_HEAD_DIM ** -0.5
    lo = lax.broadcasted_iota(jnp.int32, (q_ref.shape[0], LANES), 1) < NA_HEAD_DIM
    heads = []
    for h in range(NA_HEADS):
        ps = slice((h // 2) * LANES, (h // 2 + 1) * LANES)
        qp = q_ref[:, ps] * scale
        qh = jnp.where(lo if h % 2 == 0 else jnp.logical_not(lo), qp, jnp.zeros_like(qp))
        heads.append(dict(ps=ps, s=_dot(qh, k_ref[:, ps], _NT)))
    for u in heads:
        u['p'] = jnp.exp(u['s'] - jnp.max(u['s'], axis=-1, keepdims=True))
    for u in heads:
        u['o'] = _dot(u['p'].astype(BF16), v_ref[:, u['ps']]) / jnp.sum(u['p'], axis=-1, keepdims=True)
    for p in range(NA_HEADS // 2):
        o_ref[:, p * LANES:(p + 1) * LANES] = jnp.where(lo, heads[2 * p]['o'], heads[2 * p + 1]['o']).astype(BF16)


def ctx_attention(lay, q, kb, vb):
    blk = pl.BlockSpec((lay.ctx_len, NA_WIDTH), lambda b: (b, 0))
    return pl.pallas_call(
        _ctx_attn_kernel,
        out_shape=jax.ShapeDtypeStruct((lay.ctx_tokens, NA_WIDTH), BF16),
        grid=(lay.n_ctx,),
        in_specs=[blk, blk, blk],
        out_specs=blk,
        compiler_params=_cparams(("arbitrary",)),
        name="ctx_attention",
    )(q, kb, vb)


def na_bias_table(rpb):
    kw, kh = NA_WIN_COLS, NA_WIN_ROWS
    col = jnp.arange(GRID_W, dtype=jnp.int32)
    col_start = jnp.clip(col - kw // 2, 0, GRID_W - kw)
    col_ok = (col[None, :] >= col_start[:, None]) & (col[None, :] < col_start[:, None] + kw)
    dc = jnp.clip(col[None, :] - col[:, None] + kw - 1, 0, 2 * kw - 2)
    bias_c = jnp.where(col_ok, rpb[:, :, dc].astype(F32), NEG_INF)
    tabs = [jnp.concatenate([bias_c[:, d + i] for i in range(kh)], axis=-1) for d in range(kh)]
    return jnp.stack(tabs, axis=0)


def _na_row_start(r, rows):
    return jnp.clip(r - NA_WIN_ROWS // 2, 0, rows - NA_WIN_ROWS)


def _na_kernel(q_ref, k_ref, v_ref, ck_ref, cv_ref, bias_ref, o_ref, *, rows):
    r = pl.program_id(1)
    rs = _na_row_start(r, rows)
    n_loc = NA_WIN_ROWS * GRID_W
    start = pl.multiple_of(rs * GRID_W, GRID_W)
    scale = NA_HEAD_DIM ** -0.5
    lo = lax.broadcasted_iota(jnp.int32, (GRID_W, LANES), 1) < NA_HEAD_DIM
    heads = []
    for h in range(NA_HEADS):
        ps = slice((h // 2) * LANES, (h // 2 + 1) * LANES)
        qp = q_ref[:, ps] * scale
        qh = jnp.where(lo if h % 2 == 0 else jnp.logical_not(lo), qp, jnp.zeros_like(qp))
        s_loc = _dot(qh, k_ref[pl.ds(start, n_loc), ps], _NT) + bias_ref[0, h]
        s_ctx = _dot(qh, ck_ref[0, :, ps], _NT)
        heads.append(dict(ps=ps, s_loc=s_loc, s_ctx=s_ctx))
    for u in heads:
        u['m'] = jnp.maximum(jnp.max(u['s_loc'], axis=-1, keepdims=True),
                             jnp.max(u['s_ctx'], axis=-1, keepdims=True))
    for u in heads:
        u['p_loc'] = jnp.exp(u['s_loc'] - u['m'])
        u['p_ctx'] = jnp.exp(u['s_ctx'] - u['m'])
        u['den'] = jnp.sum(u['p_loc'], axis=-1, keepdims=True) + jnp.sum(u['p_ctx'], axis=-1, keepdims=True)
    for u in heads:
        o = (_dot(u['p_loc'].astype(BF16), v_ref[pl.ds(start, n_loc), u['ps']])
             + _dot(u['p_ctx'].astype(BF16), cv_ref[0, :, u['ps']]))
        u['o'] = o / u['den']
    for p in range(NA_HEADS // 2):
        o_ref[:, p * LANES:(p + 1) * LANES] = jnp.where(lo, heads[2 * p]['o'], heads[2 * p + 1]['o']).astype(BF16)


def na_attention(lay, q, kb, vb, ck, cv, bias_tab):
    rows = lay.dec_len // GRID_W
    assert rows >= NA_WIN_ROWS
    qoff = lay.ctx_tokens // GRID_W
    koff = lay.ctx_tokens // lay.dec_len
    assert lay.ctx_tokens % lay.dec_len == 0
    kv = pl.BlockSpec((lay.dec_len, NA_WIDTH), lambda b, r: (koff + b, 0))
    cs = pl.BlockSpec((1, ck.shape[1], NA_WIDTH), lambda b, r: (b, 0, 0))
    return pl.pallas_call(
        functools.partial(_na_kernel, rows=rows),
        out_shape=jax.ShapeDtypeStruct((lay.dec_tokens, NA_WIDTH), BF16),
        grid=(lay.n_dec, rows),
        in_specs=[pl.BlockSpec((GRID_W, NA_WIDTH), lambda b, r: (qoff + b * rows + r, 0)),
                  kv, kv, cs, cs,
                  pl.BlockSpec((1, NA_HEADS, GRID_W, NA_WIN_ROWS * GRID_W),
                               lambda b, r: (_na_row_start(r, rows) - r + NA_WIN_ROWS - 1, 0, 0, 0))],
        out_specs=pl.BlockSpec((GRID_W, NA_WIDTH), lambda b, r: (b * rows + r, 0)),
        compiler_params=_cparams(("arbitrary", "arbitrary")),
        name="na_attention",
    )(q, kb, vb, ck, cv, bias_tab)


def _softplus(x):
    return jnp.maximum(x, 0.0) + jnp.log(1.0 + jnp.exp(-jnp.abs(x)))


def _rwkv_prepare(d, z, prev_row, next_row, prm):
    (mu_ref, w0_ref, w2_ref, a0_ref, a2_ref, kk_ref, ka_ref, rk_ref) = prm
    c = CHUNK
    cw = RW_WIDTH
    row = lax.broadcasted_iota(jnp.int32, (c, 1), 0)
    prev = jnp.where(row == 0, prev_row, pltpu.roll(z, 1, axis=0))
    nxt = jnp.where(row == c - 1, next_row, pltpu.roll(z, c - 1, axis=0))
    zs = z + mu_ref[0:1, :] * (prev - z) + mu_ref[1:2, :] * (nxt - z)
    r, k, v = zs[:, :cw], zs[:, cw:2 * cw], zs[:, 2 * cw:3 * cw]
    o = 3 * cw
    zw = zs[:, o + d * RW_DECAY_RANK:o + (d + 1) * RW_DECAY_RANK]
    o += 2 * RW_DECAY_RANK
    za = zs[:, o + d * RW_ICLR_RANK:o + (d + 1) * RW_ICLR_RANK]

    w_log = -_softplus(-(w0_ref[d:d + 1, :] + _mm3(jnp.tanh(zw), w2_ref[d]))) - 0.5
    logw = -jnp.exp(w_log)
    a = _sigmoid(a0_ref[d:d + 1, :] + _mm3(za, a2_ref[d]))
    kd = k * (1.0 + (a - 1.0) * ka_ref[...])
    kk = k * kk_ref[...]

    ti = lax.broadcasted_iota(jnp.int32, (c, c), 0)
    tj = lax.broadcasted_iota(jnp.int32, (c, c), 1)
    tri = ((ti >= tj) if d == 0 else (tj >= ti)).astype(BF16)
    lh, ll = _split(logw)
    cl = _dot(tri, lh) + _dot(tri, ll)
    p_in = jnp.exp(cl)
    p_ex = jnp.exp(cl - logw)
    p_inv = jnp.exp(-cl)
    p_end = p_in[c - 1:c, :] if d == 0 else p_in[0:1, :]

    pi = lax.broadcasted_iota(jnp.int32, (LANES, LANES), 0)
    pj = lax.broadcasted_iota(jnp.int32, (LANES, LANES), 1)
    same_head = (pi // RW_HEAD_DIM) == (pj // RW_HEAD_DIM)
    ones_bd = same_head.astype(BF16)

    def head_sum(x):
        xh, xl = _split(x)
        return _dot(xh, ones_bd) + _dot(xl, ones_bd)

    t2 = lax.broadcasted_iota(jnp.int32, (2 * c, LANES), 0) % c
    j2 = lax.broadcasted_iota(jnp.int32, (2 * c, LANES), 1) % c
    diff = (t2 - j2) if d == 0 else (j2 - t2)
    upper = lax.broadcasted_iota(jnp.int32, (2 * c, LANES), 0) < c
    mask = diff >= jnp.where(upper, 1, 0)
    units = []
    for p in range(RW_HEADS // 2):
        sl = slice(p * LANES, (p + 1) * LANES)
        kkp = kk[:, sl]
        kkp = kkp * lax.rsqrt(jnp.maximum(head_sum(kkp * kkp), 1e-24))
        pinv = p_inv[:, sl]
        bt = a[:, sl] * kkp * pinv
        kt = kd[:, sl] * pinv
        ar = jnp.concatenate([-kkp * p_ex[:, sl], r[:, sl] * p_in[:, sl]], axis=0).astype(BF16)
        pe = p_end[:, sl]
        bonus = head_sum(r[:, sl] * kd[:, sl] * rk_ref[:, sl]) * v[:, sl]
        units.append(dict(d=d, p=p, ar=ar, bt=bt, kt=kt, v=v[:, sl], pe=pe,
                          bkp=jnp.concatenate([bt * pe, kt * pe], axis=0), bonus=bonus,
                          mask=mask, eye=(diff[:c] == 0).astype(F32), bd=same_head))
    return units, zs


def _bd(x):
    lo = lax.broadcasted_iota(jnp.int32, x.shape, 1) < RW_HEAD_DIM
    xb = x.astype(BF16)
    z = jnp.zeros_like(xb)
    return jnp.concatenate([jnp.where(lo, xb, z), jnp.where(lo, z, xb)], axis=0)


def _rwkv_solve(units, s_ref):
    c = CHUNK
    for u in units:
        gb = _dot(u['ar'], _bd(u['bt']), _NT)
        gk = _dot(u['ar'], _bd(u['kt']), _NT)
        gb = jnp.where(u['mask'], gb, 0.0)
        u['a_ab'], u['a_rb'] = gb[:c], gb[c:]
        u['s0'] = s_ref[u['d'], u['p']]
        u['xs'] = _dot(u['ar'], u['s0'].astype(BF16), _NT) + _mm(jnp.where(u['mask'], gk, 0.0), _bd(u['v']))
    for u in units:
        u['pw'] = _mm(u['a_ab'], _bd(u['a_ab']))
        u['tinv'] = u['eye'] + u['a_ab']
    n_sq = CHUNK.bit_length() - 2
    for i in range(n_sq):
        for u in units:
            if i < n_sq - 1:
                both = _mm(jnp.concatenate([u['pw'], u['tinv']], axis=0), _bd(u['pw']))
                u['pw'], u['tinv'] = both[:c], u['tinv'] + both[c:]
            else:
                u['tinv'] = u['tinv'] + _mm(u['tinv'], _bd(u['pw']))
    for u in units:
        u['u'] = _mm(u['tinv'], _bd(u['xs'][:c]))
    ys = []
    for u in units:
        ys.append(u['xs'][c:] + _mm(u['a_rb'], _bd(u['u'])) + u['bonus'])
        uv = jnp.concatenate([u['u'], u['v']], axis=0)
        s_ref[u['d'], u['p']] = u['s0'] * u['pe'] + jnp.where(u['bd'], _mm(uv.T, u['bkp']), 0.0)
    return ys


RW_SEQS = 2


def _rwkv_kernel(*refs, nc, has_s0):
    zf_ref, zfp_ref, zfn_ref, zr_ref, zrp_ref, zrn_ref = refs[:6]
    n_in = 6 + (1 if has_s0 else 0)
    s0_ref = refs[6] if has_s0 else None
    prm = refs[n_in:n_in + 8]
    g2_ref = refs[n_in + 8]
    yf_ref, yb_ref, g_ref, sfin_ref, s_ref = refs[n_in + 9:]
    i = pl.program_id(1)

    hd = RW_HEAD_DIM
    npair = RW_HEADS // 2

    @pl.when(i == 0)
    def _():
        s_ref[...] = jnp.zeros(s_ref.shape, F32)
        if has_s0:
            for k in range(RW_SEQS):
                for d in range(2):
                    for p in range(npair):
                        s_ref[2 * k + d, p, :hd, :hd] = s0_ref[k, d, 2 * p]
                        s_ref[2 * k + d, p, hd:, hd:] = s0_ref[k, d, 2 * p + 1]

    zero = jnp.zeros((1, B_COLS), F32)
    units = []
    for k in range(RW_SEQS):
        pf = jnp.where(i == 0, zero, zfp_ref[k, 7:8, :])
        nf = jnp.where(i == nc - 1, zero, zfn_ref[k, 0:1, :])
        uf, zs = _rwkv_prepare(0, zf_ref[k], pf, nf, prm)
        g_ref[k] = _mm(_sigmoid(zs[:, B_COLS - RW_GATE_RANK:]), g2_ref[...])
        pr = jnp.where(i == nc - 1, zero, zrp_ref[k, 7:8, :])
        nr = jnp.where(i == 0, zero, zrn_ref[k, 0:1, :])
        ub, _ = _rwkv_prepare(1, zr_ref[k], pr, nr, prm)
        for u in uf + ub:
            u['d'] = 2 * k + u['d']
        units += uf + ub
    ys = _rwkv_solve(units, s_ref)
    for k in range(RW_SEQS):
        yk = ys[2 * npair * k:2 * npair * (k + 1)]
        yf_ref[k] = jnp.concatenate(yk[:npair], axis=-1)
        yb_ref[k] = jnp.concatenate(yk[npair:], axis=-1)

    @pl.when(i == nc - 1)
    def _():
        for k in range(RW_SEQS):
            for d in range(2):
                for p in range(npair):
                    sfin_ref[k, d, 2 * p] = s_ref[2 * k + d, p, :hd, :hd]
                    sfin_ref[k, d, 2 * p + 1] = s_ref[2 * k + d, p, hd:, hd:]


def rwkv_scan(zb, seq0, nseq, s0, prm, g2):
    seqlen = zb.shape[1]
    ns = RW_SEQS
    assert seq0 % ns == 0 and nseq % ns == 0 and seqlen % CHUNK == 0
    nc = seqlen // CHUNK
    sub = CHUNK // 8
    b0 = seq0 // ns
    has_s0 = s0 is not None

    def specs(cidx):
        return [pl.BlockSpec((ns, CHUNK, B_COLS), lambda s, i: (b0 + s, cidx(i), 0)),
                pl.BlockSpec((ns, 8, B_COLS), lambda s, i: (b0 + s, jnp.maximum(cidx(i) * sub - 1, 0), 0)),
                pl.BlockSpec((ns, 8, B_COLS),
                             lambda s, i: (b0 + s, jnp.minimum((cidx(i) + 1) * sub, seqlen // 8 - 1), 0))]

    cf = lambda i: i
    cr = lambda i: nc - 1 - i
    full = lambda a: pl.BlockSpec(a.shape, lambda s, i: (0,) * a.ndim)
    sshape = (2, RW_HEADS, RW_HEAD_DIM, RW_HEAD_DIM)
    sspec = pl.BlockSpec((ns,) + sshape, lambda s, i: (s, 0, 0, 0, 0))
    in_specs = specs(cf) + specs(cr)
    args = [zb] * 6
    if has_s0:
        in_specs.append(sspec)
        args.append(s0)
    in_specs += [full(p) for p in prm] + [full(g2)]
    args += list(prm) + [g2]
    out = lambda cidx: pl.BlockSpec((ns, CHUNK, RW_WIDTH), lambda s, i: (s, cidx(i), 0))
    tok = jax.ShapeDtypeStruct((nseq, seqlen, RW_WIDTH), F32)
    yf, yb, g, sfin = pl.pallas_call(
        functools.partial(_rwkv_kernel, nc=nc, has_s0=has_s0),
        out_shape=(tok, tok, tok, jax.ShapeDtypeStruct((nseq,) + sshape, F32)),
        grid=(nseq // ns, nc),
        in_specs=in_specs,
        out_specs=(out(cf), out(cr), out(cf), sspec),
        scratch_shapes=[pltpu.VMEM((2 * ns, RW_HEADS // 2, LANES, LANES), F32)],
        compiler_params=_cparams(("arbitrary", "arbitrary")),
        name="rwkv_scan",
    )(*args)
    flat = lambda a: a.reshape(nseq * seqlen, RW_WIDTH)
    return flat(yf), flat(yb), flat(g), sfin


SUBLANES = 8


def _conv_kernel(z_ref, zp_ref, zn_ref, w_ref, b_ref, g_ref, be_ref, o_ref, h_ref, hs_ref, *, lay):
    i = pl.program_id(0)
    first, last = lay.tile_pos(i)

    def glu(z):
        return z[:, :CONV_WIDTH] * _sigmoid(z[:, CONV_WIDTH:])

    hz = jnp.zeros((CONV_HALO, CONV_WIDTH), F32)
    h_ref[0:CONV_HALO, :] = jnp.where(first, hz, glu(zp_ref[...]))
    h_ref[CONV_HALO:CONV_HALO + TM, :] = glu(z_ref[...])
    h_ref[CONV_HALO + TM:, :] = jnp.where(last, hz, glu(zn_ref[...]))
    off = CONV_HALO - CONV_K // 2
    rows = hs_ref.shape[1]
    for ph in range(SUBLANES):
        hs_ref[ph] = h_ref[ph:ph + rows, :]
    acc = jnp.zeros((TM, CONV_WIDTH), F32) + b_ref[...]
    for j in range(CONV_K):
        ph, base = (off + j) % SUBLANES, (off + j) // SUBLANES * SUBLANES
        acc = acc + w_ref[j:j + 1, :] * hs_ref[ph, base:base + TM, :]
    xc = acc - jnp.mean(acc, axis=-1, keepdims=True)
    hn = xc * lax.rsqrt(jnp.mean(xc * xc, axis=-1, keepdims=True) + LN_EPS) * g_ref[...] + be_ref[...]
    o_ref[...] = (hn * _sigmoid(hn)).astype(BF16)


def conv_module(lay, zc, w, b, g, be):
    n = lay.n
    hb = TM // CONV_HALO
    vec = pl.BlockSpec((1, CONV_WIDTH), lambda i: (0, 0))
    return pl.pallas_call(
        functools.partial(_conv_kernel, lay=lay),
        out_shape=jax.ShapeDtypeStruct((n, CONV_WIDTH), BF16),
        grid=(lay.tiles,),
        in_specs=[pl.BlockSpec((TM, C_COLS), lambda i: (i, 0)),
                  pl.BlockSpec((CONV_HALO, C_COLS), lambda i: (jnp.maximum(i * hb - 1, 0), 0)),
                  pl.BlockSpec((CONV_HALO, C_COLS), lambda i: (jnp.minimum((i + 1) * hb, n // CONV_HALO - 1), 0)),
                  pl.BlockSpec((CONV_K, CONV_WIDTH), lambda i: (0, 0)), vec, vec, vec],
        out_specs=pl.BlockSpec((TM, CONV_WIDTH), lambda i: (i, 0)),
        scratch_shapes=[pltpu.VMEM((TM + 2 * CONV_HALO, CONV_WIDTH), F32),
                        pltpu.VMEM((SUBLANES, TM + 2 * CONV_HALO - SUBLANES, CONV_WIDTH), F32)],
        compiler_params=_cparams(("arbitrary",)),
        name="conv_module",
    )(zc, zc, zc, w, b, g, be)


def _merge_kernel(x_ref, oa_c_ref, yf_c_ref, yb_c_ref, g_c_ref, oa_d_ref, yf_d_ref, yb_d_ref, g_d_ref,
                  oc_ref, gt_ref, mod_ref, ng_ref, lg_ref, lb_ref,
                  wa_ref, wr_ref, wc_ref, wo_ref, rw_ref, rb_ref,
                  x1_ref, hm_ref, ti_ref, tg_ref, rk_ref, cnt_ref, *, ctx_tiles):
    is_ctx = pl.program_id(0) < ctx_tiles
    pick = lambda c_ref, d_ref: jnp.where(is_ctx, c_ref[...], d_ref[...])
    oa = pick(oa_c_ref, oa_d_ref)
    g_rw = pick(g_c_ref, g_d_ref)
    y = pick(yf_c_ref, yf_d_ref) + pick(yb_c_ref, yb_d_ref)
    pi = lax.broadcasted_iota(jnp.int32, (LANES, LANES), 0)
    pj = lax.broadcasted_iota(jnp.int32, (LANES, LANES), 1)
    ones_bd = ((pi // RW_HEAD_DIM) == (pj // RW_HEAD_DIM)).astype(BF16)

    def head_mean(v):
        vh, vl = _split(v)
        return (_dot(vh, ones_bd) + _dot(vl, ones_bd)) * (1.0 / RW_HEAD_DIM)

    outs = []
    for p in range(RW_HEADS // 2):
        yp = y[:, p * LANES:(p + 1) * LANES]
        yc = yp - head_mean(yp)
        outs.append(yc * lax.rsqrt(head_mean(yc * yc) + GN_EPS))
    o_rw = (jnp.concatenate(outs, axis=-1) * lg_ref[...] + lb_ref[...]) * g_rw
    merged = (gt_ref[:, :D_MODEL].astype(F32) * _dot(oa, wa_ref[...])
              + gt_ref[:, D_MODEL:2 * D_MODEL].astype(F32) * _mm(o_rw, wr_ref[...])
              + gt_ref[:, 2 * D_MODEL:].astype(F32) * _dot(oc_ref[...], wc_ref[...]))
    mix = _mm(merged, wo_ref[...])
    x1 = x_ref[...] + mod_ref[0, 2:3, :] * _rms(mix, ng_ref[1:2, :])
    x1_ref[...] = x1
    hm = _rms(x1, ng_ref[2:3, :]) * (1.0 + mod_ref[0, 4:5, :]) + mod_ref[0, 3:4, :]
    hm_ref[...] = hm
    logits = _mm3(hm, rw_ref[...]) + rb_ref[...]
    lane = lax.broadcasted_iota(jnp.int32, logits.shape, 1)
    idx_out = jnp.zeros(logits.shape, jnp.int32)
    val_out = jnp.full(logits.shape, NEG_INF, F32)
    picks = []
    for j in range(TOP_K):
        m = jnp.max(logits, axis=-1, keepdims=True)
        idx = jnp.min(jnp.where(logits == m, lane, LANES), axis=-1, keepdims=True)
        idx_out = jnp.where(lane == j, idx, idx_out)
        val_out = jnp.where(lane == j, m, val_out)
        picks.append(lane == idx)
        logits = jnp.where(picks[-1], -jnp.inf, logits)
    e = jnp.exp(val_out - jnp.max(val_out, axis=-1, keepdims=True))
    ti_ref[...] = idx_out
    tg_ref[...] = e / jnp.sum(e, axis=-1, keepdims=True)

    chosen = jnp.zeros(logits.shape, F32)
    for pk in picks:
        chosen = chosen + pk.astype(F32)
    ti_ = lax.broadcasted_iota(jnp.int32, (TM, TM), 0)
    tj_ = lax.broadcasted_iota(jnp.int32, (TM, TM), 1)
    before = _dot((ti_ > tj_).astype(BF16), chosen.astype(BF16))
    rank = jnp.zeros(logits.shape, F32)
    for j, pk in enumerate(picks):
        rank = jnp.where(lane == j, jnp.sum(jnp.where(pk, before, 0.0), axis=-1, keepdims=True), rank)
    rk_ref[...] = rank.astype(jnp.int32)
    cnt_ref[0] = jnp.sum(chosen, axis=0, keepdims=True)


def merge_router(lay, x, ctx_br, dec_br, o_cv, gates, mod, ng, lg, lb, wa, wr, wc, wo, rw, rb):
    n = lay.n
    row = lambda w: pl.BlockSpec((TM, w), lambda i: (i, 0))
    crow = lambda w: pl.BlockSpec((TM, w), lambda i: (jnp.minimum(i, lay.ctx_tiles - 1), 0))
    drow = lambda w: pl.BlockSpec((TM, w), lambda i: (jnp.maximum(i - lay.ctx_tiles, 0), 0))
    full = lambda a: pl.BlockSpec(a.shape, lambda i: (0,) * a.ndim)
    br_w = (NA_WIDTH, RW_WIDTH, RW_WIDTH, RW_WIDTH)
    return pl.pallas_call(
        functools.partial(_merge_kernel, ctx_tiles=lay.ctx_tiles),
        out_shape=(jax.ShapeDtypeStruct((n, D_MODEL), F32), jax.ShapeDtypeStruct((n, D_MODEL), F32),
                   jax.ShapeDtypeStruct((n, LANES), jnp.int32), jax.ShapeDtypeStruct((n, LANES), F32),
                   jax.ShapeDtypeStruct((n, LANES), jnp.int32), jax.ShapeDtypeStruct((lay.tiles, 1, LANES), F32)),
        grid=(lay.tiles,),
        in_specs=[row(D_MODEL)] + [crow(w) for w in br_w] + [drow(w) for w in br_w] + [
                  row(CONV_WIDTH),
                  row(G_COLS), pl.BlockSpec((1, N_MOD, D_MODEL), lambda i: (lay.mod_row(i), 0, 0)),
                  full(ng), full(lg), full(lb), full(wa), full(wr), full(wc), full(wo), full(rw), full(rb)],
        out_specs=(row(D_MODEL), row(D_MODEL), row(LANES), row(LANES), row(LANES),
                   pl.BlockSpec((1, 1, LANES), lambda i: (i, 0, 0))),
        compiler_params=_cparams(("arbitrary",)),
        name="merge_router",
    )(x, *ctx_br, *dec_br, o_cv, gates, mod, ng, lg, lb, wa, wr, wc, wo, rw, rb)


RUN_ALIGN = 8
RUN_BITS = (TM // RUN_ALIGN).bit_length()
SORT_ROWS = -(-(TM * TOP_K + N_EXPERTS * (RUN_ALIGN - 1)) // LANES) * LANES


def route_tables(top_i, wrank, cnt, n):
    tiles = n // TM
    cnt = cnt[:, 0, :N_EXPERTS].astype(jnp.int32)
    run = ((cnt + RUN_ALIGN - 1) // RUN_ALIGN) * RUN_ALIGN
    off = jnp.cumsum(run, axis=1) - run
    before = jnp.cumsum(run, axis=0) - run
    region = jnp.sum(run, axis=0)
    padded = ((region + ROUTE_BLOCK - 1) // ROUTE_BLOCK) * ROUTE_BLOCK
    pad_end = jnp.cumsum(padded)
    first = (pad_end - padded)[None, :] + before
    table = jnp.concatenate([off, first, run, jnp.zeros((tiles, LANES - 3 * N_EXPERTS), jnp.int32)], axis=1)
    onehot = top_i.reshape(tiles, TM, TOP_K, 1) == jnp.arange(N_EXPERTS, dtype=jnp.int32)
    pos = jnp.sum(jnp.where(onehot, off[:, None, None, :], 0), axis=-1) + wrank.reshape(tiles, TM, TOP_K)
    n_blocks = -(-tiles * (TM * TOP_K + N_EXPERTS * (RUN_ALIGN - 1)) // ROUTE_BLOCK) + N_EXPERTS
    starts = jnp.arange(n_blocks, dtype=jnp.int32) * ROUTE_BLOCK
    block_e = jnp.minimum(jnp.sum((pad_end[None, :] <= starts[:, None]).astype(jnp.int32), axis=1),
                          N_EXPERTS - 1)
    n_used = (pad_end[-1] // ROUTE_BLOCK).astype(jnp.int32).reshape(1)
    return table.reshape(tiles, 1, LANES), pos, block_e, n_used, n_blocks


def _for_each_piece(tab_ref, fn):
    def body(e, carry):
        off = tab_ref[0, 0, e]
        first = tab_ref[0, 0, N_EXPERTS + e]
        run = tab_ref[0, 0, 2 * N_EXPERTS + e]
        for k in range(RUN_BITS):
            size = RUN_ALIGN << k

            @pl.when((run & size) != 0)
            def _():
                done = run & (-2 * size)
                fn(pl.multiple_of(off + done, RUN_ALIGN), pl.multiple_of(first + done, RUN_ALIGN), size)
        return carry

    lax.fori_loop(0, N_EXPERTS, body, 0)


def _dispatch_kernel(tab_ref, tabp_ref, pos_ref, hm_ref, xb_in_ref, xb_ref, srt_ref, sems):
    del xb_in_ref
    i = pl.program_id(0)
    slot = i % 2

    def copy(slot_, off, row, size):
        return pltpu.make_async_copy(srt_ref.at[slot_, pl.ds(off, size), :], xb_ref.at[pl.ds(row, size), :],
                                     sems.at[slot_])

    p_iota = lax.broadcasted_iota(jnp.int32, (SORT_ROWS, TM), 0)
    hit = p_iota == pos_ref[0, 0:1, :]
    for j in range(1, TOP_K):
        hit = hit | (p_iota == pos_ref[0, j:j + 1, :])
    srt_ref[slot] = _dot(jnp.where(hit, 1.0, 0.0).astype(BF16), hm_ref[...].astype(BF16))
    _for_each_piece(tab_ref, lambda off, row, size: copy(slot, off, row, size).start())

    @pl.when(i > 0)
    def _():
        _for_each_piece(tabp_ref, lambda off, row, size: copy(1 - slot, off, row, size).wait())

    @pl.when(i == pl.num_programs(0) - 1)
    def _():
        _for_each_piece(tab_ref, lambda off, row, size: copy(slot, off, row, size).wait())


def moe_dispatch(lay, hm, table, pos, xb0):
    n_rows = xb0.shape[0]
    post = jnp.pad(pos.transpose(0, 2, 1), ((0, 0), (0, 8 - TOP_K), (0, 0)))
    tab = lambda f: pl.BlockSpec((1, 1, LANES), lambda i: (f(i), 0, 0), memory_space=pltpu.SMEM)
    return pl.pallas_call(
        _dispatch_kernel,
        out_shape=jax.ShapeDtypeStruct((n_rows, D_MODEL), F32),
        grid=(lay.tiles,),
        in_specs=[tab(lambda i: i), tab(lambda i: jnp.maximum(i - 1, 0)),
                  pl.BlockSpec((1, 8, TM), lambda i: (i, 0, 0)),
                  pl.BlockSpec((TM, D_MODEL), lambda i: (i, 0)),
                  pl.BlockSpec(memory_space=pl.ANY)],
        out_specs=pl.BlockSpec(memory_space=pl.ANY),
        scratch_shapes=[pltpu.VMEM((2, SORT_ROWS, D_MODEL), F32), pltpu.SemaphoreType.DMA((2,))],
        input_output_aliases={4: 0},
        compiler_params=_cparams(("arbitrary",)),
        name="moe_dispatch",
    )(table, table, post, hm, xb0)


GL_GROUP = 2 * LANES


def _regroup_perm():
    r = jnp.arange(GL_GROUP, dtype=jnp.int32)[:, None]
    c = jnp.arange(GL_GROUP, dtype=jnp.int32)[None, :]
    src = jnp.where(c < LANES, 2 * c, 2 * (c - LANES) + 1)
    return (r == src).astype(BF16)


def _expert_kernel(be_ref, nu_ref, x_ref, w1_ref, perm_ref, b1_ref, w2_ref, b2_ref, y_ref, w1b_ref, w2b_ref):
    b = pl.program_id(0)
    used = b < nu_ref[0]

    @pl.when(used & ((b == 0) | (be_ref[b] != be_ref[jnp.maximum(b - 1, 0)])))
    def _():
        for j in range(2 * EXPERT_FF // GL_GROUP):
            cs = slice(j * GL_GROUP, (j + 1) * GL_GROUP)
            w1b_ref[:, cs] = _dot(w1_ref[:, cs].astype(BF16), perm_ref[...]).astype(BF16)
        w2b_ref[...] = w2_ref[...].astype(BF16)

    @pl.when(used)
    def _():
        x = x_ref[...].astype(BF16)
        acc = jnp.zeros((ROUTE_BLOCK, D_MODEL), F32) + b2_ref[0]
        for j in range(EXPERT_FF // GL_GROUP):
            acts = []
            for g in range(2):
                c0 = (2 * j + g) * GL_GROUP
                h = _dot(x, w1b_ref[:, c0:c0 + GL_GROUP]) + b1_ref[0, :, c0:c0 + GL_GROUP]
                hg = jnp.minimum(h[:, :LANES], SWIGLU_LIMIT)
                hl = jnp.clip(h[:, LANES:], -SWIGLU_LIMIT, SWIGLU_LIMIT)
                acts.append(hg * _sigmoid(SWIGLU_ALPHA * hg) * (hl + 1.0))
            act = jnp.concatenate(acts, axis=-1).astype(BF16)
            acc = acc + _dot(act, w2b_ref[j * GL_GROUP:(j + 1) * GL_GROUP, :])
        y_ref[...] = acc

    @pl.when(jnp.logical_not(used))
    def _():
        y_ref[...] = jnp.zeros(y_ref.shape, F32)


def moe_experts(xb, block_e, n_used, n_blocks, l, w1, b1, w2, b2):
    def last_used(b, nu):
        return jnp.minimum(b, jnp.maximum(nu[0] - 1, 0))

    wspec = lambda a: pl.BlockSpec((None, None) + a.shape[2:], lambda b, be, nu: (l, be[last_used(b, nu)], 0, 0))
    bspec = lambda a: pl.BlockSpec((1,) + a.shape[1:], lambda b, be, nu: (be[last_used(b, nu)], 0, 0))
    return pl.pallas_call(
        _expert_kernel,
        out_shape=jax.ShapeDtypeStruct(xb.shape, F32),
        grid_spec=pltpu.PrefetchScalarGridSpec(
            num_scalar_prefetch=2,
            grid=(n_blocks,),
            in_specs=[pl.BlockSpec((ROUTE_BLOCK, D_MODEL), lambda b, be, nu: (last_used(b, nu), 0)),
                      wspec(w1), pl.BlockSpec((GL_GROUP, GL_GROUP), lambda b, be, nu: (0, 0)),
                      bspec(b1), wspec(w2), bspec(b2)],
            out_specs=pl.BlockSpec((ROUTE_BLOCK, D_MODEL), lambda b, be, nu: (b, 0)),
            scratch_shapes=[pltpu.VMEM(w1.shape[2:], BF16), pltpu.VMEM(w2.shape[2:], BF16)]),
        compiler_params=_cparams(("arbitrary",)),
        name="moe_experts",
    )(block_e, n_used, xb, w1, _regroup_perm(), b1, w2, b2)


def _combine_kernel(tab_ref, tabn_ref, yb_ref, pos_ref, tg_ref, x1_ref, mod_ref, ng_ref, o_ref, stg_ref, sems):
    i = pl.program_id(0)
    last = pl.num_programs(0) - 1
    slot = i % 2

    def copy(slot_, off, row, size):
        return pltpu.make_async_copy(yb_ref.at[pl.ds(row, size), :], stg_ref.at[slot_, pl.ds(off, size), :],
                                     sems.at[slot_])

    @pl.when(i == 0)
    def _():
        stg_ref[...] = jnp.zeros(stg_ref.shape, F32)
        _for_each_piece(tab_ref, lambda off, row, size: copy(slot, off, row, size).start())

    @pl.when(i < last)
    def _():
        _for_each_piece(tabn_ref, lambda off, row, size: copy(1 - slot, off, row, size).start())

    _for_each_piece(tab_ref, lambda off, row, size: copy(slot, off, row, size).wait())
    p_iota = lax.broadcasted_iota(jnp.int32, (TM, SORT_ROWS), 1)
    w = jnp.zeros((TM, SORT_ROWS), F32)
    for j in range(TOP_K):
        w = w + jnp.where(p_iota == pos_ref[:, j:j + 1], tg_ref[:, j:j + 1], 0.0)
    y = _mm3(w, stg_ref[slot])
    o_ref[...] = x1_ref[...] + mod_ref[0, 5:6, :] * _rms(y, ng_ref[3:4, :])


def moe_combine(lay, yb, table, pos, tg, x1, mod, ng):
    n = lay.n
    posl = jnp.pad(pos.reshape(n, TOP_K), ((0, 0), (0, LANES - TOP_K)))
    row = lambda w: pl.BlockSpec((TM, w), lambda i: (i, 0))
    tab = lambda f: pl.BlockSpec((1, 1, LANES), lambda i: (f(i), 0, 0), memory_space=pltpu.SMEM)
    return pl.pallas_call(
        _combine_kernel,
        out_shape=jax.ShapeDtypeStruct((n, D_MODEL), F32),
        grid=(lay.tiles,),
        in_specs=[tab(lambda i: i), tab(lambda i: jnp.minimum(i + 1, lay.tiles - 1)),
                  pl.BlockSpec(memory_space=pl.ANY),
                  row(LANES), row(LANES), row(D_MODEL),
                  pl.BlockSpec((1, N_MOD, D_MODEL), lambda i: (lay.mod_row(i), 0, 0)),
                  pl.BlockSpec(ng.shape, lambda i: (0, 0))],
        out_specs=row(D_MODEL),
        scratch_shapes=[pltpu.VMEM((2, SORT_ROWS, D_MODEL), F32), pltpu.SemaphoreType.DMA((2,))],
        compiler_params=_cparams(("arbitrary",)),
        name="moe_combine",
    )(table, table, yb, posl, tg, x1, mod, ng)


def layer(lay, x, l, cond, P, ctx_k, ctx_v, s0_dec, xb_buf):
    mod = mod_table(cond, P['mod_w'][l], P['mod_b'][l]).reshape(cond.shape[0], N_MOD, D_MODEL)
    ng = P['norm_g'][l]
    q, kf, vf, kb, vb, zb, zc, gates = in_proj(lay, x, mod, ng[0:1], P['w_in'][l].astype(BF16),
                                               P['b_in'][l].reshape(1, P_IN))
    oa_c = ctx_attention(lay, q, kb, vb)
    oa_d = na_attention(lay, q, kb, vb, ctx_k.astype(BF16), ctx_v.astype(BF16), na_bias_table(P['rpb'][l]))
    vec = lambda a: a.reshape(1, -1)
    prm = (P['rw_mu'][l], P['rw_w0'][l], P['rw_w2'][l], P['rw_a0'][l], P['rw_a2'][l],
           vec(P['rw_k_k'][l]), vec(P['rw_k_a'][l]), vec(P['rw_r_k'][l]))
    yf_c, yb_c, g_c, s_ctx = rwkv_scan(zb.reshape(-1, lay.ctx_len, B_COLS), 0, lay.n_ctx, None, prm,
                                       P['rw_g2'][l])
    yf_d, yb_d, g_d, _ = rwkv_scan(zb.reshape(-1, lay.dec_len, B_COLS), lay.ctx_tokens // lay.dec_len,
                                   lay.n_dec, s0_dec, prm, P['rw_g2'][l])
    o_cv = conv_module(lay, zc, P['conv_w'][l], vec(P['conv_b'][l]), vec(P['conv_ln_g'][l]),
                       vec(P['conv_ln_b'][l]))
    rw = jnp.zeros((D_MODEL, LANES), F32).at[:, :N_EXPERTS].set(P['router_w'][l])
    rb = jnp.full((1, LANES), NEG_INF, F32).at[0, :N_EXPERTS].set(P['router_b'][l])
    x1, hm, top_i, top_g, rank, counts = merge_router(
        lay, x, (oa_c, yf_c, yb_c, g_c), (oa_d, yf_d, yb_d, g_d), o_cv, gates, mod, ng,
        vec(P['rw_lnx_g'][l]), vec(P['rw_lnx_b'][l]),
        P['w_o_attn'][l].astype(BF16), P['w_o_rwkv'][l].astype(BF16), P['w_o_conv'][l].astype(BF16),
        P['w_out'][l].astype(BF16), rw, rb)
    table, pos, block_e, n_used, n_blocks = route_tables(top_i[:, :TOP_K], rank[:, :TOP_K], counts, lay.n)
    if xb_buf is None:
        xb_buf = jnp.zeros((n_blocks * ROUTE_BLOCK, D_MODEL), F32)
    xb = moe_dispatch(lay, hm, table, pos, xb_buf)
    b1 = P['exp_b1'][l].reshape(N_EXPERTS, -1, LANES, 2).transpose(0, 1, 3, 2).reshape(N_EXPERTS, 1, -1)
    ybk = moe_experts(xb, block_e, n_used, n_blocks, l, P['exp_w1'], b1, P['exp_w2'], P['exp_b2'][l][:, None, :])
    x2 = moe_combine(lay, ybk, table, pos, top_g, x1, mod, ng)
    return x2, kf, vf, s_ctx, xb


def kernel(x_prompt, x_sample, cache_attn_k, cache_attn_v, state_rwkv, c, c_ctx, mod_w, mod_b, norm_g, w_in, b_in, rpb, w_o_attn, rw_mu, rw_w0, rw_w2, rw_a0, rw_a2, rw_g2, rw_k_k, rw_k_a, rw_r_k, rw_lnx_g, rw_lnx_b, w_o_rwkv, conv_w, conv_b, conv_ln_g, conv_ln_b, w_o_conv, w_out, router_w, router_b, exp_w1, exp_b1, exp_w2, exp_b2):
    P = {
        'mod_w': mod_w, 'mod_b': mod_b, 'norm_g': norm_g, 'w_in': w_in, 'b_in': b_in, 'rpb': rpb,
        'w_o_attn': w_o_attn, 'rw_mu': rw_mu, 'rw_w0': rw_w0, 'rw_w2': rw_w2, 'rw_a0': rw_a0,
        'rw_a2': rw_a2, 'rw_g2': rw_g2, 'rw_k_k': rw_k_k, 'rw_k_a': rw_k_a, 'rw_r_k': rw_r_k,
        'rw_lnx_g': rw_lnx_g, 'rw_lnx_b': rw_lnx_b, 'w_o_rwkv': w_o_rwkv, 'conv_w': conv_w,
        'conv_b': conv_b, 'conv_ln_g': conv_ln_g, 'conv_ln_b': conv_ln_b, 'w_o_conv': w_o_conv,
        'w_out': w_out, 'router_w': router_w, 'router_b': router_b, 'exp_w1': exp_w1,
        'exp_b1': exp_b1, 'exp_w2': exp_w2, 'exp_b2': exp_b2,
    }
    nb, sl, _ = x_prompt.shape
    db, ds, _ = x_sample.shape
    lay = Layout(nb, sl, db, ds)
    x = jnp.concatenate([x_prompt.reshape(-1, D_MODEL), x_sample.reshape(-1, D_MODEL)], axis=0)
    cond = jnp.zeros((8, D_MODEL), F32).at[0].set(c_ctx).at[1:1 + db].set(c)
    ks, vs, ss = [], [], []
    xb_buf = None
    for l in range(DEPTH):
        ck = cache_attn_k[:, l].reshape(db, -1, NA_WIDTH)
        cv = cache_attn_v[:, l].reshape(db, -1, NA_WIDTH)
        x, kf, vf, s_ctx, xb_buf = layer(lay, x, l, cond, P, ck, cv, state_rwkv[:, l], xb_buf)
        ks.append(kf[:lay.ctx_tokens].reshape(nb, sl, NA_HEADS, NA_HEAD_DIM))
        vs.append(vf[:lay.ctx_tokens].reshape(nb, sl, NA_HEADS, NA_HEAD_DIM))
        ss.append(s_ctx)
    y_prompt = x[:lay.ctx_tokens].reshape(nb, sl, D_MODEL)
    y_sample = x[lay.ctx_tokens:].reshape(db, ds, D_MODEL)
    return (y_prompt, y_sample, jnp.stack(ks, axis=1), jnp.stack(vs, axis=1), jnp.stack(ss, axis=1))
```

```python
import functools

import jax
import jax.numpy as jnp
from jax import lax
from jax.experimental import pallas as pl
from jax.experimental.pallas import tpu as pltpu

F32 = jnp.float32
BF16 = jnp.bfloat16

D_MODEL = 1024
BATCH, SEQ = 32, 256
DEPTH = 2
DEC_BATCH, DEC_SEQ = 4, 4096
PAST_LEN = 256
GRID_W = 64
NA_HEADS, NA_HEAD_DIM = 8, 64
NA_WIDTH = NA_HEADS * NA_HEAD_DIM
NA_WIN_ROWS, NA_WIN_COLS = 8, 16
RW_HEADS, RW_HEAD_DIM = 8, 64
RW_WIDTH = RW_HEADS * RW_HEAD_DIM
RW_DECAY_RANK, RW_ICLR_RANK, RW_GATE_RANK = 64, 64, 128
CONV_WIDTH, CONV_K = 512, 31
N_BRANCH, N_MOD = 3, 6
N_EXPERTS, TOP_K, EXPERT_FF = 32, 4, 1024
SWIGLU_LIMIT, SWIGLU_ALPHA = 7.0, 1.702
ROUTE_BLOCK = 512
RMS_EPS, LN_EPS, GN_EPS = 1e-6, 1e-5, 64e-5
NEG_INF = -1e30
A_COLS = 3 * NA_WIDTH
B_COLS = 3 * RW_WIDTH + 2 * RW_DECAY_RANK + 2 * RW_ICLR_RANK + RW_GATE_RANK
C_COLS = 2 * CONV_WIDTH
G_COLS = N_BRANCH * D_MODEL
P_IN = A_COLS + B_COLS + C_COLS + G_COLS

LANES = 128
TM = 256
CHUNK = 64
CONV_HALO = 16
VMEM_LIMIT = 56 * 1024 * 1024


def _cparams(sem):
    return pltpu.CompilerParams(dimension_semantics=sem, vmem_limit_bytes=VMEM_LIMIT)


def _dot(a, b, dims=((1,), (0,))):
    return lax.dot_general(a, b, (dims, ((), ())), preferred_element_type=F32)


def _mm(a, b, dims=((1,), (0,))):
    return _dot(a.astype(BF16), b.astype(BF16), dims)


def _split(a):
    hi = a.astype(BF16)
    lo = (a - hi.astype(F32)).astype(BF16)
    return hi, lo


def _mm3(a, b, dims=((1,), (0,))):
    ah, al = _split(a)
    bh, bl = _split(b)
    return _dot(ah, bh, dims) + (_dot(ah, bl, dims) + _dot(al, bh, dims))


_NT = ((1,), (1,))


def _rms(x, g):
    return x * lax.rsqrt(jnp.mean(x * x, axis=-1, keepdims=True) + RMS_EPS) * g


def _sigmoid(x):
    return 1.0 / (1.0 + jnp.exp(-x))


class Layout:
    def __init__(self, n_ctx, ctx_len, n_dec, dec_len):
        assert ctx_len == TM and dec_len % TM == 0
        self.n_ctx, self.ctx_len, self.n_dec, self.dec_len = n_ctx, ctx_len, n_dec, dec_len
        self.ctx_tokens = n_ctx * ctx_len
        self.dec_tokens = n_dec * dec_len
        self.n = self.ctx_tokens + self.dec_tokens
        self.ctx_tiles = self.ctx_tokens // TM
        self.dec_tiles_per_seq = dec_len // TM
        self.tiles = self.n // TM

    def mod_row(self, i):
        return jnp.where(i < self.ctx_tiles, 0, 1 + (i - self.ctx_tiles) // self.dec_tiles_per_seq)

    def tile_pos(self, i):
        j = (i - self.ctx_tiles) % self.dec_tiles_per_seq
        is_ctx = i < self.ctx_tiles
        return is_ctx | (j == 0), is_ctx | (j == self.dec_tiles_per_seq - 1)


def _mod_kernel(c_ref, w_ref, b_ref, o_ref):
    c = c_ref[...]
    o_ref[...] = _mm3(c * _sigmoid(c), w_ref[...]) + b_ref[...]


def mod_table(cond, w, b):
    tn = 1024
    nm = w.shape[1]
    return pl.pallas_call(
        _mod_kernel,
        out_shape=jax.ShapeDtypeStruct((cond.shape[0], nm), F32),
        grid=(nm // tn,),
        in_specs=[pl.BlockSpec(cond.shape, lambda j: (0, 0)),
                  pl.BlockSpec((D_MODEL, tn), lambda j: (0, j)),
                  pl.BlockSpec((1, tn), lambda j: (0, j))],
        out_specs=pl.BlockSpec((cond.shape[0], tn), lambda j: (0, j)),
        compiler_params=_cparams(("arbitrary",)),
        name="mod_table",
    )(cond, w, b.reshape(1, nm))


def _inproj_kernel(x_ref, mod_ref, g_ref, w_ref, b_ref,
                   q_ref, kf_ref, vf_ref, kb_ref, vb_ref, zb_ref, zc_ref, gt_ref, *, ctx_tiles):
    u = (_rms(x_ref[...], g_ref[...]) * (1.0 + mod_ref[0, 1:2, :]) + mod_ref[0, 0:1, :]).astype(BF16)

    def seg(c0, width):
        return _dot(u, w_ref[:, c0:c0 + width]) + b_ref[:, c0:c0 + width]

    q_ref[...] = seg(0, NA_WIDTH).astype(BF16)
    k = seg(NA_WIDTH, NA_WIDTH)
    kb_ref[...] = k.astype(BF16)
    v = seg(2 * NA_WIDTH, NA_WIDTH)
    vb_ref[...] = v.astype(BF16)

    @pl.when(pl.program_id(0) < ctx_tiles)
    def _():
        kf_ref[...] = k
        vf_ref[...] = v
    wb = 384
    for j in range(B_COLS // wb):
        zb_ref[:, j * wb:(j + 1) * wb] = seg(A_COLS + j * wb, wb)
    wc = 512
    for j in range(C_COLS // wc):
        zc_ref[:, j * wc:(j + 1) * wc] = seg(A_COLS + B_COLS + j * wc, wc)
    for j in range(G_COLS // wc):
        gt_ref[:, j * wc:(j + 1) * wc] = _sigmoid(seg(A_COLS + B_COLS + C_COLS + j * wc, wc)).astype(BF16)


def in_proj(lay, x, mod, g0, w_bf, b):
    n = lay.n
    row = lambda w: pl.BlockSpec((TM, w), lambda i: (i, 0))
    crow = pl.BlockSpec((TM, NA_WIDTH), lambda i: (jnp.minimum(i, lay.ctx_tiles - 1), 0))
    ctx_f32 = jax.ShapeDtypeStruct((lay.ctx_tokens, NA_WIDTH), F32)
    out_shape = (jax.ShapeDtypeStruct((n, NA_WIDTH), BF16), ctx_f32, ctx_f32,
                 jax.ShapeDtypeStruct((n, NA_WIDTH), BF16), jax.ShapeDtypeStruct((n, NA_WIDTH), BF16),
                 jax.ShapeDtypeStruct((n, B_COLS), F32), jax.ShapeDtypeStruct((n, C_COLS), F32),
                 jax.ShapeDtypeStruct((n, G_COLS), BF16))
    return pl.pallas_call(
        functools.partial(_inproj_kernel, ctx_tiles=lay.ctx_tiles),
        out_shape=out_shape,
        grid=(lay.tiles,),
        in_specs=[row(D_MODEL),
                  pl.BlockSpec((1, N_MOD, D_MODEL), lambda i: (lay.mod_row(i), 0, 0)),
                  pl.BlockSpec((1, D_MODEL), lambda i: (0, 0)),
                  pl.BlockSpec((D_MODEL, P_IN), lambda i: (0, 0), pipeline_mode=pl.Buffered(1)),
                  pl.BlockSpec((1, P_IN), lambda i: (0, 0))],
        out_specs=(row(NA_WIDTH), crow, crow, row(NA_WIDTH), row(NA_WIDTH),
                   row(B_COLS), row(C_COLS), row(G_COLS)),
        compiler_params=_cparams(("arbitrary",)),
        name="in_proj",
    )(x, mod, g0, w_bf, b)


def _ctx_attn_kernel(q_ref, k_ref, v_ref, o_ref):
    scale = NA_HEAD_DIM ** -0.5
    lo = lax.broadcasted_iota(jnp.int32, (q_ref.shape[0], LANES), 1) < NA_HEAD_DIM
    heads = []
    for h in range(NA_HEADS):
        ps = slice((h // 2) * LANES, (h // 2 + 1) * LANES)
        qp = q_ref[:, ps] * scale
        qh = jnp.where(lo if h % 2 == 0 else jnp.logical_not(lo), qp, jnp.zeros_like(qp))
        heads.append(dict(ps=ps, s=_dot(qh, k_ref[:, ps], _NT)))
    for u in heads:
        u['p'] = jnp.exp(u['s'] - jnp.max(u['s'], axis=-1, keepdims=True))
    for u in heads:
        u['o'] = _dot(u['p'].astype(BF16), v_ref[:, u['ps']]) / jnp.sum(u['p'], axis=-1, keepdims=True)
    for p in range(NA_HEADS // 2):
        o_ref[:, p * LANES:(p + 1) * LANES] = jnp.where(lo, heads[2 * p]['o'], heads[2 * p + 1]['o']).astype(BF16)


def ctx_attention(lay, q, kb, vb):
    blk = pl.BlockSpec((lay.ctx_len, NA_WIDTH), lambda b: (b, 0))
    return pl.pallas_call(
        _ctx_attn_kernel,
        out_shape=jax.ShapeDtypeStruct((lay.ctx_tokens, NA_WIDTH), BF16),
        grid=(lay.n_ctx,),
        in_specs=[blk, blk, blk],
        out_specs=blk,
        compiler_params=_cparams(("arbitrary",)),
        name="ctx_attention",
    )(q, kb, vb)


def na_bias_table(rpb):
    kw, kh = NA_WIN_COLS, NA_WIN_ROWS
    col = jnp.arange(GRID_W, dtype=jnp.int32)
    col_start = jnp.clip(col - kw // 2, 0, GRID_W - kw)
    col_ok = (col[None, :] >= col_start[:, None]) & (col[None, :] < col_start[:, None] + kw)
    dc = jnp.clip(col[None, :] - col[:, None] + kw - 1, 0, 2 * kw - 2)
    bias_c = jnp.where(col_ok, rpb[:, :, dc].astype(F32), NEG_INF)
    tabs = [jnp.concatenate([bias_c[:, d + i] for i in range(kh)], axis=-1) for d in range(kh)]
    return jnp.stack(tabs, axis=0)


def _na_row_start(r, rows):
    return jnp.clip(r - NA_WIN_ROWS // 2, 0, rows - NA_WIN_ROWS)


def _na_row_stages(q_ref, k_ref, v_ref, ck_ref, cv_ref, bias_ref, o_ref, r, rows):
    rs = _na_row_start(r, rows)
    n_loc = NA_WIN_ROWS * GRID_W
    start = pl.multiple_of(rs * GRID_W, GRID_W)
    scale = NA_HEAD_DIM ** -0.5
    lo = lax.broadcasted_iota(jnp.int32, (GRID_W, LANES), 1) < NA_HEAD_DIM
    heads = []
    for h in range(NA_HEADS):
        ps = slice((h // 2) * LANES, (h // 2 + 1) * LANES)
        qp = q_ref[:, ps] * scale
        qh = jnp.where(lo if h % 2 == 0 else jnp.logical_not(lo), qp, jnp.zeros_like(qp))
        s_loc = _dot(qh, k_ref[pl.ds(start, n_loc), ps], _NT) + bias_ref[0, h]
        s_ctx = _dot(qh, ck_ref[0, :, ps], _NT)
        heads.append(dict(ps=ps, s_loc=s_loc, s_ctx=s_ctx))
    yield
    for u in heads:
        u['m'] = jnp.maximum(jnp.max(u['s_loc'], axis=-1, keepdims=True),
                             jnp.max(u['s_ctx'], axis=-1, keepdims=True))
    yield
    for u in heads:
        u['p_loc'] = jnp.exp(u['s_loc'] - u['m'])
        u['p_ctx'] = jnp.exp(u['s_ctx'] - u['m'])
        u['den'] = jnp.sum(u['p_loc'], axis=-1, keepdims=True) + jnp.sum(u['p_ctx'], axis=-1, keepdims=True)
    yield
    for u in heads:
        o = (_dot(u['p_loc'].astype(BF16), v_ref[pl.ds(start, n_loc), u['ps']])
             + _dot(u['p_ctx'].astype(BF16), cv_ref[0, :, u['ps']]))
        u['o'] = o / u['den']
    yield
    for p in range(NA_HEADS // 2):
        o_ref[:, p * LANES:(p + 1) * LANES] = jnp.where(lo, heads[2 * p]['o'], heads[2 * p + 1]['o']).astype(BF16)


def _softplus(x):
    return jnp.maximum(x, 0.0) + jnp.log(1.0 + jnp.exp(-jnp.abs(x)))


def _rwkv_prepare(d, z, prev_row, next_row, prm):
    (mu_ref, w0_ref, w2_ref, a0_ref, a2_ref, kk_ref, ka_ref, rk_ref) = prm
    c = CHUNK
    cw = RW_WIDTH
    row = lax.broadcasted_iota(jnp.int32, (c, 1), 0)
    prev = jnp.where(row == 0, prev_row, pltpu.roll(z, 1, axis=0))
    nxt = jnp.where(row == c - 1, next_row, pltpu.roll(z, c - 1, axis=0))
    zs = z + mu_ref[0:1, :] * (prev - z) + mu_ref[1:2, :] * (nxt - z)
    r, k, v = zs[:, :cw], zs[:, cw:2 * cw], zs[:, 2 * cw:3 * cw]
    o = 3 * cw
    zw = zs[:, o + d * RW_DECAY_RANK:o + (d + 1) * RW_DECAY_RANK]
    o += 2 * RW_DECAY_RANK
    za = zs[:, o + d * RW_ICLR_RANK:o + (d + 1) * RW_ICLR_RANK]

    w_log = -_softplus(-(w0_ref[d:d + 1, :] + _mm3(jnp.tanh(zw), w2_ref[d]))) - 0.5
    logw = -jnp.exp(w_log)
    a = _sigmoid(a0_ref[d:d + 1, :] + _mm3(za, a2_ref[d]))
    kd = k * (1.0 + (a - 1.0) * ka_ref[...])
    kk = k * kk_ref[...]

    ti = lax.broadcasted_iota(jnp.int32, (c, c), 0)
    tj = lax.broadcasted_iota(jnp.int32, (c, c), 1)
    tri = ((ti >= tj) if d == 0 else (tj >= ti)).astype(BF16)
    lh, ll = _split(logw)
    cl = _dot(tri, lh) + _dot(tri, ll)
    p_in = jnp.exp(cl)
    p_ex = jnp.exp(cl - logw)
    p_inv = jnp.exp(-cl)
    p_end = p_in[c - 1:c, :] if d == 0 else p_in[0:1, :]

    pi = lax.broadcasted_iota(jnp.int32, (LANES, LANES), 0)
    pj = lax.broadcasted_iota(jnp.int32, (LANES, LANES), 1)
    same_head = (pi // RW_HEAD_DIM) == (pj // RW_HEAD_DIM)
    ones_bd = same_head.astype(BF16)

    def head_sum(x):
        xh, xl = _split(x)
        return _dot(xh, ones_bd) + _dot(xl, ones_bd)

    t2 = lax.broadcasted_iota(jnp.int32, (2 * c, LANES), 0) % c
    j2 = lax.broadcasted_iota(jnp.int32, (2 * c, LANES), 1) % c
    diff = (t2 - j2) if d == 0 else (j2 - t2)
    upper = lax.broadcasted_iota(jnp.int32, (2 * c, LANES), 0) < c
    mask = diff >= jnp.where(upper, 1, 0)
    units = []
    for p in range(RW_HEADS // 2):
        sl = slice(p * LANES, (p + 1) * LANES)
        kkp = kk[:, sl]
        kkp = kkp * lax.rsqrt(jnp.maximum(head_sum(kkp * kkp), 1e-24))
        pinv = p_inv[:, sl]
        bt = a[:, sl] * kkp * pinv
        kt = kd[:, sl] * pinv
        ar = jnp.concatenate([-kkp * p_ex[:, sl], r[:, sl] * p_in[:, sl]], axis=0).astype(BF16)
        pe = p_end[:, sl]
        bonus = head_sum(r[:, sl] * kd[:, sl] * rk_ref[:, sl]) * v[:, sl]
        units.append(dict(d=d, p=p, ar=ar, bt=bt, kt=kt, v=v[:, sl], pe=pe,
                          bkp=jnp.concatenate([bt * pe, kt * pe], axis=0), bonus=bonus,
                          mask=mask, eye=(diff[:c] == 0).astype(F32), bd=same_head))
    return units, zs


def _bd(x):
    lo = lax.broadcasted_iota(jnp.int32, x.shape, 1) < RW_HEAD_DIM
    xb = x.astype(BF16)
    z = jnp.zeros_like(xb)
    return jnp.concatenate([jnp.where(lo, xb, z), jnp.where(lo, z, xb)], axis=0)


def _rwkv_solve(units, s_ref, fill=()):
    c = CHUNK

    def tick():
        for f in fill:
            next(f, None)

    for u in units:
        gb = _dot(u['ar'], _bd(u['bt']), _NT)
        gk = _dot(u['ar'], _bd(u['kt']), _NT)
        gb = jnp.where(u['mask'], gb, 0.0)
        u['a_ab'], u['a_rb'] = gb[:c], gb[c:]
        u['s0'] = s_ref[u['d'], u['p']]
        u['xs'] = _dot(u['ar'], u['s0'].astype(BF16), _NT) + _mm(jnp.where(u['mask'], gk, 0.0), _bd(u['v']))
    tick()
    for u in units:
        u['pw'] = _mm(u['a_ab'], _bd(u['a_ab']))
        u['tinv'] = u['eye'] + u['a_ab']
    tick()
    n_sq = CHUNK.bit_length() - 2
    for i in range(n_sq):
        for u in units:
            if i < n_sq - 1:
                both = _mm(jnp.concatenate([u['pw'], u['tinv']], axis=0), _bd(u['pw']))
                u['pw'], u['tinv'] = both[:c], u['tinv'] + both[c:]
            else:
                u['tinv'] = u['tinv'] + _mm(u['tinv'], _bd(u['pw']))
        tick()
    for u in units:
        u['u'] = _mm(u['tinv'], _bd(u['xs'][:c]))
    tick()
    ys = []
    for u in units:
        ys.append(u['xs'][c:] + _mm(u['a_rb'], _bd(u['u'])) + u['bonus'])
        uv = jnp.concatenate([u['u'], u['v']], axis=0)
        s_ref[u['d'], u['p']] = u['s0'] * u['pe'] + jnp.where(u['bd'], _mm(uv.T, u['bkp']), 0.0)
    for f in fill:
        for _ in f:
            pass
    return ys


RW_SEQS = 2


NA_ROWS_PER_STEP = 2


def _rwkv_kernel(*refs, nc, has_s0, na_rows):
    zf_ref, zfp_ref, zfn_ref, zr_ref, zrp_ref, zrn_ref = refs[:6]
    n_in = 6 + (1 if has_s0 else 0)
    s0_ref = refs[6] if has_s0 else None
    prm = refs[n_in:n_in + 8]
    g2_ref = refs[n_in + 8]
    n_in += 9
    i = pl.program_id(1)
    if na_rows:
        q_ref, k_ref, v_ref, ck_ref, cv_ref = refs[n_in:n_in + 5]
        bias_refs = refs[n_in + 5:n_in + 5 + NA_ROWS_PER_STEP]
        yf_ref, yb_ref, g_ref, sfin_ref, oa_ref, s_ref = refs[n_in + 5 + NA_ROWS_PER_STEP:]
    else:
        yf_ref, yb_ref, g_ref, sfin_ref, s_ref = refs[n_in:]

    hd = RW_HEAD_DIM
    npair = RW_HEADS // 2

    @pl.when(i == 0)
    def _():
        s_ref[...] = jnp.zeros(s_ref.shape, F32)
        if has_s0:
            for k in range(RW_SEQS):
                for d in range(2):
                    for p in range(npair):
                        s_ref[2 * k + d, p, :hd, :hd] = s0_ref[k, d, 2 * p]
                        s_ref[2 * k + d, p, hd:, hd:] = s0_ref[k, d, 2 * p + 1]

    zero = jnp.zeros((1, B_COLS), F32)
    units = []
    for k in range(RW_SEQS):
        pf = jnp.where(i == 0, zero, zfp_ref[k, 7:8, :])
        nf = jnp.where(i == nc - 1, zero, zfn_ref[k, 0:1, :])
        uf, zs = _rwkv_prepare(0, zf_ref[k], pf, nf, prm)
        g_ref[k] = _mm(_sigmoid(zs[:, B_COLS - RW_GATE_RANK:]), g2_ref[...])
        pr = jnp.where(i == nc - 1, zero, zrp_ref[k, 7:8, :])
        nr = jnp.where(i == 0, zero, zrn_ref[k, 0:1, :])
        ub, _ = _rwkv_prepare(1, zr_ref[k], pr, nr, prm)
        for u in uf + ub:
            u['d'] = 2 * k + u['d']
        units += uf + ub
    fill = []
    if na_rows:
        r0 = ((pl.program_id(0) * nc + i) * NA_ROWS_PER_STEP) % na_rows
        for k, b_ref in enumerate(bias_refs):
            rs = pl.ds(k * GRID_W, GRID_W)
            fill.append(_na_row_stages(q_ref.at[rs], k_ref, v_ref, ck_ref, cv_ref, b_ref, oa_ref.at[rs],
                                       r0 + k, na_rows))
    ys = _rwkv_solve(units, s_ref, fill)
    for k in range(RW_SEQS):
        yk = ys[2 * npair * k:2 * npair * (k + 1)]
        yf_ref[k] = jnp.concatenate(yk[:npair], axis=-1)
        yb_ref[k] = jnp.concatenate(yk[npair:], axis=-1)

    @pl.when(i == nc - 1)
    def _():
        for k in range(RW_SEQS):
            for d in range(2):
                for p in range(npair):
                    sfin_ref[k, d, 2 * p] = s_ref[2 * k + d, p, :hd, :hd]
                    sfin_ref[k, d, 2 * p + 1] = s_ref[2 * k + d, p, hd:, hd:]


def rwkv_scan(zb, seq0, nseq, s0, prm, g2, na=None):
    seqlen = zb.shape[1]
    ns = RW_SEQS
    assert seq0 % ns == 0 and nseq % ns == 0 and seqlen % CHUNK == 0
    nc = seqlen // CHUNK
    sub = CHUNK // 8
    b0 = seq0 // ns
    has_s0 = s0 is not None

    def specs(cidx):
        return [pl.BlockSpec((ns, CHUNK, B_COLS), lambda s, i: (b0 + s, cidx(i), 0)),
                pl.BlockSpec((ns, 8, B_COLS), lambda s, i: (b0 + s, jnp.maximum(cidx(i) * sub - 1, 0), 0)),
                pl.BlockSpec((ns, 8, B_COLS),
                             lambda s, i: (b0 + s, jnp.minimum((cidx(i) + 1) * sub, seqlen // 8 - 1), 0))]

    cf = lambda i: i
    cr = lambda i: nc - 1 - i
    full = lambda a: pl.BlockSpec(a.shape, lambda s, i: (0,) * a.ndim)
    sshape = (2, RW_HEADS, RW_HEAD_DIM, RW_HEAD_DIM)
    sspec = pl.BlockSpec((ns,) + sshape, lambda s, i: (s, 0, 0, 0, 0))
    in_specs = specs(cf) + specs(cr)
    args = [zb] * 6
    if has_s0:
        in_specs.append(sspec)
        args.append(s0)
    in_specs += [full(p) for p in prm] + [full(g2)]
    args += list(prm) + [g2]
    out = lambda cidx: pl.BlockSpec((ns, CHUNK, RW_WIDTH), lambda s, i: (s, cidx(i), 0))
    tok = jax.ShapeDtypeStruct((nseq, seqlen, RW_WIDTH), F32)
    out_shape = [tok, tok, tok, jax.ShapeDtypeStruct((nseq,) + sshape, F32)]
    out_specs = [out(cf), out(cr), out(cf), sspec]
    na_rows = 0
    if na is not None:
        lay, q, kb, vb, ck, cv, bias_tab = na
        na_rows = lay.dec_len // GRID_W
        per = NA_ROWS_PER_STEP
        assert na_rows >= NA_WIN_ROWS and na_rows % per == 0 and lay.ctx_tokens % lay.dec_len == 0
        assert lay.n_dec * na_rows == per * (nseq // ns) * nc and lay.ctx_tokens % (per * GRID_W) == 0
        step = lambda s, i: s * nc + i
        batch = lambda s, i: step(s, i) * per // na_rows
        kv = pl.BlockSpec((lay.dec_len, NA_WIDTH), lambda s, i: (lay.ctx_tokens // lay.dec_len + batch(s, i), 0))
        cs = pl.BlockSpec((1, ck.shape[1], NA_WIDTH), lambda s, i: (batch(s, i), 0, 0))

        def bias_spec(k):
            def idx(s, i):
                r = step(s, i) * per % na_rows + k
                return (_na_row_start(r, na_rows) - r + NA_WIN_ROWS - 1, 0, 0, 0)
            return pl.BlockSpec((1, NA_HEADS, GRID_W, NA_WIN_ROWS * GRID_W), idx)

        in_specs += [pl.BlockSpec((per * GRID_W, NA_WIDTH),
                                  lambda s, i: (lay.ctx_tokens // (per * GRID_W) + step(s, i), 0)),
                     kv, kv, cs, cs] + [bias_spec(k) for k in range(per)]
        args += [q, kb, vb, ck, cv] + [bias_tab] * per
        out_shape.append(jax.ShapeDtypeStruct((lay.dec_tokens, NA_WIDTH), BF16))
        out_specs.append(pl.BlockSpec((per * GRID_W, NA_WIDTH), lambda s, i: (step(s, i), 0)))
    res = pl.pallas_call(
        functools.partial(_rwkv_kernel, nc=nc, has_s0=has_s0, na_rows=na_rows),
        out_shape=tuple(out_shape),
        grid=(nseq // ns, nc),
        in_specs=in_specs,
        out_specs=tuple(out_specs),
        scratch_shapes=[pltpu.VMEM((2 * ns, RW_HEADS // 2, LANES, LANES), F32)],
        compiler_params=_cparams(("arbitrary", "arbitrary")),
        name="rwkv_scan",
    )(*args)
    flat = lambda a: a.reshape(nseq * seqlen, RW_WIDTH)
    return (flat(res[0]), flat(res[1]), flat(res[2])) + tuple(res[3:])


SUBLANES = 8


def _conv_kernel(z_ref, zp_ref, zn_ref, w_ref, b_ref, g_ref, be_ref, o_ref, h_ref, hs_ref, *, lay):
    i = pl.program_id(0)
    first, last = lay.tile_pos(i)

    def glu(z):
        return z[:, :CONV_WIDTH] * _sigmoid(z[:, CONV_WIDTH:])

    hz = jnp.zeros((CONV_HALO, CONV_WIDTH), F32)
    h_ref[0:CONV_HALO, :] = jnp.where(first, hz, glu(zp_ref[...]))
    h_ref[CONV_HALO:CONV_HALO + TM, :] = glu(z_ref[...])
    h_ref[CONV_HALO + TM:, :] = jnp.where(last, hz, glu(zn_ref[...]))
    off = CONV_HALO - CONV_K // 2
    rows = hs_ref.shape[1]
    for ph in range(SUBLANES):
        hs_ref[ph] = h_ref[ph:ph + rows, :]
    acc = jnp.zeros((TM, CONV_WIDTH), F32) + b_ref[...]
    for j in range(CONV_K):
        ph, base = (off + j) % SUBLANES, (off + j) // SUBLANES * SUBLANES
        acc = acc + w_ref[j:j + 1, :] * hs_ref[ph, base:base + TM, :]
    xc = acc - jnp.mean(acc, axis=-1, keepdims=True)
    hn = xc * lax.rsqrt(jnp.mean(xc * xc, axis=-1, keepdims=True) + LN_EPS) * g_ref[...] + be_ref[...]
    o_ref[...] = (hn * _sigmoid(hn)).astype(BF16)


def conv_module(lay, zc, w, b, g, be):
    n = lay.n
    hb = TM // CONV_HALO
    vec = pl.BlockSpec((1, CONV_WIDTH), lambda i: (0, 0))
    return pl.pallas_call(
        functools.partial(_conv_kernel, lay=lay),
        out_shape=jax.ShapeDtypeStruct((n, CONV_WIDTH), BF16),
        grid=(lay.tiles,),
        in_specs=[pl.BlockSpec((TM, C_COLS), lambda i: (i, 0)),
                  pl.BlockSpec((CONV_HALO, C_COLS), lambda i: (jnp.maximum(i * hb - 1, 0), 0)),
                  pl.BlockSpec((CONV_HALO, C_COLS), lambda i: (jnp.minimum((i + 1) * hb, n // CONV_HALO - 1), 0)),
                  pl.BlockSpec((CONV_K, CONV_WIDTH), lambda i: (0, 0)), vec, vec, vec],
        out_specs=pl.BlockSpec((TM, CONV_WIDTH), lambda i: (i, 0)),
        scratch_shapes=[pltpu.VMEM((TM + 2 * CONV_HALO, CONV_WIDTH), F32),
                        pltpu.VMEM((SUBLANES, TM + 2 * CONV_HALO - SUBLANES, CONV_WIDTH), F32)],
        compiler_params=_cparams(("arbitrary",)),
        name="conv_module",
    )(zc, zc, zc, w, b, g, be)


def _merge_kernel(x_ref, oa_c_ref, yf_c_ref, yb_c_ref, g_c_ref, oa_d_ref, yf_d_ref, yb_d_ref, g_d_ref,
                  oc_ref, gt_ref, mod_ref, ng_ref, lg_ref, lb_ref,
                  wa_ref, wr_ref, wc_ref, wo_ref, rw_ref, rb_ref,
                  x1_ref, hm_ref, ti_ref, tg_ref, rk_ref, cnt_ref, *, ctx_tiles):
    is_ctx = pl.program_id(0) < ctx_tiles
    pick = lambda c_ref, d_ref: jnp.where(is_ctx, c_ref[...], d_ref[...])
    oa = pick(oa_c_ref, oa_d_ref)
    g_rw = pick(g_c_ref, g_d_ref)
    y = pick(yf_c_ref, yf_d_ref) + pick(yb_c_ref, yb_d_ref)
    pi = lax.broadcasted_iota(jnp.int32, (LANES, LANES), 0)
    pj = lax.broadcasted_iota(jnp.int32, (LANES, LANES), 1)
    ones_bd = ((pi // RW_HEAD_DIM) == (pj // RW_HEAD_DIM)).astype(BF16)

    def head_mean(v):
        vh, vl = _split(v)
        return (_dot(vh, ones_bd) + _dot(vl, ones_bd)) * (1.0 / RW_HEAD_DIM)

    outs = []
    for p in range(RW_HEADS // 2):
        yp = y[:, p * LANES:(p + 1) * LANES]
        yc = yp - head_mean(yp)
        outs.append(yc * lax.rsqrt(head_mean(yc * yc) + GN_EPS))
    o_rw = (jnp.concatenate(outs, axis=-1) * lg_ref[...] + lb_ref[...]) * g_rw
    merged = (gt_ref[:, :D_MODEL].astype(F32) * _dot(oa, wa_ref[...])
              + gt_ref[:, D_MODEL:2 * D_MODEL].astype(F32) * _mm(o_rw, wr_ref[...])
              + gt_ref[:, 2 * D_MODEL:].astype(F32) * _dot(oc_ref[...], wc_ref[...]))
    mix = _mm(merged, wo_ref[...])
    x1 = x_ref[...] + mod_ref[0, 2:3, :] * _rms(mix, ng_ref[1:2, :])
    x1_ref[...] = x1
    hm = _rms(x1, ng_ref[2:3, :]) * (1.0 + mod_ref[0, 4:5, :]) + mod_ref[0, 3:4, :]
    hm_ref[...] = hm
    logits = _mm3(hm, rw_ref[...]) + rb_ref[...]
    lane = lax.broadcasted_iota(jnp.int32, logits.shape, 1)
    idx_out = jnp.zeros(logits.shape, jnp.int32)
    val_out = jnp.full(logits.shape, NEG_INF, F32)
    picks = []
    for j in range(TOP_K):
        m = jnp.max(logits, axis=-1, keepdims=True)
        idx = jnp.min(jnp.where(logits == m, lane, LANES), axis=-1, keepdims=True)
        idx_out = jnp.where(lane == j, idx, idx_out)
        val_out = jnp.where(lane == j, m, val_out)
        picks.append(lane == idx)
        logits = jnp.where(picks[-1], -jnp.inf, logits)
    e = jnp.exp(val_out - jnp.max(val_out, axis=-1, keepdims=True))
    ti_ref[...] = idx_out
    tg_ref[...] = e / jnp.sum(e, axis=-1, keepdims=True)

    chosen = jnp.zeros(logits.shape, F32)
    for pk in picks:
        chosen = chosen + pk.astype(F32)
    ti_ = lax.broadcasted_iota(jnp.int32, (TM, TM), 0)
    tj_ = lax.broadcasted_iota(jnp.int32, (TM, TM), 1)
    before = _dot((ti_ > tj_).astype(BF16), chosen.astype(BF16))
    rank = jnp.zeros(logits.shape, F32)
    for j, pk in enumerate(picks):
        rank = jnp.where(lane == j, jnp.sum(jnp.where(pk, before, 0.0), axis=-1, keepdims=True), rank)
    rk_ref[...] = rank.astype(jnp.int32)
    cnt_ref[0] = jnp.sum(chosen, axis=0, keepdims=True)


def merge_router(lay, x, ctx_br, dec_br, o_cv, gates, mod, ng, lg, lb, wa, wr, wc, wo, rw, rb):
    n = lay.n
    row = lambda w: pl.BlockSpec((TM, w), lambda i: (i, 0))
    crow = lambda w: pl.BlockSpec((TM, w), lambda i: (jnp.minimum(i, lay.ctx_tiles - 1), 0))
    drow = lambda w: pl.BlockSpec((TM, w), lambda i: (jnp.maximum(i - lay.ctx_tiles, 0), 0))
    full = lambda a: pl.BlockSpec(a.shape, lambda i: (0,) * a.ndim)
    br_w = (NA_WIDTH, RW_WIDTH, RW_WIDTH, RW_WIDTH)
    return pl.pallas_call(
        functools.partial(_merge_kernel, ctx_tiles=lay.ctx_tiles),
        out_shape=(jax.ShapeDtypeStruct((n, D_MODEL), F32), jax.ShapeDtypeStruct((n, D_MODEL), F32),
                   jax.ShapeDtypeStruct((n, LANES), jnp.int32), jax.ShapeDtypeStruct((n, LANES), F32),
                   jax.ShapeDtypeStruct((n, LANES), jnp.int32), jax.ShapeDtypeStruct((lay.tiles, 1, LANES), F32)),
        grid=(lay.tiles,),
        in_specs=[row(D_MODEL)] + [crow(w) for w in br_w] + [drow(w) for w in br_w] + [
                  row(CONV_WIDTH),
                  row(G_COLS), pl.BlockSpec((1, N_MOD, D_MODEL), lambda i: (lay.mod_row(i), 0, 0)),
                  full(ng), full(lg), full(lb), full(wa), full(wr), full(wc), full(wo), full(rw), full(rb)],
        out_specs=(row(D_MODEL), row(D_MODEL), row(LANES), row(LANES), row(LANES),
                   pl.BlockSpec((1, 1, LANES), lambda i: (i, 0, 0))),
        compiler_params=_cparams(("arbitrary",)),
        name="merge_router",
    )(x, *ctx_br, *dec_br, o_cv, gates, mod, ng, lg, lb, wa, wr, wc, wo, rw, rb)


RUN_ALIGN = 8
RUN_BITS = (TM // RUN_ALIGN).bit_length()
SORT_ROWS = -(-(TM * TOP_K + N_EXPERTS * (RUN_ALIGN - 1)) // LANES) * LANES


def route_tables(top_i, wrank, cnt, n):
    tiles = n // TM
    cnt = cnt[:, 0, :N_EXPERTS].astype(jnp.int32)
    run = ((cnt + RUN_ALIGN - 1) // RUN_ALIGN) * RUN_ALIGN
    off = jnp.cumsum(run, axis=1) - run
    before = jnp.cumsum(run, axis=0) - run
    region = jnp.sum(run, axis=0)
    padded = ((region + ROUTE_BLOCK - 1) // ROUTE_BLOCK) * ROUTE_BLOCK
    pad_end = jnp.cumsum(padded)
    first = (pad_end - padded)[None, :] + before
    table = jnp.concatenate([off, first, run, jnp.zeros((tiles, LANES - 3 * N_EXPERTS), jnp.int32)], axis=1)
    onehot = top_i.reshape(tiles, TM, TOP_K, 1) == jnp.arange(N_EXPERTS, dtype=jnp.int32)
    pos = jnp.sum(jnp.where(onehot, off[:, None, None, :], 0), axis=-1) + wrank.reshape(tiles, TM, TOP_K)
    n_blocks = -(-tiles * (TM * TOP_K + N_EXPERTS * (RUN_ALIGN - 1)) // ROUTE_BLOCK) + N_EXPERTS
    starts = jnp.arange(n_blocks, dtype=jnp.int32) * ROUTE_BLOCK
    block_e = jnp.minimum(jnp.sum((pad_end[None, :] <= starts[:, None]).astype(jnp.int32), axis=1),
                          N_EXPERTS - 1)
    n_used = (pad_end[-1] // ROUTE_BLOCK).astype(jnp.int32).reshape(1)
    return table.reshape(tiles, 1, LANES), pos, block_e, n_used, n_blocks


def _for_each_piece(tab_ref, fn):
    def body(e, carry):
        off = tab_ref[0, 0, e]
        first = tab_ref[0, 0, N_EXPERTS + e]
        run = tab_ref[0, 0, 2 * N_EXPERTS + e]
        for k in range(RUN_BITS):
            size = RUN_ALIGN << k

            @pl.when((run & size) != 0)
            def _():
                done = run & (-2 * size)
                fn(pl.multiple_of(off + done, RUN_ALIGN), pl.multiple_of(first + done, RUN_ALIGN), size)
        return carry

    lax.fori_loop(0, N_EXPERTS, body, 0)


def _dispatch_kernel(tab_ref, tabp_ref, pos_ref, hm_ref, xb_in_ref, xb_ref, srt_ref, sems):
    del xb_in_ref
    i = pl.program_id(0)
    slot = i % 2

    def copy(slot_, off, row, size):
        return pltpu.make_async_copy(srt_ref.at[slot_, pl.ds(off, size), :], xb_ref.at[pl.ds(row, size), :],
                                     sems.at[slot_])

    p_iota = lax.broadcasted_iota(jnp.int32, (SORT_ROWS, TM), 0)
    hit = p_iota == pos_ref[0, 0:1, :]
    for j in range(1, TOP_K):
        hit = hit | (p_iota == pos_ref[0, j:j + 1, :])
    srt_ref[slot] = _dot(jnp.where(hit, 1.0, 0.0).astype(BF16), hm_ref[...].astype(BF16))
    _for_each_piece(tab_ref, lambda off, row, size: copy(slot, off, row, size).start())

    @pl.when(i > 0)
    def _():
        _for_each_piece(tabp_ref, lambda off, row, size: copy(1 - slot, off, row, size).wait())

    @pl.when(i == pl.num_programs(0) - 1)
    def _():
        _for_each_piece(tab_ref, lambda off, row, size: copy(slot, off, row, size).wait())


def moe_dispatch(lay, hm, table, pos, xb0):
    n_rows = xb0.shape[0]
    post = jnp.pad(pos.transpose(0, 2, 1), ((0, 0), (0, 8 - TOP_K), (0, 0)))
    tab = lambda f: pl.BlockSpec((1, 1, LANES), lambda i: (f(i), 0, 0), memory_space=pltpu.SMEM)
    return pl.pallas_call(
        _dispatch_kernel,
        out_shape=jax.ShapeDtypeStruct((n_rows, D_MODEL), F32),
        grid=(lay.tiles,),
        in_specs=[tab(lambda i: i), tab(lambda i: jnp.maximum(i - 1, 0)),
                  pl.BlockSpec((1, 8, TM), lambda i: (i, 0, 0)),
                  pl.BlockSpec((TM, D_MODEL), lambda i: (i, 0)),
                  pl.BlockSpec(memory_space=pl.ANY)],
        out_specs=pl.BlockSpec(memory_space=pl.ANY),
        scratch_shapes=[pltpu.VMEM((2, SORT_ROWS, D_MODEL), F32), pltpu.SemaphoreType.DMA((2,))],
        input_output_aliases={4: 0},
        compiler_params=_cparams(("arbitrary",)),
        name="moe_dispatch",
    )(table, table, post, hm, xb0)


GL_GROUP = 2 * LANES


def _regroup_perm():
    r = jnp.arange(GL_GROUP, dtype=jnp.int32)[:, None]
    c = jnp.arange(GL_GROUP, dtype=jnp.int32)[None, :]
    src = jnp.where(c < LANES, 2 * c, 2 * (c - LANES) + 1)
    return (r == src).astype(BF16)


def _expert_kernel(be_ref, nu_ref, x_ref, w1_ref, perm_ref, b1_ref, w2_ref, b2_ref, y_ref, w1b_ref, w2b_ref):
    b = pl.program_id(0)
    used = b < nu_ref[0]

    @pl.when(used & ((b == 0) | (be_ref[b] != be_ref[jnp.maximum(b - 1, 0)])))
    def _():
        for j in range(2 * EXPERT_FF // GL_GROUP):
            cs = slice(j * GL_GROUP, (j + 1) * GL_GROUP)
            w1b_ref[:, cs] = _dot(w1_ref[:, cs].astype(BF16), perm_ref[...]).astype(BF16)
        w2b_ref[...] = w2_ref[...].astype(BF16)

    @pl.when(used)
    def _():
        x = x_ref[...].astype(BF16)
        acc = jnp.zeros((ROUTE_BLOCK, D_MODEL), F32) + b2_ref[0]
        for j in range(EXPERT_FF // GL_GROUP):
            acts = []
            for g in range(2):
                c0 = (2 * j + g) * GL_GROUP
                h = _dot(x, w1b_ref[:, c0:c0 + GL_GROUP]) + b1_ref[0, :, c0:c0 + GL_GROUP]
                hg = jnp.minimum(h[:, :LANES], SWIGLU_LIMIT)
                hl = jnp.clip(h[:, LANES:], -SWIGLU_LIMIT, SWIGLU_LIMIT)
                acts.append(hg * _sigmoid(SWIGLU_ALPHA * hg) * (hl + 1.0))
            act = jnp.concatenate(acts, axis=-1).astype(BF16)
            acc = acc + _dot(act, w2b_ref[j * GL_GROUP:(j + 1) * GL_GROUP, :])
        y_ref[...] = acc

    @pl.when(jnp.logical_not(used))
    def _():
        y_ref[...] = jnp.zeros(y_ref.shape, F32)


def moe_experts(xb, block_e, n_used, n_blocks, l, w1, b1, w2, b2):
    def last_used(b, nu):
        return jnp.minimum(b, jnp.maximum(nu[0] - 1, 0))

    wspec = lambda a: pl.BlockSpec((None, None) + a.shape[2:], lambda b, be, nu: (l, be[last_used(b, nu)], 0, 0))
    bspec = lambda a: pl.BlockSpec((1,) + a.shape[1:], lambda b, be, nu: (be[last_used(b, nu)], 0, 0))
    return pl.pallas_call(
        _expert_kernel,
        out_shape=jax.ShapeDtypeStruct(xb.shape, F32),
        grid_spec=pltpu.PrefetchScalarGridSpec(
            num_scalar_prefetch=2,
            grid=(n_blocks,),
            in_specs=[pl.BlockSpec((ROUTE_BLOCK, D_MODEL), lambda b, be, nu: (last_used(b, nu), 0)),
                      wspec(w1), pl.BlockSpec((GL_GROUP, GL_GROUP), lambda b, be, nu: (0, 0)),
                      bspec(b1), wspec(w2), bspec(b2)],
            out_specs=pl.BlockSpec((ROUTE_BLOCK, D_MODEL), lambda b, be, nu: (b, 0)),
            scratch_shapes=[pltpu.VMEM(w1.shape[2:], BF16), pltpu.VMEM(w2.shape[2:], BF16)]),
        compiler_params=_cparams(("arbitrary",)),
        name="moe_experts",
    )(block_e, n_used, xb, w1, _regroup_perm(), b1, w2, b2)


def _combine_kernel(tab_ref, tabn_ref, yb_ref, pos_ref, tg_ref, x1_ref, mod_ref, ng_ref, *rest, ctx_tiles):
    o_refs, (stg_ref, sems) = rest[:-2], rest[-2:]
    i = pl.program_id(0)
    last = pl.num_programs(0) - 1
    slot = i % 2

    def copy(slot_, off, row, size):
        return pltpu.make_async_copy(yb_ref.at[pl.ds(row, size), :], stg_ref.at[slot_, pl.ds(off, size), :],
                                     sems.at[slot_])

    @pl.when(i == 0)
    def _():
        stg_ref[...] = jnp.zeros(stg_ref.shape, F32)
        _for_each_piece(tab_ref, lambda off, row, size: copy(slot, off, row, size).start())

    @pl.when(i < last)
    def _():
        _for_each_piece(tabn_ref, lambda off, row, size: copy(1 - slot, off, row, size).start())

    _for_each_piece(tab_ref, lambda off, row, size: copy(slot, off, row, size).wait())
    p_iota = lax.broadcasted_iota(jnp.int32, (TM, SORT_ROWS), 1)
    w = jnp.zeros((TM, SORT_ROWS), F32)
    for j in range(TOP_K):
        w = w + jnp.where(p_iota == pos_ref[:, j:j + 1], tg_ref[:, j:j + 1], 0.0)
    y = _mm3(w, stg_ref[slot])
    out = x1_ref[...] + mod_ref[0, 5:6, :] * _rms(y, ng_ref[3:4, :])
    if len(o_refs) == 1:
        o_refs[0][...] = out
    else:
        @pl.when(i < ctx_tiles)
        def _():
            o_refs[0][...] = out

        o_refs[1][...] = out


def moe_combine(lay, yb, table, pos, tg, x1, mod, ng, split):
    n = lay.n
    posl = jnp.pad(pos.reshape(n, TOP_K), ((0, 0), (0, LANES - TOP_K)))
    row = lambda w: pl.BlockSpec((TM, w), lambda i: (i, 0))
    tab = lambda f: pl.BlockSpec((1, 1, LANES), lambda i: (f(i), 0, 0), memory_space=pltpu.SMEM)
    if split:
        out_shape = (jax.ShapeDtypeStruct((lay.ctx_tokens, D_MODEL), F32),
                     jax.ShapeDtypeStruct((lay.dec_tokens, D_MODEL), F32))
        out_specs = (pl.BlockSpec((TM, D_MODEL), lambda i: (jnp.minimum(i, lay.ctx_tiles - 1), 0)),
                     pl.BlockSpec((TM, D_MODEL), lambda i: (jnp.maximum(i - lay.ctx_tiles, 0), 0)))
    else:
        out_shape = jax.ShapeDtypeStruct((n, D_MODEL), F32)
        out_specs = row(D_MODEL)
    return pl.pallas_call(
        functools.partial(_combine_kernel, ctx_tiles=lay.ctx_tiles),
        out_shape=out_shape,
        grid=(lay.tiles,),
        in_specs=[tab(lambda i: i), tab(lambda i: jnp.minimum(i + 1, lay.tiles - 1)),
                  pl.BlockSpec(memory_space=pl.ANY),
                  row(LANES), row(LANES), row(D_MODEL),
                  pl.BlockSpec((1, N_MOD, D_MODEL), lambda i: (lay.mod_row(i), 0, 0)),
                  pl.BlockSpec(ng.shape, lambda i: (0, 0))],
        out_specs=out_specs,
        scratch_shapes=[pltpu.VMEM((2, SORT_ROWS, D_MODEL), F32), pltpu.SemaphoreType.DMA((2,))],
        compiler_params=_cparams(("arbitrary",)),
        name="moe_combine",
    )(table, table, yb, posl, tg, x1, mod, ng)


def layer(lay, x, l, cond, P, ctx_k, ctx_v, s0_dec, xb_buf):
    mod = mod_table(cond, P['mod_w'][l], P['mod_b'][l]).reshape(cond.shape[0], N_MOD, D_MODEL)
    ng = P['norm_g'][l]
    q, kf, vf, kb, vb, zb, zc, gates = in_proj(lay, x, mod, ng[0:1], P['w_in'][l].astype(BF16),
                                               P['b_in'][l].reshape(1, P_IN))
    oa_c = ctx_attention(lay, q, kb, vb)
    vec = lambda a: a.reshape(1, -1)
    prm = (P['rw_mu'][l], P['rw_w0'][l], P['rw_w2'][l], P['rw_a0'][l], P['rw_a2'][l],
           vec(P['rw_k_k'][l]), vec(P['rw_k_a'][l]), vec(P['rw_r_k'][l]))
    yf_c, yb_c, g_c, s_ctx = rwkv_scan(zb.reshape(-1, lay.ctx_len, B_COLS), 0, lay.n_ctx, None, prm,
                                       P['rw_g2'][l])
    na = (lay, q, kb, vb, ctx_k.astype(BF16), ctx_v.astype(BF16), na_bias_table(P['rpb'][l]))
    yf_d, yb_d, g_d, _, oa_d = rwkv_scan(zb.reshape(-1, lay.dec_len, B_COLS), lay.ctx_tokens // lay.dec_len,
                                         lay.n_dec, s0_dec, prm, P['rw_g2'][l], na=na)
    o_cv = conv_module(lay, zc, P['conv_w'][l], vec(P['conv_b'][l]), vec(P['conv_ln_g'][l]),
                       vec(P['conv_ln_b'][l]))
    rw = jnp.zeros((D_MODEL, LANES), F32).at[:, :N_EXPERTS].set(P['router_w'][l])
    rb = jnp.full((1, LANES), NEG_INF, F32).at[0, :N_EXPERTS].set(P['router_b'][l])
    x1, hm, top_i, top_g, rank, counts = merge_router(
        lay, x, (oa_c, yf_c, yb_c, g_c), (oa_d, yf_d, yb_d, g_d), o_cv, gates, mod, ng,
        vec(P['rw_lnx_g'][l]), vec(P['rw_lnx_b'][l]),
        P['w_o_attn'][l].astype(BF16), P['w_o_rwkv'][l].astype(BF16), P['w_o_conv'][l].astype(BF16),
        P['w_out'][l].astype(BF16), rw, rb)
    table, pos, block_e, n_used, n_blocks = route_tables(top_i[:, :TOP_K], rank[:, :TOP_K], counts, lay.n)
    if xb_buf is None:
        xb_buf = jnp.zeros((n_blocks * ROUTE_BLOCK, D_MODEL), F32)
    xb = moe_dispatch(lay, hm, table, pos, xb_buf)
    b1 = P['exp_b1'][l].reshape(N_EXPERTS, -1, LANES, 2).transpose(0, 1, 3, 2).reshape(N_EXPERTS, 1, -1)
    ybk = moe_experts(xb, block_e, n_used, n_blocks, l, P['exp_w1'], b1, P['exp_w2'], P['exp_b2'][l][:, None, :])
    x2 = moe_combine(lay, ybk, table, pos, top_g, x1, mod, ng, split=(l == DEPTH - 1))
    return x2, kf, vf, s_ctx, xb


def kernel(x_prompt, x_sample, cache_attn_k, cache_attn_v, state_rwkv, c, c_ctx, mod_w, mod_b, norm_g, w_in, b_in, rpb, w_o_attn, rw_mu, rw_w0, rw_w2, rw_a0, rw_a2, rw_g2, rw_k_k, rw_k_a, rw_r_k, rw_lnx_g, rw_lnx_b, w_o_rwkv, conv_w, conv_b, conv_ln_g, conv_ln_b, w_o_conv, w_out, router_w, router_b, exp_w1, exp_b1, exp_w2, exp_b2):
    P = {
        'mod_w': mod_w, 'mod_b': mod_b, 'norm_g': norm_g, 'w_in': w_in, 'b_in': b_in, 'rpb': rpb,
        'w_o_attn': w_o_attn, 'rw_mu': rw_mu, 'rw_w0': rw_w0, 'rw_w2': rw_w2, 'rw_a0': rw_a0,
        'rw_a2': rw_a2, 'rw_g2': rw_g2, 'rw_k_k': rw_k_k, 'rw_k_a': rw_k_a, 'rw_r_k': rw_r_k,
        'rw_lnx_g': rw_lnx_g, 'rw_lnx_b': rw_lnx_b, 'w_o_rwkv': w_o_rwkv, 'conv_w': conv_w,
        'conv_b': conv_b, 'conv_ln_g': conv_ln_g, 'conv_ln_b': conv_ln_b, 'w_o_conv': w_o_conv,
        'w_out': w_out, 'router_w': router_w, 'router_b': router_b, 'exp_w1': exp_w1,
        'exp_b1': exp_b1, 'exp_w2': exp_w2, 'exp_b2': exp_b2,
    }
    nb, sl, _ = x_prompt.shape
    db, ds, _ = x_sample.shape
    lay = Layout(nb, sl, db, ds)
    x = jnp.concatenate([x_prompt.reshape(-1, D_MODEL), x_sample.reshape(-1, D_MODEL)], axis=0)
    cond = jnp.zeros((8, D_MODEL), F32).at[0].set(c_ctx).at[1:1 + db].set(c)
    ks, vs, ss = [], [], []
    xb_buf = None
    for l in range(DEPTH):
        ck = cache_attn_k[:, l].reshape(db, -1, NA_WIDTH)
        cv = cache_attn_v[:, l].reshape(db, -1, NA_WIDTH)
        x, kf, vf, s_ctx, xb_buf = layer(lay, x, l, cond, P, ck, cv, state_rwkv[:, l], xb_buf)
        ks.append(kf.reshape(nb, sl, NA_HEADS, NA_HEAD_DIM))
        vs.append(vf.reshape(nb, sl, NA_HEADS, NA_HEAD_DIM))
        ss.append(s_ctx)
    y_prompt = x[0].reshape(nb, sl, D_MODEL)
    y_sample = x[1].reshape(db, ds, D_MODEL)
    return (y_prompt, y_sample, jnp.stack(ks, axis=1), jnp.stack(vs, axis=1), jnp.stack(ss, axis=1))
```

```python
import functools

import jax
import jax.numpy as jnp
from jax import lax
from jax.experimental import pallas as pl
from jax.experimental.pallas import tpu as pltpu

F32 = jnp.float32
BF16 = jnp.bfloat16

D_MODEL = 1024
BATCH, SEQ = 32, 256
DEPTH = 2
DEC_BATCH, DEC_SEQ = 4, 4096
PAST_LEN = 256
GRID_W = 64
NA_HEADS, NA_HEAD_DIM = 8, 64
NA_WIDTH = NA_HEADS * NA_HEAD_DIM
NA_WIN_ROWS, NA_WIN_COLS = 8, 16
RW_HEADS, RW_HEAD_DIM = 8, 64
RW_WIDTH = RW_HEADS * RW_HEAD_DIM
RW_DECAY_RANK, RW_ICLR_RANK, RW_GATE_RANK = 64, 64, 128
CONV_WIDTH, CONV_K = 512, 31
N_BRANCH, N_MOD = 3, 6
N_EXPERTS, TOP_K, EXPERT_FF = 32, 4, 1024
SWIGLU_LIMIT, SWIGLU_ALPHA = 7.0, 1.702
ROUTE_BLOCK = 512
RMS_EPS, LN_EPS, GN_EPS = 1e-6, 1e-5, 64e-5
NEG_INF = -1e30
A_COLS = 3 * NA_WIDTH
B_COLS = 3 * RW_WIDTH + 2 * RW_DECAY_RANK + 2 * RW_ICLR_RANK + RW_GATE_RANK
C_COLS = 2 * CONV_WIDTH
G_COLS = N_BRANCH * D_MODEL
P_IN = A_COLS + B_COLS + C_COLS + G_COLS

LANES = 128
TM = 256
CHUNK = 64
CONV_HALO = 16
VMEM_LIMIT = 56 * 1024 * 1024


def _cparams(sem):
    return pltpu.CompilerParams(dimension_semantics=sem, vmem_limit_bytes=VMEM_LIMIT)


def _dot(a, b, dims=((1,), (0,))):
    return lax.dot_general(a, b, (dims, ((), ())), preferred_element_type=F32)


def _mm(a, b, dims=((1,), (0,))):
    return _dot(a.astype(BF16), b.astype(BF16), dims)


def _split(a):
    hi = a.astype(BF16)
    lo = (a - hi.astype(F32)).astype(BF16)
    return hi, lo


def _mm3(a, b, dims=((1,), (0,))):
    ah, al = _split(a)
    bh, bl = _split(b)
    return _dot(ah, bh, dims) + (_dot(ah, bl, dims) + _dot(al, bh, dims))


_NT = ((1,), (1,))


def _rms(x, g):
    return x * lax.rsqrt(jnp.mean(x * x, axis=-1, keepdims=True) + RMS_EPS) * g


def _sigmoid(x):
    return 1.0 / (1.0 + jnp.exp(-x))


class Layout:
    def __init__(self, n_ctx, ctx_len, n_dec, dec_len):
        assert ctx_len == TM and dec_len % TM == 0
        self.n_ctx, self.ctx_len, self.n_dec, self.dec_len = n_ctx, ctx_len, n_dec, dec_len
        self.ctx_tokens = n_ctx * ctx_len
        self.dec_tokens = n_dec * dec_len
        self.n = self.ctx_tokens + self.dec_tokens
        self.ctx_tiles = self.ctx_tokens // TM
        self.dec_tiles_per_seq = dec_len // TM
        self.tiles = self.n // TM

    def mod_row(self, i):
        return jnp.where(i < self.ctx_tiles, 0, 1 + (i - self.ctx_tiles) // self.dec_tiles_per_seq)

    def tile_pos(self, i):
        j = (i - self.ctx_tiles) % self.dec_tiles_per_seq
        is_ctx = i < self.ctx_tiles
        return is_ctx | (j == 0), is_ctx | (j == self.dec_tiles_per_seq - 1)


def _mod_kernel(c_ref, w_ref, b_ref, o_ref):
    c = c_ref[...]
    o_ref[...] = _mm3(c * _sigmoid(c), w_ref[...]) + b_ref[...]


def mod_table(cond, w, b):
    tn = 1024
    nm = w.shape[1]
    return pl.pallas_call(
        _mod_kernel,
        out_shape=jax.ShapeDtypeStruct((cond.shape[0], nm), F32),
        grid=(nm // tn,),
        in_specs=[pl.BlockSpec(cond.shape, lambda j: (0, 0)),
                  pl.BlockSpec((D_MODEL, tn), lambda j: (0, j)),
                  pl.BlockSpec((1, tn), lambda j: (0, j))],
        out_specs=pl.BlockSpec((cond.shape[0], tn), lambda j: (0, j)),
        compiler_params=_cparams(("arbitrary",)),
        name="mod_table",
    )(cond, w, b.reshape(1, nm))


def _inproj_kernel(x_ref, mod_ref, g_ref, w_ref, b_ref,
                   q_ref, kf_ref, vf_ref, kb_ref, vb_ref, zb_ref, zc_ref, gt_ref, *, ctx_tiles):
    u = (_rms(x_ref[...], g_ref[...]) * (1.0 + mod_ref[0, 1:2, :]) + mod_ref[0, 0:1, :]).astype(BF16)

    def seg(c0, width):
        return _dot(u, w_ref[:, c0:c0 + width]) + b_ref[:, c0:c0 + width]

    q_ref[...] = seg(0, NA_WIDTH).astype(BF16)
    k = seg(NA_WIDTH, NA_WIDTH)
    kb_ref[...] = k.astype(BF16)
    v = seg(2 * NA_WIDTH, NA_WIDTH)
    vb_ref[...] = v.astype(BF16)

    @pl.when(pl.program_id(0) < ctx_tiles)
    def _():
        kf_ref[...] = k
        vf_ref[...] = v
    wb = 384
    for j in range(B_COLS // wb):
        zb_ref[:, j * wb:(j + 1) * wb] = seg(A_COLS + j * wb, wb)
    wc = 512
    for j in range(C_COLS // wc):
        zc_ref[:, j * wc:(j + 1) * wc] = seg(A_COLS + B_COLS + j * wc, wc)
    for j in range(G_COLS // wc):
        gt_ref[:, j * wc:(j + 1) * wc] = _sigmoid(seg(A_COLS + B_COLS + C_COLS + j * wc, wc)).astype(BF16)


def in_proj(lay, x, mod, g0, w_bf, b):
    n = lay.n
    row = lambda w: pl.BlockSpec((TM, w), lambda i: (i, 0))
    crow = pl.BlockSpec((TM, NA_WIDTH), lambda i: (jnp.minimum(i, lay.ctx_tiles - 1), 0))
    ctx_f32 = jax.ShapeDtypeStruct((lay.ctx_tokens, NA_WIDTH), F32)
    out_shape = (jax.ShapeDtypeStruct((n, NA_WIDTH), BF16), ctx_f32, ctx_f32,
                 jax.ShapeDtypeStruct((n, NA_WIDTH), BF16), jax.ShapeDtypeStruct((n, NA_WIDTH), BF16),
                 jax.ShapeDtypeStruct((n, B_COLS), F32), jax.ShapeDtypeStruct((n, C_COLS), F32),
                 jax.ShapeDtypeStruct((n, G_COLS), BF16))
    return pl.pallas_call(
        functools.partial(_inproj_kernel, ctx_tiles=lay.ctx_tiles),
        out_shape=out_shape,
        grid=(lay.tiles,),
        in_specs=[row(D_MODEL),
                  pl.BlockSpec((1, N_MOD, D_MODEL), lambda i: (lay.mod_row(i), 0, 0)),
                  pl.BlockSpec((1, D_MODEL), lambda i: (0, 0)),
                  pl.BlockSpec((D_MODEL, P_IN), lambda i: (0, 0), pipeline_mode=pl.Buffered(1)),
                  pl.BlockSpec((1, P_IN), lambda i: (0, 0))],
        out_specs=(row(NA_WIDTH), crow, crow, row(NA_WIDTH), row(NA_WIDTH),
                   row(B_COLS), row(C_COLS), row(G_COLS)),
        compiler_params=_cparams(("arbitrary",)),
        name="in_proj",
    )(x, mod, g0, w_bf, b)


def _ctx_attn_kernel(q_ref, k_ref, v_ref, o_ref):
    scale = NA_HEAD_DIM ** -0.5
    lo = lax.broadcasted_iota(jnp.int32, (q_ref.shape[0], LANES), 1) < NA_HEAD_DIM
    heads = []
    for h in range(NA_HEADS):
        ps = slice((h // 2) * LANES, (h // 2 + 1) * LANES)
        qp = q_ref[:, ps] * scale
        qh = jnp.where(lo if h % 2 == 0 else jnp.logical_not(lo), qp, jnp.zeros_like(qp))
        heads.append(dict(ps=ps, s=_dot(qh, k_ref[:, ps], _NT)))
    for u in heads:
        u['p'] = jnp.exp(u['s'] - jnp.max(u['s'], axis=-1, keepdims=True))
    for u in heads:
        u['o'] = _dot(u['p'].astype(BF16), v_ref[:, u['ps']]) / jnp.sum(u['p'], axis=-1, keepdims=True)
    for p in range(NA_HEADS // 2):
        o_ref[:, p * LANES:(p + 1) * LANES] = jnp.where(lo, heads[2 * p]['o'], heads[2 * p + 1]['o']).astype(BF16)


def ctx_attention(lay, q, kb, vb):
    blk = pl.BlockSpec((lay.ctx_len, NA_WIDTH), lambda b: (b, 0))
    return pl.pallas_call(
        _ctx_attn_kernel,
        out_shape=jax.ShapeDtypeStruct((lay.ctx_tokens, NA_WIDTH), BF16),
        grid=(lay.n_ctx,),
        in_specs=[blk, blk, blk],
        out_specs=blk,
        compiler_params=_cparams(("arbitrary",)),
        name="ctx_attention",
    )(q, kb, vb)


def na_bias_table(rpb):
    kw, kh = NA_WIN_COLS, NA_WIN_ROWS
    col = jnp.arange(GRID_W, dtype=jnp.int32)
    col_start = jnp.clip(col - kw // 2, 0, GRID_W - kw)
    col_ok = (col[None, :] >= col_start[:, None]) & (col[None, :] < col_start[:, None] + kw)
    dc = jnp.clip(col[None, :] - col[:, None] + kw - 1, 0, 2 * kw - 2)
    bias_c = jnp.where(col_ok, rpb[:, :, dc].astype(F32), NEG_INF)
    tabs = [jnp.concatenate([bias_c[:, d + i] for i in range(kh)], axis=-1) for d in range(kh)]
    return jnp.stack(tabs, axis=0)


def _na_row_start(r, rows):
    return jnp.clip(r - NA_WIN_ROWS // 2, 0, rows - NA_WIN_ROWS)


def _na_row_stages(q_ref, k_ref, v_ref, ck_ref, cv_ref, bias_ref, o_ref, r, rows):
    rs = _na_row_start(r, rows)
    n_loc = NA_WIN_ROWS * GRID_W
    start = pl.multiple_of(rs * GRID_W, GRID_W)
    scale = NA_HEAD_DIM ** -0.5
    lo = lax.broadcasted_iota(jnp.int32, (GRID_W, LANES), 1) < NA_HEAD_DIM
    heads = []
    for h in range(NA_HEADS):
        ps = slice((h // 2) * LANES, (h // 2 + 1) * LANES)
        qp = q_ref[:, ps] * scale
        qh = jnp.where(lo if h % 2 == 0 else jnp.logical_not(lo), qp, jnp.zeros_like(qp))
        s_loc = _dot(qh, k_ref[pl.ds(start, n_loc), ps], _NT) + bias_ref[0, h]
        s_ctx = _dot(qh, ck_ref[0, :, ps], _NT)
        heads.append(dict(ps=ps, s_loc=s_loc, s_ctx=s_ctx))
    yield
    for u in heads:
        u['m'] = jnp.maximum(jnp.max(u['s_loc'], axis=-1, keepdims=True),
                             jnp.max(u['s_ctx'], axis=-1, keepdims=True))
    yield
    for u in heads:
        u['p_loc'] = jnp.exp(u['s_loc'] - u['m'])
        u['p_ctx'] = jnp.exp(u['s_ctx'] - u['m'])
        u['den'] = jnp.sum(u['p_loc'], axis=-1, keepdims=True) + jnp.sum(u['p_ctx'], axis=-1, keepdims=True)
    yield
    for u in heads:
        o = (_dot(u['p_loc'].astype(BF16), v_ref[pl.ds(start, n_loc), u['ps']])
             + _dot(u['p_ctx'].astype(BF16), cv_ref[0, :, u['ps']]))
        u['o'] = o / u['den']
    yield
    for p in range(NA_HEADS // 2):
        o_ref[:, p * LANES:(p + 1) * LANES] = jnp.where(lo, heads[2 * p]['o'], heads[2 * p + 1]['o']).astype(BF16)


def _softplus(x):
    return jnp.maximum(x, 0.0) + jnp.log(1.0 + jnp.exp(-jnp.abs(x)))


def _rwkv_prepare(d, z, prev_row, next_row, prm):
    (mu_ref, w0_ref, w2_ref, a0_ref, a2_ref, kk_ref, ka_ref, rk_ref) = prm
    c = CHUNK
    cw = RW_WIDTH
    row = lax.broadcasted_iota(jnp.int32, (c, 1), 0)
    prev = jnp.where(row == 0, prev_row, pltpu.roll(z, 1, axis=0))
    nxt = jnp.where(row == c - 1, next_row, pltpu.roll(z, c - 1, axis=0))
    zs = z + mu_ref[0:1, :] * (prev - z) + mu_ref[1:2, :] * (nxt - z)
    r, k, v = zs[:, :cw], zs[:, cw:2 * cw], zs[:, 2 * cw:3 * cw]
    o = 3 * cw
    zw = zs[:, o + d * RW_DECAY_RANK:o + (d + 1) * RW_DECAY_RANK]
    o += 2 * RW_DECAY_RANK
    za = zs[:, o + d * RW_ICLR_RANK:o + (d + 1) * RW_ICLR_RANK]

    w_log = -_softplus(-(w0_ref[d:d + 1, :] + _mm3(jnp.tanh(zw), w2_ref[d]))) - 0.5
    logw = -jnp.exp(w_log)
    a = _sigmoid(a0_ref[d:d + 1, :] + _mm3(za, a2_ref[d]))
    kd = k * (1.0 + (a - 1.0) * ka_ref[...])
    kk = k * kk_ref[...]

    ti = lax.broadcasted_iota(jnp.int32, (c, c), 0)
    tj = lax.broadcasted_iota(jnp.int32, (c, c), 1)
    tri = ((ti >= tj) if d == 0 else (tj >= ti)).astype(BF16)
    lh, ll = _split(logw)
    cl = _dot(tri, lh) + _dot(tri, ll)
    p_in = jnp.exp(cl)
    p_ex = jnp.exp(cl - logw)
    p_inv = jnp.exp(-cl)
    p_end = p_in[c - 1:c, :] if d == 0 else p_in[0:1, :]

    pi = lax.broadcasted_iota(jnp.int32, (LANES, LANES), 0)
    pj = lax.broadcasted_iota(jnp.int32, (LANES, LANES), 1)
    same_head = (pi // RW_HEAD_DIM) == (pj // RW_HEAD_DIM)
    ones_bd = same_head.astype(BF16)

    def head_sum(x):
        xh, xl = _split(x)
        return _dot(xh, ones_bd) + _dot(xl, ones_bd)

    t2 = lax.broadcasted_iota(jnp.int32, (2 * c, LANES), 0) % c
    j2 = lax.broadcasted_iota(jnp.int32, (2 * c, LANES), 1) % c
    diff = (t2 - j2) if d == 0 else (j2 - t2)
    upper = lax.broadcasted_iota(jnp.int32, (2 * c, LANES), 0) < c
    mask = diff >= jnp.where(upper, 1, 0)
    units = []
    for p in range(RW_HEADS // 2):
        sl = slice(p * LANES, (p + 1) * LANES)
        kkp = kk[:, sl]
        kkp = kkp * lax.rsqrt(jnp.maximum(head_sum(kkp * kkp), 1e-24))
        pinv = p_inv[:, sl]
        bt = a[:, sl] * kkp * pinv
        kt = kd[:, sl] * pinv
        ar = jnp.concatenate([-kkp * p_ex[:, sl], r[:, sl] * p_in[:, sl]], axis=0).astype(BF16)
        pe = p_end[:, sl]
        bonus = head_sum(r[:, sl] * kd[:, sl] * rk_ref[:, sl]) * v[:, sl]
        units.append(dict(d=d, p=p, ar=ar, bt=bt, kt=kt, v=v[:, sl], pe=pe,
                          bkp=jnp.concatenate([bt * pe, kt * pe], axis=0), bonus=bonus,
                          mask=mask, eye=(diff[:c] == 0).astype(F32), bd=same_head))
    return units, zs


def _bd(x):
    lo = lax.broadcasted_iota(jnp.int32, x.shape, 1) < RW_HEAD_DIM
    xb = x.astype(BF16)
    z = jnp.zeros_like(xb)
    return jnp.concatenate([jnp.where(lo, xb, z), jnp.where(lo, z, xb)], axis=0)


def _rwkv_solve(units, s_ref, fill=()):
    c = CHUNK

    def tick():
        for f in fill:
            next(f, None)

    for u in units:
        gb = _dot(u['ar'], _bd(u['bt']), _NT)
        gk = _dot(u['ar'], _bd(u['kt']), _NT)
        gb = jnp.where(u['mask'], gb, 0.0)
        u['a_ab'], u['a_rb'] = gb[:c], gb[c:]
        u['s0'] = s_ref[u['d'], u['p']]
        u['xs'] = _dot(u['ar'], u['s0'].astype(BF16), _NT) + _mm(jnp.where(u['mask'], gk, 0.0), _bd(u['v']))
    tick()
    for u in units:
        u['pw'] = _mm(u['a_ab'], _bd(u['a_ab']))
        u['tinv'] = u['eye'] + u['a_ab']
    tick()
    n_sq = CHUNK.bit_length() - 2
    for i in range(n_sq):
        for u in units:
            if i < n_sq - 1:
                both = _mm(jnp.concatenate([u['pw'], u['tinv']], axis=0), _bd(u['pw']))
                u['pw'], u['tinv'] = both[:c], u['tinv'] + both[c:]
            else:
                u['tinv'] = u['tinv'] + _mm(u['tinv'], _bd(u['pw']))
        tick()
    for u in units:
        u['u'] = _mm(u['tinv'], _bd(u['xs'][:c]))
    tick()
    ys = []
    for u in units:
        ys.append(u['xs'][c:] + _mm(u['a_rb'], _bd(u['u'])) + u['bonus'])
        uv = jnp.concatenate([u['u'], u['v']], axis=0)
        s_ref[u['d'], u['p']] = u['s0'] * u['pe'] + jnp.where(u['bd'], _mm(uv.T, u['bkp']), 0.0)
    for f in fill:
        for _ in f:
            pass
    return ys


RW_SEQS_MAX = 2
NA_ROWS_PER_STEP = 2


def _rwkv_kernel(*refs, nc, has_s0, na_rows, RW_SEQS):
    zf_ref, zfp_ref, zfn_ref, zr_ref, zrp_ref, zrn_ref = refs[:6]
    n_in = 6 + (1 if has_s0 else 0)
    s0_ref = refs[6] if has_s0 else None
    prm = refs[n_in:n_in + 8]
    g2_ref = refs[n_in + 8]
    n_in += 9
    i = pl.program_id(1)
    if na_rows:
        q_ref, k_ref, v_ref, ck_ref, cv_ref = refs[n_in:n_in + 5]
        bias_refs = refs[n_in + 5:n_in + 5 + NA_ROWS_PER_STEP]
        yf_ref, yb_ref, g_ref, sfin_ref, oa_ref, s_ref = refs[n_in + 5 + NA_ROWS_PER_STEP:]
    else:
        yf_ref, yb_ref, g_ref, sfin_ref, s_ref = refs[n_in:]

    hd = RW_HEAD_DIM
    npair = RW_HEADS // 2

    @pl.when(i == 0)
    def _():
        s_ref[...] = jnp.zeros(s_ref.shape, F32)
        if has_s0:
            for k in range(RW_SEQS):
                for d in range(2):
                    for p in range(npair):
                        s_ref[2 * k + d, p, :hd, :hd] = s0_ref[k, d, 2 * p]
                        s_ref[2 * k + d, p, hd:, hd:] = s0_ref[k, d, 2 * p + 1]

    zero = jnp.zeros((1, B_COLS), F32)
    units = []
    for k in range(RW_SEQS):
        pf = jnp.where(i == 0, zero, zfp_ref[k, 7:8, :])
        nf = jnp.where(i == nc - 1, zero, zfn_ref[k, 0:1, :])
        uf, zs = _rwkv_prepare(0, zf_ref[k], pf, nf, prm)
        g_ref[k] = _mm(_sigmoid(zs[:, B_COLS - RW_GATE_RANK:]), g2_ref[...])
        pr = jnp.where(i == nc - 1, zero, zrp_ref[k, 7:8, :])
        nr = jnp.where(i == 0, zero, zrn_ref[k, 0:1, :])
        ub, _ = _rwkv_prepare(1, zr_ref[k], pr, nr, prm)
        for u in uf + ub:
            u['d'] = 2 * k + u['d']
        units += uf + ub
    fill = []
    if na_rows:
        r0 = ((pl.program_id(0) * nc + i) * NA_ROWS_PER_STEP) % na_rows
        for k, b_ref in enumerate(bias_refs):
            rs = pl.ds(k * GRID_W, GRID_W)
            fill.append(_na_row_stages(q_ref.at[rs], k_ref, v_ref, ck_ref, cv_ref, b_ref, oa_ref.at[rs],
                                       r0 + k, na_rows))
    ys = _rwkv_solve(units, s_ref, fill)
    for k in range(RW_SEQS):
        yk = ys[2 * npair * k:2 * npair * (k + 1)]
        yf_ref[k] = jnp.concatenate(yk[:npair], axis=-1)
        yb_ref[k] = jnp.concatenate(yk[npair:], axis=-1)

    @pl.when(i == nc - 1)
    def _():
        for k in range(RW_SEQS):
            for d in range(2):
                for p in range(npair):
                    sfin_ref[k, d, 2 * p] = s_ref[2 * k + d, p, :hd, :hd]
                    sfin_ref[k, d, 2 * p + 1] = s_ref[2 * k + d, p, hd:, hd:]


def rwkv_scan(zb, seq0, nseq, s0, prm, g2, na=None):
    seqlen = zb.shape[1]
    ns = RW_SEQS_MAX
    while seq0 % ns or nseq % ns or (na is not None and na[0].n_dec * (na[0].dec_len // GRID_W)
                                     != NA_ROWS_PER_STEP * (nseq // ns) * (seqlen // CHUNK)):
        ns //= 2
    assert ns >= 1 and seqlen % CHUNK == 0
    nc = seqlen // CHUNK
    sub = CHUNK // 8
    b0 = seq0 // ns
    has_s0 = s0 is not None

    def specs(cidx):
        return [pl.BlockSpec((ns, CHUNK, B_COLS), lambda s, i: (b0 + s, cidx(i), 0)),
                pl.BlockSpec((ns, 8, B_COLS), lambda s, i: (b0 + s, jnp.maximum(cidx(i) * sub - 1, 0), 0)),
                pl.BlockSpec((ns, 8, B_COLS),
                             lambda s, i: (b0 + s, jnp.minimum((cidx(i) + 1) * sub, seqlen // 8 - 1), 0))]

    cf = lambda i: i
    cr = lambda i: nc - 1 - i
    full = lambda a: pl.BlockSpec(a.shape, lambda s, i: (0,) * a.ndim)
    sshape = (2, RW_HEADS, RW_HEAD_DIM, RW_HEAD_DIM)
    sspec = pl.BlockSpec((ns,) + sshape, lambda s, i: (s, 0, 0, 0, 0))
    in_specs = specs(cf) + specs(cr)
    args = [zb] * 6
    if has_s0:
        in_specs.append(sspec)
        args.append(s0)
    in_specs += [full(p) for p in prm] + [full(g2)]
    args += list(prm) + [g2]
    out = lambda cidx: pl.BlockSpec((ns, CHUNK, RW_WIDTH), lambda s, i: (s, cidx(i), 0))
    tok = jax.ShapeDtypeStruct((nseq, seqlen, RW_WIDTH), F32)
    out_shape = [tok, tok, tok, jax.ShapeDtypeStruct((nseq,) + sshape, F32)]
    out_specs = [out(cf), out(cr), out(cf), sspec]
    na_rows = 0
    if na is not None:
        lay, q, kb, vb, ck, cv, bias_tab = na
        na_rows = lay.dec_len // GRID_W
        per = NA_ROWS_PER_STEP
        assert na_rows >= NA_WIN_ROWS and na_rows % per == 0 and lay.ctx_tokens % lay.dec_len == 0
        assert lay.n_dec * na_rows == per * (nseq // ns) * nc and lay.ctx_tokens % (per * GRID_W) == 0
        step = lambda s, i: s * nc + i
        batch = lambda s, i: step(s, i) * per // na_rows
        kv = pl.BlockSpec((lay.dec_len, NA_WIDTH), lambda s, i: (lay.ctx_tokens // lay.dec_len + batch(s, i), 0))
        cs = pl.BlockSpec((1, ck.shape[1], NA_WIDTH), lambda s, i: (batch(s, i), 0, 0))

        def bias_spec(k):
            def idx(s, i):
                r = step(s, i) * per % na_rows + k
                return (_na_row_start(r, na_rows) - r + NA_WIN_ROWS - 1, 0, 0, 0)
            return pl.BlockSpec((1, NA_HEADS, GRID_W, NA_WIN_ROWS * GRID_W), idx)

        in_specs += [pl.BlockSpec((per * GRID_W, NA_WIDTH),
                                  lambda s, i: (lay.ctx_tokens // (per * GRID_W) + step(s, i), 0)),
                     kv, kv, cs, cs] + [bias_spec(k) for k in range(per)]
        args += [q, kb, vb, ck, cv] + [bias_tab] * per
        out_shape.append(jax.ShapeDtypeStruct((lay.dec_tokens, NA_WIDTH), BF16))
        out_specs.append(pl.BlockSpec((per * GRID_W, NA_WIDTH), lambda s, i: (step(s, i), 0)))
    res = pl.pallas_call(
        functools.partial(_rwkv_kernel, nc=nc, has_s0=has_s0, na_rows=na_rows, RW_SEQS=ns),
        out_shape=tuple(out_shape),
        grid=(nseq // ns, nc),
        in_specs=in_specs,
        out_specs=tuple(out_specs),
        scratch_shapes=[pltpu.VMEM((2 * ns, RW_HEADS // 2, LANES, LANES), F32)],
        compiler_params=_cparams(("arbitrary", "arbitrary")),
        name="rwkv_scan",
    )(*args)
    flat = lambda a: a.reshape(nseq * seqlen, RW_WIDTH)
    return (flat(res[0]), flat(res[1]), flat(res[2])) + tuple(res[3:])


SUBLANES = 8


def _conv_kernel(z_ref, zp_ref, zn_ref, w_ref, b_ref, g_ref, be_ref, o_ref, h_ref, hs_ref, *, lay):
    i = pl.program_id(0)
    first, last = lay.tile_pos(i)

    def glu(z):
        return z[:, :CONV_WIDTH] * _sigmoid(z[:, CONV_WIDTH:])

    hz = jnp.zeros((CONV_HALO, CONV_WIDTH), F32)
    h_ref[0:CONV_HALO, :] = jnp.where(first, hz, glu(zp_ref[...]))
    h_ref[CONV_HALO:CONV_HALO + TM, :] = glu(z_ref[...])
    h_ref[CONV_HALO + TM:, :] = jnp.where(last, hz, glu(zn_ref[...]))
    off = CONV_HALO - CONV_K // 2
    rows = hs_ref.shape[1]
    for ph in range(SUBLANES):
        hs_ref[ph] = h_ref[ph:ph + rows, :]
    acc = jnp.zeros((TM, CONV_WIDTH), F32) + b_ref[...]
    for j in range(CONV_K):
        ph, base = (off + j) % SUBLANES, (off + j) // SUBLANES * SUBLANES
        acc = acc + w_ref[j:j + 1, :] * hs_ref[ph, base:base + TM, :]
    xc = acc - jnp.mean(acc, axis=-1, keepdims=True)
    hn = xc * lax.rsqrt(jnp.mean(xc * xc, axis=-1, keepdims=True) + LN_EPS) * g_ref[...] + be_ref[...]
    o_ref[...] = (hn * _sigmoid(hn)).astype(BF16)


def conv_module(lay, zc, w, b, g, be):
    n = lay.n
    hb = TM // CONV_HALO
    vec = pl.BlockSpec((1, CONV_WIDTH), lambda i: (0, 0))
    return pl.pallas_call(
        functools.partial(_conv_kernel, lay=lay),
        out_shape=jax.ShapeDtypeStruct((n, CONV_WIDTH), BF16),
        grid=(lay.tiles,),
        in_specs=[pl.BlockSpec((TM, C_COLS), lambda i: (i, 0)),
                  pl.BlockSpec((CONV_HALO, C_COLS), lambda i: (jnp.maximum(i * hb - 1, 0), 0)),
                  pl.BlockSpec((CONV_HALO, C_COLS), lambda i: (jnp.minimum((i + 1) * hb, n // CONV_HALO - 1), 0)),
                  pl.BlockSpec((CONV_K, CONV_WIDTH), lambda i: (0, 0)), vec, vec, vec],
        out_specs=pl.BlockSpec((TM, CONV_WIDTH), lambda i: (i, 0)),
        scratch_shapes=[pltpu.VMEM((TM + 2 * CONV_HALO, CONV_WIDTH), F32),
                        pltpu.VMEM((SUBLANES, TM + 2 * CONV_HALO - SUBLANES, CONV_WIDTH), F32)],
        compiler_params=_cparams(("arbitrary",)),
        name="conv_module",
    )(zc, zc, zc, w, b, g, be)


def _merge_kernel(x_ref, oa_c_ref, yf_c_ref, yb_c_ref, g_c_ref, oa_d_ref, yf_d_ref, yb_d_ref, g_d_ref,
                  oc_ref, gt_ref, mod_ref, ng_ref, lg_ref, lb_ref,
                  wa_ref, wr_ref, wc_ref, wo_ref, rw_ref, rb_ref,
                  x1_ref, hm_ref, ti_ref, tg_ref, rk_ref, cnt_ref, *, ctx_tiles):
    is_ctx = pl.program_id(0) < ctx_tiles
    pick = lambda c_ref, d_ref: jnp.where(is_ctx, c_ref[...], d_ref[...])
    oa = pick(oa_c_ref, oa_d_ref)
    g_rw = pick(g_c_ref, g_d_ref)
    y = pick(yf_c_ref, yf_d_ref) + pick(yb_c_ref, yb_d_ref)
    pi = lax.broadcasted_iota(jnp.int32, (LANES, LANES), 0)
    pj = lax.broadcasted_iota(jnp.int32, (LANES, LANES), 1)
    ones_bd = ((pi // RW_HEAD_DIM) == (pj // RW_HEAD_DIM)).astype(BF16)

    def head_mean(v):
        vh, vl = _split(v)
        return (_dot(vh, ones_bd) + _dot(vl, ones_bd)) * (1.0 / RW_HEAD_DIM)

    outs = []
    for p in range(RW_HEADS // 2):
        yp = y[:, p * LANES:(p + 1) * LANES]
        yc = yp - head_mean(yp)
        outs.append(yc * lax.rsqrt(head_mean(yc * yc) + GN_EPS))
    o_rw = (jnp.concatenate(outs, axis=-1) * lg_ref[...] + lb_ref[...]) * g_rw
    merged = (gt_ref[:, :D_MODEL].astype(F32) * _dot(oa, wa_ref[...])
              + gt_ref[:, D_MODEL:2 * D_MODEL].astype(F32) * _mm(o_rw, wr_ref[...])
              + gt_ref[:, 2 * D_MODEL:].astype(F32) * _dot(oc_ref[...], wc_ref[...]))
    mix = _mm(merged, wo_ref[...])
    x1 = x_ref[...] + mod_ref[0, 2:3, :] * _rms(mix, ng_ref[1:2, :])
    x1_ref[...] = x1
    hm = _rms(x1, ng_ref[2:3, :]) * (1.0 + mod_ref[0, 4:5, :]) + mod_ref[0, 3:4, :]
    hm_ref[...] = hm.astype(BF16)
    logits = _mm3(hm, rw_ref[...]) + rb_ref[...]
    lane = lax.broadcasted_iota(jnp.int32, logits.shape, 1)
    idx_out = jnp.zeros(logits.shape, jnp.int32)
    val_out = jnp.full(logits.shape, NEG_INF, F32)
    picks = []
    for j in range(TOP_K):
        m = jnp.max(logits, axis=-1, keepdims=True)
        idx = jnp.min(jnp.where(logits == m, lane, LANES), axis=-1, keepdims=True)
        idx_out = jnp.where(lane == j, idx, idx_out)
        val_out = jnp.where(lane == j, m, val_out)
        picks.append(lane == idx)
        logits = jnp.where(picks[-1], -jnp.inf, logits)
    e = jnp.exp(val_out - jnp.max(val_out, axis=-1, keepdims=True))
    ti_ref[...] = idx_out
    tg_ref[...] = e / jnp.sum(e, axis=-1, keepdims=True)

    chosen = jnp.zeros(logits.shape, F32)
    for pk in picks:
        chosen = chosen + pk.astype(F32)
    ti_ = lax.broadcasted_iota(jnp.int32, (TM, TM), 0)
    tj_ = lax.broadcasted_iota(jnp.int32, (TM, TM), 1)
    before = _dot((ti_ > tj_).astype(BF16), chosen.astype(BF16))
    rank = jnp.zeros(logits.shape, F32)
    for j, pk in enumerate(picks):
        rank = jnp.where(lane == j, jnp.sum(jnp.where(pk, before, 0.0), axis=-1, keepdims=True), rank)
    rk_ref[...] = rank.astype(jnp.int32)
    cnt_ref[0] = jnp.sum(chosen, axis=0, keepdims=True)


def merge_router(lay, x, ctx_br, dec_br, o_cv, gates, mod, ng, lg, lb, wa, wr, wc, wo, rw, rb):
    n = lay.n
    row = lambda w: pl.BlockSpec((TM, w), lambda i: (i, 0))
    crow = lambda w: pl.BlockSpec((TM, w), lambda i: (jnp.minimum(i, lay.ctx_tiles - 1), 0))
    drow = lambda w: pl.BlockSpec((TM, w), lambda i: (jnp.maximum(i - lay.ctx_tiles, 0), 0))
    full = lambda a: pl.BlockSpec(a.shape, lambda i: (0,) * a.ndim)
    br_w = (NA_WIDTH, RW_WIDTH, RW_WIDTH, RW_WIDTH)
    return pl.pallas_call(
        functools.partial(_merge_kernel, ctx_tiles=lay.ctx_tiles),
        out_shape=(jax.ShapeDtypeStruct((n, D_MODEL), F32), jax.ShapeDtypeStruct((n, D_MODEL), BF16),
                   jax.ShapeDtypeStruct((n, LANES), jnp.int32), jax.ShapeDtypeStruct((n, LANES), F32),
                   jax.ShapeDtypeStruct((n, LANES), jnp.int32), jax.ShapeDtypeStruct((lay.tiles, 1, LANES), F32)),
        grid=(lay.tiles,),
        in_specs=[row(D_MODEL)] + [crow(w) for w in br_w] + [drow(w) for w in br_w] + [
                  row(CONV_WIDTH),
                  row(G_COLS), pl.BlockSpec((1, N_MOD, D_MODEL), lambda i: (lay.mod_row(i), 0, 0)),
                  full(ng), full(lg), full(lb), full(wa), full(wr), full(wc), full(wo), full(rw), full(rb)],
        out_specs=(row(D_MODEL), row(D_MODEL), row(LANES), row(LANES), row(LANES),
                   pl.BlockSpec((1, 1, LANES), lambda i: (i, 0, 0))),
        compiler_params=_cparams(("arbitrary",)),
        name="merge_router",
    )(x, *ctx_br, *dec_br, o_cv, gates, mod, ng, lg, lb, wa, wr, wc, wo, rw, rb)


RUN_ALIGN = 8
RUN_BITS = (TM // RUN_ALIGN).bit_length()
SORT_ROWS = -(-(TM * TOP_K + N_EXPERTS * (RUN_ALIGN - 1)) // LANES) * LANES


def route_tables(top_i, wrank, cnt, n):
    tiles = n // TM
    cnt = cnt[:, 0, :N_EXPERTS].astype(jnp.int32)
    run = ((cnt + RUN_ALIGN - 1) // RUN_ALIGN) * RUN_ALIGN
    off = jnp.cumsum(run, axis=1) - run
    before = jnp.cumsum(run, axis=0) - run
    region = jnp.sum(run, axis=0)
    padded = ((region + ROUTE_BLOCK - 1) // ROUTE_BLOCK) * ROUTE_BLOCK
    pad_end = jnp.cumsum(padded)
    first = (pad_end - padded)[None, :] + before
    table = jnp.concatenate([off, first, run, jnp.zeros((tiles, LANES - 3 * N_EXPERTS), jnp.int32)], axis=1)
    onehot = top_i.reshape(tiles, TM, TOP_K, 1) == jnp.arange(N_EXPERTS, dtype=jnp.int32)
    pos = jnp.sum(jnp.where(onehot, off[:, None, None, :], 0), axis=-1) + wrank.reshape(tiles, TM, TOP_K)
    n_blocks = -(-tiles * (TM * TOP_K + N_EXPERTS * (RUN_ALIGN - 1)) // ROUTE_BLOCK) + N_EXPERTS
    starts = jnp.arange(n_blocks, dtype=jnp.int32) * ROUTE_BLOCK
    block_e = jnp.minimum(jnp.sum((pad_end[None, :] <= starts[:, None]).astype(jnp.int32), axis=1),
                          N_EXPERTS - 1)
    n_used = (pad_end[-1] // ROUTE_BLOCK).astype(jnp.int32).reshape(1)
    return table.reshape(tiles, 1, LANES), pos, block_e, n_used, n_blocks


def _for_each_piece(tab_ref, fn):
    def body(e, carry):
        off = tab_ref[0, 0, e]
        first = tab_ref[0, 0, N_EXPERTS + e]
        run = tab_ref[0, 0, 2 * N_EXPERTS + e]
        for k in range(RUN_BITS):
            size = RUN_ALIGN << k

            @pl.when((run & size) != 0)
            def _():
                done = run & (-2 * size)
                fn(pl.multiple_of(off + done, RUN_ALIGN), pl.multiple_of(first + done, RUN_ALIGN), size)
        return carry

    lax.fori_loop(0, N_EXPERTS, body, 0)


def _dispatch_kernel(tab_ref, tabp_ref, pos_ref, hm_ref, xb_in_ref, xb_ref, srt_ref, sems):
    del xb_in_ref
    i = pl.program_id(0)
    slot = i % 2

    def copy(slot_, off, row, size):
        return pltpu.make_async_copy(srt_ref.at[slot_, pl.ds(off, size), :], xb_ref.at[pl.ds(row, size), :],
                                     sems.at[slot_])

    p_iota = lax.broadcasted_iota(jnp.int32, (SORT_ROWS, TM), 0)
    hit = p_iota == pos_ref[0, 0:1, :]
    for j in range(1, TOP_K):
        hit = hit | (p_iota == pos_ref[0, j:j + 1, :])
    srt_ref[slot] = _dot(jnp.where(hit, 1.0, 0.0).astype(BF16), hm_ref[...])
    _for_each_piece(tab_ref, lambda off, row, size: copy(slot, off, row, size).start())

    @pl.when(i > 0)
    def _():
        _for_each_piece(tabp_ref, lambda off, row, size: copy(1 - slot, off, row, size).wait())

    @pl.when(i == pl.num_programs(0) - 1)
    def _():
        _for_each_piece(tab_ref, lambda off, row, size: copy(slot, off, row, size).wait())


def moe_dispatch(lay, hm, table, pos, xb0):
    n_rows = xb0.shape[0]
    post = jnp.pad(pos.transpose(0, 2, 1), ((0, 0), (0, 8 - TOP_K), (0, 0)))
    tab = lambda f: pl.BlockSpec((1, 1, LANES), lambda i: (f(i), 0, 0), memory_space=pltpu.SMEM)
    return pl.pallas_call(
        _dispatch_kernel,
        out_shape=jax.ShapeDtypeStruct((n_rows, D_MODEL), F32),
        grid=(lay.tiles,),
        in_specs=[tab(lambda i: i), tab(lambda i: jnp.maximum(i - 1, 0)),
                  pl.BlockSpec((1, 8, TM), lambda i: (i, 0, 0)),
                  pl.BlockSpec((TM, D_MODEL), lambda i: (i, 0)),
                  pl.BlockSpec(memory_space=pl.ANY)],
        out_specs=pl.BlockSpec(memory_space=pl.ANY),
        scratch_shapes=[pltpu.VMEM((2, SORT_ROWS, D_MODEL), F32), pltpu.SemaphoreType.DMA((2,))],
        input_output_aliases={4: 0},
        compiler_params=_cparams(("arbitrary",)),
        name="moe_dispatch",
    )(table, table, post, hm, xb0)


GL_GROUP = 2 * LANES


def _regroup_perm():
    r = jnp.arange(GL_GROUP, dtype=jnp.int32)[:, None]
    c = jnp.arange(GL_GROUP, dtype=jnp.int32)[None, :]
    src = jnp.where(c < LANES, 2 * c, 2 * (c - LANES) + 1)
    return (r == src).astype(BF16)


def _expert_kernel(be_ref, nu_ref, x_ref, w1_ref, perm_ref, b1_ref, w2_ref, b2_ref, y_ref, w1b_ref, w2b_ref):
    b = pl.program_id(0)
    used = b < nu_ref[0]

    @pl.when(used & ((b == 0) | (be_ref[b] != be_ref[jnp.maximum(b - 1, 0)])))
    def _():
        for j in range(2 * EXPERT_FF // GL_GROUP):
            cs = slice(j * GL_GROUP, (j + 1) * GL_GROUP)
            w1b_ref[:, cs] = _dot(w1_ref[:, cs].astype(BF16), perm_ref[...]).astype(BF16)
        w2b_ref[...] = w2_ref[...].astype(BF16)

    @pl.when(used)
    def _():
        x = x_ref[...].astype(BF16)
        n_grp = 2 * EXPERT_FF // GL_GROUP
        hs = [_dot(x, w1b_ref[:, g * GL_GROUP:(g + 1) * GL_GROUP]) + b1_ref[0, :, g * GL_GROUP:(g + 1) * GL_GROUP]
              for g in range(n_grp)]
        acts = []
        for h in hs:
            hg = jnp.minimum(h[:, :LANES], SWIGLU_LIMIT)
            hl = jnp.clip(h[:, LANES:], -SWIGLU_LIMIT, SWIGLU_LIMIT)
            acts.append((hg * _sigmoid(SWIGLU_ALPHA * hg) * (hl + 1.0)).astype(BF16))
        acc = jnp.zeros((ROUTE_BLOCK, D_MODEL), F32) + b2_ref[0]
        for j in range(n_grp // 2):
            act = jnp.concatenate(acts[2 * j:2 * j + 2], axis=-1)
            acc = acc + _dot(act, w2b_ref[j * GL_GROUP:(j + 1) * GL_GROUP, :])
        y_ref[...] = acc

    @pl.when(jnp.logical_not(used))
    def _():
        y_ref[...] = jnp.zeros(y_ref.shape, F32)


def moe_experts(xb, block_e, n_used, n_blocks, l, w1, b1, w2, b2):
    def last_used(b, nu):
        return jnp.minimum(b, jnp.maximum(nu[0] - 1, 0))

    wspec = lambda a: pl.BlockSpec((None, None) + a.shape[2:], lambda b, be, nu: (l, be[last_used(b, nu)], 0, 0))
    bspec = lambda a: pl.BlockSpec((1,) + a.shape[1:], lambda b, be, nu: (be[last_used(b, nu)], 0, 0))
    return pl.pallas_call(
        _expert_kernel,
        out_shape=jax.ShapeDtypeStruct(xb.shape, F32),
        grid_spec=pltpu.PrefetchScalarGridSpec(
            num_scalar_prefetch=2,
            grid=(n_blocks,),
            in_specs=[pl.BlockSpec((ROUTE_BLOCK, D_MODEL), lambda b, be, nu: (last_used(b, nu), 0)),
                      wspec(w1), pl.BlockSpec((GL_GROUP, GL_GROUP), lambda b, be, nu: (0, 0)),
                      bspec(b1), wspec(w2), bspec(b2)],
            out_specs=pl.BlockSpec((ROUTE_BLOCK, D_MODEL), lambda b, be, nu: (b, 0)),
            scratch_shapes=[pltpu.VMEM(w1.shape[2:], BF16), pltpu.VMEM(w2.shape[2:], BF16)]),
        compiler_params=_cparams(("arbitrary",)),
        name="moe_experts",
    )(block_e, n_used, xb, w1, _regroup_perm(), b1, w2, b2)


def _combine_kernel(tab_ref, tabn_ref, yb_ref, pos_ref, tg_ref, x1_ref, mod_ref, ng_ref, *rest, ctx_tiles):
    o_refs, (stg_ref, sems) = rest[:-2], rest[-2:]
    i = pl.program_id(0)
    last = pl.num_programs(0) - 1
    slot = i % 2

    def copy(slot_, off, row, size):
        return pltpu.make_async_copy(yb_ref.at[pl.ds(row, size), :], stg_ref.at[slot_, pl.ds(off, size), :],
                                     sems.at[slot_])

    @pl.when(i == 0)
    def _():
        stg_ref[...] = jnp.zeros(stg_ref.shape, F32)
        _for_each_piece(tab_ref, lambda off, row, size: copy(slot, off, row, size).start())

    @pl.when(i < last)
    def _():
        _for_each_piece(tabn_ref, lambda off, row, size: copy(1 - slot, off, row, size).start())

    _for_each_piece(tab_ref, lambda off, row, size: copy(slot, off, row, size).wait())
    p_iota = lax.broadcasted_iota(jnp.int32, (TM, SORT_ROWS), 1)
    w = jnp.zeros((TM, SORT_ROWS), F32)
    for j in range(TOP_K):
        w = w + jnp.where(p_iota == pos_ref[:, j:j + 1], tg_ref[:, j:j + 1], 0.0)
    y = _mm3(w, stg_ref[slot])
    out = x1_ref[...] + mod_ref[0, 5:6, :] * _rms(y, ng_ref[3:4, :])
    if len(o_refs) == 1:
        o_refs[0][...] = out
    else:
        @pl.when(i < ctx_tiles)
        def _():
            o_refs[0][...] = out

        o_refs[1][...] = out


def moe_combine(lay, yb, table, pos, tg, x1, mod, ng, split):
    n = lay.n
    posl = jnp.pad(pos.reshape(n, TOP_K), ((0, 0), (0, LANES - TOP_K)))
    row = lambda w: pl.BlockSpec((TM, w), lambda i: (i, 0))
    tab = lambda f: pl.BlockSpec((1, 1, LANES), lambda i: (f(i), 0, 0), memory_space=pltpu.SMEM)
    if split:
        out_shape = (jax.ShapeDtypeStruct((lay.ctx_tokens, D_MODEL), F32),
                     jax.ShapeDtypeStruct((lay.dec_tokens, D_MODEL), F32))
        out_specs = (pl.BlockSpec((TM, D_MODEL), lambda i: (jnp.minimum(i, lay.ctx_tiles - 1), 0)),
                     pl.BlockSpec((TM, D_MODEL), lambda i: (jnp.maximum(i - lay.ctx_tiles, 0), 0)))
    else:
        out_shape = jax.ShapeDtypeStruct((n, D_MODEL), F32)
        out_specs = row(D_MODEL)
    return pl.pallas_call(
        functools.partial(_combine_kernel, ctx_tiles=lay.ctx_tiles),
        out_shape=out_shape,
        grid=(lay.tiles,),
        in_specs=[tab(lambda i: i), tab(lambda i: jnp.minimum(i + 1, lay.tiles - 1)),
                  pl.BlockSpec(memory_space=pl.ANY),
                  row(LANES), row(LANES), row(D_MODEL),
                  pl.BlockSpec((1, N_MOD, D_MODEL), lambda i: (lay.mod_row(i), 0, 0)),
                  pl.BlockSpec(ng.shape, lambda i: (0, 0))],
        out_specs=out_specs,
        scratch_shapes=[pltpu.VMEM((2, SORT_ROWS, D_MODEL), F32), pltpu.SemaphoreType.DMA((2,))],
        compiler_params=_cparams(("arbitrary",)),
        name="moe_combine",
    )(table, table, yb, posl, tg, x1, mod, ng)


def layer(lay, x, l, cond, P, ctx_k, ctx_v, s0_dec, xb_buf):
    mod = mod_table(cond, P['mod_w'][l], P['mod_b'][l]).reshape(cond.shape[0], N_MOD, D_MODEL)
    ng = P['norm_g'][l]
    q, kf, vf, kb, vb, zb, zc, gates = in_proj(lay, x, mod, ng[0:1], P['w_in'][l].astype(BF16),
                                               P['b_in'][l].reshape(1, P_IN))
    oa_c = ctx_attention(lay, q, kb, vb)
    vec = lambda a: a.reshape(1, -1)
    prm = (P['rw_mu'][l], P['rw_w0'][l], P['rw_w2'][l], P['rw_a0'][l], P['rw_a2'][l],
           vec(P['rw_k_k'][l]), vec(P['rw_k_a'][l]), vec(P['rw_r_k'][l]))
    yf_c, yb_c, g_c, s_ctx = rwkv_scan(zb.reshape(-1, lay.ctx_len, B_COLS), 0, lay.n_ctx, None, prm,
                                       P['rw_g2'][l])
    na = (lay, q, kb, vb, ctx_k.astype(BF16), ctx_v.astype(BF16), na_bias_table(P['rpb'][l]))
    yf_d, yb_d, g_d, _, oa_d = rwkv_scan(zb.reshape(-1, lay.dec_len, B_COLS), lay.ctx_tokens // lay.dec_len,
                                         lay.n_dec, s0_dec, prm, P['rw_g2'][l], na=na)
    o_cv = conv_module(lay, zc, P['conv_w'][l], vec(P['conv_b'][l]), vec(P['conv_ln_g'][l]),
                       vec(P['conv_ln_b'][l]))
    rw = jnp.zeros((D_MODEL, LANES), F32).at[:, :N_EXPERTS].set(P['router_w'][l])
    rb = jnp.full((1, LANES), NEG_INF, F32).at[0, :N_EXPERTS].set(P['router_b'][l])
    x1, hm, top_i, top_g, rank, counts = merge_router(
        lay, x, (oa_c, yf_c, yb_c, g_c), (oa_d, yf_d, yb_d, g_d), o_cv, gates, mod, ng,
        vec(P['rw_lnx_g'][l]), vec(P['rw_lnx_b'][l]),
        P['w_o_attn'][l].astype(BF16), P['w_o_rwkv'][l].astype(BF16), P['w_o_conv'][l].astype(BF16),
        P['w_out'][l].astype(BF16), rw, rb)
    table, pos, block_e, n_used, n_blocks = route_tables(top_i[:, :TOP_K], rank[:, :TOP_K], counts, lay.n)
    if xb_buf is None:
        xb_buf = jnp.zeros((n_blocks * ROUTE_BLOCK, D_MODEL), F32)
    xb = moe_dispatch(lay, hm, table, pos, xb_buf)
    b1 = P['exp_b1'][l].reshape(N_EXPERTS, -1, LANES, 2).transpose(0, 1, 3, 2).reshape(N_EXPERTS, 1, -1)
    ybk = moe_experts(xb, block_e, n_used, n_blocks, l, P['exp_w1'], b1, P['exp_w2'], P['exp_b2'][l][:, None, :])
    x2 = moe_combine(lay, ybk, table, pos, top_g, x1, mod, ng, split=(l == DEPTH - 1))
    return x2, kf, vf, s_ctx, xb


def kernel(x_prompt, x_sample, cache_attn_k, cache_attn_v, state_rwkv, c, c_ctx, mod_w, mod_b, norm_g, w_in, b_in, rpb, w_o_attn, rw_mu, rw_w0, rw_w2, rw_a0, rw_a2, rw_g2, rw_k_k, rw_k_a, rw_r_k, rw_lnx_g, rw_lnx_b, w_o_rwkv, conv_w, conv_b, conv_ln_g, conv_ln_b, w_o_conv, w_out, router_w, router_b, exp_w1, exp_b1, exp_w2, exp_b2):
    P = {
        'mod_w': mod_w, 'mod_b': mod_b, 'norm_g': norm_g, 'w_in': w_in, 'b_in': b_in, 'rpb': rpb,
        'w_o_attn': w_o_attn, 'rw_mu': rw_mu, 'rw_w0': rw_w0, 'rw_w2': rw_w2, 'rw_a0': rw_a0,
        'rw_a2': rw_a2, 'rw_g2': rw_g2, 'rw_k_k': rw_k_k, 'rw_k_a': rw_k_a, 'rw_r_k': rw_r_k,
        'rw_lnx_g': rw_lnx_g, 'rw_lnx_b': rw_lnx_b, 'w_o_rwkv': w_o_rwkv, 'conv_w': conv_w,
        'conv_b': conv_b, 'conv_ln_g': conv_ln_g, 'conv_ln_b': conv_ln_b, 'w_o_conv': w_o_conv,
        'w_out': w_out, 'router_w': router_w, 'router_b': router_b, 'exp_w1': exp_w1,
        'exp_b1': exp_b1, 'exp_w2': exp_w2, 'exp_b2': exp_b2,
    }
    nb, sl, _ = x_prompt.shape
    db, ds, _ = x_sample.shape
    lay = Layout(nb, sl, db, ds)
    x = jnp.concatenate([x_prompt.reshape(-1, D_MODEL), x_sample.reshape(-1, D_MODEL)], axis=0)
    cond = jnp.zeros((8, D_MODEL), F32).at[0].set(c_ctx).at[1:1 + db].set(c)
    ks, vs, ss = [], [], []
    xb_buf = None
    for l in range(DEPTH):
        ck = cache_attn_k[:, l].reshape(db, -1, NA_WIDTH)
        cv = cache_attn_v[:, l].reshape(db, -1, NA_WIDTH)
        x, kf, vf, s_ctx, xb_buf = layer(lay, x, l, cond, P, ck, cv, state_rwkv[:, l], xb_buf)
        ks.append(kf.reshape(nb, sl, NA_HEADS, NA_HEAD_DIM))
        vs.append(vf.reshape(nb, sl, NA_HEADS, NA_HEAD_DIM))
        ss.append(s_ctx)
    y_prompt = x[0].reshape(nb, sl, D_MODEL)
    y_sample = x[1].reshape(db, ds, D_MODEL)
    return (y_prompt, y_sample, jnp.stack(ks, axis=1), jnp.stack(vs, axis=1), jnp.stack(ss, axis=1))
```

```python
import functools

import jax
import jax.numpy as jnp
from jax import lax
from jax.experimental import pallas as pl
from jax.experimental.pallas import tpu as pltpu

F32 = jnp.float32
BF16 = jnp.bfloat16

D_MODEL = 1024
BATCH, SEQ = 32, 256
DEPTH = 2
DEC_BATCH, DEC_SEQ = 4, 4096
PAST_LEN = 256
GRID_W = 64
NA_HEADS, NA_HEAD_DIM = 8, 64
NA_WIDTH = NA_HEADS * NA_HEAD_DIM
NA_WIN_ROWS, NA_WIN_COLS = 8, 16
RW_HEADS, RW_HEAD_DIM = 8, 64
RW_WIDTH = RW_HEADS * RW_HEAD_DIM
RW_DECAY_RANK, RW_ICLR_RANK, RW_GATE_RANK = 64, 64, 128
CONV_WIDTH, CONV_K = 512, 31
N_BRANCH, N_MOD = 3, 6
N_EXPERTS, TOP_K, EXPERT_FF = 32, 4, 1024
SWIGLU_LIMIT, SWIGLU_ALPHA = 7.0, 1.702
ROUTE_BLOCK = 512
RMS_EPS, LN_EPS, GN_EPS = 1e-6, 1e-5, 64e-5
NEG_INF = -1e30
A_COLS = 3 * NA_WIDTH
B_COLS = 3 * RW_WIDTH + 2 * RW_DECAY_RANK + 2 * RW_ICLR_RANK + RW_GATE_RANK
C_COLS = 2 * CONV_WIDTH
G_COLS = N_BRANCH * D_MODEL
P_IN = A_COLS + B_COLS + C_COLS + G_COLS

LANES = 128
TM = 256
CHUNK = 64
CONV_HALO = 16
VMEM_LIMIT = 56 * 1024 * 1024


def _cparams(sem):
    return pltpu.CompilerParams(dimension_semantics=sem, vmem_limit_bytes=VMEM_LIMIT)


def _dot(a, b, dims=((1,), (0,))):
    return lax.dot_general(a, b, (dims, ((), ())), preferred_element_type=F32)


def _mm(a, b, dims=((1,), (0,))):
    return _dot(a.astype(BF16), b.astype(BF16), dims)


def _split(a):
    hi = a.astype(BF16)
    lo = (a - hi.astype(F32)).astype(BF16)
    return hi, lo


def _mm3(a, b, dims=((1,), (0,))):
    ah, al = _split(a)
    bh, bl = _split(b)
    return _dot(ah, bh, dims) + (_dot(ah, bl, dims) + _dot(al, bh, dims))


def _mm4(a, b):
    m, n = a.shape[0], b.shape[1]
    out = _dot(jnp.concatenate(_split(a), axis=0), jnp.concatenate(_split(b), axis=1))
    return (out[:m, :n] + out[:m, n:]) + (out[m:, :n] + out[m:, n:])


_NT = ((1,), (1,))


def _rms(x, g):
    return x * lax.rsqrt(jnp.mean(x * x, axis=-1, keepdims=True) + RMS_EPS) * g


def _sigmoid(x):
    return 1.0 / (1.0 + jnp.exp(-x))


class Layout:
    def __init__(self, n_ctx, ctx_len, n_dec, dec_len):
        assert ctx_len == TM and dec_len % TM == 0
        self.n_ctx, self.ctx_len, self.n_dec, self.dec_len = n_ctx, ctx_len, n_dec, dec_len
        self.ctx_tokens = n_ctx * ctx_len
        self.dec_tokens = n_dec * dec_len
        self.n = self.ctx_tokens + self.dec_tokens
        self.ctx_tiles = self.ctx_tokens // TM
        self.dec_tiles_per_seq = dec_len // TM
        self.tiles = self.n // TM

    def mod_row(self, i):
        return jnp.where(i < self.ctx_tiles, 0, 1 + (i - self.ctx_tiles) // self.dec_tiles_per_seq)

    def tile_pos(self, i):
        j = (i - self.ctx_tiles) % self.dec_tiles_per_seq
        is_ctx = i < self.ctx_tiles
        return is_ctx | (j == 0), is_ctx | (j == self.dec_tiles_per_seq - 1)


def _mod_kernel(c_ref, w_ref, b_ref, o_ref):
    c = c_ref[...]
    o_ref[...] = _mm3(c * _sigmoid(c), w_ref[...]) + b_ref[...]


def mod_table(cond, w, b):
    tn = 1024
    nm = w.shape[1]
    return pl.pallas_call(
        _mod_kernel,
        out_shape=jax.ShapeDtypeStruct((cond.shape[0], nm), F32),
        grid=(nm // tn,),
        in_specs=[pl.BlockSpec(cond.shape, lambda j: (0, 0)),
                  pl.BlockSpec((D_MODEL, tn), lambda j: (0, j)),
                  pl.BlockSpec((1, tn), lambda j: (0, j))],
        out_specs=pl.BlockSpec((cond.shape[0], tn), lambda j: (0, j)),
        compiler_params=_cparams(("arbitrary",)),
        name="mod_table",
    )(cond, w, b.reshape(1, nm))


def _inproj_kernel(x_ref, mod_ref, g_ref, w_ref, b_ref,
                   q_ref, kf_ref, vf_ref, kb_ref, vb_ref, zb_ref, zc_ref, gt_ref, *, ctx_tiles):
    u = (_rms(x_ref[...], g_ref[...]) * (1.0 + mod_ref[0, 1:2, :]) + mod_ref[0, 0:1, :]).astype(BF16)

    def seg(c0, width):
        return _dot(u, w_ref[:, c0:c0 + width]) + b_ref[:, c0:c0 + width]

    q_ref[...] = seg(0, NA_WIDTH).astype(BF16)
    k = seg(NA_WIDTH, NA_WIDTH)
    kb_ref[...] = k.astype(BF16)
    v = seg(2 * NA_WIDTH, NA_WIDTH)
    vb_ref[...] = v.astype(BF16)

    @pl.when(pl.program_id(0) < ctx_tiles)
    def _():
        kf_ref[...] = k
        vf_ref[...] = v
    wb = 384
    for j in range(B_COLS // wb):
        zb_ref[:, j * wb:(j + 1) * wb] = seg(A_COLS + j * wb, wb)
    wc = 512
    for j in range(C_COLS // wc):
        zc_ref[:, j * wc:(j + 1) * wc] = seg(A_COLS + B_COLS + j * wc, wc)
    for j in range(G_COLS // wc):
        gt_ref[:, j * wc:(j + 1) * wc] = _sigmoid(seg(A_COLS + B_COLS + C_COLS + j * wc, wc)).astype(BF16)


def in_proj(lay, x, mod, g0, w_bf, b):
    n = lay.n
    row = lambda w: pl.BlockSpec((TM, w), lambda i: (i, 0))
    crow = pl.BlockSpec((TM, NA_WIDTH), lambda i: (jnp.minimum(i, lay.ctx_tiles - 1), 0))
    ctx_f32 = jax.ShapeDtypeStruct((lay.ctx_tokens, NA_WIDTH), F32)
    out_shape = (jax.ShapeDtypeStruct((n, NA_WIDTH), BF16), ctx_f32, ctx_f32,
                 jax.ShapeDtypeStruct((n, NA_WIDTH), BF16), jax.ShapeDtypeStruct((n, NA_WIDTH), BF16),
                 jax.ShapeDtypeStruct((n, B_COLS), F32), jax.ShapeDtypeStruct((n, C_COLS), F32),
                 jax.ShapeDtypeStruct((n, G_COLS), BF16))
    return pl.pallas_call(
        functools.partial(_inproj_kernel, ctx_tiles=lay.ctx_tiles),
        out_shape=out_shape,
        grid=(lay.tiles,),
        in_specs=[row(D_MODEL),
                  pl.BlockSpec((1, N_MOD, D_MODEL), lambda i: (lay.mod_row(i), 0, 0)),
                  pl.BlockSpec((1, D_MODEL), lambda i: (0, 0)),
                  pl.BlockSpec((D_MODEL, P_IN), lambda i: (0, 0), pipeline_mode=pl.Buffered(1)),
                  pl.BlockSpec((1, P_IN), lambda i: (0, 0))],
        out_specs=(row(NA_WIDTH), crow, crow, row(NA_WIDTH), row(NA_WIDTH),
                   row(B_COLS), row(C_COLS), row(G_COLS)),
        compiler_params=_cparams(("arbitrary",)),
        name="in_proj",
    )(x, mod, g0, w_bf, b)


def _ctx_attn_kernel(q_ref, k_ref, v_ref, o_ref):
    scale = NA_HEAD_DIM ** -0.5
    lo = lax.broadcasted_iota(jnp.int32, (q_ref.shape[0], LANES), 1) < NA_HEAD_DIM
    heads = []
    for h in range(NA_HEADS):
        ps = slice((h // 2) * LANES, (h // 2 + 1) * LANES)
        qp = q_ref[:, ps] * scale
        qh = jnp.where(lo if h % 2 == 0 else jnp.logical_not(lo), qp, jnp.zeros_like(qp))
        heads.append(dict(ps=ps, s=_dot(qh, k_ref[:, ps], _NT)))
    for u in heads:
        u['p'] = jnp.exp(u['s'] - jnp.max(u['s'], axis=-1, keepdims=True))
    for u in heads:
        u['o'] = _dot(u['p'].astype(BF16), v_ref[:, u['ps']]) / jnp.sum(u['p'], axis=-1, keepdims=True)
    for p in range(NA_HEADS // 2):
        o_ref[:, p * LANES:(p + 1) * LANES] = jnp.where(lo, heads[2 * p]['o'], heads[2 * p + 1]['o']).astype(BF16)


def ctx_attention(lay, q, kb, vb):
    blk = pl.BlockSpec((lay.ctx_len, NA_WIDTH), lambda b: (b, 0))
    return pl.pallas_call(
        _ctx_attn_kernel,
        out_shape=jax.ShapeDtypeStruct((lay.ctx_tokens, NA_WIDTH), BF16),
        grid=(lay.n_ctx,),
        in_specs=[blk, blk, blk],
        out_specs=blk,
        compiler_params=_cparams(("arbitrary",)),
        name="ctx_attention",
    )(q, kb, vb)


def na_bias_table(rpb):
    kw, kh = NA_WIN_COLS, NA_WIN_ROWS
    col = jnp.arange(GRID_W, dtype=jnp.int32)
    col_start = jnp.clip(col - kw // 2, 0, GRID_W - kw)
    col_ok = (col[None, :] >= col_start[:, None]) & (col[None, :] < col_start[:, None] + kw)
    dc = jnp.clip(col[None, :] - col[:, None] + kw - 1, 0, 2 * kw - 2)
    bias_c = jnp.where(col_ok, rpb[:, :, dc].astype(F32), NEG_INF)
    tabs = [jnp.concatenate([bias_c[:, d + i] for i in range(kh)], axis=-1) for d in range(kh)]
    return jnp.stack(tabs, axis=0)


def _na_row_start(r, rows):
    return jnp.clip(r - NA_WIN_ROWS // 2, 0, rows - NA_WIN_ROWS)


def _na_row_stages(q_ref, k_ref, v_ref, ck_ref, cv_ref, bias_ref, o_ref, r, rows):
    rs = _na_row_start(r, rows)
    n_loc = NA_WIN_ROWS * GRID_W
    start = pl.multiple_of(rs * GRID_W, GRID_W)
    scale = NA_HEAD_DIM ** -0.5
    lo = lax.broadcasted_iota(jnp.int32, (GRID_W, LANES), 1) < NA_HEAD_DIM
    pairs = [slice(p * LANES, (p + 1) * LANES) for p in range(NA_HEADS // 2)]
    keys = [jnp.concatenate([k_ref[pl.ds(start, n_loc), ps], ck_ref[0, :, ps]], axis=0) for ps in pairs]
    vals = [jnp.concatenate([v_ref[pl.ds(start, n_loc), ps], cv_ref[0, :, ps]], axis=0) for ps in pairs]
    heads = []
    for h in range(NA_HEADS):
        qp = q_ref[:, pairs[h // 2]] * scale
        qh = jnp.where(lo if h % 2 == 0 else jnp.logical_not(lo), qp, jnp.zeros_like(qp))
        s = _dot(qh, keys[h // 2], _NT)
        heads.append(dict(vals=vals[h // 2], s_loc=s[:, :n_loc] + bias_ref[0, h], s_ctx=s[:, n_loc:]))
    yield
    for u in heads:
        u['m'] = jnp.maximum(jnp.max(u['s_loc'], axis=-1, keepdims=True),
                             jnp.max(u['s_ctx'], axis=-1, keepdims=True))
    yield
    for u in heads:
        u['p_loc'] = jnp.exp(u['s_loc'] - u['m'])
        u['p_ctx'] = jnp.exp(u['s_ctx'] - u['m'])
        u['den'] = jnp.sum(u['p_loc'], axis=-1, keepdims=True) + jnp.sum(u['p_ctx'], axis=-1, keepdims=True)
    yield
    for u in heads:
        p = jnp.concatenate([u['p_loc'].astype(BF16), u['p_ctx'].astype(BF16)], axis=1)
        u['o'] = _dot(p, u['vals']) / u['den']
    yield
    for p in range(NA_HEADS // 2):
        o_ref[:, p * LANES:(p + 1) * LANES] = jnp.where(lo, heads[2 * p]['o'], heads[2 * p + 1]['o']).astype(BF16)


def _softplus(x):
    return jnp.maximum(x, 0.0) + jnp.log(1.0 + jnp.exp(-jnp.abs(x)))


def _rwkv_prepare(d, z, prev_row, next_row, prm):
    (mu_ref, w0_ref, w2_ref, a0_ref, a2_ref, kk_ref, ka_ref, rk_ref) = prm
    c = CHUNK
    cw = RW_WIDTH
    row = lax.broadcasted_iota(jnp.int32, (c, 1), 0)
    prev = jnp.where(row == 0, prev_row, pltpu.roll(z, 1, axis=0))
    nxt = jnp.where(row == c - 1, next_row, pltpu.roll(z, c - 1, axis=0))
    zs = z + mu_ref[0:1, :] * (prev - z) + mu_ref[1:2, :] * (nxt - z)
    r, k, v = zs[:, :cw], zs[:, cw:2 * cw], zs[:, 2 * cw:3 * cw]
    o = 3 * cw
    zw = zs[:, o + d * RW_DECAY_RANK:o + (d + 1) * RW_DECAY_RANK]
    o += 2 * RW_DECAY_RANK
    za = zs[:, o + d * RW_ICLR_RANK:o + (d + 1) * RW_ICLR_RANK]

    w_log = -_softplus(-(w0_ref[d:d + 1, :] + _mm4(jnp.tanh(zw), w2_ref[d]))) - 0.5
    logw = -jnp.exp(w_log)
    a = _sigmoid(a0_ref[d:d + 1, :] + _mm4(za, a2_ref[d]))
    kd = k * (1.0 + (a - 1.0) * ka_ref[...])
    kk = k * kk_ref[...]

    ti = lax.broadcasted_iota(jnp.int32, (c, c), 0)
    tj = lax.broadcasted_iota(jnp.int32, (c, c), 1)
    tri = ((ti >= tj) if d == 0 else (tj >= ti)).astype(BF16)
    lh, ll = _split(logw)
    cl2 = _dot(tri, jnp.concatenate([lh, ll], axis=1))
    cl = cl2[:, :cw] + cl2[:, cw:]
    p_in = jnp.exp(cl)
    p_ex = jnp.exp(cl - logw)
    p_inv = jnp.exp(-cl)
    p_end = p_in[c - 1:c, :] if d == 0 else p_in[0:1, :]

    pi = lax.broadcasted_iota(jnp.int32, (LANES, LANES), 0)
    pj = lax.broadcasted_iota(jnp.int32, (LANES, LANES), 1)
    same_head = (pi // RW_HEAD_DIM) == (pj // RW_HEAD_DIM)
    ones_bd = same_head.astype(BF16)

    def head_sum(x):
        xh, xl = _split(x)
        s2 = _dot(jnp.concatenate([xh, xl], axis=0), ones_bd)
        return s2[:c] + s2[c:]

    t2 = lax.broadcasted_iota(jnp.int32, (2 * c, LANES), 0) % c
    j2 = lax.broadcasted_iota(jnp.int32, (2 * c, LANES), 1) % c
    diff = (t2 - j2) if d == 0 else (j2 - t2)
    upper = lax.broadcasted_iota(jnp.int32, (2 * c, LANES), 0) < c
    mask = diff >= jnp.where(upper, 1, 0)
    units = []
    for p in range(RW_HEADS // 2):
        sl = slice(p * LANES, (p + 1) * LANES)
        kkp = kk[:, sl]
        kkp = kkp * lax.rsqrt(jnp.maximum(head_sum(kkp * kkp), 1e-24))
        pinv = p_inv[:, sl]
        bt = a[:, sl] * kkp * pinv
        kt = kd[:, sl] * pinv
        ar = jnp.concatenate([-kkp * p_ex[:, sl], r[:, sl] * p_in[:, sl]], axis=0).astype(BF16)
        pe = p_end[:, sl]
        bonus = head_sum(r[:, sl] * kd[:, sl] * rk_ref[:, sl]) * v[:, sl]
        units.append(dict(d=d, p=p, ar=ar, bt=bt, kt=kt, v=v[:, sl], pe=pe,
                          bkp=jnp.concatenate([bt * pe, kt * pe], axis=0), bonus=bonus,
                          mask=mask, eye=(diff[:c] == 0).astype(F32), bd=same_head))
    return units, zs


def _bd(x):
    lo = lax.broadcasted_iota(jnp.int32, x.shape, 1) < RW_HEAD_DIM
    xb = x.astype(BF16)
    z = jnp.zeros_like(xb)
    return jnp.concatenate([jnp.where(lo, xb, z), jnp.where(lo, z, xb)], axis=0)


def _rwkv_solve(units, s_ref, fill=()):
    c = CHUNK

    def tick():
        for f in fill:
            next(f, None)

    for u in units:
        g = _dot(u['ar'], jnp.concatenate([_bd(u['bt']), _bd(u['kt'])], axis=0), _NT)
        gb = jnp.where(u['mask'], g[:, :LANES], 0.0)
        gk = g[:, LANES:]
        u['a_ab'], u['a_rb'] = gb[:c], gb[c:]
        u['s0'] = s_ref[u['d'], u['p']]
        lhs = jnp.concatenate([u['ar'], jnp.where(u['mask'], gk, 0.0).astype(BF16)], axis=1)
        u['xs'] = _dot(lhs, jnp.concatenate([u['s0'].T.astype(BF16), _bd(u['v'])], axis=0))
    tick()
    for u in units:
        u['pw'] = _mm(u['a_ab'], _bd(u['a_ab']))
        u['tinv'] = u['eye'] + u['a_ab']
    tick()
    n_sq = CHUNK.bit_length() - 2
    for i in range(n_sq):
        for u in units:
            if i < n_sq - 1:
                both = _mm(jnp.concatenate([u['pw'], u['tinv']], axis=0), _bd(u['pw']))
                u['pw'], u['tinv'] = both[:c], u['tinv'] + both[c:]
            else:
                u['tinv'] = u['tinv'] + _mm(u['tinv'], _bd(u['pw']))
        tick()
    for u in units:
        u['u'] = _mm(u['tinv'], _bd(u['xs'][:c]))
    tick()
    ys = []
    for u in units:
        ys.append(u['xs'][c:] + _mm(u['a_rb'], _bd(u['u'])) + u['bonus'])
        uv = jnp.concatenate([u['u'], u['v']], axis=0)
        s_ref[u['d'], u['p']] = u['s0'] * u['pe'] + jnp.where(u['bd'], _mm(uv.T, u['bkp']), 0.0)
    for f in fill:
        for _ in f:
            pass
    return ys


RW_SEQS_MAX = 2
NA_ROWS_PER_STEP = 2


def _rwkv_kernel(*refs, nc, has_s0, na_rows, n_seq):
    zf_ref, zfp_ref, zfn_ref, zr_ref, zrp_ref, zrn_ref = refs[:6]
    n_in = 6 + (1 if has_s0 else 0)
    s0_ref = refs[6] if has_s0 else None
    prm = refs[n_in:n_in + 8]
    g2_ref = refs[n_in + 8]
    n_in += 9
    i = pl.program_id(1)
    if na_rows:
        q_ref, k_ref, v_ref, ck_ref, cv_ref = refs[n_in:n_in + 5]
        bias_refs = refs[n_in + 5:n_in + 5 + NA_ROWS_PER_STEP]
        yf_ref, yb_ref, g_ref, sfin_ref, oa_ref, s_ref = refs[n_in + 5 + NA_ROWS_PER_STEP:]
    else:
        yf_ref, yb_ref, g_ref, sfin_ref, s_ref = refs[n_in:]

    hd = RW_HEAD_DIM
    npair = RW_HEADS // 2

    @pl.when(i == 0)
    def _():
        s_ref[...] = jnp.zeros(s_ref.shape, F32)
        if has_s0:
            for k in range(n_seq):
                for d in range(2):
                    for p in range(npair):
                        s_ref[2 * k + d, p, :hd, :hd] = s0_ref[k, d, 2 * p]
                        s_ref[2 * k + d, p, hd:, hd:] = s0_ref[k, d, 2 * p + 1]

    zero = jnp.zeros((1, B_COLS), F32)
    units = []
    for k in range(n_seq):
        pf = jnp.where(i == 0, zero, zfp_ref[k, 7:8, :])
        nf = jnp.where(i == nc - 1, zero, zfn_ref[k, 0:1, :])
        uf, zs = _rwkv_prepare(0, zf_ref[k], pf, nf, prm)
        g_ref[k] = _mm(_sigmoid(zs[:, B_COLS - RW_GATE_RANK:]), g2_ref[...])
        pr = jnp.where(i == nc - 1, zero, zrp_ref[k, 7:8, :])
        nr = jnp.where(i == 0, zero, zrn_ref[k, 0:1, :])
        ub, _ = _rwkv_prepare(1, zr_ref[k], pr, nr, prm)
        for u in uf + ub:
            u['d'] = 2 * k + u['d']
        units += uf + ub
    fill = []
    if na_rows:
        r0 = ((pl.program_id(0) * nc + i) * NA_ROWS_PER_STEP) % na_rows
        for k, b_ref in enumerate(bias_refs):
            rs = pl.ds(k * GRID_W, GRID_W)
            fill.append(_na_row_stages(q_ref.at[rs], k_ref, v_ref, ck_ref, cv_ref, b_ref, oa_ref.at[rs],
                                       r0 + k, na_rows))
    ys = _rwkv_solve(units, s_ref, fill)
    for k in range(n_seq):
        yk = ys[2 * npair * k:2 * npair * (k + 1)]
        yf_ref[k] = jnp.concatenate(yk[:npair], axis=-1)
        yb_ref[k] = jnp.concatenate(yk[npair:], axis=-1)

    @pl.when(i == nc - 1)
    def _():
        for k in range(n_seq):
            for d in range(2):
                for p in range(npair):
                    sfin_ref[k, d, 2 * p] = s_ref[2 * k + d, p, :hd, :hd]
                    sfin_ref[k, d, 2 * p + 1] = s_ref[2 * k + d, p, hd:, hd:]


def rwkv_scan(zb, seq0, nseq, s0, prm, g2, na=None):
    seqlen = zb.shape[1]
    ns = RW_SEQS_MAX
    while seq0 % ns or nseq % ns or (na is not None and na[0].n_dec * (na[0].dec_len // GRID_W)
                                     != NA_ROWS_PER_STEP * (nseq // ns) * (seqlen // CHUNK)):
        ns //= 2
    assert ns >= 1 and seqlen % CHUNK == 0
    nc = seqlen // CHUNK
    sub = CHUNK // 8
    b0 = seq0 // ns
    has_s0 = s0 is not None

    def specs(cidx):
        return [pl.BlockSpec((ns, CHUNK, B_COLS), lambda s, i: (b0 + s, cidx(i), 0)),
                pl.BlockSpec((ns, 8, B_COLS), lambda s, i: (b0 + s, jnp.maximum(cidx(i) * sub - 1, 0), 0)),
                pl.BlockSpec((ns, 8, B_COLS),
                             lambda s, i: (b0 + s, jnp.minimum((cidx(i) + 1) * sub, seqlen // 8 - 1), 0))]

    cf = lambda i: i
    cr = lambda i: nc - 1 - i
    full = lambda a: pl.BlockSpec(a.shape, lambda s, i: (0,) * a.ndim)
    sshape = (2, RW_HEADS, RW_HEAD_DIM, RW_HEAD_DIM)
    sspec = pl.BlockSpec((ns,) + sshape, lambda s, i: (s, 0, 0, 0, 0))
    in_specs = specs(cf) + specs(cr)
    args = [zb] * 6
    if has_s0:
        in_specs.append(sspec)
        args.append(s0)
    in_specs += [full(p) for p in prm] + [full(g2)]
    args += list(prm) + [g2]
    out = lambda cidx: pl.BlockSpec((ns, CHUNK, RW_WIDTH), lambda s, i: (s, cidx(i), 0))
    tok = jax.ShapeDtypeStruct((nseq, seqlen, RW_WIDTH), F32)
    out_shape = [tok, tok, tok, jax.ShapeDtypeStruct((nseq,) + sshape, F32)]
    out_specs = [out(cf), out(cr), out(cf), sspec]
    na_rows = 0
    if na is not None:
        lay, q, kb, vb, ck, cv, bias_tab = na
        na_rows = lay.dec_len // GRID_W
        per = NA_ROWS_PER_STEP
        assert na_rows >= NA_WIN_ROWS and na_rows % per == 0 and lay.ctx_tokens % lay.dec_len == 0
        assert lay.n_dec * na_rows == per * (nseq // ns) * nc and lay.ctx_tokens % (per * GRID_W) == 0
        step = lambda s, i: s * nc + i
        batch = lambda s, i: step(s, i) * per // na_rows
        kv = pl.BlockSpec((lay.dec_len, NA_WIDTH), lambda s, i: (lay.ctx_tokens // lay.dec_len + batch(s, i), 0))
        cs = pl.BlockSpec((1, ck.shape[1], NA_WIDTH), lambda s, i: (batch(s, i), 0, 0))

        def bias_spec(k):
            def idx(s, i):
                r = step(s, i) * per % na_rows + k
                return (_na_row_start(r, na_rows) - r + NA_WIN_ROWS - 1, 0, 0, 0)
            return pl.BlockSpec((1, NA_HEADS, GRID_W, NA_WIN_ROWS * GRID_W), idx)

        in_specs += [pl.BlockSpec((per * GRID_W, NA_WIDTH),
                                  lambda s, i: (lay.ctx_tokens // (per * GRID_W) + step(s, i), 0)),
                     kv, kv, cs, cs] + [bias_spec(k) for k in range(per)]
        args += [q, kb, vb, ck, cv] + [bias_tab] * per
        out_shape.append(jax.ShapeDtypeStruct((lay.dec_tokens, NA_WIDTH), BF16))
        out_specs.append(pl.BlockSpec((per * GRID_W, NA_WIDTH), lambda s, i: (step(s, i), 0)))
    res = pl.pallas_call(
        functools.partial(_rwkv_kernel, nc=nc, has_s0=has_s0, na_rows=na_rows, n_seq=ns),
        out_shape=tuple(out_shape),
        grid=(nseq // ns, nc),
        in_specs=in_specs,
        out_specs=tuple(out_specs),
        scratch_shapes=[pltpu.VMEM((2 * ns, RW_HEADS // 2, LANES, LANES), F32)],
        compiler_params=_cparams(("arbitrary", "arbitrary")),
        name="rwkv_scan",
    )(*args)
    flat = lambda a: a.reshape(nseq * seqlen, RW_WIDTH)
    return (flat(res[0]), flat(res[1]), flat(res[2])) + tuple(res[3:])


SUBLANES = 8


def _conv_kernel(z_ref, zp_ref, zn_ref, w_ref, b_ref, g_ref, be_ref, o_ref, h_ref, hs_ref, *, lay):
    i = pl.program_id(0)
    first, last = lay.tile_pos(i)

    def glu(z):
        return z[:, :CONV_WIDTH] * _sigmoid(z[:, CONV_WIDTH:])

    hz = jnp.zeros((CONV_HALO, CONV_WIDTH), F32)
    h_ref[0:CONV_HALO, :] = jnp.where(first, hz, glu(zp_ref[...]))
    h_ref[CONV_HALO:CONV_HALO + TM, :] = glu(z_ref[...])
    h_ref[CONV_HALO + TM:, :] = jnp.where(last, hz, glu(zn_ref[...]))
    off = CONV_HALO - CONV_K // 2
    rows = hs_ref.shape[1]
    for ph in range(SUBLANES):
        hs_ref[ph] = h_ref[ph:ph + rows, :]
    acc = jnp.zeros((TM, CONV_WIDTH), F32) + b_ref[...]
    for j in range(CONV_K):
        ph, base = (off + j) % SUBLANES, (off + j) // SUBLANES * SUBLANES
        acc = acc + w_ref[j:j + 1, :] * hs_ref[ph, base:base + TM, :]
    xc = acc - jnp.mean(acc, axis=-1, keepdims=True)
    hn = xc * lax.rsqrt(jnp.mean(xc * xc, axis=-1, keepdims=True) + LN_EPS) * g_ref[...] + be_ref[...]
    o_ref[...] = (hn * _sigmoid(hn)).astype(BF16)


def conv_module(lay, zc, w, b, g, be):
    n = lay.n
    hb = TM // CONV_HALO
    vec = pl.BlockSpec((1, CONV_WIDTH), lambda i: (0, 0))
    return pl.pallas_call(
        functools.partial(_conv_kernel, lay=lay),
        out_shape=jax.ShapeDtypeStruct((n, CONV_WIDTH), BF16),
        grid=(lay.tiles,),
        in_specs=[pl.BlockSpec((TM, C_COLS), lambda i: (i, 0)),
                  pl.BlockSpec((CONV_HALO, C_COLS), lambda i: (jnp.maximum(i * hb - 1, 0), 0)),
                  pl.BlockSpec((CONV_HALO, C_COLS), lambda i: (jnp.minimum((i + 1) * hb, n // CONV_HALO - 1), 0)),
                  pl.BlockSpec((CONV_K, CONV_WIDTH), lambda i: (0, 0)), vec, vec, vec],
        out_specs=pl.BlockSpec((TM, CONV_WIDTH), lambda i: (i, 0)),
        scratch_shapes=[pltpu.VMEM((TM + 2 * CONV_HALO, CONV_WIDTH), F32),
                        pltpu.VMEM((SUBLANES, TM + 2 * CONV_HALO - SUBLANES, CONV_WIDTH), F32)],
        compiler_params=_cparams(("arbitrary",)),
        name="conv_module",
    )(zc, zc, zc, w, b, g, be)


def _merge_kernel(x_ref, oa_c_ref, yf_c_ref, yb_c_ref, g_c_ref, oa_d_ref, yf_d_ref, yb_d_ref, g_d_ref,
                  oc_ref, gt_ref, mod_ref, ng_ref, lg_ref, lb_ref,
                  wa_ref, wr_ref, wc_ref, wo_ref, rw_ref, rb_ref,
                  x1_ref, hm_ref, ti_ref, tg_ref, rk_ref, cnt_ref, *, ctx_tiles):
    is_ctx = pl.program_id(0) < ctx_tiles
    pick = lambda c_ref, d_ref: jnp.where(is_ctx, c_ref[...], d_ref[...])
    oa = pick(oa_c_ref, oa_d_ref)
    g_rw = pick(g_c_ref, g_d_ref)
    y = pick(yf_c_ref, yf_d_ref) + pick(yb_c_ref, yb_d_ref)
    pi = lax.broadcasted_iota(jnp.int32, (LANES, LANES), 0)
    pj = lax.broadcasted_iota(jnp.int32, (LANES, LANES), 1)
    ones_bd = ((pi // RW_HEAD_DIM) == (pj // RW_HEAD_DIM)).astype(BF16)

    def head_mean(v):
        vh, vl = _split(v)
        s2 = _dot(jnp.concatenate([vh, vl], axis=0), ones_bd)
        return (s2[:TM] + s2[TM:]) * (1.0 / RW_HEAD_DIM)

    outs = []
    for p in range(RW_HEADS // 2):
        yp = y[:, p * LANES:(p + 1) * LANES]
        yc = yp - head_mean(yp)
        outs.append(yc * lax.rsqrt(head_mean(yc * yc) + GN_EPS))
    o_rw = (jnp.concatenate(outs, axis=-1) * lg_ref[...] + lb_ref[...]) * g_rw
    merged = (gt_ref[:, :D_MODEL].astype(F32) * _dot(oa, wa_ref[...])
              + gt_ref[:, D_MODEL:2 * D_MODEL].astype(F32) * _mm(o_rw, wr_ref[...])
              + gt_ref[:, 2 * D_MODEL:].astype(F32) * _dot(oc_ref[...], wc_ref[...]))
    mix = _mm(merged, wo_ref[...])
    x1 = x_ref[...] + mod_ref[0, 2:3, :] * _rms(mix, ng_ref[1:2, :])
    x1_ref[...] = x1
    hm = _rms(x1, ng_ref[2:3, :]) * (1.0 + mod_ref[0, 4:5, :]) + mod_ref[0, 3:4, :]
    hm_ref[...] = hm.astype(BF16)
    logits = _mm3(hm, rw_ref[...]) + rb_ref[...]
    lane = lax.broadcasted_iota(jnp.int32, logits.shape, 1)
    idx_out = jnp.zeros(logits.shape, jnp.int32)
    val_out = jnp.full(logits.shape, NEG_INF, F32)
    picks = []
    for j in range(TOP_K):
        m = jnp.max(logits, axis=-1, keepdims=True)
        idx = jnp.min(jnp.where(logits == m, lane, LANES), axis=-1, keepdims=True)
        idx_out = jnp.where(lane == j, idx, idx_out)
        val_out = jnp.where(lane == j, m, val_out)
        picks.append(lane == idx)
        logits = jnp.where(picks[-1], -jnp.inf, logits)
    e = jnp.exp(val_out - jnp.max(val_out, axis=-1, keepdims=True))
    ti_ref[...] = idx_out
    tg_ref[...] = e / jnp.sum(e, axis=-1, keepdims=True)

    chosen = jnp.zeros(logits.shape, F32)
    for pk in picks:
        chosen = chosen + pk.astype(F32)
    ti_ = lax.broadcasted_iota(jnp.int32, (TM, TM), 0)
    tj_ = lax.broadcasted_iota(jnp.int32, (TM, TM), 1)
    before = _dot((ti_ > tj_).astype(BF16), chosen.astype(BF16))
    rank = jnp.zeros(logits.shape, F32)
    for j, pk in enumerate(picks):
        rank = jnp.where(lane == j, jnp.sum(jnp.where(pk, before, 0.0), axis=-1, keepdims=True), rank)
    rk_ref[...] = rank.astype(jnp.int32)
    cnt_ref[0] = jnp.sum(chosen, axis=0, keepdims=True)


def merge_router(lay, x, ctx_br, dec_br, o_cv, gates, mod, ng, lg, lb, wa, wr, wc, wo, rw, rb):
    n = lay.n
    row = lambda w: pl.BlockSpec((TM, w), lambda i: (i, 0))
    crow = lambda w: pl.BlockSpec((TM, w), lambda i: (jnp.minimum(i, lay.ctx_tiles - 1), 0))
    drow = lambda w: pl.BlockSpec((TM, w), lambda i: (jnp.maximum(i - lay.ctx_tiles, 0), 0))
    full = lambda a: pl.BlockSpec(a.shape, lambda i: (0,) * a.ndim)
    br_w = (NA_WIDTH, RW_WIDTH, RW_WIDTH, RW_WIDTH)
    return pl.pallas_call(
        functools.partial(_merge_kernel, ctx_tiles=lay.ctx_tiles),
        out_shape=(jax.ShapeDtypeStruct((n, D_MODEL), F32), jax.ShapeDtypeStruct((n, D_MODEL), BF16),
                   jax.ShapeDtypeStruct((n, LANES), jnp.int32), jax.ShapeDtypeStruct((n, LANES), F32),
                   jax.ShapeDtypeStruct((n, LANES), jnp.int32), jax.ShapeDtypeStruct((lay.tiles, 1, LANES), F32)),
        grid=(lay.tiles,),
        in_specs=[row(D_MODEL)] + [crow(w) for w in br_w] + [drow(w) for w in br_w] + [
                  row(CONV_WIDTH),
                  row(G_COLS), pl.BlockSpec((1, N_MOD, D_MODEL), lambda i: (lay.mod_row(i), 0, 0)),
                  full(ng), full(lg), full(lb), full(wa), full(wr), full(wc), full(wo), full(rw), full(rb)],
        out_specs=(row(D_MODEL), row(D_MODEL), row(LANES), row(LANES), row(LANES),
                   pl.BlockSpec((1, 1, LANES), lambda i: (i, 0, 0))),
        compiler_params=_cparams(("arbitrary",)),
        name="merge_router",
    )(x, *ctx_br, *dec_br, o_cv, gates, mod, ng, lg, lb, wa, wr, wc, wo, rw, rb)


RUN_ALIGN = 8
RUN_BITS = (TM // RUN_ALIGN).bit_length()
SORT_ROWS = -(-(TM * TOP_K + N_EXPERTS * (RUN_ALIGN - 1)) // LANES) * LANES


def route_tables(top_i, wrank, cnt, n):
    tiles = n // TM
    cnt = cnt[:, 0, :N_EXPERTS].astype(jnp.int32)
    run = ((cnt + RUN_ALIGN - 1) // RUN_ALIGN) * RUN_ALIGN
    off = jnp.cumsum(run, axis=1) - run
    before = jnp.cumsum(run, axis=0) - run
    region = jnp.sum(run, axis=0)
    padded = ((region + ROUTE_BLOCK - 1) // ROUTE_BLOCK) * ROUTE_BLOCK
    pad_end = jnp.cumsum(padded)
    first = (pad_end - padded)[None, :] + before
    table = jnp.concatenate([off, first, run, jnp.zeros((tiles, LANES - 3 * N_EXPERTS), jnp.int32)], axis=1)
    onehot = top_i.reshape(tiles, TM, TOP_K, 1) == jnp.arange(N_EXPERTS, dtype=jnp.int32)
    pos = jnp.sum(jnp.where(onehot, off[:, None, None, :], 0), axis=-1) + wrank.reshape(tiles, TM, TOP_K)
    n_blocks = -(-tiles * (TM * TOP_K + N_EXPERTS * (RUN_ALIGN - 1)) // ROUTE_BLOCK) + N_EXPERTS
    starts = jnp.arange(n_blocks, dtype=jnp.int32) * ROUTE_BLOCK
    block_e = jnp.minimum(jnp.sum((pad_end[None, :] <= starts[:, None]).astype(jnp.int32), axis=1),
                          N_EXPERTS - 1)
    n_used = (pad_end[-1] // ROUTE_BLOCK).astype(jnp.int32).reshape(1)
    return table.reshape(tiles, 1, LANES), pos, block_e, n_used, n_blocks


def _for_each_piece(tab_ref, fn):
    def body(e, carry):
        off = tab_ref[0, 0, e]
        first = tab_ref[0, 0, N_EXPERTS + e]
        run = tab_ref[0, 0, 2 * N_EXPERTS + e]
        for k in range(RUN_BITS):
            size = RUN_ALIGN << k

            @pl.when((run & size) != 0)
            def _():
                done = run & (-2 * size)
                fn(pl.multiple_of(off + done, RUN_ALIGN), pl.multiple_of(first + done, RUN_ALIGN), size)
        return carry

    lax.fori_loop(0, N_EXPERTS, body, 0)


def _dispatch_kernel(tab_ref, tabp_ref, pos_ref, hm_ref, xb_in_ref, xb_ref, srt_ref, sems):
    del xb_in_ref
    i = pl.program_id(0)
    slot = i % 2

    def copy(slot_, off, row, size):
        return pltpu.make_async_copy(srt_ref.at[slot_, pl.ds(off, size), :], xb_ref.at[pl.ds(row, size), :],
                                     sems.at[slot_])

    p_iota = lax.broadcasted_iota(jnp.int32, (SORT_ROWS, TM), 0)
    hit = p_iota == pos_ref[0, 0:1, :]
    for j in range(1, TOP_K):
        hit = hit | (p_iota == pos_ref[0, j:j + 1, :])
    srt_ref[slot] = _dot(jnp.where(hit, 1.0, 0.0).astype(BF16), hm_ref[...])
    _for_each_piece(tab_ref, lambda off, row, size: copy(slot, off, row, size).start())

    @pl.when(i > 0)
    def _():
        _for_each_piece(tabp_ref, lambda off, row, size: copy(1 - slot, off, row, size).wait())

    @pl.when(i == pl.num_programs(0) - 1)
    def _():
        _for_each_piece(tab_ref, lambda off, row, size: copy(slot, off, row, size).wait())


def moe_dispatch(lay, hm, table, pos, xb0):
    n_rows = xb0.shape[0]
    post = jnp.pad(pos.transpose(0, 2, 1), ((0, 0), (0, 8 - TOP_K), (0, 0)))
    tab = lambda f: pl.BlockSpec((1, 1, LANES), lambda i: (f(i), 0, 0), memory_space=pltpu.SMEM)
    return pl.pallas_call(
        _dispatch_kernel,
        out_shape=jax.ShapeDtypeStruct((n_rows, D_MODEL), F32),
        grid=(lay.tiles,),
        in_specs=[tab(lambda i: i), tab(lambda i: jnp.maximum(i - 1, 0)),
                  pl.BlockSpec((1, 8, TM), lambda i: (i, 0, 0)),
                  pl.BlockSpec((TM, D_MODEL), lambda i: (i, 0)),
                  pl.BlockSpec(memory_space=pl.ANY)],
        out_specs=pl.BlockSpec(memory_space=pl.ANY),
        scratch_shapes=[pltpu.VMEM((2, SORT_ROWS, D_MODEL), F32), pltpu.SemaphoreType.DMA((2,))],
        input_output_aliases={4: 0},
        compiler_params=_cparams(("arbitrary",)),
        name="moe_dispatch",
    )(table, table, post, hm, xb0)


GL_GROUP = 2 * LANES


def _regroup_perm():
    r = jnp.arange(GL_GROUP, dtype=jnp.int32)[:, None]
    c = jnp.arange(GL_GROUP, dtype=jnp.int32)[None, :]
    src = jnp.where(c < LANES, 2 * c, 2 * (c - LANES) + 1)
    return (r == src).astype(BF16)


def _expert_kernel(be_ref, nu_ref, x_ref, w1_ref, perm_ref, b1_ref, w2_ref, b2_ref, y_ref, w1b_ref, w2b_ref):
    b = pl.program_id(0)
    used = b < nu_ref[0]

    @pl.when(used & ((b == 0) | (be_ref[b] != be_ref[jnp.maximum(b - 1, 0)])))
    def _():
        for j in range(2 * EXPERT_FF // GL_GROUP):
            cs = slice(j * GL_GROUP, (j + 1) * GL_GROUP)
            w1b_ref[:, cs] = _dot(w1_ref[:, cs].astype(BF16), perm_ref[...]).astype(BF16)
        w2b_ref[...] = w2_ref[...].astype(BF16)

    @pl.when(used)
    def _():
        x = x_ref[...].astype(BF16)
        n_grp = 2 * EXPERT_FF // GL_GROUP
        hs = [_dot(x, w1b_ref[:, g * GL_GROUP:(g + 1) * GL_GROUP]) + b1_ref[0, :, g * GL_GROUP:(g + 1) * GL_GROUP]
              for g in range(n_grp)]
        acts = []
        for h in hs:
            hg = jnp.minimum(h[:, :LANES], SWIGLU_LIMIT)
            hl = jnp.clip(h[:, LANES:], -SWIGLU_LIMIT, SWIGLU_LIMIT)
            acts.append((hg * _sigmoid(SWIGLU_ALPHA * hg) * (hl + 1.0)).astype(BF16))
        acc = jnp.zeros((ROUTE_BLOCK, D_MODEL), F32) + b2_ref[0]
        for j in range(n_grp // 2):
            act = jnp.concatenate(acts[2 * j:2 * j + 2], axis=-1)
            acc = acc + _dot(act, w2b_ref[j * GL_GROUP:(j + 1) * GL_GROUP, :])
        y_ref[...] = acc

    @pl.when(jnp.logical_not(used))
    def _():
        y_ref[...] = jnp.zeros(y_ref.shape, F32)


def moe_experts(xb, block_e, n_used, n_blocks, l, w1, b1, w2, b2):
    def last_used(b, nu):
        return jnp.minimum(b, jnp.maximum(nu[0] - 1, 0))

    wspec = lambda a: pl.BlockSpec((None, None) + a.shape[2:], lambda b, be, nu: (l, be[last_used(b, nu)], 0, 0))
    bspec = lambda a: pl.BlockSpec((1,) + a.shape[1:], lambda b, be, nu: (be[last_used(b, nu)], 0, 0))
    return pl.pallas_call(
        _expert_kernel,
        out_shape=jax.ShapeDtypeStruct(xb.shape, F32),
        grid_spec=pltpu.PrefetchScalarGridSpec(
            num_scalar_prefetch=2,
            grid=(n_blocks,),
            in_specs=[pl.BlockSpec((ROUTE_BLOCK, D_MODEL), lambda b, be, nu: (last_used(b, nu), 0)),
                      wspec(w1), pl.BlockSpec((GL_GROUP, GL_GROUP), lambda b, be, nu: (0, 0)),
                      bspec(b1), wspec(w2), bspec(b2)],
            out_specs=pl.BlockSpec((ROUTE_BLOCK, D_MODEL), lambda b, be, nu: (b, 0)),
            scratch_shapes=[pltpu.VMEM(w1.shape[2:], BF16), pltpu.VMEM(w2.shape[2:], BF16)]),
        compiler_params=_cparams(("arbitrary",)),
        name="moe_experts",
    )(block_e, n_used, xb, w1, _regroup_perm(), b1, w2, b2)


def _combine_kernel(tab_ref, tabn_ref, yb_ref, pos_ref, tg_ref, x1_ref, mod_ref, ng_ref, *rest, ctx_tiles):
    o_refs, (stg_ref, sems) = rest[:-2], rest[-2:]
    i = pl.program_id(0)
    last = pl.num_programs(0) - 1
    slot = i % 2

    def copy(slot_, off, row, size):
        return pltpu.make_async_copy(yb_ref.at[pl.ds(row, size), :], stg_ref.at[slot_, pl.ds(off, size), :],
                                     sems.at[slot_])

    @pl.when(i == 0)
    def _():
        stg_ref[...] = jnp.zeros(stg_ref.shape, F32)
        _for_each_piece(tab_ref, lambda off, row, size: copy(slot, off, row, size).start())

    @pl.when(i < last)
    def _():
        _for_each_piece(tabn_ref, lambda off, row, size: copy(1 - slot, off, row, size).start())

    _for_each_piece(tab_ref, lambda off, row, size: copy(slot, off, row, size).wait())
    p_iota = lax.broadcasted_iota(jnp.int32, (TM, SORT_ROWS), 1)
    w = jnp.zeros((TM, SORT_ROWS), F32)
    for j in range(TOP_K):
        w = w + jnp.where(p_iota == pos_ref[:, j:j + 1], tg_ref[:, j:j + 1], 0.0)
    y = _mm3(w, stg_ref[slot])
    out = x1_ref[...] + mod_ref[0, 5:6, :] * _rms(y, ng_ref[3:4, :])
    if len(o_refs) == 1:
        o_refs[0][...] = out
    else:
        @pl.when(i < ctx_tiles)
        def _():
            o_refs[0][...] = out

        o_refs[1][...] = out


def moe_combine(lay, yb, table, pos, tg, x1, mod, ng, split):
    n = lay.n
    posl = jnp.pad(pos.reshape(n, TOP_K), ((0, 0), (0, LANES - TOP_K)))
    row = lambda w: pl.BlockSpec((TM, w), lambda i: (i, 0))
    tab = lambda f: pl.BlockSpec((1, 1, LANES), lambda i: (f(i), 0, 0), memory_space=pltpu.SMEM)
    if split:
        out_shape = (jax.ShapeDtypeStruct((lay.ctx_tokens, D_MODEL), F32),
                     jax.ShapeDtypeStruct((lay.dec_tokens, D_MODEL), F32))
        out_specs = (pl.BlockSpec((TM, D_MODEL), lambda i: (jnp.minimum(i, lay.ctx_tiles - 1), 0)),
                     pl.BlockSpec((TM, D_MODEL), lambda i: (jnp.maximum(i - lay.ctx_tiles, 0), 0)))
    else:
        out_shape = jax.ShapeDtypeStruct((n, D_MODEL), F32)
        out_specs = row(D_MODEL)
    return pl.pallas_call(
        functools.partial(_combine_kernel, ctx_tiles=lay.ctx_tiles),
        out_shape=out_shape,
        grid=(lay.tiles,),
        in_specs=[tab(lambda i: i), tab(lambda i: jnp.minimum(i + 1, lay.tiles - 1)),
                  pl.BlockSpec(memory_space=pl.ANY),
                  row(LANES), row(LANES), row(D_MODEL),
                  pl.BlockSpec((1, N_MOD, D_MODEL), lambda i: (lay.mod_row(i), 0, 0)),
                  pl.BlockSpec(ng.shape, lambda i: (0, 0))],
        out_specs=out_specs,
        scratch_shapes=[pltpu.VMEM((2, SORT_ROWS, D_MODEL), F32), pltpu.SemaphoreType.DMA((2,))],
        compiler_params=_cparams(("arbitrary",)),
        name="moe_combine",
    )(table, table, yb, posl, tg, x1, mod, ng)


def layer(lay, x, l, cond, P, ctx_k, ctx_v, s0_dec, xb_buf):
    mod = mod_table(cond, P['mod_w'][l], P['mod_b'][l]).reshape(cond.shape[0], N_MOD, D_MODEL)
    ng = P['norm_g'][l]
    q, kf, vf, kb, vb, zb, zc, gates = in_proj(lay, x, mod, ng[0:1], P['w_in'][l].astype(BF16),
                                               P['b_in'][l].reshape(1, P_IN))
    oa_c = ctx_attention(lay, q, kb, vb)
    vec = lambda a: a.reshape(1, -1)
    prm = (P['rw_mu'][l], P['rw_w0'][l], P['rw_w2'][l], P['rw_a0'][l], P['rw_a2'][l],
           vec(P['rw_k_k'][l]), vec(P['rw_k_a'][l]), vec(P['rw_r_k'][l]))
    yf_c, yb_c, g_c, s_ctx = rwkv_scan(zb.reshape(-1, lay.ctx_len, B_COLS), 0, lay.n_ctx, None, prm,
                                       P['rw_g2'][l])
    na = (lay, q, kb, vb, ctx_k.astype(BF16), ctx_v.astype(BF16), na_bias_table(P['rpb'][l]))
    yf_d, yb_d, g_d, _, oa_d = rwkv_scan(zb.reshape(-1, lay.dec_len, B_COLS), lay.ctx_tokens // lay.dec_len,
                                         lay.n_dec, s0_dec, prm, P['rw_g2'][l], na=na)
    o_cv = conv_module(lay, zc, P['conv_w'][l], vec(P['conv_b'][l]), vec(P['conv_ln_g'][l]),
                       vec(P['conv_ln_b'][l]))
    rw = jnp.zeros((D_MODEL, LANES), F32).at[:, :N_EXPERTS].set(P['router_w'][l])
    rb = jnp.full((1, LANES), NEG_INF, F32).at[0, :N_EXPERTS].set(P['router_b'][l])
    x1, hm, top_i, top_g, rank, counts = merge_router(
        lay, x, (oa_c, yf_c, yb_c, g_c), (oa_d, yf_d, yb_d, g_d), o_cv, gates, mod, ng,
        vec(P['rw_lnx_g'][l]), vec(P['rw_lnx_b'][l]),
        P['w_o_attn'][l].astype(BF16), P['w_o_rwkv'][l].astype(BF16), P['w_o_conv'][l].astype(BF16),
        P['w_out'][l].astype(BF16), rw, rb)
    table, pos, block_e, n_used, n_blocks = route_tables(top_i[:, :TOP_K], rank[:, :TOP_K], counts, lay.n)
    if xb_buf is None:
        xb_buf = jnp.zeros((n_blocks * ROUTE_BLOCK, D_MODEL), F32)
    xb = moe_dispatch(lay, hm, table, pos, xb_buf)
    b1 = P['exp_b1'][l].reshape(N_EXPERTS, -1, LANES, 2).transpose(0, 1, 3, 2).reshape(N_EXPERTS, 1, -1)
    ybk = moe_experts(xb, block_e, n_used, n_blocks, l, P['exp_w1'], b1, P['exp_w2'], P['exp_b2'][l][:, None, :])
    x2 = moe_combine(lay, ybk, table, pos, top_g, x1, mod, ng, split=(l == DEPTH - 1))
    return x2, kf, vf, s_ctx, xb


def kernel(x_prompt, x_sample, cache_attn_k, cache_attn_v, state_rwkv, c, c_ctx, mod_w, mod_b, norm_g, w_in, b_in, rpb, w_o_attn, rw_mu, rw_w0, rw_w2, rw_a0, rw_a2, rw_g2, rw_k_k, rw_k_a, rw_r_k, rw_lnx_g, rw_lnx_b, w_o_rwkv, conv_w, conv_b, conv_ln_g, conv_ln_b, w_o_conv, w_out, router_w, router_b, exp_w1, exp_b1, exp_w2, exp_b2):
    P = {
        'mod_w': mod_w, 'mod_b': mod_b, 'norm_g': norm_g, 'w_in': w_in, 'b_in': b_in, 'rpb': rpb,
        'w_o_attn': w_o_attn, 'rw_mu': rw_mu, 'rw_w0': rw_w0, 'rw_w2': rw_w2, 'rw_a0': rw_a0,
        'rw_a2': rw_a2, 'rw_g2': rw_g2, 'rw_k_k': rw_k_k, 'rw_k_a': rw_k_a, 'rw_r_k': rw_r_k,
        'rw_lnx_g': rw_lnx_g, 'rw_lnx_b': rw_lnx_b, 'w_o_rwkv': w_o_rwkv, 'conv_w': conv_w,
        'conv_b': conv_b, 'conv_ln_g': conv_ln_g, 'conv_ln_b': conv_ln_b, 'w_o_conv': w_o_conv,
        'w_out': w_out, 'router_w': router_w, 'router_b': router_b, 'exp_w1': exp_w1,
        'exp_b1': exp_b1, 'exp_w2': exp_w2, 'exp_b2': exp_b2,
    }
    nb, sl, _ = x_prompt.shape
    db, ds, _ = x_sample.shape
    lay = Layout(nb, sl, db, ds)
    x = jnp.concatenate([x_prompt.reshape(-1, D_MODEL), x_sample.reshape(-1, D_MODEL)], axis=0)
    cond = jnp.zeros((8, D_MODEL), F32).at[0].set(c_ctx).at[1:1 + db].set(c)
    ks, vs, ss = [], [], []
    xb_buf = None
    for l in range(DEPTH):
        ck = cache_attn_k[:, l].reshape(db, -1, NA_WIDTH)
        cv = cache_attn_v[:, l].reshape(db, -1, NA_WIDTH)
        x, kf, vf, s_ctx, xb_buf = layer(lay, x, l, cond, P, ck, cv, state_rwkv[:, l], xb_buf)
        ks.append(kf.reshape(nb, sl, NA_HEADS, NA_HEAD_DIM))
        vs.append(vf.reshape(nb, sl, NA_HEADS, NA_HEAD_DIM))
        ss.append(s_ctx)
    y_prompt = x[0].reshape(nb, sl, D_MODEL)
    y_sample = x[1].reshape(db, ds, D_MODEL)
    return (y_prompt, y_sample, jnp.stack(ks, axis=1), jnp.stack(vs, axis=1), jnp.stack(ss, axis=1))
```

```python
import functools

import jax
import jax.numpy as jnp
from jax import lax
from jax.experimental import pallas as pl
from jax.experimental.pallas import tpu as pltpu

F32 = jnp.float32
BF16 = jnp.bfloat16

D_MODEL = 1024
BATCH, SEQ = 32, 256
DEPTH = 2
DEC_BATCH, DEC_SEQ = 4, 4096
PAST_LEN = 256
GRID_W = 64
NA_HEADS, NA_HEAD_DIM = 8, 64
NA_WIDTH = NA_HEADS * NA_HEAD_DIM
NA_WIN_ROWS, NA_WIN_COLS = 8, 16
RW_HEADS, RW_HEAD_DIM = 8, 64
RW_WIDTH = RW_HEADS * RW_HEAD_DIM
RW_DECAY_RANK, RW_ICLR_RANK, RW_GATE_RANK = 64, 64, 128
CONV_WIDTH, CONV_K = 512, 31
N_BRANCH, N_MOD = 3, 6
N_EXPERTS, TOP_K, EXPERT_FF = 32, 4, 1024
SWIGLU_LIMIT, SWIGLU_ALPHA = 7.0, 1.702
ROUTE_BLOCK = 512
RMS_EPS, LN_EPS, GN_EPS = 1e-6, 1e-5, 64e-5
NEG_INF = -1e30
A_COLS = 3 * NA_WIDTH
B_COLS = 3 * RW_WIDTH + 2 * RW_DECAY_RANK + 2 * RW_ICLR_RANK + RW_GATE_RANK
C_COLS = 2 * CONV_WIDTH
G_COLS = N_BRANCH * D_MODEL
P_IN = A_COLS + B_COLS + C_COLS + G_COLS

LANES = 128
SUBLANES = 8
TM = 256
CHUNK = 64
CONV_HALO = 16
VMEM_LIMIT = 56 * 1024 * 1024


def _cparams(sem):
    return pltpu.CompilerParams(dimension_semantics=sem, vmem_limit_bytes=VMEM_LIMIT)


def _dot(a, b, dims=((1,), (0,))):
    return lax.dot_general(a, b, (dims, ((), ())), preferred_element_type=F32)


def _mm(a, b, dims=((1,), (0,))):
    return _dot(a.astype(BF16), b.astype(BF16), dims)


def _split(a):
    hi = a.astype(BF16)
    lo = (a - hi.astype(F32)).astype(BF16)
    return hi, lo


def _mm3(a, b, dims=((1,), (0,))):
    ah, al = _split(a)
    bh, bl = _split(b)
    return _dot(ah, bh, dims) + (_dot(ah, bl, dims) + _dot(al, bh, dims))


def _mm4(a, b_hi_lo):
    m, n = a.shape[0], b_hi_lo[0].shape[1]
    out = _dot(jnp.concatenate(_split(a), axis=0), jnp.concatenate(b_hi_lo, axis=1))
    return (out[:m, :n] + out[:m, n:]) + (out[m:, :n] + out[m:, n:])


_NT = ((1,), (1,))


def _rms(x, g):
    return x * lax.rsqrt(jnp.mean(x * x, axis=-1, keepdims=True) + RMS_EPS) * g


def _sigmoid(x):
    return 1.0 / (1.0 + jnp.exp(-x))


class Layout:
    def __init__(self, n_ctx, ctx_len, n_dec, dec_len):
        assert ctx_len == TM and dec_len % TM == 0
        self.n_ctx, self.ctx_len, self.n_dec, self.dec_len = n_ctx, ctx_len, n_dec, dec_len
        self.ctx_tokens = n_ctx * ctx_len
        self.dec_tokens = n_dec * dec_len
        self.n = self.ctx_tokens + self.dec_tokens
        self.ctx_tiles = self.ctx_tokens // TM
        self.dec_tiles_per_seq = dec_len // TM
        self.tiles = self.n // TM

    def mod_row(self, i):
        return jnp.where(i < self.ctx_tiles, 0, 1 + (i - self.ctx_tiles) // self.dec_tiles_per_seq)

    def tile_pos(self, i):
        j = (i - self.ctx_tiles) % self.dec_tiles_per_seq
        is_ctx = i < self.ctx_tiles
        return is_ctx | (j == 0), is_ctx | (j == self.dec_tiles_per_seq - 1)


def _mod_kernel(c_ref, w_ref, b_ref, o_ref):
    c = c_ref[...]
    o_ref[...] = _mm3(c * _sigmoid(c), w_ref[...]) + b_ref[...]


def mod_table(cond, w, b):
    tn = 1024
    nm = w.shape[1]
    return pl.pallas_call(
        _mod_kernel,
        out_shape=jax.ShapeDtypeStruct((cond.shape[0], nm), F32),
        grid=(nm // tn,),
        in_specs=[pl.BlockSpec(cond.shape, lambda j: (0, 0)),
                  pl.BlockSpec((D_MODEL, tn), lambda j: (0, j)),
                  pl.BlockSpec((1, tn), lambda j: (0, j))],
        out_specs=pl.BlockSpec((cond.shape[0], tn), lambda j: (0, j)),
        compiler_params=_cparams(("arbitrary",)),
        name="mod_table",
    )(cond, w, b.reshape(1, nm))


def _inproj_kernel(x_ref, mod_ref, g_ref, w_ref, b_ref,
                   q_ref, kf_ref, vf_ref, kb_ref, vb_ref, zb_ref, zc_ref, gt_ref, *, ctx_tiles):
    u = (_rms(x_ref[...], g_ref[...]) * (1.0 + mod_ref[0, 1:2, :]) + mod_ref[0, 0:1, :]).astype(BF16)

    def seg(c0, width):
        return _dot(u, w_ref[:, c0:c0 + width]) + b_ref[:, c0:c0 + width]

    q_ref[...] = seg(0, NA_WIDTH).astype(BF16)
    k = seg(NA_WIDTH, NA_WIDTH)
    kb_ref[...] = k.astype(BF16)
    v = seg(2 * NA_WIDTH, NA_WIDTH)
    vb_ref[...] = v.astype(BF16)

    @pl.when(pl.program_id(0) < ctx_tiles)
    def _():
        kf_ref[...] = k
        vf_ref[...] = v
    wb = 384
    for j in range(B_COLS // wb):
        zb_ref[:, j * wb:(j + 1) * wb] = seg(A_COLS + j * wb, wb)
    wc = 512
    for j in range(C_COLS // wc):
        zc_ref[:, j * wc:(j + 1) * wc] = seg(A_COLS + B_COLS + j * wc, wc)
    for j in range(G_COLS // wc):
        gt_ref[:, j * wc:(j + 1) * wc] = _sigmoid(seg(A_COLS + B_COLS + C_COLS + j * wc, wc)).astype(BF16)


def in_proj(lay, x, mod, g0, w_bf, b):
    n = lay.n
    row = lambda w: pl.BlockSpec((TM, w), lambda i: (i, 0))
    crow = pl.BlockSpec((TM, NA_WIDTH), lambda i: (jnp.minimum(i, lay.ctx_tiles - 1), 0))
    ctx_f32 = jax.ShapeDtypeStruct((lay.ctx_tokens, NA_WIDTH), F32)
    out_shape = (jax.ShapeDtypeStruct((n, NA_WIDTH), BF16), ctx_f32, ctx_f32,
                 jax.ShapeDtypeStruct((n, NA_WIDTH), BF16), jax.ShapeDtypeStruct((n, NA_WIDTH), BF16),
                 jax.ShapeDtypeStruct((n, B_COLS), F32), jax.ShapeDtypeStruct((n, C_COLS), F32),
                 jax.ShapeDtypeStruct((n, G_COLS), BF16))
    return pl.pallas_call(
        functools.partial(_inproj_kernel, ctx_tiles=lay.ctx_tiles),
        out_shape=out_shape,
        grid=(lay.tiles,),
        in_specs=[row(D_MODEL),
                  pl.BlockSpec((1, N_MOD, D_MODEL), lambda i: (lay.mod_row(i), 0, 0)),
                  pl.BlockSpec((1, D_MODEL), lambda i: (0, 0)),
                  pl.BlockSpec((D_MODEL, P_IN), lambda i: (0, 0), pipeline_mode=pl.Buffered(1)),
                  pl.BlockSpec((1, P_IN), lambda i: (0, 0))],
        out_specs=(row(NA_WIDTH), crow, crow, row(NA_WIDTH), row(NA_WIDTH),
                   row(B_COLS), row(C_COLS), row(G_COLS)),
        compiler_params=_cparams(("arbitrary",)),
        name="in_proj",
    )(x, mod, g0, w_bf, b)


def _ctx_attn_kernel(q_ref, k_ref, v_ref, o_ref):
    scale = NA_HEAD_DIM ** -0.5
    lo = lax.broadcasted_iota(jnp.int32, (q_ref.shape[0], LANES), 1) < NA_HEAD_DIM
    heads = []
    for h in range(NA_HEADS):
        ps = slice((h // 2) * LANES, (h // 2 + 1) * LANES)
        qp = q_ref[:, ps] * scale
        qh = jnp.where(lo if h % 2 == 0 else jnp.logical_not(lo), qp, jnp.zeros_like(qp))
        heads.append(dict(ps=ps, s=_dot(qh, k_ref[:, ps], _NT)))
    for u in heads:
        u['p'] = jnp.exp(u['s'] - jnp.max(u['s'], axis=-1, keepdims=True))
    for u in heads:
        u['o'] = _dot(u['p'].astype(BF16), v_ref[:, u['ps']]) / jnp.sum(u['p'], axis=-1, keepdims=True)
    for p in range(NA_HEADS // 2):
        o_ref[:, p * LANES:(p + 1) * LANES] = jnp.where(lo, heads[2 * p]['o'], heads[2 * p + 1]['o']).astype(BF16)


def ctx_attention(lay, q, kb, vb):
    blk = pl.BlockSpec((lay.ctx_len, NA_WIDTH), lambda b: (b, 0))
    return pl.pallas_call(
        _ctx_attn_kernel,
        out_shape=jax.ShapeDtypeStruct((lay.ctx_tokens, NA_WIDTH), BF16),
        grid=(lay.n_ctx,),
        in_specs=[blk, blk, blk],
        out_specs=blk,
        compiler_params=_cparams(("arbitrary",)),
        name="ctx_attention",
    )(q, kb, vb)


def na_bias_table(rpb):
    kw, kh = NA_WIN_COLS, NA_WIN_ROWS
    col = jnp.arange(GRID_W, dtype=jnp.int32)
    col_start = jnp.clip(col - kw // 2, 0, GRID_W - kw)
    col_ok = (col[None, :] >= col_start[:, None]) & (col[None, :] < col_start[:, None] + kw)
    dc = jnp.clip(col[None, :] - col[:, None] + kw - 1, 0, 2 * kw - 2)
    bias_c = jnp.where(col_ok, rpb[:, :, dc].astype(F32), NEG_INF)
    tabs = [jnp.concatenate([bias_c[:, d + i] for i in range(kh)], axis=-1) for d in range(kh)]
    return jnp.stack(tabs, axis=0)


def _na_row_start(r, rows):
    return jnp.clip(r - NA_WIN_ROWS // 2, 0, rows - NA_WIN_ROWS)


def _na_row_stages(q_ref, k_ref, v_ref, ck_ref, cv_ref, bias_ref, o_ref, r, rows):
    rs = _na_row_start(r, rows)
    n_loc = NA_WIN_ROWS * GRID_W
    start = pl.multiple_of(rs * GRID_W, GRID_W)
    scale = NA_HEAD_DIM ** -0.5
    lo = lax.broadcasted_iota(jnp.int32, (GRID_W, LANES), 1) < NA_HEAD_DIM
    pairs = [slice(p * LANES, (p + 1) * LANES) for p in range(NA_HEADS // 2)]
    keys = [jnp.concatenate([k_ref[pl.ds(start, n_loc), ps], ck_ref[0, :, ps]], axis=0) for ps in pairs]
    vals = [jnp.concatenate([v_ref[pl.ds(start, n_loc), ps], cv_ref[0, :, ps]], axis=0) for ps in pairs]
    heads = []
    for h in range(NA_HEADS):
        qp = q_ref[:, pairs[h // 2]] * scale
        qh = jnp.where(lo if h % 2 == 0 else jnp.logical_not(lo), qp, jnp.zeros_like(qp))
        s = _dot(qh, keys[h // 2], _NT)
        heads.append(dict(vals=vals[h // 2], s_loc=s[:, :n_loc] + bias_ref[0, h], s_ctx=s[:, n_loc:]))
    yield
    for u in heads:
        u['m'] = jnp.maximum(jnp.max(u['s_loc'], axis=-1, keepdims=True),
                             jnp.max(u['s_ctx'], axis=-1, keepdims=True))
    yield
    for u in heads:
        u['p_loc'] = jnp.exp(u['s_loc'] - u['m'])
        u['p_ctx'] = jnp.exp(u['s_ctx'] - u['m'])
        u['den'] = jnp.sum(u['p_loc'], axis=-1, keepdims=True) + jnp.sum(u['p_ctx'], axis=-1, keepdims=True)
    yield
    for u in heads:
        p = jnp.concatenate([u['p_loc'].astype(BF16), u['p_ctx'].astype(BF16)], axis=1)
        u['o'] = _dot(p, u['vals']) / u['den']
    yield
    for p in range(NA_HEADS // 2):
        o_ref[:, p * LANES:(p + 1) * LANES] = jnp.where(lo, heads[2 * p]['o'], heads[2 * p + 1]['o']).astype(BF16)


def _softplus(x):
    return jnp.maximum(x, 0.0) + jnp.log(1.0 + jnp.exp(-jnp.abs(x)))


def _rwkv_prepare(d, z, prev_row, next_row, prm):
    (mu_ref, w0_ref, w2h_ref, w2l_ref, a0_ref, a2h_ref, a2l_ref, kk_ref, ka_ref, rk_ref) = prm
    c = CHUNK
    cw = RW_WIDTH
    row = lax.broadcasted_iota(jnp.int32, (SUBLANES, 1), 0)
    down, up = pltpu.roll(z, 1, axis=0), pltpu.roll(z, c - 1, axis=0)
    prev = jnp.concatenate([jnp.where(row == 0, prev_row, down[:SUBLANES]), down[SUBLANES:]], axis=0)
    nxt = jnp.concatenate([up[:c - SUBLANES], jnp.where(row == SUBLANES - 1, next_row, up[c - SUBLANES:])], axis=0)
    zs = mu_ref[2:3, :] * z + mu_ref[0:1, :] * prev + mu_ref[1:2, :] * nxt
    r, k, v = zs[:, :cw], zs[:, cw:2 * cw], zs[:, 2 * cw:3 * cw]
    o = 3 * cw
    zw = zs[:, o + d * RW_DECAY_RANK:o + (d + 1) * RW_DECAY_RANK]
    o += 2 * RW_DECAY_RANK
    za = zs[:, o + d * RW_ICLR_RANK:o + (d + 1) * RW_ICLR_RANK]

    w_log = -_softplus(-(w0_ref[d:d + 1, :] + _mm4(jnp.tanh(zw), (w2h_ref[d], w2l_ref[d])))) - 0.5
    logw = -jnp.exp(w_log)
    a = _sigmoid(a0_ref[d:d + 1, :] + _mm4(za, (a2h_ref[d], a2l_ref[d])))
    kd = k * (1.0 + (a - 1.0) * ka_ref[...])
    kk = k * kk_ref[...]

    ti = lax.broadcasted_iota(jnp.int32, (c, c), 0)
    tj = lax.broadcasted_iota(jnp.int32, (c, c), 1)
    tri = ((ti >= tj) if d == 0 else (tj >= ti)).astype(BF16)
    lh, ll = _split(logw)
    cl2 = _dot(tri, jnp.concatenate([lh, ll], axis=1))
    cl = cl2[:, :cw] + cl2[:, cw:]
    p_in = jnp.exp(cl)
    p_ex = jnp.exp(cl - logw)
    p_inv = jnp.exp(-cl)
    p_end = p_in[c - 1:c, :] if d == 0 else p_in[0:1, :]

    pi = lax.broadcasted_iota(jnp.int32, (LANES, LANES), 0)
    pj = lax.broadcasted_iota(jnp.int32, (LANES, LANES), 1)
    same_head = (pi // RW_HEAD_DIM) == (pj // RW_HEAD_DIM)
    ones_bd = same_head.astype(BF16)

    def head_sum(x):
        xh, xl = _split(x)
        s2 = _dot(jnp.concatenate([xh, xl], axis=0), ones_bd)
        return s2[:c] + s2[c:]

    t2 = lax.broadcasted_iota(jnp.int32, (2 * c, LANES), 0) % c
    j2 = lax.broadcasted_iota(jnp.int32, (2 * c, LANES), 1) % c
    diff = (t2 - j2) if d == 0 else (j2 - t2)
    upper = lax.broadcasted_iota(jnp.int32, (2 * c, LANES), 0) < c
    mask = diff >= jnp.where(upper, 1, 0)
    units = []
    for p in range(RW_HEADS // 2):
        sl = slice(p * LANES, (p + 1) * LANES)
        kkp = kk[:, sl]
        kkp = kkp * lax.rsqrt(jnp.maximum(head_sum(kkp * kkp), 1e-24))
        pinv = p_inv[:, sl]
        bt = a[:, sl] * kkp * pinv
        kt = kd[:, sl] * pinv
        ar = jnp.concatenate([-kkp * p_ex[:, sl], r[:, sl] * p_in[:, sl]], axis=0).astype(BF16)
        pe = p_end[:, sl]
        bonus = head_sum(r[:, sl] * kd[:, sl] * rk_ref[:, sl]) * v[:, sl]
        units.append(dict(d=d, p=p, ar=ar, bt=bt, kt=kt, v=v[:, sl], pe=pe,
                          bkp=jnp.concatenate([bt * pe, kt * pe], axis=0), bonus=bonus,
                          mask=mask, eye=(diff[:c] == 0).astype(F32), bd=same_head))
    return units, zs


def _bd(x):
    lo = lax.broadcasted_iota(jnp.int32, x.shape, 1) < RW_HEAD_DIM
    xb = x.astype(BF16)
    z = jnp.zeros_like(xb)
    return jnp.concatenate([jnp.where(lo, xb, z), jnp.where(lo, z, xb)], axis=0)


def _rwkv_solve(units, s_ref, fill=()):
    c = CHUNK

    def tick():
        for f in fill:
            next(f, None)

    for u in units:
        g = _dot(u['ar'], jnp.concatenate([_bd(u['bt']), _bd(u['kt'])], axis=0), _NT)
        gb = jnp.where(u['mask'], g[:, :LANES], 0.0)
        gk = g[:, LANES:]
        u['a_ab'], u['a_rb'] = gb[:c], gb[c:]
        u['s0'] = s_ref[u['d'], u['p']]
        lhs = jnp.concatenate([u['ar'], jnp.where(u['mask'], gk, 0.0).astype(BF16)], axis=1)
        u['xs'] = _dot(lhs, jnp.concatenate([u['s0'].T.astype(BF16), _bd(u['v'])], axis=0))
    tick()
    for u in units:
        u['pw'] = _mm(u['a_ab'], _bd(u['a_ab']))
        u['tinv'] = u['eye'] + u['a_ab']
    tick()
    n_sq = CHUNK.bit_length() - 2
    for i in range(n_sq):
        for u in units:
            if i < n_sq - 1:
                both = _mm(jnp.concatenate([u['pw'], u['tinv']], axis=0), _bd(u['pw']))
                u['pw'], u['tinv'] = both[:c], u['tinv'] + both[c:]
            else:
                u['tinv'] = u['tinv'] + _mm(u['tinv'], _bd(u['pw']))
        tick()
    for u in units:
        u['u'] = _mm(u['tinv'], _bd(u['xs'][:c]))
    tick()
    ys = []
    for u in units:
        ys.append(u['xs'][c:] + _mm(u['a_rb'], _bd(u['u'])) + u['bonus'])
        uv = jnp.concatenate([u['u'], u['v']], axis=0)
        s_ref[u['d'], u['p']] = u['s0'] * u['pe'] + jnp.where(u['bd'], _mm(uv.T, u['bkp']), 0.0)
    for f in fill:
        for _ in f:
            pass
    return ys


RW_SEQS_MAX = 2
NA_ROWS_PER_STEP = 2


def _rwkv_kernel(*refs, nc, has_s0, na_rows, n_seq):
    zf_ref, zfp_ref, zfn_ref, zr_ref, zrp_ref, zrn_ref = refs[:6]
    n_in = 6 + (1 if has_s0 else 0)
    s0_ref = refs[6] if has_s0 else None
    prm = refs[n_in:n_in + 10]
    g2_ref = refs[n_in + 10]
    n_in += 11
    i = pl.program_id(1)
    if na_rows:
        q_ref, k_ref, v_ref, ck_ref, cv_ref = refs[n_in:n_in + 5]
        bias_refs = refs[n_in + 5:n_in + 5 + NA_ROWS_PER_STEP]
        yf_ref, yb_ref, g_ref, sfin_ref, oa_ref, s_ref = refs[n_in + 5 + NA_ROWS_PER_STEP:]
    else:
        yf_ref, yb_ref, g_ref, sfin_ref, s_ref = refs[n_in:]

    hd = RW_HEAD_DIM
    npair = RW_HEADS // 2

    @pl.when(i == 0)
    def _():
        s_ref[...] = jnp.zeros(s_ref.shape, F32)
        if has_s0:
            for k in range(n_seq):
                for d in range(2):
                    for p in range(npair):
                        s_ref[2 * k + d, p, :hd, :hd] = s0_ref[k, d, 2 * p]
                        s_ref[2 * k + d, p, hd:, hd:] = s0_ref[k, d, 2 * p + 1]

    zero = jnp.zeros((1, B_COLS), F32)
    units = []
    for k in range(n_seq):
        pf = jnp.where(i == 0, zero, zfp_ref[k, 7:8, :])
        nf = jnp.where(i == nc - 1, zero, zfn_ref[k, 0:1, :])
        uf, zs = _rwkv_prepare(0, zf_ref[k], pf, nf, prm)
        g_ref[k] = _mm(_sigmoid(zs[:, B_COLS - RW_GATE_RANK:]), g2_ref[...])
        pr = jnp.where(i == nc - 1, zero, zrp_ref[k, 7:8, :])
        nr = jnp.where(i == 0, zero, zrn_ref[k, 0:1, :])
        ub, _ = _rwkv_prepare(1, zr_ref[k], pr, nr, prm)
        for u in uf + ub:
            u['d'] = 2 * k + u['d']
        units += uf + ub
    fill = []
    if na_rows:
        r0 = ((pl.program_id(0) * nc + i) * NA_ROWS_PER_STEP) % na_rows
        for k, b_ref in enumerate(bias_refs):
            rs = pl.ds(k * GRID_W, GRID_W)
            fill.append(_na_row_stages(q_ref.at[rs], k_ref, v_ref, ck_ref, cv_ref, b_ref, oa_ref.at[rs],
                                       r0 + k, na_rows))
    ys = _rwkv_solve(units, s_ref, fill)
    for k in range(n_seq):
        yk = ys[2 * npair * k:2 * npair * (k + 1)]
        yf_ref[k] = jnp.concatenate(yk[:npair], axis=-1)
        yb_ref[k] = jnp.concatenate(yk[npair:], axis=-1)

    @pl.when(i == nc - 1)
    def _():
        for k in range(n_seq):
            for d in range(2):
                for p in range(npair):
                    sfin_ref[k, d, 2 * p] = s_ref[2 * k + d, p, :hd, :hd]
                    sfin_ref[k, d, 2 * p + 1] = s_ref[2 * k + d, p, hd:, hd:]


def rwkv_params(mu, w0, w2, a0, a2, k_k, k_a, r_k):
    vec = lambda a: a.reshape(1, -1)
    mu3 = jnp.concatenate([mu, 1.0 - mu[0:1] - mu[1:2]], axis=0)
    return (mu3, w0) + _split(w2) + (a0,) + _split(a2) + (vec(k_k), vec(k_a), vec(r_k))


def rwkv_scan(zb, seq0, nseq, s0, prm, g2, na=None):
    seqlen = zb.shape[1]
    ns = RW_SEQS_MAX
    while seq0 % ns or nseq % ns or (na is not None and na[0].n_dec * (na[0].dec_len // GRID_W)
                                     != NA_ROWS_PER_STEP * (nseq // ns) * (seqlen // CHUNK)):
        ns //= 2
    assert ns >= 1 and seqlen % CHUNK == 0
    nc = seqlen // CHUNK
    sub = CHUNK // 8
    b0 = seq0 // ns
    has_s0 = s0 is not None

    def specs(cidx):
        return [pl.BlockSpec((ns, CHUNK, B_COLS), lambda s, i: (b0 + s, cidx(i), 0)),
                pl.BlockSpec((ns, 8, B_COLS), lambda s, i: (b0 + s, jnp.maximum(cidx(i) * sub - 1, 0), 0)),
                pl.BlockSpec((ns, 8, B_COLS),
                             lambda s, i: (b0 + s, jnp.minimum((cidx(i) + 1) * sub, seqlen // 8 - 1), 0))]

    cf = lambda i: i
    cr = lambda i: nc - 1 - i
    full = lambda a: pl.BlockSpec(a.shape, lambda s, i: (0,) * a.ndim)
    sshape = (2, RW_HEADS, RW_HEAD_DIM, RW_HEAD_DIM)
    sspec = pl.BlockSpec((ns,) + sshape, lambda s, i: (s, 0, 0, 0, 0))
    in_specs = specs(cf) + specs(cr)
    args = [zb] * 6
    if has_s0:
        in_specs.append(sspec)
        args.append(s0)
    in_specs += [full(p) for p in prm] + [full(g2)]
    args += list(prm) + [g2]
    out = lambda cidx: pl.BlockSpec((ns, CHUNK, RW_WIDTH), lambda s, i: (s, cidx(i), 0))
    tok = jax.ShapeDtypeStruct((nseq, seqlen, RW_WIDTH), F32)
    out_shape = [tok, tok, tok, jax.ShapeDtypeStruct((nseq,) + sshape, F32)]
    out_specs = [out(cf), out(cr), out(cf), sspec]
    na_rows = 0
    if na is not None:
        lay, q, kb, vb, ck, cv, bias_tab = na
        na_rows = lay.dec_len // GRID_W
        per = NA_ROWS_PER_STEP
        assert na_rows >= NA_WIN_ROWS and na_rows % per == 0 and lay.ctx_tokens % lay.dec_len == 0
        assert lay.n_dec * na_rows == per * (nseq // ns) * nc and lay.ctx_tokens % (per * GRID_W) == 0
        step = lambda s, i: s * nc + i
        batch = lambda s, i: step(s, i) * per // na_rows
        kv = pl.BlockSpec((lay.dec_len, NA_WIDTH), lambda s, i: (lay.ctx_tokens // lay.dec_len + batch(s, i), 0))
        cs = pl.BlockSpec((1, ck.shape[1], NA_WIDTH), lambda s, i: (batch(s, i), 0, 0))

        def bias_spec(k):
            def idx(s, i):
                r = step(s, i) * per % na_rows + k
                return (_na_row_start(r, na_rows) - r + NA_WIN_ROWS - 1, 0, 0, 0)
            return pl.BlockSpec((1, NA_HEADS, GRID_W, NA_WIN_ROWS * GRID_W), idx)

        in_specs += [pl.BlockSpec((per * GRID_W, NA_WIDTH),
                                  lambda s, i: (lay.ctx_tokens // (per * GRID_W) + step(s, i), 0)),
                     kv, kv, cs, cs] + [bias_spec(k) for k in range(per)]
        args += [q, kb, vb, ck, cv] + [bias_tab] * per
        out_shape.append(jax.ShapeDtypeStruct((lay.dec_tokens, NA_WIDTH), BF16))
        out_specs.append(pl.BlockSpec((per * GRID_W, NA_WIDTH), lambda s, i: (step(s, i), 0)))
    res = pl.pallas_call(
        functools.partial(_rwkv_kernel, nc=nc, has_s0=has_s0, na_rows=na_rows, n_seq=ns),
        out_shape=tuple(out_shape),
        grid=(nseq // ns, nc),
        in_specs=in_specs,
        out_specs=tuple(out_specs),
        scratch_shapes=[pltpu.VMEM((2 * ns, RW_HEADS // 2, LANES, LANES), F32)],
        compiler_params=_cparams(("arbitrary", "arbitrary")),
        name="rwkv_scan",
    )(*args)
    flat = lambda a: a.reshape(nseq * seqlen, RW_WIDTH)
    return (flat(res[0]), flat(res[1]), flat(res[2])) + tuple(res[3:])


def _conv_kernel(z_ref, zp_ref, zn_ref, w_ref, b_ref, g_ref, be_ref, o_ref, h_ref, hs_ref, *, lay):
    i = pl.program_id(0)
    first, last = lay.tile_pos(i)

    def glu(z):
        return z[:, :CONV_WIDTH] * _sigmoid(z[:, CONV_WIDTH:])

    hz = jnp.zeros((CONV_HALO, CONV_WIDTH), F32)
    h_ref[0:CONV_HALO, :] = jnp.where(first, hz, glu(zp_ref[...]))
    h_ref[CONV_HALO:CONV_HALO + TM, :] = glu(z_ref[...])
    h_ref[CONV_HALO + TM:, :] = jnp.where(last, hz, glu(zn_ref[...]))
    off = CONV_HALO - CONV_K // 2
    rows = hs_ref.shape[1]
    for ph in range(SUBLANES):
        hs_ref[ph] = h_ref[ph:ph + rows, :]
    acc = jnp.zeros((TM, CONV_WIDTH), F32) + b_ref[...]
    for j in range(CONV_K):
        ph, base = (off + j) % SUBLANES, (off + j) // SUBLANES * SUBLANES
        acc = acc + w_ref[j:j + 1, :] * hs_ref[ph, base:base + TM, :]
    xc = acc - jnp.mean(acc, axis=-1, keepdims=True)
    hn = xc * lax.rsqrt(jnp.mean(xc * xc, axis=-1, keepdims=True) + LN_EPS) * g_ref[...] + be_ref[...]
    o_ref[...] = (hn * _sigmoid(hn)).astype(BF16)


def conv_module(lay, zc, w, b, g, be):
    n = lay.n
    hb = TM // CONV_HALO
    vec = pl.BlockSpec((1, CONV_WIDTH), lambda i: (0, 0))
    return pl.pallas_call(
        functools.partial(_conv_kernel, lay=lay),
        out_shape=jax.ShapeDtypeStruct((n, CONV_WIDTH), BF16),
        grid=(lay.tiles,),
        in_specs=[pl.BlockSpec((TM, C_COLS), lambda i: (i, 0)),
                  pl.BlockSpec((CONV_HALO, C_COLS), lambda i: (jnp.maximum(i * hb - 1, 0), 0)),
                  pl.BlockSpec((CONV_HALO, C_COLS), lambda i: (jnp.minimum((i + 1) * hb, n // CONV_HALO - 1), 0)),
                  pl.BlockSpec((CONV_K, CONV_WIDTH), lambda i: (0, 0)), vec, vec, vec],
        out_specs=pl.BlockSpec((TM, CONV_WIDTH), lambda i: (i, 0)),
        scratch_shapes=[pltpu.VMEM((TM + 2 * CONV_HALO, CONV_WIDTH), F32),
                        pltpu.VMEM((SUBLANES, TM + 2 * CONV_HALO - SUBLANES, CONV_WIDTH), F32)],
        compiler_params=_cparams(("arbitrary",)),
        name="conv_module",
    )(zc, zc, zc, w, b, g, be)


def _merge_kernel(x_ref, oa_c_ref, yf_c_ref, yb_c_ref, g_c_ref, oa_d_ref, yf_d_ref, yb_d_ref, g_d_ref,
                  oc_ref, gt_ref, mod_ref, ng_ref, lg_ref, lb_ref,
                  wa_ref, wr_ref, wc_ref, wo_ref, rw_ref, rb_ref,
                  x1_ref, hm_ref, ti_ref, tg_ref, rk_ref, cnt_ref, *, ctx_tiles):
    is_ctx = pl.program_id(0) < ctx_tiles
    pick = lambda c_ref, d_ref: jnp.where(is_ctx, c_ref[...], d_ref[...])
    oa = pick(oa_c_ref, oa_d_ref)
    g_rw = pick(g_c_ref, g_d_ref)
    y = pick(yf_c_ref, yf_d_ref) + pick(yb_c_ref, yb_d_ref)
    pi = lax.broadcasted_iota(jnp.int32, (LANES, LANES), 0)
    pj = lax.broadcasted_iota(jnp.int32, (LANES, LANES), 1)
    ones_bd = ((pi // RW_HEAD_DIM) == (pj // RW_HEAD_DIM)).astype(BF16)

    def head_mean(v):
        vh, vl = _split(v)
        s2 = _dot(jnp.concatenate([vh, vl], axis=0), ones_bd)
        return (s2[:TM] + s2[TM:]) * (1.0 / RW_HEAD_DIM)

    outs = []
    for p in range(RW_HEADS // 2):
        yp = y[:, p * LANES:(p + 1) * LANES]
        yc = yp - head_mean(yp)
        outs.append(yc * lax.rsqrt(head_mean(yc * yc) + GN_EPS))
    o_rw = (jnp.concatenate(outs, axis=-1) * lg_ref[...] + lb_ref[...]) * g_rw
    merged = (gt_ref[:, :D_MODEL].astype(F32) * _dot(oa, wa_ref[...])
              + gt_ref[:, D_MODEL:2 * D_MODEL].astype(F32) * _mm(o_rw, wr_ref[...])
              + gt_ref[:, 2 * D_MODEL:].astype(F32) * _dot(oc_ref[...], wc_ref[...]))
    mix = _mm(merged, wo_ref[...])
    x1 = x_ref[...] + mod_ref[0, 2:3, :] * _rms(mix, ng_ref[1:2, :])
    x1_ref[...] = x1
    hm = _rms(x1, ng_ref[2:3, :]) * (1.0 + mod_ref[0, 4:5, :]) + mod_ref[0, 3:4, :]
    hm_ref[...] = hm.astype(BF16)
    logits = _mm3(hm, rw_ref[...]) + rb_ref[...]
    lane = lax.broadcasted_iota(jnp.int32, logits.shape, 1)
    idx_out = jnp.zeros(logits.shape, jnp.int32)
    val_out = jnp.full(logits.shape, NEG_INF, F32)
    picks = []
    for j in range(TOP_K):
        m = jnp.max(logits, axis=-1, keepdims=True)
        idx = jnp.min(jnp.where(logits == m, lane, LANES), axis=-1, keepdims=True)
        idx_out = jnp.where(lane == j, idx, idx_out)
        val_out = jnp.where(lane == j, m, val_out)
        picks.append(lane == idx)
        logits = jnp.where(picks[-1], -jnp.inf, logits)
    e = jnp.exp(val_out - jnp.max(val_out, axis=-1, keepdims=True))
    ti_ref[...] = idx_out
    tg_ref[...] = e / jnp.sum(e, axis=-1, keepdims=True)

    chosen = jnp.zeros(logits.shape, F32)
    for pk in picks:
        chosen = chosen + pk.astype(F32)
    ti_ = lax.broadcasted_iota(jnp.int32, (TM, TM), 0)
    tj_ = lax.broadcasted_iota(jnp.int32, (TM, TM), 1)
    before = _dot((ti_ > tj_).astype(BF16), chosen.astype(BF16))
    rank = jnp.zeros(logits.shape, F32)
    for j, pk in enumerate(picks):
        rank = jnp.where(lane == j, jnp.sum(jnp.where(pk, before, 0.0), axis=-1, keepdims=True), rank)
    rk_ref[...] = rank.astype(jnp.int32)
    cnt_ref[0] = jnp.sum(chosen, axis=0, keepdims=True)


def merge_router(lay, x, ctx_br, dec_br, o_cv, gates, mod, ng, lg, lb, wa, wr, wc, wo, rw, rb):
    n = lay.n
    row = lambda w: pl.BlockSpec((TM, w), lambda i: (i, 0))
    crow = lambda w: pl.BlockSpec((TM, w), lambda i: (jnp.minimum(i, lay.ctx_tiles - 1), 0))
    drow = lambda w: pl.BlockSpec((TM, w), lambda i: (jnp.maximum(i - lay.ctx_tiles, 0), 0))
    full = lambda a: pl.BlockSpec(a.shape, lambda i: (0,) * a.ndim)
    br_w = (NA_WIDTH, RW_WIDTH, RW_WIDTH, RW_WIDTH)
    return pl.pallas_call(
        functools.partial(_merge_kernel, ctx_tiles=lay.ctx_tiles),
        out_shape=(jax.ShapeDtypeStruct((n, D_MODEL), F32), jax.ShapeDtypeStruct((n, D_MODEL), BF16),
                   jax.ShapeDtypeStruct((n, LANES), jnp.int32), jax.ShapeDtypeStruct((n, LANES), F32),
                   jax.ShapeDtypeStruct((n, LANES), jnp.int32), jax.ShapeDtypeStruct((lay.tiles, 1, LANES), F32)),
        grid=(lay.tiles,),
        in_specs=[row(D_MODEL)] + [crow(w) for w in br_w] + [drow(w) for w in br_w] + [
                  row(CONV_WIDTH),
                  row(G_COLS), pl.BlockSpec((1, N_MOD, D_MODEL), lambda i: (lay.mod_row(i), 0, 0)),
                  full(ng), full(lg), full(lb), full(wa), full(wr), full(wc), full(wo), full(rw), full(rb)],
        out_specs=(row(D_MODEL), row(D_MODEL), row(LANES), row(LANES), row(LANES),
                   pl.BlockSpec((1, 1, LANES), lambda i: (i, 0, 0))),
        compiler_params=_cparams(("arbitrary",)),
        name="merge_router",
    )(x, *ctx_br, *dec_br, o_cv, gates, mod, ng, lg, lb, wa, wr, wc, wo, rw, rb)


RUN_ALIGN = 8
RUN_BITS = (TM // RUN_ALIGN).bit_length()
SORT_ROWS = -(-(TM * TOP_K + N_EXPERTS * (RUN_ALIGN - 1)) // LANES) * LANES
PIECE_BASE = 8
TABLE_LEN = 512


def route_tables(top_i, wrank, cnt, n):
    tiles = n // TM
    cnt = cnt[:, 0, :N_EXPERTS].astype(jnp.int32)
    run = ((cnt + RUN_ALIGN - 1) // RUN_ALIGN) * RUN_ALIGN
    off = jnp.cumsum(run, axis=1) - run
    before = jnp.cumsum(run, axis=0) - run
    region = jnp.sum(run, axis=0)
    padded = ((region + ROUTE_BLOCK - 1) // ROUTE_BLOCK) * ROUTE_BLOCK
    pad_end = jnp.cumsum(padded)
    first = (pad_end - padded)[None, :] + before
    k_bits = jnp.arange(RUN_BITS, dtype=jnp.int32)
    size = RUN_ALIGN << k_bits
    has = (run[:, :, None] & size) != 0
    done = run[:, :, None] & (-2 * size)
    slot = jnp.cumsum(has.astype(jnp.int32), axis=1) - 1
    sel = has[:, None] & (slot[:, None] == jnp.arange(N_EXPERTS, dtype=jnp.int32)[None, :, None, None])
    pick = lambda v: jnp.sum(jnp.where(sel, v[:, None], 0), axis=2)
    pieces = jnp.stack([pick(off[:, :, None] + done), pick(first[:, :, None] + done)], axis=-1)
    pieces = pieces.transpose(0, 2, 1, 3).reshape(tiles, RUN_BITS * N_EXPERTS * 2)
    counts = jnp.sum(has.astype(jnp.int32), axis=1)
    table = jnp.concatenate([counts, jnp.zeros((tiles, PIECE_BASE - RUN_BITS), jnp.int32), pieces,
                             jnp.zeros((tiles, TABLE_LEN - PIECE_BASE - pieces.shape[1]), jnp.int32)], axis=1)
    onehot = top_i.reshape(tiles, TM, TOP_K, 1) == jnp.arange(N_EXPERTS, dtype=jnp.int32)
    pos = jnp.sum(jnp.where(onehot, off[:, None, None, :], 0), axis=-1) + wrank.reshape(tiles, TM, TOP_K)
    n_blocks = -(-tiles * (TM * TOP_K + N_EXPERTS * (RUN_ALIGN - 1)) // ROUTE_BLOCK) + N_EXPERTS
    starts = jnp.arange(n_blocks, dtype=jnp.int32) * ROUTE_BLOCK
    block_e = jnp.minimum(jnp.sum((pad_end[None, :] <= starts[:, None]).astype(jnp.int32), axis=1),
                          N_EXPERTS - 1)
    n_used = (pad_end[-1] // ROUTE_BLOCK).astype(jnp.int32).reshape(1)
    return table.reshape(tiles, 1, TABLE_LEN), pos, block_e, n_used, n_blocks


def _for_each_piece(tab_ref, fn):
    for k in range(RUN_BITS):
        def body(i, carry, k=k):
            at = PIECE_BASE + (k * N_EXPERTS + i) * 2
            fn(pl.multiple_of(tab_ref[0, 0, at], RUN_ALIGN), pl.multiple_of(tab_ref[0, 0, at + 1], RUN_ALIGN),
               RUN_ALIGN << k)
            return carry

        lax.fori_loop(0, tab_ref[0, 0, k], body, 0)


def _dispatch_kernel(tab_ref, tabp_ref, pos_ref, hm_ref, xb_in_ref, xb_ref, srt_ref, sems):
    del xb_in_ref
    i = pl.program_id(0)
    slot = i % 2

    def copy(slot_, off, row, size):
        return pltpu.make_async_copy(srt_ref.at[slot_, pl.ds(off, size), :], xb_ref.at[pl.ds(row, size), :],
                                     sems.at[slot_])

    p_iota = lax.broadcasted_iota(jnp.int32, (SORT_ROWS, TM), 0)
    hit = p_iota == pos_ref[0, 0:1, :]
    for j in range(1, TOP_K):
        hit = hit | (p_iota == pos_ref[0, j:j + 1, :])
    srt_ref[slot] = _dot(jnp.where(hit, 1.0, 0.0).astype(BF16), hm_ref[...])
    _for_each_piece(tab_ref, lambda off, row, size: copy(slot, off, row, size).start())

    @pl.when(i > 0)
    def _():
        _for_each_piece(tabp_ref, lambda off, row, size: copy(1 - slot, off, row, size).wait())

    @pl.when(i == pl.num_programs(0) - 1)
    def _():
        _for_each_piece(tab_ref, lambda off, row, size: copy(slot, off, row, size).wait())


def moe_dispatch(lay, hm, table, pos, xb0):
    n_rows = xb0.shape[0]
    post = jnp.pad(pos.transpose(0, 2, 1), ((0, 0), (0, 8 - TOP_K), (0, 0)))
    tab = lambda f: pl.BlockSpec((1, 1, TABLE_LEN), lambda i: (f(i), 0, 0), memory_space=pltpu.SMEM)
    return pl.pallas_call(
        _dispatch_kernel,
        out_shape=jax.ShapeDtypeStruct((n_rows, D_MODEL), F32),
        grid=(lay.tiles,),
        in_specs=[tab(lambda i: i), tab(lambda i: jnp.maximum(i - 1, 0)),
                  pl.BlockSpec((1, 8, TM), lambda i: (i, 0, 0)),
                  pl.BlockSpec((TM, D_MODEL), lambda i: (i, 0)),
                  pl.BlockSpec(memory_space=pl.ANY)],
        out_specs=pl.BlockSpec(memory_space=pl.ANY),
        scratch_shapes=[pltpu.VMEM((2, SORT_ROWS, D_MODEL), F32), pltpu.SemaphoreType.DMA((2,))],
        input_output_aliases={4: 0},
        compiler_params=_cparams(("arbitrary",)),
        name="moe_dispatch",
    )(table, table, post, hm, xb0)


GL_GROUP = 2 * LANES


def _regroup_perm():
    r = jnp.arange(GL_GROUP, dtype=jnp.int32)[:, None]
    c = jnp.arange(GL_GROUP, dtype=jnp.int32)[None, :]
    src = jnp.where(c < LANES, 2 * c, 2 * (c - LANES) + 1)
    return (r == src).astype(BF16)


def _expert_kernel(be_ref, nu_ref, x_ref, w1_ref, perm_ref, b1_ref, w2_ref, b2_ref, y_ref, w1b_ref, w2b_ref):
    b = pl.program_id(0)
    used = b < nu_ref[0]

    @pl.when(used & ((b == 0) | (be_ref[b] != be_ref[jnp.maximum(b - 1, 0)])))
    def _():
        for j in range(2 * EXPERT_FF // GL_GROUP):
            cs = slice(j * GL_GROUP, (j + 1) * GL_GROUP)
            w1b_ref[:, cs] = _dot(w1_ref[:, cs].astype(BF16), perm_ref[...]).astype(BF16)
        w2b_ref[...] = w2_ref[...].astype(BF16)

    @pl.when(used)
    def _():
        x = x_ref[...].astype(BF16)
        n_grp = 2 * EXPERT_FF // GL_GROUP
        hs = [_dot(x, w1b_ref[:, g * GL_GROUP:(g + 1) * GL_GROUP]) + b1_ref[0, :, g * GL_GROUP:(g + 1) * GL_GROUP]
              for g in range(n_grp)]
        acts = []
        for h in hs:
            hg = jnp.minimum(h[:, :LANES], SWIGLU_LIMIT)
            hl = jnp.clip(h[:, LANES:], -SWIGLU_LIMIT, SWIGLU_LIMIT)
            acts.append((hg * _sigmoid(SWIGLU_ALPHA * hg) * (hl + 1.0)).astype(BF16))
        acc = jnp.zeros((ROUTE_BLOCK, D_MODEL), F32) + b2_ref[0]
        for j in range(n_grp // 2):
            act = jnp.concatenate(acts[2 * j:2 * j + 2], axis=-1)
            acc = acc + _dot(act, w2b_ref[j * GL_GROUP:(j + 1) * GL_GROUP, :])
        y_ref[...] = acc

    @pl.when(jnp.logical_not(used))
    def _():
        y_ref[...] = jnp.zeros(y_ref.shape, F32)


def moe_experts(xb, block_e, n_used, n_blocks, l, w1, b1, w2, b2):
    def last_used(b, nu):
        return jnp.minimum(b, jnp.maximum(nu[0] - 1, 0))

    wspec = lambda a: pl.BlockSpec((None, None) + a.shape[2:], lambda b, be, nu: (l, be[last_used(b, nu)], 0, 0))
    bspec = lambda a: pl.BlockSpec((1,) + a.shape[1:], lambda b, be, nu: (be[last_used(b, nu)], 0, 0))
    return pl.pallas_call(
        _expert_kernel,
        out_shape=jax.ShapeDtypeStruct(xb.shape, F32),
        grid_spec=pltpu.PrefetchScalarGridSpec(
            num_scalar_prefetch=2,
            grid=(n_blocks,),
            in_specs=[pl.BlockSpec((ROUTE_BLOCK, D_MODEL), lambda b, be, nu: (last_used(b, nu), 0)),
                      wspec(w1), pl.BlockSpec((GL_GROUP, GL_GROUP), lambda b, be, nu: (0, 0)),
                      bspec(b1), wspec(w2), bspec(b2)],
            out_specs=pl.BlockSpec((ROUTE_BLOCK, D_MODEL), lambda b, be, nu: (b, 0)),
            scratch_shapes=[pltpu.VMEM(w1.shape[2:], BF16), pltpu.VMEM(w2.shape[2:], BF16)]),
        compiler_params=_cparams(("arbitrary",)),
        name="moe_experts",
    )(block_e, n_used, xb, w1, _regroup_perm(), b1, w2, b2)


def _combine_kernel(tab_ref, tabn_ref, yb_ref, pos_ref, tg_ref, x1_ref, mod_ref, ng_ref, *rest, ctx_tiles):
    o_refs, (stg_ref, sems) = rest[:-2], rest[-2:]
    i = pl.program_id(0)
    last = pl.num_programs(0) - 1
    slot = i % 2

    def copy(slot_, off, row, size):
        return pltpu.make_async_copy(yb_ref.at[pl.ds(row, size), :], stg_ref.at[slot_, pl.ds(off, size), :],
                                     sems.at[slot_])

    @pl.when(i == 0)
    def _():
        stg_ref[...] = jnp.zeros(stg_ref.shape, F32)
        _for_each_piece(tab_ref, lambda off, row, size: copy(slot, off, row, size).start())

    @pl.when(i < last)
    def _():
        _for_each_piece(tabn_ref, lambda off, row, size: copy(1 - slot, off, row, size).start())

    _for_each_piece(tab_ref, lambda off, row, size: copy(slot, off, row, size).wait())
    p_iota = lax.broadcasted_iota(jnp.int32, (TM, SORT_ROWS), 1)
    w = jnp.zeros((TM, SORT_ROWS), F32)
    for j in range(TOP_K):
        w = w + jnp.where(p_iota == pos_ref[:, j:j + 1], tg_ref[:, j:j + 1], 0.0)
    y = _mm3(w, stg_ref[slot])
    out = x1_ref[...] + mod_ref[0, 5:6, :] * _rms(y, ng_ref[3:4, :])
    if len(o_refs) == 1:
        o_refs[0][...] = out
    else:
        @pl.when(i < ctx_tiles)
        def _():
            o_refs[0][...] = out

        o_refs[1][...] = out


def moe_combine(lay, yb, table, pos, tg, x1, mod, ng, split):
    n = lay.n
    posl = jnp.pad(pos.reshape(n, TOP_K), ((0, 0), (0, LANES - TOP_K)))
    row = lambda w: pl.BlockSpec((TM, w), lambda i: (i, 0))
    tab = lambda f: pl.BlockSpec((1, 1, TABLE_LEN), lambda i: (f(i), 0, 0), memory_space=pltpu.SMEM)
    if split:
        out_shape = (jax.ShapeDtypeStruct((lay.ctx_tokens, D_MODEL), F32),
                     jax.ShapeDtypeStruct((lay.dec_tokens, D_MODEL), F32))
        out_specs = (pl.BlockSpec((TM, D_MODEL), lambda i: (jnp.minimum(i, lay.ctx_tiles - 1), 0)),
                     pl.BlockSpec((TM, D_MODEL), lambda i: (jnp.maximum(i - lay.ctx_tiles, 0), 0)))
    else:
        out_shape = jax.ShapeDtypeStruct((n, D_MODEL), F32)
        out_specs = row(D_MODEL)
    return pl.pallas_call(
        functools.partial(_combine_kernel, ctx_tiles=lay.ctx_tiles),
        out_shape=out_shape,
        grid=(lay.tiles,),
        in_specs=[tab(lambda i: i), tab(lambda i: jnp.minimum(i + 1, lay.tiles - 1)),
                  pl.BlockSpec(memory_space=pl.ANY),
                  row(LANES), row(LANES), row(D_MODEL),
                  pl.BlockSpec((1, N_MOD, D_MODEL), lambda i: (lay.mod_row(i), 0, 0)),
                  pl.BlockSpec(ng.shape, lambda i: (0, 0))],
        out_specs=out_specs,
        scratch_shapes=[pltpu.VMEM((2, SORT_ROWS, D_MODEL), F32), pltpu.SemaphoreType.DMA((2,))],
        compiler_params=_cparams(("arbitrary",)),
        name="moe_combine",
    )(table, table, yb, posl, tg, x1, mod, ng)


def layer(lay, x, l, cond, P, ctx_k, ctx_v, s0_dec, xb_buf):
    mod = mod_table(cond, P['mod_w'][l], P['mod_b'][l]).reshape(cond.shape[0], N_MOD, D_MODEL)
    ng = P['norm_g'][l]
    q, kf, vf, kb, vb, zb, zc, gates = in_proj(lay, x, mod, ng[0:1], P['w_in'][l].astype(BF16),
                                               P['b_in'][l].reshape(1, P_IN))
    oa_c = ctx_attention(lay, q, kb, vb)
    vec = lambda a: a.reshape(1, -1)
    prm = rwkv_params(P['rw_mu'][l], P['rw_w0'][l], P['rw_w2'][l], P['rw_a0'][l], P['rw_a2'][l],
                      P['rw_k_k'][l], P['rw_k_a'][l], P['rw_r_k'][l])
    yf_c, yb_c, g_c, s_ctx = rwkv_scan(zb.reshape(-1, lay.ctx_len, B_COLS), 0, lay.n_ctx, None, prm,
                                       P['rw_g2'][l])
    na = (lay, q, kb, vb, ctx_k.astype(BF16), ctx_v.astype(BF16), na_bias_table(P['rpb'][l]))
    yf_d, yb_d, g_d, _, oa_d = rwkv_scan(zb.reshape(-1, lay.dec_len, B_COLS), lay.ctx_tokens // lay.dec_len,
                                         lay.n_dec, s0_dec, prm, P['rw_g2'][l], na=na)
    o_cv = conv_module(lay, zc, P['conv_w'][l], vec(P['conv_b'][l]), vec(P['conv_ln_g'][l]),
                       vec(P['conv_ln_b'][l]))
    rw = jnp.zeros((D_MODEL, LANES), F32).at[:, :N_EXPERTS].set(P['router_w'][l])
    rb = jnp.full((1, LANES), NEG_INF, F32).at[0, :N_EXPERTS].set(P['router_b'][l])
    x1, hm, top_i, top_g, rank, counts = merge_router(
        lay, x, (oa_c, yf_c, yb_c, g_c), (oa_d, yf_d, yb_d, g_d), o_cv, gates, mod, ng,
        vec(P['rw_lnx_g'][l]), vec(P['rw_lnx_b'][l]),
        P['w_o_attn'][l].astype(BF16), P['w_o_rwkv'][l].astype(BF16), P['w_o_conv'][l].astype(BF16),
        P['w_out'][l].astype(BF16), rw, rb)
    table, pos, block_e, n_used, n_blocks = route_tables(top_i[:, :TOP_K], rank[:, :TOP_K], counts, lay.n)
    if xb_buf is None:
        xb_buf = jnp.zeros((n_blocks * ROUTE_BLOCK, D_MODEL), F32)
    xb = moe_dispatch(lay, hm, table, pos, xb_buf)
    b1 = P['exp_b1'][l].reshape(N_EXPERTS, -1, LANES, 2).transpose(0, 1, 3, 2).reshape(N_EXPERTS, 1, -1)
    ybk = moe_experts(xb, block_e, n_used, n_blocks, l, P['exp_w1'], b1, P['exp_w2'], P['exp_b2'][l][:, None, :])
    x2 = moe_combine(lay, ybk, table, pos, top_g, x1, mod, ng, split=(l == DEPTH - 1))
    return x2, kf, vf, s_ctx, xb


def kernel(x_prompt, x_sample, cache_attn_k, cache_attn_v, state_rwkv, c, c_ctx, mod_w, mod_b, norm_g, w_in, b_in, rpb, w_o_attn, rw_mu, rw_w0, rw_w2, rw_a0, rw_a2, rw_g2, rw_k_k, rw_k_a, rw_r_k, rw_lnx_g, rw_lnx_b, w_o_rwkv, conv_w, conv_b, conv_ln_g, conv_ln_b, w_o_conv, w_out, router_w, router_b, exp_w1, exp_b1, exp_w2, exp_b2):
    P = {
        'mod_w': mod_w, 'mod_b': mod_b, 'norm_g': norm_g, 'w_in': w_in, 'b_in': b_in, 'rpb': rpb,
        'w_o_attn': w_o_attn, 'rw_mu': rw_mu, 'rw_w0': rw_w0, 'rw_w2': rw_w2, 'rw_a0': rw_a0,
        'rw_a2': rw_a2, 'rw_g2': rw_g2, 'rw_k_k': rw_k_k, 'rw_k_a': rw_k_a, 'rw_r_k': rw_r_k,
        'rw_lnx_g': rw_lnx_g, 'rw_lnx_b': rw_lnx_b, 'w_o_rwkv': w_o_rwkv, 'conv_w': conv_w,
        'conv_b': conv_b, 'conv_ln_g': conv_ln_g, 'conv_ln_b': conv_ln_b, 'w_o_conv': w_o_conv,
        'w_out': w_out, 'router_w': router_w, 'router_b': router_b, 'exp_w1': exp_w1,
        'exp_b1': exp_b1, 'exp_w2': exp_w2, 'exp_b2': exp_b2,
    }
    nb, sl, _ = x_prompt.shape
    db, ds, _ = x_sample.shape
    lay = Layout(nb, sl, db, ds)
    x = jnp.concatenate([x_prompt.reshape(-1, D_MODEL), x_sample.reshape(-1, D_MODEL)], axis=0)
    cond = jnp.zeros((8, D_MODEL), F32).at[0].set(c_ctx).at[1:1 + db].set(c)
    ks, vs, ss = [], [], []
    xb_buf = None
    for l in range(DEPTH):
        ck = cache_attn_k[:, l].reshape(db, -1, NA_WIDTH)
        cv = cache_attn_v[:, l].reshape(db, -1, NA_WIDTH)
        x, kf, vf, s_ctx, xb_buf = layer(lay, x, l, cond, P, ck, cv, state_rwkv[:, l], xb_buf)
        ks.append(kf.reshape(nb, sl, NA_HEADS, NA_HEAD_DIM))
        vs.append(vf.reshape(nb, sl, NA_HEADS, NA_HEAD_DIM))
        ss.append(s_ctx)
    y_prompt = x[0].reshape(nb, sl, D_MODEL)
    y_sample = x[1].reshape(db, ds, D_MODEL)
    return (y_prompt, y_sample, jnp.stack(ks, axis=1), jnp.stack(vs, axis=1), jnp.stack(ss, axis=1))
```

```python
import functools

import jax
import jax.numpy as jnp
from jax import lax
from jax.experimental import pallas as pl
from jax.experimental.pallas import tpu as pltpu

F32 = jnp.float32
BF16 = jnp.bfloat16

D_MODEL = 1024
BATCH, SEQ = 32, 256
DEPTH = 2
DEC_BATCH, DEC_SEQ = 4, 4096
PAST_LEN = 256
GRID_W = 64
NA_HEADS, NA_HEAD_DIM = 8, 64
NA_WIDTH = NA_HEADS * NA_HEAD_DIM
NA_WIN_ROWS, NA_WIN_COLS = 8, 16
RW_HEADS, RW_HEAD_DIM = 8, 64
RW_WIDTH = RW_HEADS * RW_HEAD_DIM
RW_DECAY_RANK, RW_ICLR_RANK, RW_GATE_RANK = 64, 64, 128
CONV_WIDTH, CONV_K = 512, 31
N_BRANCH, N_MOD = 3, 6
N_EXPERTS, TOP_K, EXPERT_FF = 32, 4, 1024
SWIGLU_LIMIT, SWIGLU_ALPHA = 7.0, 1.702
ROUTE_BLOCK = 512
RMS_EPS, LN_EPS, GN_EPS = 1e-6, 1e-5, 64e-5
NEG_INF = -1e30
A_COLS = 3 * NA_WIDTH
B_COLS = 3 * RW_WIDTH + 2 * RW_DECAY_RANK + 2 * RW_ICLR_RANK + RW_GATE_RANK
C_COLS = 2 * CONV_WIDTH
G_COLS = N_BRANCH * D_MODEL
P_IN = A_COLS + B_COLS + C_COLS + G_COLS

LANES = 128
SUBLANES = 8
TM = 256
CHUNK = 64
CONV_HALO = 16
VMEM_LIMIT = 56 * 1024 * 1024


def _cparams(sem):
    return pltpu.CompilerParams(dimension_semantics=sem, vmem_limit_bytes=VMEM_LIMIT)


def _dot(a, b, dims=((1,), (0,))):
    return lax.dot_general(a, b, (dims, ((), ())), preferred_element_type=F32)


def _mm(a, b, dims=((1,), (0,))):
    return _dot(a.astype(BF16), b.astype(BF16), dims)


def _split(a):
    hi = a.astype(BF16)
    lo = (a - hi.astype(F32)).astype(BF16)
    return hi, lo


def _mm3(a, b, dims=((1,), (0,))):
    ah, al = _split(a)
    bh, bl = _split(b)
    return _dot(ah, bh, dims) + (_dot(ah, bl, dims) + _dot(al, bh, dims))


def _mm4(a, b_hi_lo):
    m, n = a.shape[0], b_hi_lo[0].shape[1]
    out = _dot(jnp.concatenate(_split(a), axis=0), jnp.concatenate(b_hi_lo, axis=1))
    return (out[:m, :n] + out[:m, n:]) + (out[m:, :n] + out[m:, n:])


_NT = ((1,), (1,))


def _rms(x, g):
    return x * lax.rsqrt(jnp.mean(x * x, axis=-1, keepdims=True) + RMS_EPS) * g


def _sigmoid(x):
    return 1.0 / (1.0 + jnp.exp(-x))


class Layout:
    def __init__(self, n_ctx, ctx_len, n_dec, dec_len):
        assert ctx_len == TM and dec_len % TM == 0
        self.n_ctx, self.ctx_len, self.n_dec, self.dec_len = n_ctx, ctx_len, n_dec, dec_len
        self.ctx_tokens = n_ctx * ctx_len
        self.dec_tokens = n_dec * dec_len
        self.n = self.ctx_tokens + self.dec_tokens
        self.ctx_tiles = self.ctx_tokens // TM
        self.dec_tiles_per_seq = dec_len // TM
        self.tiles = self.n // TM

    def mod_row(self, i):
        return jnp.where(i < self.ctx_tiles, 0, 1 + (i - self.ctx_tiles) // self.dec_tiles_per_seq)

    def tile_pos(self, i):
        j = (i - self.ctx_tiles) % self.dec_tiles_per_seq
        is_ctx = i < self.ctx_tiles
        return is_ctx | (j == 0), is_ctx | (j == self.dec_tiles_per_seq - 1)


def _mod_kernel(c_ref, w_ref, b_ref, o_ref):
    c = c_ref[...]
    o_ref[...] = _mm3(c * _sigmoid(c), w_ref[...]) + b_ref[...]


def mod_table(cond, w, b):
    tn = 1024
    nm = w.shape[1]
    return pl.pallas_call(
        _mod_kernel,
        out_shape=jax.ShapeDtypeStruct((cond.shape[0], nm), F32),
        grid=(nm // tn,),
        in_specs=[pl.BlockSpec(cond.shape, lambda j: (0, 0)),
                  pl.BlockSpec((D_MODEL, tn), lambda j: (0, j)),
                  pl.BlockSpec((1, tn), lambda j: (0, j))],
        out_specs=pl.BlockSpec((cond.shape[0], tn), lambda j: (0, j)),
        compiler_params=_cparams(("arbitrary",)),
        name="mod_table",
    )(cond, w, b.reshape(1, nm))


def _inproj_kernel(x_ref, mod_ref, g_ref, w_ref, b_ref,
                   q_ref, kf_ref, vf_ref, kb_ref, vb_ref, zb_ref, zc_ref, gt_ref, *, ctx_tiles):
    u = (_rms(x_ref[...], g_ref[...]) * (1.0 + mod_ref[0, 1:2, :]) + mod_ref[0, 0:1, :]).astype(BF16)

    def seg(c0, width):
        return _dot(u, w_ref[:, c0:c0 + width]) + b_ref[:, c0:c0 + width]

    q_ref[...] = seg(0, NA_WIDTH).astype(BF16)
    k = seg(NA_WIDTH, NA_WIDTH)
    kb_ref[...] = k.astype(BF16)
    v = seg(2 * NA_WIDTH, NA_WIDTH)
    vb_ref[...] = v.astype(BF16)

    @pl.when(pl.program_id(0) < ctx_tiles)
    def _():
        kf_ref[...] = k
        vf_ref[...] = v
    wb = 384
    for j in range(B_COLS // wb):
        zb_ref[:, j * wb:(j + 1) * wb] = seg(A_COLS + j * wb, wb)
    wc = 512
    for j in range(C_COLS // wc):
        zc_ref[:, j * wc:(j + 1) * wc] = seg(A_COLS + B_COLS + j * wc, wc)
    for j in range(G_COLS // wc):
        gt_ref[:, j * wc:(j + 1) * wc] = _sigmoid(seg(A_COLS + B_COLS + C_COLS + j * wc, wc)).astype(BF16)


def in_proj(lay, x, mod, g0, w_bf, b):
    n = lay.n
    row = lambda w: pl.BlockSpec((TM, w), lambda i: (i, 0))
    crow = pl.BlockSpec((TM, NA_WIDTH), lambda i: (jnp.minimum(i, lay.ctx_tiles - 1), 0))
    ctx_f32 = jax.ShapeDtypeStruct((lay.ctx_tokens, NA_WIDTH), F32)
    out_shape = (jax.ShapeDtypeStruct((n, NA_WIDTH), BF16), ctx_f32, ctx_f32,
                 jax.ShapeDtypeStruct((n, NA_WIDTH), BF16), jax.ShapeDtypeStruct((n, NA_WIDTH), BF16),
                 jax.ShapeDtypeStruct((n, B_COLS), F32), jax.ShapeDtypeStruct((n, C_COLS), F32),
                 jax.ShapeDtypeStruct((n, G_COLS), BF16))
    return pl.pallas_call(
        functools.partial(_inproj_kernel, ctx_tiles=lay.ctx_tiles),
        out_shape=out_shape,
        grid=(lay.tiles,),
        in_specs=[row(D_MODEL),
                  pl.BlockSpec((1, N_MOD, D_MODEL), lambda i: (lay.mod_row(i), 0, 0)),
                  pl.BlockSpec((1, D_MODEL), lambda i: (0, 0)),
                  pl.BlockSpec((D_MODEL, P_IN), lambda i: (0, 0), pipeline_mode=pl.Buffered(1)),
                  pl.BlockSpec((1, P_IN), lambda i: (0, 0))],
        out_specs=(row(NA_WIDTH), crow, crow, row(NA_WIDTH), row(NA_WIDTH),
                   row(B_COLS), row(C_COLS), row(G_COLS)),
        compiler_params=_cparams(("arbitrary",)),
        name="in_proj",
    )(x, mod, g0, w_bf, b)


def _ctx_attn_kernel(q_ref, k_ref, v_ref, o_ref):
    scale = NA_HEAD_DIM ** -0.5
    lo = lax.broadcasted_iota(jnp.int32, (q_ref.shape[0], LANES), 1) < NA_HEAD_DIM
    heads = []
    for h in range(NA_HEADS):
        ps = slice((h // 2) * LANES, (h // 2 + 1) * LANES)
        qp = q_ref[:, ps] * scale
        qh = jnp.where(lo if h % 2 == 0 else jnp.logical_not(lo), qp, jnp.zeros_like(qp))
        heads.append(dict(ps=ps, s=_dot(qh, k_ref[:, ps], _NT)))
    for u in heads:
        u['p'] = jnp.exp(u['s'] - jnp.max(u['s'], axis=-1, keepdims=True))
    for u in heads:
        u['o'] = _dot(u['p'].astype(BF16), v_ref[:, u['ps']]) / jnp.sum(u['p'], axis=-1, keepdims=True)
    for p in range(NA_HEADS // 2):
        o_ref[:, p * LANES:(p + 1) * LANES] = jnp.where(lo, heads[2 * p]['o'], heads[2 * p + 1]['o']).astype(BF16)


def ctx_attention(lay, q, kb, vb):
    blk = pl.BlockSpec((lay.ctx_len, NA_WIDTH), lambda b: (b, 0))
    return pl.pallas_call(
        _ctx_attn_kernel,
        out_shape=jax.ShapeDtypeStruct((lay.ctx_tokens, NA_WIDTH), BF16),
        grid=(lay.n_ctx,),
        in_specs=[blk, blk, blk],
        out_specs=blk,
        compiler_params=_cparams(("arbitrary",)),
        name="ctx_attention",
    )(q, kb, vb)


def na_bias_table(rpb):
    kw, kh = NA_WIN_COLS, NA_WIN_ROWS
    col = jnp.arange(GRID_W, dtype=jnp.int32)
    col_start = jnp.clip(col - kw // 2, 0, GRID_W - kw)
    col_ok = (col[None, :] >= col_start[:, None]) & (col[None, :] < col_start[:, None] + kw)
    dc = jnp.clip(col[None, :] - col[:, None] + kw - 1, 0, 2 * kw - 2)
    bias_c = jnp.where(col_ok, rpb[:, :, dc].astype(F32), NEG_INF)
    tabs = [jnp.concatenate([bias_c[:, d + i] for i in range(kh)], axis=-1) for d in range(kh)]
    return jnp.stack(tabs, axis=0)


def _na_row_start(r, rows):
    return jnp.clip(r - NA_WIN_ROWS // 2, 0, rows - NA_WIN_ROWS)


def _na_row_stages(q_ref, k_ref, v_ref, ck_ref, cv_ref, bias_ref, o_ref, r, rows):
    rs = _na_row_start(r, rows)
    n_loc = NA_WIN_ROWS * GRID_W
    start = pl.multiple_of(rs * GRID_W, GRID_W)
    scale = NA_HEAD_DIM ** -0.5
    lo = lax.broadcasted_iota(jnp.int32, (GRID_W, LANES), 1) < NA_HEAD_DIM
    pairs = [slice(p * LANES, (p + 1) * LANES) for p in range(NA_HEADS // 2)]
    keys = [jnp.concatenate([k_ref[pl.ds(start, n_loc), ps], ck_ref[0, :, ps]], axis=0) for ps in pairs]
    vals = [jnp.concatenate([v_ref[pl.ds(start, n_loc), ps], cv_ref[0, :, ps]], axis=0) for ps in pairs]
    heads = []
    for h in range(NA_HEADS):
        qp = q_ref[:, pairs[h // 2]] * scale
        qh = jnp.where(lo if h % 2 == 0 else jnp.logical_not(lo), qp, jnp.zeros_like(qp))
        s = _dot(qh, keys[h // 2], _NT)
        heads.append(dict(vals=vals[h // 2], s_loc=s[:, :n_loc] + bias_ref[0, h], s_ctx=s[:, n_loc:]))
    yield
    for u in heads:
        u['m'] = jnp.maximum(jnp.max(u['s_loc'], axis=-1, keepdims=True),
                             jnp.max(u['s_ctx'], axis=-1, keepdims=True))
    yield
    for u in heads:
        u['p_loc'] = jnp.exp(u['s_loc'] - u['m'])
        u['p_ctx'] = jnp.exp(u['s_ctx'] - u['m'])
        u['den'] = jnp.sum(u['p_loc'], axis=-1, keepdims=True) + jnp.sum(u['p_ctx'], axis=-1, keepdims=True)
    yield
    for u in heads:
        p = jnp.concatenate([u['p_loc'].astype(BF16), u['p_ctx'].astype(BF16)], axis=1)
        u['o'] = _dot(p, u['vals']) / u['den']
    yield
    for p in range(NA_HEADS // 2):
        o_ref[:, p * LANES:(p + 1) * LANES] = jnp.where(lo, heads[2 * p]['o'], heads[2 * p + 1]['o']).astype(BF16)


def _softplus(x):
    return jnp.maximum(x, 0.0) + jnp.log(1.0 + jnp.exp(-jnp.abs(x)))


def _rwkv_prepare(d, z, prev_row, next_row, prm):
    (mu_ref, w0_ref, w2h_ref, w2l_ref, a0_ref, a2h_ref, a2l_ref, kk_ref, ka_ref, rk_ref) = prm
    c = CHUNK
    cw = RW_WIDTH
    row = lax.broadcasted_iota(jnp.int32, (SUBLANES, 1), 0)
    down, up = pltpu.roll(z, 1, axis=0), pltpu.roll(z, c - 1, axis=0)
    prev = jnp.concatenate([jnp.where(row == 0, prev_row, down[:SUBLANES]), down[SUBLANES:]], axis=0)
    nxt = jnp.concatenate([up[:c - SUBLANES], jnp.where(row == SUBLANES - 1, next_row, up[c - SUBLANES:])], axis=0)
    zs = mu_ref[2:3, :] * z + mu_ref[0:1, :] * prev + mu_ref[1:2, :] * nxt
    r, k, v = zs[:, :cw], zs[:, cw:2 * cw], zs[:, 2 * cw:3 * cw]
    o = 3 * cw
    zw = zs[:, o + d * RW_DECAY_RANK:o + (d + 1) * RW_DECAY_RANK]
    o += 2 * RW_DECAY_RANK
    za = zs[:, o + d * RW_ICLR_RANK:o + (d + 1) * RW_ICLR_RANK]

    w_log = -_softplus(-(w0_ref[d:d + 1, :] + _mm4(jnp.tanh(zw), (w2h_ref[d], w2l_ref[d])))) - 0.5
    logw = -jnp.exp(w_log)
    a = _sigmoid(a0_ref[d:d + 1, :] + _mm4(za, (a2h_ref[d], a2l_ref[d])))
    kd = k * (1.0 + (a - 1.0) * ka_ref[...])
    kk = k * kk_ref[...]

    ti = lax.broadcasted_iota(jnp.int32, (c, c), 0)
    tj = lax.broadcasted_iota(jnp.int32, (c, c), 1)
    tri = ((ti >= tj) if d == 0 else (tj >= ti)).astype(BF16)
    lh, ll = _split(logw)
    cl2 = _dot(tri, jnp.concatenate([lh, ll], axis=1))
    cl = cl2[:, :cw] + cl2[:, cw:]
    p_in = jnp.exp(cl)
    p_ex = jnp.exp(cl - logw)
    p_inv = jnp.exp(-cl)
    p_end = p_in[c - 1:c, :] if d == 0 else p_in[0:1, :]

    pi = lax.broadcasted_iota(jnp.int32, (LANES, LANES), 0)
    pj = lax.broadcasted_iota(jnp.int32, (LANES, LANES), 1)
    same_head = (pi // RW_HEAD_DIM) == (pj // RW_HEAD_DIM)
    ones_bd = same_head.astype(BF16)

    def head_sum(x):
        xh, xl = _split(x)
        s2 = _dot(jnp.concatenate([xh, xl], axis=0), ones_bd)
        return s2[:c] + s2[c:]

    t2 = lax.broadcasted_iota(jnp.int32, (2 * c, LANES), 0) % c
    j2 = lax.broadcasted_iota(jnp.int32, (2 * c, LANES), 1) % c
    diff = (t2 - j2) if d == 0 else (j2 - t2)
    upper = lax.broadcasted_iota(jnp.int32, (2 * c, LANES), 0) < c
    mask = diff >= jnp.where(upper, 1, 0)
    units = []
    for p in range(RW_HEADS // 2):
        sl = slice(p * LANES, (p + 1) * LANES)
        kkp = kk[:, sl]
        kkp = kkp * lax.rsqrt(jnp.maximum(head_sum(kkp * kkp), 1e-24))
        pinv = p_inv[:, sl]
        bt = a[:, sl] * kkp * pinv
        kt = kd[:, sl] * pinv
        ar = jnp.concatenate([-kkp * p_ex[:, sl], r[:, sl] * p_in[:, sl]], axis=0).astype(BF16)
        pe = p_end[:, sl]
        bonus = head_sum(r[:, sl] * kd[:, sl] * rk_ref[:, sl]) * v[:, sl]
        units.append(dict(d=d, p=p, ar=ar, bt=bt, kt=kt, v=v[:, sl], pe=pe,
                          bkp=jnp.concatenate([bt * pe, kt * pe], axis=0), bonus=bonus,
                          mask=mask, eye=(diff[:c] == 0).astype(F32), bd=same_head))
    return units, zs


def _bd(x):
    lo = lax.broadcasted_iota(jnp.int32, x.shape, 1) < RW_HEAD_DIM
    xb = x.astype(BF16)
    z = jnp.zeros_like(xb)
    return jnp.concatenate([jnp.where(lo, xb, z), jnp.where(lo, z, xb)], axis=0)


def _rwkv_solve(units, s_ref, fill=()):
    c = CHUNK

    def tick():
        for f in fill:
            next(f, None)

    for u in units:
        g = _dot(u['ar'], jnp.concatenate([_bd(u['bt']), _bd(u['kt'])], axis=0), _NT)
        gb = jnp.where(u['mask'], g[:, :LANES], 0.0)
        gk = g[:, LANES:]
        u['a_ab'], u['a_rb'] = gb[:c], gb[c:]
        u['s0'] = s_ref[u['d'], u['p']]
        lhs = jnp.concatenate([u['ar'], jnp.where(u['mask'], gk, 0.0).astype(BF16)], axis=1)
        u['xs'] = _dot(lhs, jnp.concatenate([u['s0'].T.astype(BF16), _bd(u['v'])], axis=0))
    tick()
    for u in units:
        u['pw'] = _mm(u['a_ab'], _bd(u['a_ab']))
        u['tinv'] = u['eye'] + u['a_ab']
    tick()
    n_sq = CHUNK.bit_length() - 2
    for i in range(n_sq):
        for u in units:
            if i < n_sq - 1:
                both = _mm(jnp.concatenate([u['pw'], u['tinv']], axis=0), _bd(u['pw']))
                u['pw'], u['tinv'] = both[:c], u['tinv'] + both[c:]
            else:
                u['tinv'] = u['tinv'] + _mm(u['tinv'], _bd(u['pw']))
        tick()
    for u in units:
        u['u'] = _mm(u['tinv'], _bd(u['xs'][:c]))
    tick()
    ys = []
    for u in units:
        ys.append(u['xs'][c:] + _mm(u['a_rb'], _bd(u['u'])) + u['bonus'])
        uv = jnp.concatenate([u['u'], u['v']], axis=0)
        s_ref[u['d'], u['p']] = u['s0'] * u['pe'] + jnp.where(u['bd'], _mm(uv.T, u['bkp']), 0.0)
    for f in fill:
        for _ in f:
            pass
    return ys


RW_SEQS_MAX = 2
NA_ROWS_PER_STEP = 2


def _rwkv_kernel(*refs, nc, has_s0, na_rows, n_seq):
    zf_ref, zfp_ref, zfn_ref, zr_ref, zrp_ref, zrn_ref = refs[:6]
    n_in = 6 + (1 if has_s0 else 0)
    s0_ref = refs[6] if has_s0 else None
    prm = refs[n_in:n_in + 10]
    g2_ref = refs[n_in + 10]
    n_in += 11
    i = pl.program_id(1)
    if na_rows:
        q_ref, k_ref, v_ref, ck_ref, cv_ref = refs[n_in:n_in + 5]
        bias_refs = refs[n_in + 5:n_in + 5 + NA_ROWS_PER_STEP]
        yf_ref, yb_ref, g_ref, sfin_ref, oa_ref, s_ref = refs[n_in + 5 + NA_ROWS_PER_STEP:]
    else:
        yf_ref, yb_ref, g_ref, sfin_ref, s_ref = refs[n_in:]

    hd = RW_HEAD_DIM
    npair = RW_HEADS // 2

    @pl.when(i == 0)
    def _():
        s_ref[...] = jnp.zeros(s_ref.shape, F32)
        if has_s0:
            for k in range(n_seq):
                for d in range(2):
                    for p in range(npair):
                        s_ref[2 * k + d, p, :hd, :hd] = s0_ref[k, d, 2 * p]
                        s_ref[2 * k + d, p, hd:, hd:] = s0_ref[k, d, 2 * p + 1]

    zero = jnp.zeros((1, B_COLS), F32)
    units = []
    for k in range(n_seq):
        pf = jnp.where(i == 0, zero, zfp_ref[k, 7:8, :])
        nf = jnp.where(i == nc - 1, zero, zfn_ref[k, 0:1, :])
        uf, zs = _rwkv_prepare(0, zf_ref[k], pf, nf, prm)
        g_ref[k] = _mm(_sigmoid(zs[:, B_COLS - RW_GATE_RANK:]), g2_ref[...])
        pr = jnp.where(i == nc - 1, zero, zrp_ref[k, 7:8, :])
        nr = jnp.where(i == 0, zero, zrn_ref[k, 0:1, :])
        ub, _ = _rwkv_prepare(1, zr_ref[k], pr, nr, prm)
        for u in uf + ub:
            u['d'] = 2 * k + u['d']
        units += uf + ub
    fill = []
    if na_rows:
        r0 = ((pl.program_id(0) * nc + i) * NA_ROWS_PER_STEP) % na_rows
        for k, b_ref in enumerate(bias_refs):
            rs = pl.ds(k * GRID_W, GRID_W)
            fill.append(_na_row_stages(q_ref.at[rs], k_ref, v_ref, ck_ref, cv_ref, b_ref, oa_ref.at[rs],
                                       r0 + k, na_rows))
    ys = _rwkv_solve(units, s_ref, fill)
    for k in range(n_seq):
        yk = ys[2 * npair * k:2 * npair * (k + 1)]
        yf_ref[k] = jnp.concatenate(yk[:npair], axis=-1)
        yb_ref[k] = jnp.concatenate(yk[npair:], axis=-1)

    @pl.when(i == nc - 1)
    def _():
        for k in range(n_seq):
            for d in range(2):
                for p in range(npair):
                    sfin_ref[k, d, 2 * p] = s_ref[2 * k + d, p, :hd, :hd]
                    sfin_ref[k, d, 2 * p + 1] = s_ref[2 * k + d, p, hd:, hd:]


def rwkv_params(mu, w0, w2, a0, a2, k_k, k_a, r_k):
    vec = lambda a: a.reshape(1, -1)
    mu3 = jnp.concatenate([mu, 1.0 - mu[0:1] - mu[1:2]], axis=0)
    return (mu3, w0) + _split(w2) + (a0,) + _split(a2) + (vec(k_k), vec(k_a), vec(r_k))


def rwkv_scan(zb, seq0, nseq, s0, prm, g2, na=None):
    seqlen = zb.shape[1]
    ns = RW_SEQS_MAX
    while seq0 % ns or nseq % ns or (na is not None and na[0].n_dec * (na[0].dec_len // GRID_W)
                                     != NA_ROWS_PER_STEP * (nseq // ns) * (seqlen // CHUNK)):
        ns //= 2
    assert ns >= 1 and seqlen % CHUNK == 0
    nc = seqlen // CHUNK
    sub = CHUNK // 8
    b0 = seq0 // ns
    has_s0 = s0 is not None

    def specs(cidx):
        return [pl.BlockSpec((ns, CHUNK, B_COLS), lambda s, i: (b0 + s, cidx(i), 0)),
                pl.BlockSpec((ns, 8, B_COLS), lambda s, i: (b0 + s, jnp.maximum(cidx(i) * sub - 1, 0), 0)),
                pl.BlockSpec((ns, 8, B_COLS),
                             lambda s, i: (b0 + s, jnp.minimum((cidx(i) + 1) * sub, seqlen // 8 - 1), 0))]

    cf = lambda i: i
    cr = lambda i: nc - 1 - i
    full = lambda a: pl.BlockSpec(a.shape, lambda s, i: (0,) * a.ndim)
    sshape = (2, RW_HEADS, RW_HEAD_DIM, RW_HEAD_DIM)
    sspec = pl.BlockSpec((ns,) + sshape, lambda s, i: (s, 0, 0, 0, 0))
    in_specs = specs(cf) + specs(cr)
    args = [zb] * 6
    if has_s0:
        in_specs.append(sspec)
        args.append(s0)
    in_specs += [full(p) for p in prm] + [full(g2)]
    args += list(prm) + [g2]
    out = lambda cidx: pl.BlockSpec((ns, CHUNK, RW_WIDTH), lambda s, i: (s, cidx(i), 0))
    tok = jax.ShapeDtypeStruct((nseq, seqlen, RW_WIDTH), F32)
    out_shape = [tok, tok, tok, jax.ShapeDtypeStruct((nseq,) + sshape, F32)]
    out_specs = [out(cf), out(cr), out(cf), sspec]
    na_rows = 0
    if na is not None:
        lay, q, kb, vb, ck, cv, bias_tab = na
        na_rows = lay.dec_len // GRID_W
        per = NA_ROWS_PER_STEP
        assert na_rows >= NA_WIN_ROWS and na_rows % per == 0 and lay.ctx_tokens % lay.dec_len == 0
        assert lay.n_dec * na_rows == per * (nseq // ns) * nc and lay.ctx_tokens % (per * GRID_W) == 0
        step = lambda s, i: s * nc + i
        batch = lambda s, i: step(s, i) * per // na_rows
        kv = pl.BlockSpec((lay.dec_len, NA_WIDTH), lambda s, i: (lay.ctx_tokens // lay.dec_len + batch(s, i), 0))
        cs = pl.BlockSpec((1, ck.shape[1], NA_WIDTH), lambda s, i: (batch(s, i), 0, 0))

        def bias_spec(k):
            def idx(s, i):
                r = step(s, i) * per % na_rows + k
                return (_na_row_start(r, na_rows) - r + NA_WIN_ROWS - 1, 0, 0, 0)
            return pl.BlockSpec((1, NA_HEADS, GRID_W, NA_WIN_ROWS * GRID_W), idx)

        in_specs += [pl.BlockSpec((per * GRID_W, NA_WIDTH),
                                  lambda s, i: (lay.ctx_tokens // (per * GRID_W) + step(s, i), 0)),
                     kv, kv, cs, cs] + [bias_spec(k) for k in range(per)]
        args += [q, kb, vb, ck, cv] + [bias_tab] * per
        out_shape.append(jax.ShapeDtypeStruct((lay.dec_tokens, NA_WIDTH), BF16))
        out_specs.append(pl.BlockSpec((per * GRID_W, NA_WIDTH), lambda s, i: (step(s, i), 0)))
    res = pl.pallas_call(
        functools.partial(_rwkv_kernel, nc=nc, has_s0=has_s0, na_rows=na_rows, n_seq=ns),
        out_shape=tuple(out_shape),
        grid=(nseq // ns, nc),
        in_specs=in_specs,
        out_specs=tuple(out_specs),
        scratch_shapes=[pltpu.VMEM((2 * ns, RW_HEADS // 2, LANES, LANES), F32)],
        compiler_params=_cparams(("arbitrary", "arbitrary")),
        name="rwkv_scan",
    )(*args)
    flat = lambda a: a.reshape(nseq * seqlen, RW_WIDTH)
    return (flat(res[0]), flat(res[1]), flat(res[2])) + tuple(res[3:])


def _conv_kernel(z_ref, zp_ref, zn_ref, w_ref, b_ref, g_ref, be_ref, o_ref, h_ref, hs_ref, *, lay):
    i = pl.program_id(0)
    first, last = lay.tile_pos(i)

    def glu(z):
        return z[:, :CONV_WIDTH] * _sigmoid(z[:, CONV_WIDTH:])

    hz = jnp.zeros((CONV_HALO, CONV_WIDTH), F32)
    h_ref[0:CONV_HALO, :] = jnp.where(first, hz, glu(zp_ref[...]))
    h_ref[CONV_HALO:CONV_HALO + TM, :] = glu(z_ref[...])
    h_ref[CONV_HALO + TM:, :] = jnp.where(last, hz, glu(zn_ref[...]))
    off = CONV_HALO - CONV_K // 2
    rows = hs_ref.shape[1]
    for ph in range(SUBLANES):
        hs_ref[ph] = h_ref[ph:ph + rows, :]
    acc = jnp.zeros((TM, CONV_WIDTH), F32) + b_ref[...]
    for j in range(CONV_K):
        ph, base = (off + j) % SUBLANES, (off + j) // SUBLANES * SUBLANES
        acc = acc + w_ref[j:j + 1, :] * hs_ref[ph, base:base + TM, :]
    xc = acc - jnp.mean(acc, axis=-1, keepdims=True)
    hn = xc * lax.rsqrt(jnp.mean(xc * xc, axis=-1, keepdims=True) + LN_EPS) * g_ref[...] + be_ref[...]
    o_ref[...] = (hn * _sigmoid(hn)).astype(BF16)


def conv_module(lay, zc, w, b, g, be):
    n = lay.n
    hb = TM // CONV_HALO
    vec = pl.BlockSpec((1, CONV_WIDTH), lambda i: (0, 0))
    return pl.pallas_call(
        functools.partial(_conv_kernel, lay=lay),
        out_shape=jax.ShapeDtypeStruct((n, CONV_WIDTH), BF16),
        grid=(lay.tiles,),
        in_specs=[pl.BlockSpec((TM, C_COLS), lambda i: (i, 0)),
                  pl.BlockSpec((CONV_HALO, C_COLS), lambda i: (jnp.maximum(i * hb - 1, 0), 0)),
                  pl.BlockSpec((CONV_HALO, C_COLS), lambda i: (jnp.minimum((i + 1) * hb, n // CONV_HALO - 1), 0)),
                  pl.BlockSpec((CONV_K, CONV_WIDTH), lambda i: (0, 0)), vec, vec, vec],
        out_specs=pl.BlockSpec((TM, CONV_WIDTH), lambda i: (i, 0)),
        scratch_shapes=[pltpu.VMEM((TM + 2 * CONV_HALO, CONV_WIDTH), F32),
                        pltpu.VMEM((SUBLANES, TM + 2 * CONV_HALO - SUBLANES, CONV_WIDTH), F32)],
        compiler_params=_cparams(("arbitrary",)),
        name="conv_module",
    )(zc, zc, zc, w, b, g, be)


def _merge_kernel(x_ref, oa_c_ref, yf_c_ref, yb_c_ref, g_c_ref, oa_d_ref, yf_d_ref, yb_d_ref, g_d_ref,
                  oc_ref, gt_ref, mod_ref, ng_ref, lg_ref, lb_ref,
                  wa_ref, wr_ref, wc_ref, wo_ref, rw_ref, rb_ref,
                  x1_ref, hm_ref, ti_ref, tg_ref, rk_ref, cnt_ref, *, ctx_tiles):
    is_ctx = pl.program_id(0) < ctx_tiles
    pick = lambda c_ref, d_ref: jnp.where(is_ctx, c_ref[...], d_ref[...])
    oa = pick(oa_c_ref, oa_d_ref)
    g_rw = pick(g_c_ref, g_d_ref)
    y = pick(yf_c_ref, yf_d_ref) + pick(yb_c_ref, yb_d_ref)
    pi = lax.broadcasted_iota(jnp.int32, (LANES, LANES), 0)
    pj = lax.broadcasted_iota(jnp.int32, (LANES, LANES), 1)
    ones_bd = ((pi // RW_HEAD_DIM) == (pj // RW_HEAD_DIM)).astype(BF16)

    def head_mean(v):
        vh, vl = _split(v)
        s2 = _dot(jnp.concatenate([vh, vl], axis=0), ones_bd)
        return (s2[:TM] + s2[TM:]) * (1.0 / RW_HEAD_DIM)

    outs = []
    for p in range(RW_HEADS // 2):
        yp = y[:, p * LANES:(p + 1) * LANES]
        yc = yp - head_mean(yp)
        outs.append(yc * lax.rsqrt(head_mean(yc * yc) + GN_EPS))
    o_rw = (jnp.concatenate(outs, axis=-1) * lg_ref[...] + lb_ref[...]) * g_rw
    merged = (gt_ref[:, :D_MODEL].astype(F32) * _dot(oa, wa_ref[...])
              + gt_ref[:, D_MODEL:2 * D_MODEL].astype(F32) * _mm(o_rw, wr_ref[...])
              + gt_ref[:, 2 * D_MODEL:].astype(F32) * _dot(oc_ref[...], wc_ref[...]))
    mix = _mm(merged, wo_ref[...])
    x1 = x_ref[...] + mod_ref[0, 2:3, :] * _rms(mix, ng_ref[1:2, :])
    x1_ref[...] = x1
    hm = _rms(x1, ng_ref[2:3, :]) * (1.0 + mod_ref[0, 4:5, :]) + mod_ref[0, 3:4, :]
    hm_ref[...] = hm.astype(BF16)
    logits = _mm3(hm, rw_ref[...]) + rb_ref[...]
    lane = lax.broadcasted_iota(jnp.int32, logits.shape, 1)
    idx_out = jnp.zeros(logits.shape, jnp.int32)
    val_out = jnp.full(logits.shape, NEG_INF, F32)
    picks = []
    for j in range(TOP_K):
        m = jnp.max(logits, axis=-1, keepdims=True)
        idx = jnp.min(jnp.where(logits == m, lane, LANES), axis=-1, keepdims=True)
        idx_out = jnp.where(lane == j, idx, idx_out)
        val_out = jnp.where(lane == j, m, val_out)
        picks.append(lane == idx)
        logits = jnp.where(picks[-1], -jnp.inf, logits)
    e = jnp.exp(val_out - jnp.max(val_out, axis=-1, keepdims=True))
    ti_ref[...] = idx_out
    tg_ref[...] = e / jnp.sum(e, axis=-1, keepdims=True)

    chosen = jnp.zeros(logits.shape, F32)
    for pk in picks:
        chosen = chosen + pk.astype(F32)
    ti_ = lax.broadcasted_iota(jnp.int32, (TM, TM), 0)
    tj_ = lax.broadcasted_iota(jnp.int32, (TM, TM), 1)
    before = _dot((ti_ > tj_).astype(BF16), chosen.astype(BF16))
    rank = jnp.zeros(logits.shape, F32)
    for j, pk in enumerate(picks):
        rank = jnp.where(lane == j, jnp.sum(jnp.where(pk, before, 0.0), axis=-1, keepdims=True), rank)
    rk_ref[...] = rank.astype(jnp.int32)
    cnt_ref[0] = jnp.sum(chosen, axis=0, keepdims=True)


def merge_router(lay, x, ctx_br, dec_br, o_cv, gates, mod, ng, lg, lb, wa, wr, wc, wo, rw, rb):
    n = lay.n
    row = lambda w: pl.BlockSpec((TM, w), lambda i: (i, 0))
    crow = lambda w: pl.BlockSpec((TM, w), lambda i: (jnp.minimum(i, lay.ctx_tiles - 1), 0))
    drow = lambda w: pl.BlockSpec((TM, w), lambda i: (jnp.maximum(i - lay.ctx_tiles, 0), 0))
    full = lambda a: pl.BlockSpec(a.shape, lambda i: (0,) * a.ndim)
    br_w = (NA_WIDTH, RW_WIDTH, RW_WIDTH, RW_WIDTH)
    return pl.pallas_call(
        functools.partial(_merge_kernel, ctx_tiles=lay.ctx_tiles),
        out_shape=(jax.ShapeDtypeStruct((n, D_MODEL), F32), jax.ShapeDtypeStruct((n, D_MODEL), BF16),
                   jax.ShapeDtypeStruct((n, LANES), jnp.int32), jax.ShapeDtypeStruct((n, LANES), F32),
                   jax.ShapeDtypeStruct((n, LANES), jnp.int32), jax.ShapeDtypeStruct((lay.tiles, 1, LANES), F32)),
        grid=(lay.tiles,),
        in_specs=[row(D_MODEL)] + [crow(w) for w in br_w] + [drow(w) for w in br_w] + [
                  row(CONV_WIDTH),
                  row(G_COLS), pl.BlockSpec((1, N_MOD, D_MODEL), lambda i: (lay.mod_row(i), 0, 0)),
                  full(ng), full(lg), full(lb), full(wa), full(wr), full(wc), full(wo), full(rw), full(rb)],
        out_specs=(row(D_MODEL), row(D_MODEL), row(LANES), row(LANES), row(LANES),
                   pl.BlockSpec((1, 1, LANES), lambda i: (i, 0, 0))),
        compiler_params=_cparams(("arbitrary",)),
        name="merge_router",
    )(x, *ctx_br, *dec_br, o_cv, gates, mod, ng, lg, lb, wa, wr, wc, wo, rw, rb)


RUN_ALIGN = 8
RUN_BITS = (TM // RUN_ALIGN).bit_length()
SORT_ROWS = -(-(TM * TOP_K + N_EXPERTS * (RUN_ALIGN - 1)) // LANES) * LANES
PIECE_BASE = 8
TABLE_LEN = 512


def route_tables(top_i, wrank, cnt, n):
    tiles = n // TM
    cnt = cnt[:, 0, :N_EXPERTS].astype(jnp.int32)
    run = ((cnt + RUN_ALIGN - 1) // RUN_ALIGN) * RUN_ALIGN
    off = jnp.cumsum(run, axis=1) - run
    before = jnp.cumsum(run, axis=0) - run
    region = jnp.sum(run, axis=0)
    padded = ((region + ROUTE_BLOCK - 1) // ROUTE_BLOCK) * ROUTE_BLOCK
    pad_end = jnp.cumsum(padded)
    first = (pad_end - padded)[None, :] + before
    k_bits = jnp.arange(RUN_BITS, dtype=jnp.int32)
    size = RUN_ALIGN << k_bits
    has = (run[:, :, None] & size) != 0
    done = run[:, :, None] & (-2 * size)
    slot = jnp.cumsum(has.astype(jnp.int32), axis=1) - 1
    sel = has[:, None] & (slot[:, None] == jnp.arange(N_EXPERTS, dtype=jnp.int32)[None, :, None, None])
    pick = lambda v: jnp.sum(jnp.where(sel, v[:, None], 0), axis=2)
    pieces = jnp.stack([pick(off[:, :, None] + done), pick(first[:, :, None] + done)], axis=-1)
    pieces = pieces.transpose(0, 2, 1, 3).reshape(tiles, RUN_BITS * N_EXPERTS * 2)
    counts = jnp.sum(has.astype(jnp.int32), axis=1)
    table = jnp.concatenate([counts, jnp.zeros((tiles, PIECE_BASE - RUN_BITS), jnp.int32), pieces,
                             jnp.zeros((tiles, TABLE_LEN - PIECE_BASE - pieces.shape[1]), jnp.int32)], axis=1)
    onehot = top_i.reshape(tiles, TM, TOP_K, 1) == jnp.arange(N_EXPERTS, dtype=jnp.int32)
    pos = jnp.sum(jnp.where(onehot, off[:, None, None, :], 0), axis=-1) + wrank.reshape(tiles, TM, TOP_K)
    n_blocks = -(-tiles * (TM * TOP_K + N_EXPERTS * (RUN_ALIGN - 1)) // ROUTE_BLOCK) + N_EXPERTS
    starts = jnp.arange(n_blocks, dtype=jnp.int32) * ROUTE_BLOCK
    block_e = jnp.minimum(jnp.sum((pad_end[None, :] <= starts[:, None]).astype(jnp.int32), axis=1),
                          N_EXPERTS - 1)
    n_used = (pad_end[-1] // ROUTE_BLOCK).astype(jnp.int32).reshape(1)
    return table.reshape(tiles, 1, TABLE_LEN), pos, block_e, n_used, n_blocks


def _piece_priority(size):
    return (size // RUN_ALIGN).bit_length() % 2


def _for_each_piece(tab_ref, fn):
    for k in range(RUN_BITS):
        def body(i, carry, k=k):
            at = PIECE_BASE + (k * N_EXPERTS + i) * 2
            fn(pl.multiple_of(tab_ref[0, 0, at], RUN_ALIGN), pl.multiple_of(tab_ref[0, 0, at + 1], RUN_ALIGN),
               RUN_ALIGN << k)
            return carry

        lax.fori_loop(0, tab_ref[0, 0, k], body, 0)


def _dispatch_kernel(tab_ref, tabp_ref, pos_ref, hm_ref, xb_in_ref, xb_ref, srt_ref, sems):
    del xb_in_ref
    i = pl.program_id(0)
    slot = i % 2

    def copy(slot_, off, row, size):
        return pltpu.make_async_copy(srt_ref.at[slot_, pl.ds(off, size), :], xb_ref.at[pl.ds(row, size), :],
                                     sems.at[slot_])

    p_iota = lax.broadcasted_iota(jnp.int32, (SORT_ROWS, TM), 0)
    hit = p_iota == pos_ref[0, 0:1, :]
    for j in range(1, TOP_K):
        hit = hit | (p_iota == pos_ref[0, j:j + 1, :])
    srt_ref[slot] = _dot(jnp.where(hit, 1.0, 0.0).astype(BF16), hm_ref[...])
    _for_each_piece(tab_ref, lambda off, row, size: copy(slot, off, row, size).start(priority=_piece_priority(size)))

    @pl.when(i > 0)
    def _():
        _for_each_piece(tabp_ref, lambda off, row, size: copy(1 - slot, off, row, size).wait())

    @pl.when(i == pl.num_programs(0) - 1)
    def _():
        _for_each_piece(tab_ref, lambda off, row, size: copy(slot, off, row, size).wait())


def moe_dispatch(lay, hm, table, pos, xb0):
    n_rows = xb0.shape[0]
    post = jnp.pad(pos.transpose(0, 2, 1), ((0, 0), (0, 8 - TOP_K), (0, 0)))
    tab = lambda f: pl.BlockSpec((1, 1, TABLE_LEN), lambda i: (f(i), 0, 0), memory_space=pltpu.SMEM)
    return pl.pallas_call(
        _dispatch_kernel,
        out_shape=jax.ShapeDtypeStruct((n_rows, D_MODEL), F32),
        grid=(lay.tiles,),
        in_specs=[tab(lambda i: i), tab(lambda i: jnp.maximum(i - 1, 0)),
                  pl.BlockSpec((1, 8, TM), lambda i: (i, 0, 0)),
                  pl.BlockSpec((TM, D_MODEL), lambda i: (i, 0)),
                  pl.BlockSpec(memory_space=pl.ANY)],
        out_specs=pl.BlockSpec(memory_space=pl.ANY),
        scratch_shapes=[pltpu.VMEM((2, SORT_ROWS, D_MODEL), F32), pltpu.SemaphoreType.DMA((2,))],
        input_output_aliases={4: 0},
        compiler_params=_cparams(("arbitrary",)),
        name="moe_dispatch",
    )(table, table, post, hm, xb0)


GL_GROUP = 2 * LANES


def _regroup_perm():
    r = jnp.arange(GL_GROUP, dtype=jnp.int32)[:, None]
    c = jnp.arange(GL_GROUP, dtype=jnp.int32)[None, :]
    src = jnp.where(c < LANES, 2 * c, 2 * (c - LANES) + 1)
    return (r == src).astype(BF16)


def _expert_kernel(be_ref, nu_ref, x_ref, w1_ref, perm_ref, b1_ref, w2_ref, b2_ref, y_ref, w1b_ref, w2b_ref):
    b = pl.program_id(0)
    used = b < nu_ref[0]

    @pl.when(used & ((b == 0) | (be_ref[b] != be_ref[jnp.maximum(b - 1, 0)])))
    def _():
        for j in range(2 * EXPERT_FF // GL_GROUP):
            cs = slice(j * GL_GROUP, (j + 1) * GL_GROUP)
            w1b_ref[:, cs] = _dot(w1_ref[:, cs].astype(BF16), perm_ref[...]).astype(BF16)
        w2b_ref[...] = w2_ref[...].astype(BF16)

    @pl.when(used)
    def _():
        x = x_ref[...].astype(BF16)
        n_grp = 2 * EXPERT_FF // GL_GROUP
        hs = [_dot(x, w1b_ref[:, g * GL_GROUP:(g + 1) * GL_GROUP]) + b1_ref[0, :, g * GL_GROUP:(g + 1) * GL_GROUP]
              for g in range(n_grp)]
        acts = []
        for h in hs:
            hg = jnp.minimum(h[:, :LANES], SWIGLU_LIMIT)
            hl = jnp.clip(h[:, LANES:], -SWIGLU_LIMIT, SWIGLU_LIMIT)
            acts.append((hg * _sigmoid(SWIGLU_ALPHA * hg) * (hl + 1.0)).astype(BF16))
        acc = jnp.zeros((ROUTE_BLOCK, D_MODEL), F32) + b2_ref[0]
        for j in range(n_grp // 2):
            act = jnp.concatenate(acts[2 * j:2 * j + 2], axis=-1)
            acc = acc + _dot(act, w2b_ref[j * GL_GROUP:(j + 1) * GL_GROUP, :])
        y_ref[...] = acc

    @pl.when(jnp.logical_not(used))
    def _():
        y_ref[...] = jnp.zeros(y_ref.shape, F32)


def moe_experts(xb, block_e, n_used, n_blocks, l, w1, b1, w2, b2):
    def last_used(b, nu):
        return jnp.minimum(b, jnp.maximum(nu[0] - 1, 0))

    wspec = lambda a: pl.BlockSpec((None, None) + a.shape[2:], lambda b, be, nu: (l, be[last_used(b, nu)], 0, 0))
    bspec = lambda a: pl.BlockSpec((1,) + a.shape[1:], lambda b, be, nu: (be[last_used(b, nu)], 0, 0))
    return pl.pallas_call(
        _expert_kernel,
        out_shape=jax.ShapeDtypeStruct(xb.shape, F32),
        grid_spec=pltpu.PrefetchScalarGridSpec(
            num_scalar_prefetch=2,
            grid=(n_blocks,),
            in_specs=[pl.BlockSpec((ROUTE_BLOCK, D_MODEL), lambda b, be, nu: (last_used(b, nu), 0)),
                      wspec(w1), pl.BlockSpec((GL_GROUP, GL_GROUP), lambda b, be, nu: (0, 0)),
                      bspec(b1), wspec(w2), bspec(b2)],
            out_specs=pl.BlockSpec((ROUTE_BLOCK, D_MODEL), lambda b, be, nu: (b, 0)),
            scratch_shapes=[pltpu.VMEM(w1.shape[2:], BF16), pltpu.VMEM(w2.shape[2:], BF16)]),
        compiler_params=_cparams(("arbitrary",)),
        name="moe_experts",
    )(block_e, n_used, xb, w1, _regroup_perm(), b1, w2, b2)


def _combine_kernel(tab_ref, tabn_ref, yb_ref, pos_ref, tg_ref, x1_ref, mod_ref, ng_ref, *rest, ctx_tiles):
    o_refs, (stg_ref, sems) = rest[:-2], rest[-2:]
    i = pl.program_id(0)
    last = pl.num_programs(0) - 1
    slot = i % 2

    def copy(slot_, off, row, size):
        return pltpu.make_async_copy(yb_ref.at[pl.ds(row, size), :], stg_ref.at[slot_, pl.ds(off, size), :],
                                     sems.at[slot_])

    @pl.when(i == 0)
    def _():
        stg_ref[...] = jnp.zeros(stg_ref.shape, F32)
        _for_each_piece(tab_ref, lambda off, row, size: copy(slot, off, row, size).start(priority=_piece_priority(size)))

    @pl.when(i < last)
    def _():
        _for_each_piece(tabn_ref, lambda off, row, size: copy(1 - slot, off, row, size).start(priority=_piece_priority(size)))

    _for_each_piece(tab_ref, lambda off, row, size: copy(slot, off, row, size).wait())
    p_iota = lax.broadcasted_iota(jnp.int32, (TM, SORT_ROWS), 1)
    w = jnp.zeros((TM, SORT_ROWS), F32)
    for j in range(TOP_K):
        w = w + jnp.where(p_iota == pos_ref[:, j:j + 1], tg_ref[:, j:j + 1], 0.0)
    y = _mm3(w, stg_ref[slot])
    out = x1_ref[...] + mod_ref[0, 5:6, :] * _rms(y, ng_ref[3:4, :])
    if len(o_refs) == 1:
        o_refs[0][...] = out
    else:
        @pl.when(i < ctx_tiles)
        def _():
            o_refs[0][...] = out

        o_refs[1][...] = out


def moe_combine(lay, yb, table, pos, tg, x1, mod, ng, split):
    n = lay.n
    posl = jnp.pad(pos.reshape(n, TOP_K), ((0, 0), (0, LANES - TOP_K)))
    row = lambda w: pl.BlockSpec((TM, w), lambda i: (i, 0))
    tab = lambda f: pl.BlockSpec((1, 1, TABLE_LEN), lambda i: (f(i), 0, 0), memory_space=pltpu.SMEM)
    if split:
        out_shape = (jax.ShapeDtypeStruct((lay.ctx_tokens, D_MODEL), F32),
                     jax.ShapeDtypeStruct((lay.dec_tokens, D_MODEL), F32))
        out_specs = (pl.BlockSpec((TM, D_MODEL), lambda i: (jnp.minimum(i, lay.ctx_tiles - 1), 0)),
                     pl.BlockSpec((TM, D_MODEL), lambda i: (jnp.maximum(i - lay.ctx_tiles, 0), 0)))
    else:
        out_shape = jax.ShapeDtypeStruct((n, D_MODEL), F32)
        out_specs = row(D_MODEL)
    return pl.pallas_call(
        functools.partial(_combine_kernel, ctx_tiles=lay.ctx_tiles),
        out_shape=out_shape,
        grid=(lay.tiles,),
        in_specs=[tab(lambda i: i), tab(lambda i: jnp.minimum(i + 1, lay.tiles - 1)),
                  pl.BlockSpec(memory_space=pl.ANY),
                  row(LANES), row(LANES), row(D_MODEL),
                  pl.BlockSpec((1, N_MOD, D_MODEL), lambda i: (lay.mod_row(i), 0, 0)),
                  pl.BlockSpec(ng.shape, lambda i: (0, 0))],
        out_specs=out_specs,
        scratch_shapes=[pltpu.VMEM((2, SORT_ROWS, D_MODEL), F32), pltpu.SemaphoreType.DMA((2,))],
        compiler_params=_cparams(("arbitrary",)),
        name="moe_combine",
    )(table, table, yb, posl, tg, x1, mod, ng)


def layer(lay, x, l, cond, P, ctx_k, ctx_v, s0_dec, xb_buf):
    mod = mod_table(cond, P['mod_w'][l], P['mod_b'][l]).reshape(cond.shape[0], N_MOD, D_MODEL)
    ng = P['norm_g'][l]
    q, kf, vf, kb, vb, zb, zc, gates = in_proj(lay, x, mod, ng[0:1], P['w_in'][l].astype(BF16),
                                               P['b_in'][l].reshape(1, P_IN))
    oa_c = ctx_attention(lay, q, kb, vb)
    vec = lambda a: a.reshape(1, -1)
    prm = rwkv_params(P['rw_mu'][l], P['rw_w0'][l], P['rw_w2'][l], P['rw_a0'][l], P['rw_a2'][l],
                      P['rw_k_k'][l], P['rw_k_a'][l], P['rw_r_k'][l])
    yf_c, yb_c, g_c, s_ctx = rwkv_scan(zb.reshape(-1, lay.ctx_len, B_COLS), 0, lay.n_ctx, None, prm,
                                       P['rw_g2'][l])
    na = (lay, q, kb, vb, ctx_k.astype(BF16), ctx_v.astype(BF16), na_bias_table(P['rpb'][l]))
    yf_d, yb_d, g_d, _, oa_d = rwkv_scan(zb.reshape(-1, lay.dec_len, B_COLS), lay.ctx_tokens // lay.dec_len,
                                         lay.n_dec, s0_dec, prm, P['rw_g2'][l], na=na)
    o_cv = conv_module(lay, zc, P['conv_w'][l], vec(P['conv_b'][l]), vec(P['conv_ln_g'][l]),
                       vec(P['conv_ln_b'][l]))
    rw = jnp.zeros((D_MODEL, LANES), F32).at[:, :N_EXPERTS].set(P['router_w'][l])
    rb = jnp.full((1, LANES), NEG_INF, F32).at[0, :N_EXPERTS].set(P['router_b'][l])
    x1, hm, top_i, top_g, rank, counts = merge_router(
        lay, x, (oa_c, yf_c, yb_c, g_c), (oa_d, yf_d, yb_d, g_d), o_cv, gates, mod, ng,
        vec(P['rw_lnx_g'][l]), vec(P['rw_lnx_b'][l]),
        P['w_o_attn'][l].astype(BF16), P['w_o_rwkv'][l].astype(BF16), P['w_o_conv'][l].astype(BF16),
        P['w_out'][l].astype(BF16), rw, rb)
    table, pos, block_e, n_used, n_blocks = route_tables(top_i[:, :TOP_K], rank[:, :TOP_K], counts, lay.n)
    if xb_buf is None:
        xb_buf = jnp.zeros((n_blocks * ROUTE_BLOCK, D_MODEL), F32)
    xb = moe_dispatch(lay, hm, table, pos, xb_buf)
    b1 = P['exp_b1'][l].reshape(N_EXPERTS, -1, LANES, 2).transpose(0, 1, 3, 2).reshape(N_EXPERTS, 1, -1)
    ybk = moe_experts(xb, block_e, n_used, n_blocks, l, P['exp_w1'], b1, P['exp_w2'], P['exp_b2'][l][:, None, :])
    x2 = moe_combine(lay, ybk, table, pos, top_g, x1, mod, ng, split=(l == DEPTH - 1))
    return x2, kf, vf, s_ctx, xb


def kernel(x_prompt, x_sample, cache_attn_k, cache_attn_v, state_rwkv, c, c_ctx, mod_w, mod_b, norm_g, w_in, b_in, rpb, w_o_attn, rw_mu, rw_w0, rw_w2, rw_a0, rw_a2, rw_g2, rw_k_k, rw_k_a, rw_r_k, rw_lnx_g, rw_lnx_b, w_o_rwkv, conv_w, conv_b, conv_ln_g, conv_ln_b, w_o_conv, w_out, router_w, router_b, exp_w1, exp_b1, exp_w2, exp_b2):
    P = {
        'mod_w': mod_w, 'mod_b': mod_b, 'norm_g': norm_g, 'w_in': w_in, 'b_in': b_in, 'rpb': rpb,
        'w_o_attn': w_o_attn, 'rw_mu': rw_mu, 'rw_w0': rw_w0, 'rw_w2': rw_w2, 'rw_a0': rw_a0,
        'rw_a2': rw_a2, 'rw_g2': rw_g2, 'rw_k_k': rw_k_k, 'rw_k_a': rw_k_a, 'rw_r_k': rw_r_k,
        'rw_lnx_g': rw_lnx_g, 'rw_lnx_b': rw_lnx_b, 'w_o_rwkv': w_o_rwkv, 'conv_w': conv_w,
        'conv_b': conv_b, 'conv_ln_g': conv_ln_g, 'conv_ln_b': conv_ln_b, 'w_o_conv': w_o_conv,
        'w_out': w_out, 'router_w': router_w, 'router_b': router_b, 'exp_w1': exp_w1,
        'exp_b1': exp_b1, 'exp_w2': exp_w2, 'exp_b2': exp_b2,
    }
    nb, sl, _ = x_prompt.shape
    db, ds, _ = x_sample.shape
    lay = Layout(nb, sl, db, ds)
    x = jnp.concatenate([x_prompt.reshape(-1, D_MODEL), x_sample.reshape(-1, D_MODEL)], axis=0)
    cond = jnp.zeros((8, D_MODEL), F32).at[0].set(c_ctx).at[1:1 + db].set(c)
    ks, vs, ss = [], [], []
    xb_buf = None
    for l in range(DEPTH):
        ck = cache_attn_k[:, l].reshape(db, -1, NA_WIDTH)
        cv = cache_attn_v[:, l].reshape(db, -1, NA_WIDTH)
        x, kf, vf, s_ctx, xb_buf = layer(lay, x, l, cond, P, ck, cv, state_rwkv[:, l], xb_buf)
        ks.append(kf.reshape(nb, sl, NA_HEADS, NA_HEAD_DIM))
        vs.append(vf.reshape(nb, sl, NA_HEADS, NA_HEAD_DIM))
        ss.append(s_ctx)
    y_prompt = x[0].reshape(nb, sl, D_MODEL)
    y_sample = x[1].reshape(db, ds, D_MODEL)
    return (y_prompt, y_sample, jnp.stack(ks, axis=1), jnp.stack(vs, axis=1), jnp.stack(ss, axis=1))
```
